```python
import jax
import jax.numpy as jnp
from jax import lax
import numpy as np

D_MODEL = 1024
BATCH = 8
SEQ = 2048
DEPTH = 1
DEC_BATCH = 32
DEC_SEQ = 16
PAST_LEN = 2048

CHUNK = 64
EPS = 1e-6
GN_EPS = 64e-5
ROPE_THETA = 10000.0
RW_HEAD_DIM = 64
RW_HEADS = D_MODEL // RW_HEAD_DIM
RW_W = RW_HEADS * RW_HEAD_DIM
D_DECAY_LORA = max(32, int(round(1.8 * D_MODEL ** 0.5 / 32)) * 32)
D_AAA_LORA = D_DECAY_LORA
D_GATE_LORA = max(32, int(round(0.6 * D_MODEL ** 0.8 / 32)) * 32)
ATT_HEAD_DIM = 64
ATT_HEADS = D_MODEL // ATT_HEAD_DIM
ATT_W = ATT_HEADS * ATT_HEAD_DIM
IDX_HEADS = 8
IDX_DIM = 64
TOPK_MAX = 256
Q_BLOCK = 64
PEER_HEADS = 8
PEER_NKEYS = 128
PEER_TOPK = 16
PEER_DKEY = 128
PEER_DHALF = PEER_DKEY // 2
N_EXPERTS = PEER_NKEYS * PEER_NKEYS
PEER_BLOCK = 128
RW_IN = 3 * RW_W + D_DECAY_LORA + D_AAA_LORA + D_GATE_LORA
DSA_IN = 3 * ATT_W + IDX_HEADS * IDX_DIM + IDX_DIM + IDX_HEADS
GATE_IN = 2 * D_MODEL
IN_W = RW_IN + DSA_IN + GATE_IN

kernel_name = 'rwkv7_dsa_peer_streaming_step'


def _rmsnorm(x, g):
    xf = x.astype(jnp.float32)
    y = xf * lax.rsqrt(jnp.mean(xf * xf, axis=-1, keepdims=True) + EPS)
    return (y * g.astype(jnp.float32)).astype(x.dtype)


def _split(a, sizes):
    offs = np.cumsum(sizes)[:-1].tolist()
    return jnp.split(a, offs, axis=-1)


def _rope(x, pos):
    half = x.shape[-1] // 2
    inv = ROPE_THETA ** (-jnp.arange(half, dtype=jnp.float32) / half)
    ang = pos.astype(jnp.float32)[:, None] * inv[None, :]
    cos = jnp.cos(ang)[:, None, :]
    sin = jnp.sin(ang)[:, None, :]
    xf = x.astype(jnp.float32)
    x1, x2 = xf[..., :half], xf[..., half:]
    return jnp.concatenate([x1 * cos - x2 * sin, x2 * cos + x1 * sin], axis=-1).astype(x.dtype)


def _rwkv7_step(S, inp):
    r, w, k, v, a, b = inp
    sa = jnp.einsum('bhij,bhj->bhi', S, a)
    S = S * w[:, :, None, :] + sa[..., None] * b[:, :, None, :] + v[..., None] * k[:, :, None, :]
    y = jnp.einsum('bhij,bhj->bhi', S, r)
    return S, y


def _rwkv7(P, shift_prev, S0, mu_rw, w0, w_up, a0, a_up, g_up, k_k, k_a, r_k, lnx_w, lnx_b):
    B, T, _ = P.shape
    f32 = jnp.float32
    P_prev = jnp.concatenate([shift_prev[:, None, :], P[:, :-1]], axis=1)
    X = P + (P_prev - P) * mu_rw
    r, k, v, wd, ad, gd = _split(X, [RW_W, RW_W, RW_W, D_DECAY_LORA, D_AAA_LORA, D_GATE_LORA])
    w_log = -jax.nn.softplus(-(w0 + jnp.tanh(wd) @ w_up).astype(f32)) - 0.5
    decay = jnp.exp(-jnp.exp(w_log))
    a = jax.nn.sigmoid((a0 + ad @ a_up).astype(f32))
    g = jax.nn.sigmoid(gd) @ g_up
    hd = lambda t: t.reshape(B, T, RW_HEADS, RW_HEAD_DIM).astype(f32)
    kk = hd(k * k_k)
    kk = kk * lax.rsqrt(jnp.sum(kk * kk, axis=-1, keepdims=True) + 1e-12)
    k_mod = k.astype(f32) * (1.0 + (a - 1.0) * k_a.astype(f32))
    r_h, w_h, k_h, v_h, a_h = hd(r), hd(decay), hd(k_mod), hd(v), hd(a)
    xs = tuple(jnp.moveaxis(t, 1, 0) for t in (r_h, w_h, k_h, v_h, -kk, kk * a_h))
    S_fin, ys = lax.scan(_rwkv7_step, S0.astype(f32), xs)
    y = jnp.moveaxis(ys, 0, 1)
    mean = jnp.mean(y, axis=-1, keepdims=True)
    var = jnp.mean(jnp.square(y - mean), axis=-1, keepdims=True)
    y = ((y - mean) * lax.rsqrt(var + GN_EPS)).reshape(B, T, RW_W) * lnx_w.astype(f32) + lnx_b.astype(f32)
    bonus = jnp.sum(r_h * k_h * r_k.astype(f32), axis=-1, keepdims=True) * v_h
    y = y + bonus.reshape(B, T, RW_W)
    out = (y * g.astype(f32)).astype(P.dtype)
    return out, S_fin.astype(P.dtype), P[:, -1]


def _sparse_attend(q, qi, wi, q_pos, k_all, v_all, ki_all, k_pos):
    B, T, H, dh = q.shape
    L = k_all.shape[1]
    topk = min(TOPK_MAX, L // 4)
    qb = min(Q_BLOCK, T)
    nb = T // qb
    k_chunk = k_pos // CHUNK
    gather = jax.vmap(lambda rows, idx: rows[idx])

    def block(args):
        qh, qih, wih, qp = args
        rel = jax.nn.relu(jnp.einsum('bqhd,bsd->bqhs', qih, ki_all))
        score = jnp.einsum('bqh,bqhs->bqs', wih, rel)
        adm = (qp[:, None] // CHUNK) >= k_chunk[None, :]
        score = jnp.where(adm[None], score, -jnp.inf)
        vals, idx = lax.top_k(score, topk)
        ok = vals > -jnp.inf
        ks = gather(k_all, idx)
        vs = gather(v_all, idx)
        s = jnp.einsum('bqhd,bqkhd->bhqk', qh, ks).astype(jnp.float32) * (dh ** -0.5)
        s = jnp.where(ok[:, None], s, -jnp.inf)
        p = jax.nn.softmax(s, axis=-1).astype(vs.dtype)
        return jnp.einsum('bhqk,bqkhd->bqhd', p, vs)

    to_blocks = lambda t: jnp.moveaxis(t.reshape((B, nb, qb) + t.shape[2:]), 1, 0)
    out = lax.map(block, (to_blocks(q), to_blocks(qi), to_blocks(wi), q_pos.reshape(nb, qb)))
    return jnp.moveaxis(out, 0, 1).reshape(B, T, H, dh)


def _dsa(P_dsa, pos, k_pos, cache_k, cache_v, cache_kidx, q_norm_w, k_norm_w):
    B, T, _ = P_dsa.shape
    q, k, v, qi, ki, wi = _split(P_dsa, [ATT_W, ATT_W, ATT_W, IDX_HEADS * IDX_DIM, IDX_DIM, IDX_HEADS])
    hs = lambda t: t.reshape(B, T, ATT_HEADS, ATT_HEAD_DIM)
    q = _rope(_rmsnorm(hs(q), q_norm_w), pos)
    k = _rope(_rmsnorm(hs(k), k_norm_w), pos)
    v = hs(v)
    qi = _rope(qi.reshape(B, T, IDX_HEADS, IDX_DIM), pos)
    ki = _rope(ki[:, :, None, :], pos)[:, :, 0]
    wi = wi * (IDX_HEADS * IDX_DIM) ** -0.5
    k_all = jnp.concatenate([cache_k, k], axis=1)
    v_all = jnp.concatenate([cache_v, v], axis=1)
    ki_all = jnp.concatenate([cache_kidx, ki], axis=1)
    o = _sparse_attend(q, qi, wi, pos, k_all, v_all, ki_all, k_pos)
    return o.reshape(B, T, ATT_W), k, v, ki


def _peer(h, w_pq, peer_keys, peer_u, peer_v):
    B, T, D = h.shape
    n = B * T
    nb = -(-n // PEER_BLOCK)
    hp = jnp.pad(h.reshape(n, D), ((0, nb * PEER_BLOCK - n), (0, 0))).reshape(nb, PEER_BLOCK, D)

    def block(hb):
        q = (hb @ w_pq).reshape(PEER_BLOCK, PEER_HEADS, 2, PEER_DHALF)
        s = jnp.einsum('nhpd,hpkd->nhpk', q, peer_keys).astype(jnp.float32)
        s1, i1 = lax.top_k(s[:, :, 0], PEER_TOPK)
        s2, i2 = lax.top_k(s[:, :, 1], PEER_TOPK)
        cand = (s1[..., :, None] + s2[..., None, :]).reshape(PEER_BLOCK, PEER_HEADS, PEER_TOPK * PEER_TOPK)
        cidx = (i1[..., :, None] * PEER_NKEYS + i2[..., None, :]).reshape(PEER_BLOCK, PEER_HEADS, PEER_TOPK * PEER_TOPK)
        sc, j = lax.top_k(cand, PEER_TOPK)
        e = jnp.take_along_axis(cidx, j, axis=-1)
        g = jax.nn.softmax(sc, axis=-1)
        act = jax.nn.gelu(jnp.einsum('nhkd,nd->nhk', peer_u[e], hb).astype(jnp.float32))
        return jnp.einsum('nhk,nhkd->nd', (g * act).astype(hb.dtype), peer_v[e])

    out = lax.map(block, hp).reshape(nb * PEER_BLOCK, D)[:n]
    return out.reshape(B, T, D)


def _layer(x, c, pos, k_pos, shift_prev, S0, cache_k, cache_v, cache_kidx, lp):
    mod = jax.nn.silu(c) @ lp['w_ada'] + lp['b_ada']
    sh1, sc1, g1, sh2, sc2, g2 = [m[:, None, :] for m in jnp.split(mod, 6, axis=-1)]
    h = _rmsnorm(x, lp['norm1_w']) * (1 + sc1) + sh1
    P = h @ lp['w_in']
    P_rw, P_dsa, P_gate = _split(P, [RW_IN, DSA_IN, GATE_IN])
    o_a, S_fin, shift_last = _rwkv7(P_rw, shift_prev, S0, lp['mu_rw'], lp['w0'], lp['w_up'], lp['a0'], lp['a_up'],
                                   lp['g_up'], lp['k_k'], lp['k_a'], lp['r_k'], lp['lnx_w'], lp['lnx_b'])
    o_b, k_new, v_new, ki_new = _dsa(P_dsa, pos, k_pos, cache_k, cache_v, cache_kidx, lp['q_norm_w'], lp['k_norm_w'])
    gate_a, gate_b = jnp.split(jax.nn.sigmoid(P_gate + lp['b_gate']), 2, axis=-1)
    m = gate_a * (o_a @ lp['w_proj_a']) + gate_b * (o_b @ lp['w_proj_b'])
    x = x + g1 * (m @ lp['w_out'])
    h2 = _rmsnorm(x, lp['norm2_w']) * (1 + sc2) + sh2
    x = x + g2 * _peer(h2, lp['w_pq'], lp['peer_keys'], lp['peer_u'], lp['peer_v'])
    return x, S_fin, shift_last, k_new, v_new, ki_new


def setup_inputs(seed: int = 0) -> dict:
    key = jax.random.key(seed)
    ks = iter(jax.random.split(key, 40))
    nrm = lambda shape, scale: jax.random.normal(next(ks), shape, jnp.float32) * scale
    D = D_MODEL
    L = DEPTH
    return {
        'x_prompt': nrm((BATCH, SEQ, D), 1.0),
        'x_sample': nrm((DEC_BATCH, DEC_SEQ, D), 1.0),
        'c_prompt': nrm((BATCH, D), 1.0),
        'c_sample': nrm((DEC_BATCH, D), 1.0),
        'cache_k': nrm((L, DEC_BATCH, PAST_LEN, ATT_HEADS, ATT_HEAD_DIM), 1.0),
        'cache_v': nrm((L, DEC_BATCH, PAST_LEN, ATT_HEADS, ATT_HEAD_DIM), 1.0),
        'cache_kidx': nrm((L, DEC_BATCH, PAST_LEN, IDX_DIM), 1.0),
        'state_wkv': nrm((L, DEC_BATCH, RW_HEADS, RW_HEAD_DIM, RW_HEAD_DIM), 0.5),
        'state_shift': nrm((L, DEC_BATCH, RW_IN), 1.0),
        'w_ada': nrm((L, D, 6 * D), 0.5 * D ** -0.5),
        'b_ada': nrm((L, 6 * D), 0.02),
        'norm1_w': 1.0 + nrm((L, D), 0.02),
        'w_in': nrm((L, D, IN_W), D ** -0.5),
        'b_gate': nrm((L, GATE_IN), 0.02),
        'mu_rw': jax.random.uniform(next(ks), (L, RW_IN), jnp.float32),
        'w0': nrm((L, RW_W), 1.0) - 1.0,
        'w_up': nrm((L, D_DECAY_LORA, RW_W), 0.5 * D_DECAY_LORA ** -0.5),
        'a0': nrm((L, RW_W), 0.1),
        'a_up': nrm((L, D_AAA_LORA, RW_W), D_AAA_LORA ** -0.5),
        'g_up': nrm((L, D_GATE_LORA, RW_W), D_GATE_LORA ** -0.5),
        'k_k': 0.85 + nrm((L, RW_W), 0.02),
        'k_a': 1.0 + nrm((L, RW_W), 0.02),
        'r_k': nrm((L, RW_HEADS, RW_HEAD_DIM), 0.1),
        'lnx_w': 1.0 + nrm((L, RW_W), 0.02),
        'lnx_b': nrm((L, RW_W), 0.02),
        'q_norm_w': 1.0 + nrm((L, ATT_HEAD_DIM), 0.02),
        'k_norm_w': 1.0 + nrm((L, ATT_HEAD_DIM), 0.02),
        'w_proj_a': nrm((L, RW_W, D), RW_W ** -0.5),
        'w_proj_b': nrm((L, ATT_W, D), ATT_W ** -0.5),
        'w_out': nrm((L, D, D), D ** -0.5),
        'norm2_w': 1.0 + nrm((L, D), 0.02),
        'w_pq': nrm((L, D, PEER_HEADS * PEER_DKEY), D ** -0.5),
        'peer_keys': nrm((L, PEER_HEADS, 2, PEER_NKEYS, PEER_DHALF), PEER_DHALF ** -0.5),
        'peer_u': nrm((L, N_EXPERTS, D), D ** -0.5),
        'peer_v': nrm((L, N_EXPERTS, D), (PEER_HEADS * PEER_TOPK) ** -0.5),
    }


def reference(x_prompt, x_sample, c_prompt, c_sample, cache_k, cache_v, cache_kidx, state_wkv, state_shift,
              w_ada, b_ada, norm1_w, w_in, b_gate, mu_rw, w0, w_up, a0, a_up, g_up, k_k, k_a, r_k, lnx_w, lnx_b,
              q_norm_w, k_norm_w, w_proj_a, w_proj_b, w_out, norm2_w, w_pq, peer_keys, peer_u, peer_v):
    B, T = x_prompt.shape[:2]
    Ts = x_sample.shape[1]
    past = cache_k.shape[2]
    dt = x_prompt.dtype
    pos_p = jnp.arange(T, dtype=jnp.int32)
    pos_s = past + jnp.arange(Ts, dtype=jnp.int32)
    kpos_s = jnp.arange(past + Ts, dtype=jnp.int32)
    empty_kv = jnp.zeros((B, 0, ATT_HEADS, ATT_HEAD_DIM), dt)
    empty_ki = jnp.zeros((B, 0, IDX_DIM), dt)
    zero_shift = jnp.zeros((B, RW_IN), dt)
    zero_wkv = jnp.zeros((B, RW_HEADS, RW_HEAD_DIM, RW_HEAD_DIM), dt)
    xp, xs = x_prompt, x_sample
    wkv_p, shift_p, k_p, v_p, ki_p = [], [], [], [], []
    wkv_s, shift_s, k_s, v_s, ki_s = [], [], [], [], []
    for l in range(DEPTH):
        lp = {'w_ada': w_ada[l], 'b_ada': b_ada[l], 'norm1_w': norm1_w[l], 'w_in': w_in[l], 'b_gate': b_gate[l],
              'mu_rw': mu_rw[l], 'w0': w0[l], 'w_up': w_up[l], 'a0': a0[l], 'a_up': a_up[l], 'g_up': g_up[l],
              'k_k': k_k[l], 'k_a': k_a[l], 'r_k': r_k[l], 'lnx_w': lnx_w[l], 'lnx_b': lnx_b[l],
              'q_norm_w': q_norm_w[l], 'k_norm_w': k_norm_w[l], 'w_proj_a': w_proj_a[l], 'w_proj_b': w_proj_b[l],
              'w_out': w_out[l], 'norm2_w': norm2_w[l], 'w_pq': w_pq[l], 'peer_keys': peer_keys[l],
              'peer_u': peer_u[l], 'peer_v': peer_v[l]}
        xp, a1, a2, a3, a4, a5 = _layer(xp, c_prompt, pos_p, pos_p, zero_shift, zero_wkv,
                                        empty_kv, empty_kv, empty_ki, lp)
        wkv_p.append(a1); shift_p.append(a2); k_p.append(a3); v_p.append(a4); ki_p.append(a5)
        xs, b1, b2, b3, b4, b5 = _layer(xs, c_sample, pos_s, kpos_s, state_shift[l], state_wkv[l],
                                        cache_k[l], cache_v[l], cache_kidx[l], lp)
        wkv_s.append(b1); shift_s.append(b2); k_s.append(b3); v_s.append(b4); ki_s.append(b5)
    return (xp, xs,
            jnp.stack(wkv_p), jnp.stack(shift_p), jnp.stack(k_p), jnp.stack(v_p), jnp.stack(ki_p),
            jnp.stack(wkv_s), jnp.stack(shift_s), jnp.stack(k_s), jnp.stack(v_s), jnp.stack(ki_s))
```

```python
import functools

import jax
import jax.numpy as jnp
from jax import lax
from jax.experimental import pallas as pl
from jax.experimental.pallas import tpu as pltpu

F32 = jnp.float32
BF16 = jnp.bfloat16
I32 = jnp.int32

LANES = 128
D_MODEL = 1024
EPS = 1e-6
GN_EPS = 64e-5
ROPE_THETA = 10000.0
CHUNK = 64
TOPK_MAX = 256
HEAD_DIM = 64
N_HEADS = D_MODEL // HEAD_DIM
N_PAIRS = N_HEADS // 2
IDX_HEADS = 8
IDX_DIM = 64
D_DECAY = 64
D_AAA = 64
D_GATE = 160
RW_IN = 3 * D_MODEL + D_DECAY + D_AAA + D_GATE
PEER_HEADS = 8
PEER_NKEYS = 128
PEER_TOPK = 16
PEER_DHALF = 64
N_EXPERTS = PEER_NKEYS * PEER_NKEYS
RW_CHUNK = 64
VMEM_LIMIT = 56 * 1024 * 1024

C_R, C_K, C_V = 0, 1024, 2048
C_Q, C_KD, C_VD = 3072, 4096, 5120
C_GA, C_GB = 6144, 7168
C_QI = 8192
C_G = 8704
C_M = 8960
C_KW = 9088
P_COLS = 9216
IN_W = 9064

NT = (((1,), (1,)), ((), ()))
NN = (((1,), (0,)), ((), ()))


def _pack_in(w):
    z = lambda k: jnp.zeros(w.shape[:-1] + (k,), w.dtype)
    return jnp.concatenate([w[..., 0:3072], w[..., 3360:6432], w[..., 7016:9064], w[..., 6432:6944],
                            w[..., 3200:3360], z(256 - D_GATE), w[..., 3072:3200],
                            w[..., 6944:7016], z(LANES - IDX_DIM - IDX_HEADS)], axis=-1)


def _pack_rw(a):
    return _pack_in(jnp.concatenate([a, jnp.zeros(a.shape[:-1] + (IN_W - RW_IN,), a.dtype)], axis=-1))


def _unpack_rw(p):
    return jnp.concatenate([p[..., :3072], p[..., C_M:C_M + 128], p[..., C_G:C_G + D_GATE]], axis=-1)


def _split_bf16(x, n):
    parts = []
    r = x
    for _ in range(n):
        p = r.astype(BF16)
        parts.append(p)
        r = r - p.astype(F32)
    return parts


def _mm(a, b, pa=1, pb=1, dims=NN):
    aps = _split_bf16(a, pa) if a.dtype != BF16 else [a]
    bps = _split_bf16(b, pb) if b.dtype != BF16 else [b]
    order = max(len(aps), len(bps))
    out = None
    for i, ap in enumerate(aps):
        for j, bp in enumerate(bps):
            if i + j >= order:
                continue
            t = lax.dot_general(ap, bp, dims, preferred_element_type=F32)
            out = t if out is None else out + t
    return out


def _sigmoid(x):
    return 1.0 / (1.0 + jnp.exp(-x))


def _softplus(z):
    return jnp.maximum(z, 0.0) + jnp.log(1.0 + jnp.exp(-jnp.abs(z)))


def _cparams(sem):
    return pltpu.CompilerParams(dimension_semantics=sem, vmem_limit_bytes=VMEM_LIMIT)


def _ada_kernel(c_ref, w_ref, b_ref, o_ref):
    c = c_ref[...]
    s = c * _sigmoid(c)
    o_ref[...] = _mm(s, w_ref[...], 2, 2) + b_ref[...]


def _ada(c, w, b):
    m, d = c.shape
    n = w.shape[1]
    tn = 1024
    return pl.pallas_call(
        _ada_kernel,
        out_shape=jax.ShapeDtypeStruct((m, n), F32),
        grid=(n // tn,),
        in_specs=[pl.BlockSpec((m, d), lambda j: (0, 0)),
                  pl.BlockSpec((d, tn), lambda j: (0, j)),
                  pl.BlockSpec((1, tn), lambda j: (0, j))],
        out_specs=pl.BlockSpec((m, tn), lambda j: (0, j)),
        compiler_params=_cparams(("arbitrary",)),
    )(c, w, b.reshape(1, n))


def _seq_operand(vec, seq_len, tm):
    b, d = vec.shape
    if seq_len % tm == 0:
        per = seq_len // tm
        arr = vec.reshape(b, 1, d)
        spec = pl.BlockSpec((None, 1, d), lambda *g: (g[0] // per, 0, 0))
    else:
        assert tm % seq_len == 0
        arr = jnp.repeat(vec, seq_len, axis=0)
        spec = pl.BlockSpec((tm, d), lambda *g: (g[0], 0))
    return arr, spec


def _row_tile(n, seq_len, cap):
    tm = min(cap, n)
    while n % tm or (seq_len % tm and tm % seq_len):
        tm //= 2
    return tm


def _inproj_kernel(x_ref, sc_ref, sh_ref, nw_ref, w_ref, o_ref, h_scr):
    @pl.when(pl.program_id(1) == 0)
    def _():
        x = x_ref[...]
        y = x * lax.rsqrt(jnp.mean(x * x, axis=-1, keepdims=True) + EPS) * nw_ref[...]
        h_scr[...] = (y * (1.0 + sc_ref[...]) + sh_ref[...]).astype(BF16)

    o_ref[...] = jnp.dot(h_scr[...], w_ref[...], preferred_element_type=F32)


def _inproj(x2, sc, sh, nw, w16, seq_len):
    n, d = x2.shape
    tm = _row_tile(n, seq_len, 1024)
    tn = 1024
    sc_a, sc_s = _seq_operand(sc, seq_len, tm)
    sh_a, sh_s = _seq_operand(sh, seq_len, tm)
    return pl.pallas_call(
        _inproj_kernel,
        out_shape=jax.ShapeDtypeStruct((n, P_COLS), F32),
        grid=(n // tm, P_COLS // tn),
        in_specs=[pl.BlockSpec((tm, d), lambda i, j: (i, 0)), sc_s, sh_s,
                  pl.BlockSpec((1, d), lambda i, j: (0, 0)),
                  pl.BlockSpec((d, tn), lambda i, j: (0, j))],
        out_specs=pl.BlockSpec((tm, tn), lambda i, j: (i, j)),
        scratch_shapes=[pltpu.VMEM((tm, d), BF16)],
        compiler_params=_cparams(("parallel", "arbitrary")),
    )(x2, sc_a, sh_a, nw.reshape(1, d), w16)


def _lane_lo(shape):
    return lax.broadcasted_iota(I32, shape, len(shape) - 1) < HEAD_DIM


def _pair_sum(x):
    lo = _lane_lo(x.shape)
    s0 = jnp.sum(jnp.where(lo, x, 0.0), axis=-1, keepdims=True)
    s1 = jnp.sum(jnp.where(lo, 0.0, x), axis=-1, keepdims=True)
    return jnp.where(lo, s0, s1)


def _stack2(x):
    lo = _lane_lo(x.shape)
    return jnp.concatenate([jnp.where(lo, x, 0.0), jnp.where(lo, 0.0, x)], axis=0)


def _rwkv_kernel(t_real, pr, pk, pv, pg, pm, sr, sk, sv, sg, sm, mr, mk, mv, mg, mmu, w0, a0, kkw, kaw, rkw, lnw, lnb,
                 wup, aup, gup, z0, o_ref, zf_ref, r_s, lw_s, k_s, v_s, a_s, b_s, y_s):
    C = RW_CHUNK
    t_pad = r_s.shape[0]

    def mix(p_ref, s_ref, m_ref):
        p = p_ref[...]
        prev = pltpu.roll(p, 1, 0)
        row = lax.broadcasted_iota(I32, p.shape, 0)
        prev = jnp.where(row == 0, s_ref[...], prev)
        return p + (prev - p) * m_ref[...]

    xr, xk, xv = mix(pr, sr, mr), mix(pk, sk, mk), mix(pv, sv, mv)
    xg, xm = mix(pg, sg, mg), mix(pm, sm, mmu)

    dw = jnp.dot(jnp.tanh(xm).astype(BF16), wup[...], preferred_element_type=F32)
    lw = -jnp.exp(-_softplus(-(w0[...] + dw)) - 0.5)
    asig = _sigmoid(a0[...] + jnp.dot(xm.astype(BF16), aup[...], preferred_element_type=F32))
    g = jnp.dot(_sigmoid(xg).astype(BF16), gup[...], preferred_element_type=F32)
    kk = xk * kkw[...]
    kk = kk * lax.rsqrt(_pair_sum(kk * kk) + 1e-12)
    kmod = xk * (1.0 + (asig - 1.0) * kaw[...])
    bonus = _pair_sum(xr * kmod * rkw[...]) * xv

    def put(ref, val):
        if t_pad > t_real:
            val = jnp.concatenate([val, jnp.zeros((t_pad - t_real, LANES), F32)], axis=0)
        ref[...] = val

    put(r_s, xr)
    put(lw_s, lw)
    put(k_s, kmod)
    put(v_s, xv)
    put(a_s, -kk)
    put(b_s, kk * asig)

    n2 = 2 * C
    ri = lax.broadcasted_iota(I32, (n2, n2), 0)
    ci = lax.broadcasted_iota(I32, (n2, n2), 1)
    same = (ri // C) == (ci // C)
    strict = same & ((ri % C) > (ci % C))
    incl = same & ((ri % C) >= (ci % C))
    eye = ri == ci
    eye_f = jnp.where(eye, 1.0, 0.0)
    tri = jnp.where(lax.broadcasted_iota(I32, (C, C), 0) >= lax.broadcasted_iota(I32, (C, C), 1), 1.0, 0.0
                    ).astype(BF16)
    zeros_sq = jnp.zeros((n2, LANES), F32)

    def chunk(c, z):
        sl = pl.ds(pl.multiple_of(c * C, C), C)
        lwc = lw_s[sl, :]
        cum = _mm(tri, lwc, 1, 3)
        cum_last = cum[C - 1:C, :]
        ec, ecp = jnp.exp(cum), jnp.exp(cum - lwc)
        eci, ecl = jnp.exp(-cum), jnp.exp(cum_last - cum)
        a_c, b_c, k_c = a_s[sl, :], b_s[sl, :], k_s[sl, :]
        As, Rs = _stack2(a_c * ecp), _stack2(r_s[sl, :] * ec)
        Bs, Ks = _stack2(b_c * eci), _stack2(k_c * eci)
        Bt, Kt = _stack2(b_c * ecl), _stack2(k_c * ecl)
        Vs = _stack2(v_s[sl, :])

        G = _mm(jnp.concatenate([As, Rs], axis=0), jnp.concatenate([Bs, Ks], axis=0), 2, 2, NT)
        a_ab = jnp.where(strict, G[:n2, :n2], 0.0)
        a_ak = jnp.where(strict, G[:n2, n2:], 0.0)
        a_rb = jnp.where(incl, G[n2:, :n2], 0.0)
        a_rk = jnp.where(incl, G[n2:, n2:], 0.0)

        lp = a_ab
        tm_ = eye_f + a_ab
        step = 2
        while step < C:
            lp = _mm(lp, lp, 2, 2)
            tm_ = tm_ + _mm(tm_, lp, 2, 2)
            step *= 2

        w1 = _mm(a_ak, Vs, 2, 2)
        mu_ = _mm(tm_, jnp.concatenate([As, w1], axis=1), 2, 2)
        rhs = jnp.concatenate([mu_, jnp.concatenate([zeros_sq, Vs], axis=1)], axis=0)
        lhs = jnp.concatenate([jnp.concatenate([a_rb, a_rk], axis=1),
                               jnp.concatenate([Bt.T, Kt.T], axis=1)], axis=0)
        out2 = _mm(lhs, rhs, 2, 2)
        m2 = Rs + out2[:n2, :LANES]
        y_loc = out2[:n2, LANES:]
        m3 = jnp.where(eye, jnp.exp(cum_last), 0.0) + out2[n2:, :LANES]
        z_loc = out2[n2:, LANES:]
        yz = _mm(jnp.concatenate([m2, m3], axis=0), z, 2, 2)
        y = yz[:n2] + y_loc
        y_s[sl, :] = y[:C] + y[C:]
        return yz[n2:] + z_loc

    z = lax.fori_loop(0, t_pad // C, chunk, z0[...])
    zf_ref[...] = z

    y = y_s[0:t_real, :]
    mean = _pair_sum(y) * (1.0 / HEAD_DIM)
    dlt = y - mean
    var = _pair_sum(dlt * dlt) * (1.0 / HEAD_DIM)
    yn = dlt * lax.rsqrt(var + GN_EPS) * lnw[...] + lnb[...]
    o_ref[...] = ((yn + bonus) * g).astype(o_ref.dtype)


def _rwkv(P, nb, t, prev, mu, w0, a0, k_k, k_a, r_k, lnx_w, lnx_b, wup, aup, gup, z0):
    t_pad = max(t, RW_CHUNK)
    assert t % 8 == 0 and t_pad % RW_CHUNK == 0

    def cblk(c0, w, per_pair):
        return (lambda p: c0 // w + p) if per_pair else (lambda p: c0 // w)

    def pcol(c0, w, pp):
        f = cblk(c0, w, pp)
        return pl.BlockSpec((t, w), lambda b, p: (b, f(p)))

    def prevcol(c0, w, pp):
        f = cblk(c0, w, pp)
        return pl.BlockSpec((None, 1, w), lambda b, p: (b, 0, f(p)))

    def mucol(c0, w, pp):
        f = cblk(c0, w, pp)
        return pl.BlockSpec((1, w), lambda b, p: (0, f(p)))

    def hvec():
        return pl.BlockSpec((1, LANES), lambda b, p: (0, p))

    cols = [(C_R, LANES, True), (C_K, LANES, True), (C_V, LANES, True), (C_G, 256, False), (C_M, LANES, False)]
    in_specs = ([pcol(*c) for c in cols] + [prevcol(*c) for c in cols] + [mucol(*c) for c in cols]
                + [hvec() for _ in range(7)]
                + [pl.BlockSpec((LANES, LANES), lambda b, p: (0, p)),
                   pl.BlockSpec((LANES, LANES), lambda b, p: (0, p)),
                   pl.BlockSpec((256, LANES), lambda b, p: (0, p)),
                   pl.BlockSpec((None, None, LANES, LANES), lambda b, p: (b, p, 0, 0))])
    vecs = [v.reshape(1, D_MODEL) for v in (w0, a0, k_k, k_a, r_k, lnx_w, lnx_b)]
    o, zf = pl.pallas_call(
        functools.partial(_rwkv_kernel, t),
        out_shape=(jax.ShapeDtypeStruct((nb * t, D_MODEL), BF16),
                   jax.ShapeDtypeStruct((nb, N_PAIRS, LANES, LANES), F32)),
        grid=(nb, N_PAIRS),
        in_specs=in_specs,
        out_specs=(pl.BlockSpec((t, LANES), lambda b, p: (b, p)),
                   pl.BlockSpec((None, None, LANES, LANES), lambda b, p: (b, p, 0, 0))),
        scratch_shapes=[pltpu.VMEM((t_pad, LANES), F32) for _ in range(7)],
        compiler_params=_cparams(("parallel", "arbitrary")),
    )(P, P, P, P, P, prev, prev, prev, prev, prev, mu, mu, mu, mu, mu, *vecs, wup, aup, gup, z0)
    return o, zf


def _state_to_pairs(s):
    nb = s.shape[0]
    zt = jnp.swapaxes(s, -1, -2).reshape(nb, N_PAIRS, 2, HEAD_DIM, HEAD_DIM)
    zero = jnp.zeros_like(zt[:, :, 0])
    top = jnp.concatenate([zt[:, :, 0], zero], axis=-1)
    bot = jnp.concatenate([zero, zt[:, :, 1]], axis=-1)
    return jnp.concatenate([top, bot], axis=-2)


def _pairs_to_state(z):
    nb = z.shape[0]
    h0 = z[:, :, :HEAD_DIM, :HEAD_DIM]
    h1 = z[:, :, HEAD_DIM:, HEAD_DIM:]
    s = jnp.stack([h0, h1], axis=2).reshape(nb, N_HEADS, HEAD_DIM, HEAD_DIM)
    return jnp.swapaxes(s, -1, -2)


def _rope(x, cos, sin_signed):
    w = x.shape[1]
    reps = w // LANES
    cw = jnp.concatenate([cos] * reps, axis=1) if reps > 1 else cos
    sw = jnp.concatenate([sin_signed] * reps, axis=1) if reps > 1 else sin_signed
    lane = lax.broadcasted_iota(I32, x.shape, 1)
    fwd = pltpu.roll(x, w - 32, 1)
    bwd = pltpu.roll(x, 32, 1)
    partner = jnp.where((lane % HEAD_DIM) < 32, fwd, bwd)
    return x * cw + partner * sw


def _head_rms(x, nw, e_dn, e_up):
    ms = _mm(x * x, e_dn, 2, 1) * (1.0 / HEAD_DIM)
    r = lax.rsqrt(ms + EPS)
    return x * _mm(r, e_up, 2, 1) * nw


def _dsa_prep_kernel(pq, pkd, pvd, pqi, pkw, cos_ref, sin_ref, qn, kn, edn, eup,
                     q16, k32, k16, v32, v16, qi16, kw32, ki2):
    cos, sin = cos_ref[...], sin_ref[...]
    e_dn, e_up = edn[...], eup[...]
    q = _rope(_head_rms(pq[...], qn[...], e_dn, e_up), cos, sin)
    q16[...] = (q * (HEAD_DIM ** -0.5)).astype(BF16)
    k = _rope(_head_rms(pkd[...], kn[...], e_dn, e_up), cos, sin)
    k32[...] = k
    k16[...] = k.astype(BF16)
    v = pvd[...]
    v32[...] = v
    v16[...] = v.astype(BF16)
    qi16[...] = _rope(pqi[...], cos, sin).astype(BF16)
    kw = pkw[...]
    lane = lax.broadcasted_iota(I32, kw.shape, 1)
    wi_scale = (IDX_HEADS * IDX_DIM) ** -0.5
    kr = _rope(kw, cos, sin)
    kw32[...] = jnp.where(lane < IDX_DIM, kr, jnp.where(lane < IDX_DIM + IDX_HEADS, kw * wi_scale, 0.0))
    ki2[...] = jnp.where(lane < IDX_DIM, kr, pltpu.roll(kr, IDX_DIM, 1)).astype(BF16)


def _dsa_prep(P, pos_rows, q_norm_w, k_norm_w):
    n = P.shape[0]
    tm = 512 if n % 512 == 0 else n
    half = HEAD_DIM // 2
    inv = ROPE_THETA ** (-jnp.arange(half, dtype=F32) / half)
    ang = pos_rows.astype(F32)[:, None] * inv[None, :]
    cos = jnp.tile(jnp.cos(ang), (1, 4))
    sin = jnp.sin(ang)
    sin_signed = jnp.tile(jnp.concatenate([-sin, sin], axis=1), (1, 2))
    head_of = jnp.arange(D_MODEL) // HEAD_DIM
    e_dn = (head_of[:, None] == jnp.arange(LANES)[None, :]).astype(BF16)
    e_up = e_dn.T
    qn = jnp.tile(q_norm_w, N_HEADS).reshape(1, D_MODEL)
    kn = jnp.tile(k_norm_w, N_HEADS).reshape(1, D_MODEL)

    def col(c0, w):
        return pl.BlockSpec((tm, w), lambda i, c0=c0, w=w: (i, c0 // w))

    def row(w):
        return pl.BlockSpec((tm, w), lambda i: (i, 0))

    def const(shape):
        return pl.BlockSpec(shape, lambda i: (0, 0))

    return pl.pallas_call(
        _dsa_prep_kernel,
        out_shape=(jax.ShapeDtypeStruct((n, D_MODEL), BF16), jax.ShapeDtypeStruct((n, D_MODEL), F32),
                   jax.ShapeDtypeStruct((n, D_MODEL), BF16), jax.ShapeDtypeStruct((n, D_MODEL), F32),
                   jax.ShapeDtypeStruct((n, D_MODEL), BF16), jax.ShapeDtypeStruct((n, IDX_HEADS * IDX_DIM), BF16),
                   jax.ShapeDtypeStruct((n, LANES), F32), jax.ShapeDtypeStruct((n, LANES), BF16)),
        grid=(n // tm,),
        in_specs=[col(C_Q, 1024), col(C_KD, 1024), col(C_VD, 1024), col(C_QI, 512), col(C_KW, LANES),
                  row(LANES), row(LANES), const((1, D_MODEL)), const((1, D_MODEL)),
                  const((D_MODEL, LANES)), const((LANES, D_MODEL))],
        out_specs=(row(D_MODEL), row(D_MODEL), row(D_MODEL), row(D_MODEL), row(D_MODEL), row(512), row(LANES),
                   row(LANES)),
        compiler_params=_cparams(("parallel",)),
    )(P, P, P, P, P, cos, sin_signed, qn, kn, e_dn, e_up)


INT_MIN = -2 ** 31


def _sortable_key(score, admissible):
    bits = lax.bitcast_convert_type(score + 0.0, I32)
    key = jnp.where(bits < 0, bits ^ jnp.int32(0x7FFFFFFF), bits)
    return jnp.where(admissible, key, jnp.int32(INT_MIN))


def _index_scores(qi, wi, ki_list):
    outs = []
    for ki in ki_list:
        acc = None
        for h in range(IDX_HEADS):
            qpair = qi[:, (h // 2) * LANES:(h // 2 + 1) * LANES]
            lo = _lane_lo(qpair.shape)
            qh = jnp.where(lo if h % 2 == 0 else jnp.logical_not(lo), qpair, jnp.zeros_like(qpair))
            rel = lax.dot_general(qh, ki, NT, preferred_element_type=F32)
            term = wi[:, IDX_DIM + h:IDX_DIM + h + 1] * jnp.maximum(rel, 0.0)
            acc = term if acc is None else acc + term
        outs.append(acc)
    return outs


def _select_topk(keys, topk):
    tq = keys[0].shape[0]

    def count(pred_list):
        tot = None
        for p in pred_list:
            c = jnp.sum(jnp.where(p, 1.0, 0.0), axis=-1, keepdims=True)
            tot = c if tot is None else tot + c
        return tot

    def bit_step(i, c):
        trial = c + jnp.left_shift(jnp.int32(1), 31 - i)
        cnt = count([k >= trial for k in keys])
        return jnp.where(cnt >= topk, trial, c)

    thr = lax.fori_loop(0, 32, bit_step, jnp.full((tq, 1), INT_MIN, I32))
    gt = [k > thr for k in keys]
    need = topk - count(gt)
    ties = [(k == thr) & (k != jnp.int32(INT_MIN)) for k in keys]
    offs, idx = 0, []
    for k in keys:
        idx.append(lax.broadcasted_iota(I32, k.shape, 1) + offs)
        offs += k.shape[1]
    nbits = max(1, (offs - 1).bit_length() + 1)

    def idx_step(i, m):
        trial = m + jnp.left_shift(jnp.int32(1), nbits - 1 - i)
        cnt = count([t & (ix < trial) for t, ix in zip(ties, idx)])
        return jnp.where(cnt <= need, trial, m)

    cut = lax.fori_loop(0, nbits, idx_step, jnp.zeros((tq, 1), I32))
    return [g | (t & (ix < cut)) for g, t, ix in zip(gt, ties, idx)]


def _attend_pair(q_pair, k_list, v_list, sel_list):
    lo = _lane_lo(q_pair.shape)
    zero = jnp.zeros_like(q_pair)
    outs = []
    for qh in (jnp.where(lo, q_pair, zero), jnp.where(lo, zero, q_pair)):
        s_list = [jnp.where(sel, lax.dot_general(qh, k, NT, preferred_element_type=F32), -jnp.inf)
                  for k, sel in zip(k_list, sel_list)]
        m = None
        for s in s_list:
            mx = jnp.max(s, axis=-1, keepdims=True)
            m = mx if m is None else jnp.maximum(m, mx)
        den, acc = None, None
        for s, v in zip(s_list, v_list):
            p = jnp.exp(s - m)
            d = jnp.sum(p, axis=-1, keepdims=True)
            o = jnp.dot(p.astype(BF16), v, preferred_element_type=F32)
            den = d if den is None else den + d
            acc = o if acc is None else acc + o
        outs.append(acc / den)
    return jnp.where(_lane_lo(outs[0].shape), outs[0], outs[1])


def _attn_prompt_kernel(topk, q_ref, qi_ref, kw_ref, k_ref, v_ref, ki2_ref, o_ref):
    tq = q_ref.shape[0]
    L = k_ref.shape[0]
    q0 = pl.program_id(1) * tq
    score = _index_scores(qi_ref[...], kw_ref[...], [ki2_ref[...]])[0]
    qpos = q0 + lax.broadcasted_iota(I32, (tq, L), 0)
    kpos = lax.broadcasted_iota(I32, (tq, L), 1)
    adm = (qpos // CHUNK) >= (kpos // CHUNK)
    sel = _select_topk([_sortable_key(score, adm)], topk)[0]
    for p in range(N_PAIRS):
        cs = slice(p * LANES, (p + 1) * LANES)
        o_ref[:, cs] = _attend_pair(q_ref[:, cs], [k_ref[:, cs]], [v_ref[:, cs]], [sel]).astype(o_ref.dtype)


def _attn_prompt(q16, qi16, kw32, k16, v16, ki2, nb, t):
    tq = min(256, t)
    topk = min(TOPK_MAX, t // 4)
    nq = t // tq

    def qrow(w):
        return pl.BlockSpec((tq, w), lambda b, i: (b * nq + i, 0))

    def kall(w):
        return pl.BlockSpec((t, w), lambda b, i: (b, 0))

    return pl.pallas_call(
        functools.partial(_attn_prompt_kernel, topk),
        out_shape=jax.ShapeDtypeStruct((nb * t, D_MODEL), BF16),
        grid=(nb, nq),
        in_specs=[qrow(D_MODEL), qrow(512), qrow(LANES), kall(D_MODEL), kall(D_MODEL), kall(LANES)],
        out_specs=qrow(D_MODEL),
        compiler_params=_cparams(("parallel", "arbitrary")),
    )(q16, qi16, kw32, k16, v16, ki2)


def _attn_sample_kernel(topk, past, q_ref, qi_ref, kw_ref, ck_ref, cv_ref, cki2_ref, k_ref, v_ref, ki2_ref, o_ref,
                        selc_s, seln_s):
    ts = q_ref.shape[0]

    @pl.when(pl.program_id(1) == 0)
    def _():
        sc, sn = _index_scores(qi_ref[...], kw_ref[...], [cki2_ref[...], ki2_ref[...]])
        qpos = past + lax.broadcasted_iota(I32, (ts, 1), 0)
        kpos_c = lax.broadcasted_iota(I32, sc.shape, 1)
        kpos_n = past + lax.broadcasted_iota(I32, sn.shape, 1)
        keys = [_sortable_key(sc, (qpos // CHUNK) >= (kpos_c // CHUNK)),
                _sortable_key(sn, (qpos // CHUNK) >= (kpos_n // CHUNK))]
        sel_c, sel_n = _select_topk(keys, topk)
        selc_s[...] = jnp.where(sel_c, 1.0, 0.0)
        seln_s[...] = jnp.where(sel_n, 1.0, 0.0)

    sels = [selc_s[...] > 0.5, seln_s[...] > 0.5]
    o_ref[...] = _attend_pair(q_ref[...], [ck_ref[...].astype(BF16), k_ref[...]],
                              [cv_ref[...].astype(BF16), v_ref[...]], sels).astype(o_ref.dtype)


def _attn_sample(q16, qi16, kw32, k16, v16, ki2, cache_k, cache_v, cache_kidx, nb, ts):
    past = cache_k.shape[1]
    cki2 = jnp.concatenate([cache_kidx, cache_kidx], axis=-1).astype(BF16)
    topk = min(TOPK_MAX, (past + ts) // 4)

    def qrow(w, pair):
        return pl.BlockSpec((ts, w), (lambda b, p: (b, p)) if pair else (lambda b, p: (b, 0)))

    def cache(w, pair):
        return pl.BlockSpec((None, past, w), (lambda b, p: (b, 0, p)) if pair else (lambda b, p: (b, 0, 0)))

    return pl.pallas_call(
        functools.partial(_attn_sample_kernel, topk, past),
        out_shape=jax.ShapeDtypeStruct((nb * ts, D_MODEL), BF16),
        grid=(nb, N_PAIRS),
        in_specs=[qrow(LANES, True), qrow(512, False), qrow(LANES, False),
                  cache(LANES, True), cache(LANES, True), cache(LANES, False),
                  qrow(LANES, True), qrow(LANES, True), qrow(LANES, False)],
        out_specs=qrow(LANES, True),
        scratch_shapes=[pltpu.VMEM((ts, past), F32), pltpu.VMEM((ts, ts), F32)],
        compiler_params=_cparams(("parallel", "arbitrary")),
    )(q16, qi16, kw32, cache_k, cache_v, cki2, k16, v16, ki2)


def _merge_kernel(x_ref, oa_ref, ob_ref, pga_ref, pgb_ref, bga_ref, bgb_ref, g1_ref, sc2_ref, sh2_ref, nw_ref,
                  wpa_ref, wpb_ref, wout_ref, x1_ref, h2_ref):
    ga = _sigmoid(pga_ref[...] + bga_ref[...])
    gb = _sigmoid(pgb_ref[...] + bgb_ref[...])
    m = (ga * jnp.dot(oa_ref[...], wpa_ref[...], preferred_element_type=F32)
         + gb * jnp.dot(ob_ref[...], wpb_ref[...], preferred_element_type=F32))
    x1 = x_ref[...] + g1_ref[...] * jnp.dot(m.astype(BF16), wout_ref[...], preferred_element_type=F32)
    x1_ref[...] = x1
    y = x1 * lax.rsqrt(jnp.mean(x1 * x1, axis=-1, keepdims=True) + EPS) * nw_ref[...]
    h2_ref[...] = (y * (1.0 + sc2_ref[...]) + sh2_ref[...]).astype(BF16)


def _merge(x2, o_a, o_b, P, b_gate, g1, sc2, sh2, nw2, wpa, wpb, wout, seq_len):
    n, d = x2.shape
    tm = _row_tile(n, seq_len, 512)
    g1_a, g1_s = _seq_operand(g1, seq_len, tm)
    sc_a, sc_s = _seq_operand(sc2, seq_len, tm)
    sh_a, sh_s = _seq_operand(sh2, seq_len, tm)

    def row():
        return pl.BlockSpec((tm, d), lambda i: (i, 0))

    def const(shape):
        return pl.BlockSpec(shape, lambda i: (0, 0))

    bg = b_gate.reshape(1, 2 * d)
    return pl.pallas_call(
        _merge_kernel,
        out_shape=(jax.ShapeDtypeStruct((n, d), F32), jax.ShapeDtypeStruct((n, d), BF16)),
        grid=(n // tm,),
        in_specs=[row(), row(), row(),
                  pl.BlockSpec((tm, d), lambda i: (i, C_GA // d)), pl.BlockSpec((tm, d), lambda i: (i, C_GB // d)),
                  pl.BlockSpec((1, d), lambda i: (0, 0)), pl.BlockSpec((1, d), lambda i: (0, 1)),
                  g1_s, sc_s, sh_s, const((1, d)), const((d, d)), const((d, d)), const((d, d))],
        out_specs=(row(), row()),
        compiler_params=_cparams(("parallel",)),
    )(x2, o_a, o_b, P, P, bg, bg, g1_a, sc_a, sh_a, nw2.reshape(1, d), wpa, wpb, wout)


def _top_rows(s, k):
    rows = lax.broadcasted_iota(I32, s.shape, 0).astype(F32)
    cur = s
    picked = None
    vals = []
    for _ in range(k):
        m = jnp.max(cur, axis=0, keepdims=True)
        first = jnp.min(jnp.where(cur == m, rows, 1e9), axis=0, keepdims=True)
        hit = rows == first
        vals.append(m)
        picked = hit if picked is None else picked | hit
        cur = jnp.where(hit, -jnp.inf, cur)
    return vals, picked


def _peer_sel_kernel(h_ref, wpqt_ref, kbd_ref, s1_ref, g_ref, s2_ref, p2_ref, tau_ref):
    qt = lax.dot_general(wpqt_ref[...], h_ref[...], NT, preferred_element_type=F32)
    st = jnp.dot(kbd_ref[...], qt.astype(BF16), preferred_element_type=F32)
    taus = []
    for hd in range(PEER_HEADS):
        r0 = hd * 2 * PEER_NKEYS
        s1 = st[r0:r0 + PEER_NKEYS]
        s2 = st[r0 + PEER_NKEYS:r0 + 2 * PEER_NKEYS]
        c1, sel1 = _top_rows(s1, PEER_TOPK)
        c2, sel2 = _top_rows(s2, PEER_TOPK)
        c2a = jnp.concatenate(c2, axis=0)
        cand = jnp.concatenate([c + c2a for c in c1], axis=0)
        cv, csel = _top_rows(cand, PEER_TOPK)
        m = c1[0] + c2[0]
        z = jnp.sum(jnp.where(csel, jnp.exp(cand - m), 0.0), axis=0, keepdims=True)
        s1_ref[hd] = s1
        g_ref[hd] = jnp.where(sel1, jnp.exp(s1 - c1[0]) / z, 0.0)
        s2_ref[hd] = jnp.where(sel2, s2, -jnp.inf)
        p2_ref[hd] = jnp.where(sel2, jnp.exp(s2 - c2[0]), 0.0)
        taus.append(cv[PEER_TOPK - 1])
    tau_ref[...] = jnp.concatenate(taus, axis=0)


def _peer_select(h2, wpqt, kbd):
    n, d = h2.shape
    tm = 256 if n % 256 == 0 else n
    big = jax.ShapeDtypeStruct((PEER_HEADS, PEER_NKEYS, n), F32)

    def blk():
        return pl.BlockSpec((PEER_HEADS, PEER_NKEYS, tm), lambda i: (0, 0, i))

    return pl.pallas_call(
        _peer_sel_kernel,
        out_shape=(big, big, big, big, jax.ShapeDtypeStruct((PEER_HEADS, n), F32)),
        grid=(n // tm,),
        in_specs=[pl.BlockSpec((tm, d), lambda i: (i, 0)),
                  pl.BlockSpec((d, d), lambda i: (0, 0)),
                  pl.BlockSpec((2 * d, d), lambda i: (0, 0))],
        out_specs=(blk(), blk(), blk(), blk(), pl.BlockSpec((PEER_HEADS, tm), lambda i: (0, i))),
        compiler_params=_cparams(("parallel",)),
    )(h2, wpqt, kbd)


def _gelu_tanh(x):
    return 0.5 * x * (1.0 + jnp.tanh(0.7978845608028654 * (x + 0.044715 * (x * x * x))))


def _peer_main_kernel(ni1, h_ref, x1_ref, g2_ref, u_ref, vt_ref, s1_ref, g_ref, s2_ref, p2_ref, tau_ref, y_ref, acc):
    j = pl.program_id(1)

    @pl.when(j == 0)
    def _():
        acc[...] = jnp.zeros_like(acc)

    act = lax.dot_general(u_ref[...], h_ref[...], NT, preferred_element_type=F32)
    ge = _gelu_tanh(act)
    parts = []
    for l in range(ni1):
        w = None
        for hd in range(PEER_HEADS):
            x = s2_ref[hd] + s1_ref[hd, l:l + 1, :]
            t = jnp.where(x >= tau_ref[hd:hd + 1, :], p2_ref[hd] * g_ref[hd, l:l + 1, :], 0.0)
            w = t if w is None else w + t
        parts.append(w)
    gate = jnp.concatenate(parts, axis=0)
    acc[...] += jnp.dot(vt_ref[...], (gate * ge).astype(BF16), preferred_element_type=F32)

    @pl.when(j == pl.num_programs(1) - 1)
    def _():
        y_ref[...] = x1_ref[...] + g2_ref[...] * acc[...].T


def _peer_main(h2, x1, g2, u16, vt16, s1, g, s2m, p2, tau, seq_len):
    n, d = h2.shape
    tm = _row_tile(n, seq_len, 512)
    ni1 = 8
    et = ni1 * PEER_NKEYS
    g2_a, g2_s = _seq_operand(g2, seq_len, tm)

    def row():
        return pl.BlockSpec((tm, d), lambda i, j: (i, 0))

    return pl.pallas_call(
        functools.partial(_peer_main_kernel, ni1),
        out_shape=jax.ShapeDtypeStruct((n, d), F32),
        grid=(n // tm, N_EXPERTS // et),
        in_specs=[row(), row(), g2_s,
                  pl.BlockSpec((et, d), lambda i, j: (j, 0)),
                  pl.BlockSpec((d, et), lambda i, j: (0, j)),
                  pl.BlockSpec((PEER_HEADS, ni1, tm), lambda i, j: (0, j, i)),
                  pl.BlockSpec((PEER_HEADS, ni1, tm), lambda i, j: (0, j, i)),
                  pl.BlockSpec((PEER_HEADS, PEER_NKEYS, tm), lambda i, j: (0, 0, i)),
                  pl.BlockSpec((PEER_HEADS, PEER_NKEYS, tm), lambda i, j: (0, 0, i)),
                  pl.BlockSpec((PEER_HEADS, tm), lambda i, j: (0, i))],
        out_specs=row(),
        scratch_shapes=[pltpu.VMEM((d, tm), F32)],
        compiler_params=_cparams(("parallel", "arbitrary")),
    )(h2, x1, g2_a, u16, vt16, s1, g, s2m, p2, tau)


def _layer(x, mod, pos, shift_prev, s0, cache, lw):
    nb, t, d = x.shape
    n = nb * t
    sh1, sc1, g1, sh2, sc2, g2 = [mod[:, i * d:(i + 1) * d] for i in range(6)]
    x2 = x.reshape(n, d)
    P = _inproj(x2, sc1, sh1, lw['norm1_w'], lw['w_in16'], t)

    prev = _pack_rw(shift_prev).reshape(nb, 1, P_COLS)
    o_a, zf = _rwkv(P, nb, t, prev, lw['mu'], lw['w0'], lw['a0'], lw['k_k'], lw['k_a'], lw['r_k'], lw['lnx_w'],
                    lw['lnx_b'], lw['wup'], lw['aup'], lw['gup'], _state_to_pairs(s0))
    wkv = _pairs_to_state(zf)
    shift_last = _unpack_rw(P.reshape(nb, t, P_COLS)[:, -1, :])

    q16, k32, k16, v32, v16, qi16, kw32, ki2 = _dsa_prep(P, jnp.tile(pos, nb), lw['q_norm_w'], lw['k_norm_w'])
    if cache is None:
        o_b = _attn_prompt(q16, qi16, kw32, k16, v16, ki2, nb, t)
    else:
        ck, cv, cki = cache
        past = ck.shape[1]
        o_b = _attn_sample(q16, qi16, kw32, k16, v16, ki2, ck.reshape(nb, past, d), cv.reshape(nb, past, d), cki,
                           nb, t)

    x1, h2 = _merge(x2, o_a, o_b, P, lw['b_gate'], g1, sc2, sh2, lw['norm2_w'], lw['wpa'], lw['wpb'], lw['wout'], t)
    s1, g, s2m, p2, tau = _peer_select(h2, lw['wpqt'], lw['kbd'])
    y = _peer_main(h2, x1, g2, lw['u16'], lw['vt16'], s1, g, s2m, p2, tau, t)

    k_new = k32.reshape(nb, t, N_HEADS, HEAD_DIM)
    v_new = v32.reshape(nb, t, N_HEADS, HEAD_DIM)
    ki_new = kw32[:, :IDX_DIM].reshape(nb, t, IDX_DIM)
    return y.reshape(nb, t, d), wkv, shift_last, k_new, v_new, ki_new


def _layer_weights(l, w_in, b_gate, mu_rw, w0, w_up, a0, a_up, g_up, k_k, k_a, r_k, lnx_w, lnx_b, q_norm_w, k_norm_w,
                   w_proj_a, w_proj_b, w_out, norm1_w, norm2_w, w_pq, peer_keys, peer_u, peer_v):
    d = D_MODEL
    zeros = lambda r: jnp.zeros((r, d), F32)
    keys = peer_keys[l].reshape(2 * PEER_HEADS, PEER_NKEYS, PEER_DHALF)
    eye = jnp.eye(2 * PEER_HEADS, dtype=F32)
    kbd = (eye[:, None, :, None] * keys[:, :, None, :]).reshape(2 * d, d)
    return {
        'w_in16': _pack_in(w_in[l]).astype(BF16), 'b_gate': b_gate[l], 'mu': _pack_rw(mu_rw[l]).reshape(1, P_COLS),
        'w0': w0[l], 'a0': a0[l], 'k_k': k_k[l], 'k_a': k_a[l], 'r_k': r_k[l].reshape(d), 'lnx_w': lnx_w[l],
        'lnx_b': lnx_b[l],
        'wup': jnp.concatenate([w_up[l], zeros(LANES - D_DECAY)], axis=0).astype(BF16),
        'aup': jnp.concatenate([zeros(D_DECAY), a_up[l]], axis=0).astype(BF16),
        'gup': jnp.concatenate([g_up[l], zeros(256 - D_GATE)], axis=0).astype(BF16),
        'q_norm_w': q_norm_w[l], 'k_norm_w': k_norm_w[l], 'norm1_w': norm1_w[l], 'norm2_w': norm2_w[l],
        'wpa': w_proj_a[l].astype(BF16), 'wpb': w_proj_b[l].astype(BF16), 'wout': w_out[l].astype(BF16),
        'wpqt': w_pq[l].T.astype(BF16), 'kbd': kbd.astype(BF16),
        'u16': peer_u[l].astype(BF16), 'vt16': peer_v[l].T.astype(BF16),
    }


def kernel(x_prompt, x_sample, c_prompt, c_sample, cache_k, cache_v, cache_kidx, state_wkv, state_shift, w_ada, b_ada,
           norm1_w, w_in, b_gate, mu_rw, w0, w_up, a0, a_up, g_up, k_k, k_a, r_k, lnx_w, lnx_b, q_norm_w, k_norm_w,
           w_proj_a, w_proj_b, w_out, norm2_w, w_pq, peer_keys, peer_u, peer_v):
    depth = w_in.shape[0]
    bp, tp = x_prompt.shape[:2]
    bs, ts = x_sample.shape[:2]
    past = cache_k.shape[2]
    dt = x_prompt.dtype
    pos_p = jnp.arange(tp, dtype=I32)
    pos_s = past + jnp.arange(ts, dtype=I32)
    zero_shift = jnp.zeros((bp, RW_IN), dt)
    zero_wkv = jnp.zeros((bp, N_HEADS, HEAD_DIM, HEAD_DIM), dt)
    c_all = jnp.concatenate([c_prompt, c_sample], axis=0)
    xp, xs = x_prompt, x_sample
    outs_p, outs_s = [], []
    for l in range(depth):
        lw = _layer_weights(l, w_in, b_gate, mu_rw, w0, w_up, a0, a_up, g_up, k_k, k_a, r_k, lnx_w, lnx_b, q_norm_w,
                            k_norm_w, w_proj_a, w_proj_b, w_out, norm1_w, norm2_w, w_pq, peer_keys, peer_u, peer_v)
        mod = _ada(c_all, w_ada[l], b_ada[l])
        xp, *rest_p = _layer(xp, mod[:bp], pos_p, zero_shift, zero_wkv, None, lw)
        xs, *rest_s = _layer(xs, mod[bp:], pos_s, state_shift[l], state_wkv[l],
                             (cache_k[l], cache_v[l], cache_kidx[l]), lw)
        outs_p.append(rest_p)
        outs_s.append(rest_s)
    stack = lambda outs, i: jnp.stack([o[i] for o in outs])
    return (xp, xs,
            stack(outs_p, 0), stack(outs_p, 1), stack(outs_p, 2), stack(outs_p, 3), stack(outs_p, 4),
            stack(outs_s, 0), stack(outs_s, 1), stack(outs_s, 2), stack(outs_s, 3), stack(outs_s, 4))
```

```python
import functools

import jax
import jax.numpy as jnp
from jax import lax
from jax.experimental import pallas as pl
from jax.experimental.pallas import tpu as pltpu

F32 = jnp.float32
BF16 = jnp.bfloat16
I32 = jnp.int32

LANES = 128
D_MODEL = 1024
EPS = 1e-6
GN_EPS = 64e-5
ROPE_THETA = 10000.0
CHUNK = 64
TOPK_MAX = 256
HEAD_DIM = 64
N_HEADS = D_MODEL // HEAD_DIM
N_PAIRS = N_HEADS // 2
IDX_HEADS = 8
IDX_DIM = 64
D_DECAY = 64
D_AAA = 64
D_GATE = 160
RW_IN = 3 * D_MODEL + D_DECAY + D_AAA + D_GATE
PEER_HEADS = 8
PEER_NKEYS = 128
PEER_TOPK = 16
PEER_DHALF = 64
N_EXPERTS = PEER_NKEYS * PEER_NKEYS
RW_CHUNK = 64
RW_INTERLEAVE = 8
RW_PASSES = (2, 1, 1, 1)
VMEM_LIMIT = 56 * 1024 * 1024
LOG2E = 1.4426950408889634
GATE_ROWS = 32

C_R, C_K, C_V = 0, 1024, 2048
C_Q, C_KD, C_VD = 3072, 4096, 5120
C_GA, C_GB = 6144, 7168
C_QI = 8192
C_G = 8704
C_M = 8960
C_KW = 9088
P_COLS = 9216
IN_W = 9064

NT = (((1,), (1,)), ((), ()))
NN = (((1,), (0,)), ((), ()))


def _pack_in(w):
    z = lambda k: jnp.zeros(w.shape[:-1] + (k,), w.dtype)
    return jnp.concatenate([w[..., 0:3072], w[..., 3360:6432], w[..., 7016:9064], w[..., 6432:6944],
                            w[..., 3200:3360], z(256 - D_GATE), w[..., 3072:3200],
                            w[..., 6944:7016], z(LANES - IDX_DIM - IDX_HEADS)], axis=-1)


def _pack_rw(a):
    return _pack_in(jnp.concatenate([a, jnp.zeros(a.shape[:-1] + (IN_W - RW_IN,), a.dtype)], axis=-1))


def _unpack_rw(p):
    return jnp.concatenate([p[..., :3072], p[..., C_M:C_M + 128], p[..., C_G:C_G + D_GATE]], axis=-1)


def _split_bf16(x, n):
    parts = []
    r = x
    for _ in range(n):
        p = r.astype(BF16)
        parts.append(p)
        r = r - p.astype(F32)
    return parts


def _mm(a, b, pa=1, pb=1, dims=NN):
    aps = _split_bf16(a, pa) if a.dtype != BF16 else [a]
    bps = _split_bf16(b, pb) if b.dtype != BF16 else [b]
    order = max(len(aps), len(bps))
    out = None
    for i, ap in enumerate(aps):
        for j, bp in enumerate(bps):
            if i + j >= order:
                continue
            t = lax.dot_general(ap, bp, dims, preferred_element_type=F32)
            out = t if out is None else out + t
    return out


def _sigmoid(x):
    return 1.0 / (1.0 + jnp.exp(-x))


def _softplus(z):
    return jnp.maximum(z, 0.0) + jnp.log(1.0 + jnp.exp(-jnp.abs(z)))


def _cparams(sem):
    return pltpu.CompilerParams(dimension_semantics=sem, vmem_limit_bytes=VMEM_LIMIT)


def _ada_kernel(c_ref, w_ref, b_ref, o_ref):
    c = c_ref[...]
    s = c * _sigmoid(c)
    o_ref[...] = _mm(s, w_ref[...], 2, 2) + b_ref[...]


def _ada(c, w, b):
    m, d = c.shape
    n = w.shape[1]
    tn = 1024
    return pl.pallas_call(
        _ada_kernel,
        out_shape=jax.ShapeDtypeStruct((m, n), F32),
        grid=(n // tn,),
        in_specs=[pl.BlockSpec((m, d), lambda j: (0, 0)),
                  pl.BlockSpec((d, tn), lambda j: (0, j)),
                  pl.BlockSpec((1, tn), lambda j: (0, j))],
        out_specs=pl.BlockSpec((m, tn), lambda j: (0, j)),
        compiler_params=_cparams(("arbitrary",)),
    )(c, w, b.reshape(1, n))


def _seq_operand(vec, seq_len, tm):
    b, d = vec.shape
    if seq_len % tm == 0:
        per = seq_len // tm
        arr = vec.reshape(b, 1, d)
        spec = pl.BlockSpec((None, 1, d), lambda *g: (g[0] // per, 0, 0))
    else:
        assert tm % seq_len == 0
        arr = jnp.repeat(vec, seq_len, axis=0)
        spec = pl.BlockSpec((tm, d), lambda *g: (g[0], 0))
    return arr, spec


def _row_tile(n, seq_len, cap):
    tm = min(cap, n)
    while n % tm or (seq_len % tm and tm % seq_len):
        tm //= 2
    return tm


def _inproj_kernel(x_ref, sc_ref, sh_ref, nw_ref, w_ref, o_ref, h_scr):
    @pl.when(pl.program_id(1) == 0)
    def _():
        x = x_ref[...]
        y = x * lax.rsqrt(jnp.mean(x * x, axis=-1, keepdims=True) + EPS) * nw_ref[...]
        h_scr[...] = (y * (1.0 + sc_ref[...]) + sh_ref[...]).astype(BF16)

    o_ref[...] = jnp.dot(h_scr[...], w_ref[...], preferred_element_type=F32)


def _inproj(x2, sc, sh, nw, w16, seq_len):
    n, d = x2.shape
    tm = _row_tile(n, seq_len, 1024)
    tn = 1024
    sc_a, sc_s = _seq_operand(sc, seq_len, tm)
    sh_a, sh_s = _seq_operand(sh, seq_len, tm)
    return pl.pallas_call(
        _inproj_kernel,
        out_shape=jax.ShapeDtypeStruct((n, P_COLS), F32),
        grid=(n // tm, P_COLS // tn),
        in_specs=[pl.BlockSpec((tm, d), lambda i, j: (i, 0)), sc_s, sh_s,
                  pl.BlockSpec((1, d), lambda i, j: (0, 0)),
                  pl.BlockSpec((d, tn), lambda i, j: (0, j))],
        out_specs=pl.BlockSpec((tm, tn), lambda i, j: (i, j)),
        scratch_shapes=[pltpu.VMEM((tm, d), BF16)],
        compiler_params=_cparams(("parallel", "arbitrary")),
    )(x2, sc_a, sh_a, nw.reshape(1, d), w16)


def _lane_lo(shape):
    return lax.broadcasted_iota(I32, shape, len(shape) - 1) < HEAD_DIM


def _pair_sum(x):
    lo = _lane_lo(x.shape)
    s0 = jnp.sum(jnp.where(lo, x, 0.0), axis=-1, keepdims=True)
    s1 = jnp.sum(jnp.where(lo, 0.0, x), axis=-1, keepdims=True)
    return jnp.where(lo, s0, s1)


def _stack2(x):
    lo = _lane_lo(x.shape)
    return jnp.concatenate([jnp.where(lo, x, 0.0), jnp.where(lo, 0.0, x)], axis=0)


def _rwkv_kernel(t_real, npair, nchunk, pr, pk, pv, pg, pm, sr, sk, sv, sg, sm, mr, mk, mv, mg, mmu,
                 w0, a0, kkw, kaw, rkw, lnw, lnb, wup, aup, gup, z0, o_ref, zf_ref,
                 r_s, lw_s, k_s, v_s, a_s, b_s, y_s, bonus_s, g_s):
    C = RW_CHUNK
    t_pad = r_s.shape[1]

    def mix(p_ref, s_ref, m_ref):
        p = p_ref[...]
        prev = pltpu.roll(p, 1, 0)
        row = lax.broadcasted_iota(I32, p.shape, 0)
        prev = jnp.where(row == 0, s_ref[...], prev)
        return p + (prev - p) * m_ref[...]

    xg, xm = mix(pg, sg, mg), mix(pm, sm, mmu)
    th16, xm16, sg16 = jnp.tanh(xm).astype(BF16), xm.astype(BF16), _sigmoid(xg).astype(BF16)
    xr_all, xk_all, xv_all = mix(pr, sr, mr), mix(pk, sk, mk), mix(pv, sv, mv)

    def put(ref, pp, val):
        if t_pad > t_real:
            val = jnp.concatenate([val, jnp.zeros((t_pad - t_real, LANES), F32)], axis=0)
        ref[pp] = val

    for pp in range(npair):
        cs = slice(pp * LANES, (pp + 1) * LANES)
        xr, xk, xv = xr_all[:, cs], xk_all[:, cs], xv_all[:, cs]
        dw = jnp.dot(th16, wup[:, cs], preferred_element_type=F32)
        lw = -jnp.exp(-_softplus(-(w0[:, cs] + dw)) - 0.5)
        asig = _sigmoid(a0[:, cs] + jnp.dot(xm16, aup[:, cs], preferred_element_type=F32))
        g_s[pp] = jnp.dot(sg16, gup[:, cs], preferred_element_type=F32)
        kk = xk * kkw[:, cs]
        kk = kk * lax.rsqrt(_pair_sum(kk * kk) + 1e-12)
        kmod = xk * (1.0 + (asig - 1.0) * kaw[:, cs])
        bonus_s[pp] = _pair_sum(xr * kmod * rkw[:, cs]) * xv
        put(r_s, pp, xr)
        put(lw_s, pp, lw)
        put(k_s, pp, kmod)
        put(v_s, pp, xv)
        put(a_s, pp, -kk)
        put(b_s, pp, kk * asig)

    n2 = 2 * C
    ri = lax.broadcasted_iota(I32, (n2, n2), 0)
    ci = lax.broadcasted_iota(I32, (n2, n2), 1)
    same = (ri // C) == (ci // C)
    strict = same & ((ri % C) > (ci % C))
    incl = same & ((ri % C) >= (ci % C))
    eye = ri == ci
    eye_f = jnp.where(eye, 1.0, 0.0)
    tri = jnp.where(lax.broadcasted_iota(I32, (C, C), 0) >= lax.broadcasted_iota(I32, (C, C), 1), 1.0, 0.0
                    ).astype(BF16)
    zeros_sq = jnp.zeros((n2, LANES), F32)

    pg_, pi_, po_, ps_ = RW_PASSES

    def local(chains):
        each = lambda f, *cols: [f(*xs) for xs in zip(*cols)]
        lwc = [lw_s[pp, sl, :] for sl, pp in chains]
        cum = each(lambda l: _mm(tri, l, 1, 3), lwc)
        cum_last = each(lambda c: c[C - 1:C, :], cum)
        ec, eci = each(jnp.exp, cum), each(lambda c: jnp.exp(-c), cum)
        ecp = each(lambda c, l: jnp.exp(c - l), cum, lwc)
        ecl = each(lambda c, cl: jnp.exp(cl - c), cum, cum_last)
        a_c = [a_s[pp, sl, :] for sl, pp in chains]
        b_c = [b_s[pp, sl, :] for sl, pp in chains]
        k_c = [k_s[pp, sl, :] for sl, pp in chains]
        r_c = [r_s[pp, sl, :] for sl, pp in chains]
        As = each(lambda a, e: _stack2(a * e), a_c, ecp)
        Rs = each(lambda r, e: _stack2(r * e), r_c, ec)
        Bs = each(lambda b, e: _stack2(b * e), b_c, eci)
        Ks = each(lambda k, e: _stack2(k * e), k_c, eci)
        Bt = each(lambda b, e: _stack2(b * e), b_c, ecl)
        Kt = each(lambda k, e: _stack2(k * e), k_c, ecl)
        Vs = [_stack2(v_s[pp, sl, :]) for sl, pp in chains]

        G = each(lambda a, r, b, k: _mm(jnp.concatenate([a, r], axis=0), jnp.concatenate([b, k], axis=0),
                                        pg_, pg_, NT), As, Rs, Bs, Ks)
        a_ab = each(lambda g: jnp.where(strict, g[:n2, :n2], 0.0), G)
        a_ak = each(lambda g: jnp.where(strict, g[:n2, n2:], 0.0), G)
        a_rb = each(lambda g: jnp.where(incl, g[n2:, :n2], 0.0), G)
        a_rk = each(lambda g: jnp.where(incl, g[n2:, n2:], 0.0), G)

        lp = a_ab
        tm_ = each(lambda a: eye_f + a, a_ab)
        step = 2
        while step < C:
            lp = each(lambda l: _mm(l, l, pi_, pi_), lp)
            tm_ = each(lambda t, l: t + _mm(t, l, pi_, pi_), tm_, lp)
            step *= 2

        w1 = each(lambda a, v: _mm(a, v, po_, po_), a_ak, Vs)
        mu_ = each(lambda t, a, w: _mm(t, jnp.concatenate([a, w], axis=1), po_, po_), tm_, As, w1)
        rhs = each(lambda m, v: jnp.concatenate([m, jnp.concatenate([zeros_sq, v], axis=1)], axis=0), mu_, Vs)
        lhs = each(lambda rb, rk, b, k: jnp.concatenate([jnp.concatenate([rb, rk], axis=1),
                                                         jnp.concatenate([b.T, k.T], axis=1)], axis=0),
                   a_rb, a_rk, Bt, Kt)
        out2 = each(lambda l, r: _mm(l, r, po_, po_), lhs, rhs)
        m23 = each(lambda r, o, cl: jnp.concatenate([r + o[:n2, :LANES],
                                                     jnp.where(eye, jnp.exp(cl), 0.0) + o[n2:, :LANES]], axis=0),
                   Rs, out2, cum_last)
        return [(m, o[:n2, LANES:], o[n2:, LANES:]) for m, o in zip(m23, out2)]

    def step_chunks(i, zs):
        sls = [pl.ds(pl.multiple_of((i * nchunk + j) * C, C), C) for j in range(nchunk)]
        parts = local([(sl, pp) for sl in sls for pp in range(npair)])
        zs = list(zs)
        for j, sl in enumerate(sls):
            for pp in range(npair):
                m23, y_loc, z_loc = parts[j * npair + pp]
                yz = _mm(m23, zs[pp], ps_, ps_)
                y = yz[:n2] + y_loc
                y_s[pp, sl, :] = y[:C] + y[C:]
                zs[pp] = yz[n2:] + z_loc
        return tuple(zs)

    zs = lax.fori_loop(0, t_pad // (C * nchunk), step_chunks, tuple(z0[pp] for pp in range(npair)))
    for pp in range(npair):
        zf_ref[pp] = zs[pp]
        cs = slice(pp * LANES, (pp + 1) * LANES)
        y = y_s[pp, 0:t_real, :]
        mean = _pair_sum(y) * (1.0 / HEAD_DIM)
        dlt = y - mean
        var = _pair_sum(dlt * dlt) * (1.0 / HEAD_DIM)
        yn = dlt * lax.rsqrt(var + GN_EPS) * lnw[:, cs] + lnb[:, cs]
        o_ref[:, cs] = ((yn + bonus_s[pp]) * g_s[pp]).astype(o_ref.dtype)


def _rwkv(P, nb, t, prev, mu, w0, a0, k_k, k_a, r_k, lnx_w, lnx_b, wup, aup, gup, z0):
    t_pad = max(t, RW_CHUNK)
    assert t % 8 == 0 and t_pad % RW_CHUNK == 0
    n_chunks = t_pad // RW_CHUNK
    nchunk = min(RW_INTERLEAVE, n_chunks)
    npair = min(N_PAIRS, max(1, RW_INTERLEAVE // nchunk))
    wp = npair * LANES

    def cblk(c0, w, per_pair):
        return (lambda p: c0 // w + p) if per_pair else (lambda p: c0 // w)

    def pcol(c0, w, pp):
        f = cblk(c0, w, pp)
        return pl.BlockSpec((t, w), lambda b, p: (b, f(p)))

    def prevcol(c0, w, pp):
        f = cblk(c0, w, pp)
        return pl.BlockSpec((None, 1, w), lambda b, p: (b, 0, f(p)))

    def mucol(c0, w, pp):
        f = cblk(c0, w, pp)
        return pl.BlockSpec((1, w), lambda b, p: (0, f(p)))

    def hvec():
        return pl.BlockSpec((1, wp), lambda b, p: (0, p))

    cols = [(C_R, wp, True), (C_K, wp, True), (C_V, wp, True), (C_G, 256, False), (C_M, LANES, False)]
    in_specs = ([pcol(*c) for c in cols] + [prevcol(*c) for c in cols] + [mucol(*c) for c in cols]
                + [hvec() for _ in range(7)]
                + [pl.BlockSpec((LANES, wp), lambda b, p: (0, p)),
                   pl.BlockSpec((LANES, wp), lambda b, p: (0, p)),
                   pl.BlockSpec((256, wp), lambda b, p: (0, p)),
                   pl.BlockSpec((None, npair, LANES, LANES), lambda b, p: (b, p, 0, 0))])
    vecs = [v.reshape(1, D_MODEL) for v in (w0, a0, k_k, k_a, r_k, lnx_w, lnx_b)]
    o, zf = pl.pallas_call(
        functools.partial(_rwkv_kernel, t, npair, nchunk),
        out_shape=(jax.ShapeDtypeStruct((nb * t, D_MODEL), BF16),
                   jax.ShapeDtypeStruct((nb, N_PAIRS, LANES, LANES), F32)),
        grid=(nb, N_PAIRS // npair),
        in_specs=in_specs,
        out_specs=(pl.BlockSpec((t, wp), lambda b, p: (b, p)),
                   pl.BlockSpec((None, npair, LANES, LANES), lambda b, p: (b, p, 0, 0))),
        scratch_shapes=([pltpu.VMEM((npair, t_pad, LANES), F32) for _ in range(7)]
                        + [pltpu.VMEM((npair, t, LANES), F32) for _ in range(2)]),
        compiler_params=_cparams(("parallel", "arbitrary")),
    )(P, P, P, P, P, prev, prev, prev, prev, prev, mu, mu, mu, mu, mu, *vecs, wup, aup, gup, z0)
    return o, zf


def _state_to_pairs(s):
    nb = s.shape[0]
    zt = jnp.swapaxes(s, -1, -2).reshape(nb, N_PAIRS, 2, HEAD_DIM, HEAD_DIM)
    zero = jnp.zeros_like(zt[:, :, 0])
    top = jnp.concatenate([zt[:, :, 0], zero], axis=-1)
    bot = jnp.concatenate([zero, zt[:, :, 1]], axis=-1)
    return jnp.concatenate([top, bot], axis=-2)


def _pairs_to_state(z):
    nb = z.shape[0]
    h0 = z[:, :, :HEAD_DIM, :HEAD_DIM]
    h1 = z[:, :, HEAD_DIM:, HEAD_DIM:]
    s = jnp.stack([h0, h1], axis=2).reshape(nb, N_HEADS, HEAD_DIM, HEAD_DIM)
    return jnp.swapaxes(s, -1, -2)


def _rope(x, cos, sin_signed):
    w = x.shape[1]
    reps = w // LANES
    cw = jnp.concatenate([cos] * reps, axis=1) if reps > 1 else cos
    sw = jnp.concatenate([sin_signed] * reps, axis=1) if reps > 1 else sin_signed
    lane = lax.broadcasted_iota(I32, x.shape, 1)
    fwd = pltpu.roll(x, w - 32, 1)
    bwd = pltpu.roll(x, 32, 1)
    partner = jnp.where((lane % HEAD_DIM) < 32, fwd, bwd)
    return x * cw + partner * sw


def _head_rms(x, nw, e_dn, e_up):
    ms = _mm(x * x, e_dn, 2, 1) * (1.0 / HEAD_DIM)
    r = lax.rsqrt(ms + EPS)
    return x * _mm(r, e_up, 2, 1) * nw


def _dsa_prep_kernel(pq, pkd, pvd, pqi, pkw, cos_ref, sin_ref, qn, kn, edn, eup,
                     q16, k32, k16, v32, v16, qi16, kw32, ki2):
    cos, sin = cos_ref[...], sin_ref[...]
    e_dn, e_up = edn[...], eup[...]
    def put_pairs(ref, x):
        for p in range(N_PAIRS):
            ref[p] = x[:, p * LANES:(p + 1) * LANES].astype(ref.dtype)

    q = _rope(_head_rms(pq[...], qn[...], e_dn, e_up), cos, sin)
    put_pairs(q16, q * (HEAD_DIM ** -0.5 * LOG2E))
    k = _rope(_head_rms(pkd[...], kn[...], e_dn, e_up), cos, sin)
    k32[...] = k
    put_pairs(k16, k)
    v = pvd[...]
    v32[...] = v
    put_pairs(v16, v)
    qi16[...] = _rope(pqi[...], cos, sin).astype(BF16)
    kw = pkw[...]
    lane = lax.broadcasted_iota(I32, kw.shape, 1)
    wi_scale = (IDX_HEADS * IDX_DIM) ** -0.5
    kr = _rope(kw, cos, sin)
    kw32[...] = jnp.where(lane < IDX_DIM, kr, jnp.where(lane < IDX_DIM + IDX_HEADS, kw * wi_scale, 0.0))
    ki2[...] = jnp.where(lane < IDX_DIM, kr, pltpu.roll(kr, IDX_DIM, 1)).astype(BF16)


def _dsa_prep(P, pos_rows, q_norm_w, k_norm_w):
    n = P.shape[0]
    tm = 512 if n % 512 == 0 else n
    half = HEAD_DIM // 2
    inv = ROPE_THETA ** (-jnp.arange(half, dtype=F32) / half)
    ang = pos_rows.astype(F32)[:, None] * inv[None, :]
    cos = jnp.tile(jnp.cos(ang), (1, 4))
    sin = jnp.sin(ang)
    sin_signed = jnp.tile(jnp.concatenate([-sin, sin], axis=1), (1, 2))
    head_of = jnp.arange(D_MODEL) // HEAD_DIM
    e_dn = (head_of[:, None] == jnp.arange(LANES)[None, :]).astype(BF16)
    e_up = e_dn.T
    qn = jnp.tile(q_norm_w, N_HEADS).reshape(1, D_MODEL)
    kn = jnp.tile(k_norm_w, N_HEADS).reshape(1, D_MODEL)

    def col(c0, w):
        return pl.BlockSpec((tm, w), lambda i, c0=c0, w=w: (i, c0 // w))

    def row(w):
        return pl.BlockSpec((tm, w), lambda i: (i, 0))

    def const(shape):
        return pl.BlockSpec(shape, lambda i: (0, 0))

    pairs = jax.ShapeDtypeStruct((N_PAIRS, n, LANES), BF16)
    pair_spec = pl.BlockSpec((N_PAIRS, tm, LANES), lambda i: (0, i, 0))
    return pl.pallas_call(
        _dsa_prep_kernel,
        out_shape=(pairs, jax.ShapeDtypeStruct((n, D_MODEL), F32), pairs, jax.ShapeDtypeStruct((n, D_MODEL), F32),
                   pairs, jax.ShapeDtypeStruct((n, IDX_HEADS * IDX_DIM), BF16),
                   jax.ShapeDtypeStruct((n, LANES), F32), jax.ShapeDtypeStruct((n, LANES), BF16)),
        grid=(n // tm,),
        in_specs=[col(C_Q, 1024), col(C_KD, 1024), col(C_VD, 1024), col(C_QI, 512), col(C_KW, LANES),
                  row(LANES), row(LANES), const((1, D_MODEL)), const((1, D_MODEL)),
                  const((D_MODEL, LANES)), const((LANES, D_MODEL))],
        out_specs=(pair_spec, row(D_MODEL), pair_spec, row(D_MODEL), pair_spec, row(512), row(LANES), row(LANES)),
        compiler_params=_cparams(("parallel",)),
    )(P, P, P, P, P, cos, sin_signed, qn, kn, e_dn, e_up)


INT_MIN = -2 ** 31


def _sortable_key(score, admissible):
    bits = lax.bitcast_convert_type(score + 0.0, I32)
    key = jnp.where(bits < 0, bits ^ jnp.int32(0x7FFFFFFF), bits)
    return jnp.where(admissible, key, jnp.int32(INT_MIN))


def _index_scores(qi, wi, ki_list):
    outs = []
    for ki in ki_list:
        acc = None
        for h in range(IDX_HEADS):
            qpair = qi[:, (h // 2) * LANES:(h // 2 + 1) * LANES]
            lo = _lane_lo(qpair.shape)
            qh = jnp.where(lo if h % 2 == 0 else jnp.logical_not(lo), qpair, jnp.zeros_like(qpair))
            rel = lax.dot_general(qh, ki, NT, preferred_element_type=F32)
            term = wi[:, IDX_DIM + h:IDX_DIM + h + 1] * jnp.maximum(rel, 0.0)
            acc = term if acc is None else acc + term
        outs.append(acc)
    return outs


def _select_topk(keys, topk, bias_refs):
    tq = keys[0].shape[0]
    int_min = jnp.int32(INT_MIN)

    def write(masks):
        for ref, k, msk in zip(bias_refs, keys, masks):
            ref[:, 0:k.shape[1]] = jnp.where(msk, 0.0, -jnp.inf)

    def count(pred_list):
        tot = None
        for p in pred_list:
            c = jnp.sum(jnp.where(p, 1.0, 0.0), axis=-1, keepdims=True)
            tot = c if tot is None else tot + c
        return tot

    def bit_step(i, c):
        trial = c + jnp.left_shift(jnp.int32(1), 31 - i)
        cnt = count([k >= trial for k in keys])
        return jnp.where(cnt >= topk, trial, c)

    thr = lax.fori_loop(0, 32, bit_step, jnp.full((tq, 1), INT_MIN, I32))
    ge = [(k >= thr) & (k != int_min) for k in keys]
    write(ge)
    surplus = jnp.max(count(ge)) > topk

    @pl.when(surplus)
    def _():
        gt = [k > thr for k in keys]
        need = topk - count(gt)
        ties = [(k == thr) & (k != int_min) for k in keys]
        offs, idx = 0, []
        for k in keys:
            idx.append(lax.broadcasted_iota(I32, k.shape, 1) + offs)
            offs += k.shape[1]
        nbits = max(1, (offs - 1).bit_length() + 1)

        def idx_step(i, m):
            trial = m + jnp.left_shift(jnp.int32(1), nbits - 1 - i)
            cnt = count([t & (ix < trial) for t, ix in zip(ties, idx)])
            return jnp.where(cnt <= need, trial, m)

        cut = lax.fori_loop(0, nbits, idx_step, jnp.zeros((tq, 1), I32))
        write([g | (t & (ix < cut)) for g, t, ix in zip(gt, ties, idx)])


def _attend_pair(q_pair, k_list, v_list, bias_list):
    lo = _lane_lo(q_pair.shape)
    zero = jnp.zeros_like(q_pair)
    outs = []
    for qh in (jnp.where(lo, q_pair, zero), jnp.where(lo, zero, q_pair)):
        s_list = [lax.dot_general(qh, k, NT, preferred_element_type=F32) + b for k, b in zip(k_list, bias_list)]
        m = None
        for s in s_list:
            mx = jnp.max(s, axis=-1, keepdims=True)
            m = mx if m is None else jnp.maximum(m, mx)
        den, acc = None, None
        for s, v in zip(s_list, v_list):
            p = jnp.exp2(s - m)
            d = jnp.sum(p, axis=-1, keepdims=True)
            o = jnp.dot(p.astype(BF16), v, preferred_element_type=F32)
            den = d if den is None else den + d
            acc = o if acc is None else acc + o
        outs.append(acc / den)
    return jnp.where(_lane_lo(outs[0].shape), outs[0], outs[1])


def _attn_prompt_kernel(topk, ncase, q_ref, qi_ref, kw_ref, k_ref, v_ref, ki2_ref, o_ref, bias_s):
    tq = q_ref.shape[1]
    t = k_ref.shape[1]
    i = pl.program_id(1)
    lstep = t // ncase
    case = ((i + 1) * tq - 1) // lstep

    def run(L):
        score = _index_scores(qi_ref[...], kw_ref[...], [ki2_ref[0:L, :]])[0]
        qpos = i * tq + lax.broadcasted_iota(I32, (tq, L), 0)
        kpos = lax.broadcasted_iota(I32, (tq, L), 1)
        adm = (qpos // CHUNK) >= (kpos // CHUNK)
        _select_topk([_sortable_key(score, adm)], topk, [bias_s])

        def pair(p, carry):
            o = _attend_pair(q_ref[p], [k_ref[p, 0:L, :]], [v_ref[p, 0:L, :]], [bias_s[:, 0:L]])
            o_ref[p] = o.astype(o_ref.dtype)
            return carry

        lax.fori_loop(0, N_PAIRS, pair, 0)

    for c in range(ncase):
        pl.when(case == c)(functools.partial(run, (c + 1) * lstep))


def _attn_prompt(q16, qi16, kw32, k16, v16, ki2, nb, t):
    tq = min(256, t)
    topk = min(TOPK_MAX, t // 4)
    nq = t // tq
    ncase = min(4, nq)

    def qrow(w):
        return pl.BlockSpec((tq, w), lambda b, i: (b * nq + i, 0))

    def qpairs():
        return pl.BlockSpec((N_PAIRS, tq, LANES), lambda b, i: (0, b * nq + i, 0))

    def kpairs():
        return pl.BlockSpec((N_PAIRS, t, LANES), lambda b, i: (0, b, 0))

    return pl.pallas_call(
        functools.partial(_attn_prompt_kernel, topk, ncase),
        out_shape=jax.ShapeDtypeStruct((N_PAIRS, nb * t, LANES), BF16),
        grid=(nb, nq),
        in_specs=[qpairs(), qrow(512), qrow(LANES), kpairs(), kpairs(),
                  pl.BlockSpec((t, LANES), lambda b, i: (b, 0))],
        out_specs=qpairs(),
        scratch_shapes=[pltpu.VMEM((tq, t), F32)],
        compiler_params=_cparams(("parallel", "arbitrary")),
    )(q16, qi16, kw32, k16, v16, ki2)


def _attn_sample_kernel(topk, past, q_ref, qi_ref, kw_ref, ck_ref, cv_ref, cki2_ref, k_ref, v_ref, ki2_ref, o_ref,
                        biasc_s, biasn_s):
    ts = q_ref.shape[0]

    @pl.when(pl.program_id(1) == 0)
    def _():
        sc, sn = _index_scores(qi_ref[...], kw_ref[...], [cki2_ref[...], ki2_ref[...]])
        qpos = past + lax.broadcasted_iota(I32, (ts, 1), 0)
        kpos_c = lax.broadcasted_iota(I32, sc.shape, 1)
        kpos_n = past + lax.broadcasted_iota(I32, sn.shape, 1)
        keys = [_sortable_key(sc, (qpos // CHUNK) >= (kpos_c // CHUNK)),
                _sortable_key(sn, (qpos // CHUNK) >= (kpos_n // CHUNK))]
        _select_topk(keys, topk, [biasc_s, biasn_s])

    o_ref[...] = _attend_pair(q_ref[...], [ck_ref[...].astype(BF16), k_ref[...]],
                              [cv_ref[...].astype(BF16), v_ref[...]],
                              [biasc_s[...], biasn_s[...]]).astype(o_ref.dtype)


def _attn_sample(q16, qi16, kw32, k16, v16, ki2, cache_k, cache_v, cache_kidx, nb, ts):
    past = cache_k.shape[1]
    cki2 = jnp.concatenate([cache_kidx, cache_kidx], axis=-1).astype(BF16)
    topk = min(TOPK_MAX, (past + ts) // 4)

    def qrow(w):
        return pl.BlockSpec((ts, w), lambda b, p: (b, 0))

    def qpair():
        return pl.BlockSpec((None, ts, LANES), lambda b, p: (p, b, 0))

    def cache(pair):
        return pl.BlockSpec((None, past, LANES), (lambda b, p: (b, 0, p)) if pair else (lambda b, p: (b, 0, 0)))

    return pl.pallas_call(
        functools.partial(_attn_sample_kernel, topk, past),
        out_shape=jax.ShapeDtypeStruct((N_PAIRS, nb * ts, LANES), BF16),
        grid=(nb, N_PAIRS),
        in_specs=[qpair(), qrow(512), qrow(LANES), cache(True), cache(True), cache(False),
                  qpair(), qpair(), qrow(LANES)],
        out_specs=qpair(),
        scratch_shapes=[pltpu.VMEM((ts, past), F32), pltpu.VMEM((ts, ts), F32)],
        compiler_params=_cparams(("parallel", "arbitrary")),
    )(q16, qi16, kw32, cache_k, cache_v, cki2, k16, v16, ki2)


def _merge_kernel(x_ref, oa_ref, ob_ref, pga_ref, pgb_ref, bga_ref, bgb_ref, g1_ref, sc2_ref, sh2_ref, nw_ref,
                  wpa_ref, wpb_ref, wout_ref, x1_ref, h2_ref):
    ga = _sigmoid(pga_ref[...] + bga_ref[...])
    gb = _sigmoid(pgb_ref[...] + bgb_ref[...])
    ob = jnp.concatenate([ob_ref[p] for p in range(N_PAIRS)], axis=1)
    m = (ga * jnp.dot(oa_ref[...], wpa_ref[...], preferred_element_type=F32)
         + gb * jnp.dot(ob, wpb_ref[...], preferred_element_type=F32))
    x1 = x_ref[...] + g1_ref[...] * jnp.dot(m.astype(BF16), wout_ref[...], preferred_element_type=F32)
    x1_ref[...] = x1
    y = x1 * lax.rsqrt(jnp.mean(x1 * x1, axis=-1, keepdims=True) + EPS) * nw_ref[...]
    h2_ref[...] = (y * (1.0 + sc2_ref[...]) + sh2_ref[...]).astype(BF16)


def _merge(x2, o_a, o_b, P, b_gate, g1, sc2, sh2, nw2, wpa, wpb, wout, seq_len):
    n, d = x2.shape
    tm = _row_tile(n, seq_len, 512)
    g1_a, g1_s = _seq_operand(g1, seq_len, tm)
    sc_a, sc_s = _seq_operand(sc2, seq_len, tm)
    sh_a, sh_s = _seq_operand(sh2, seq_len, tm)

    def row():
        return pl.BlockSpec((tm, d), lambda i: (i, 0))

    def const(shape):
        return pl.BlockSpec(shape, lambda i: (0, 0))

    bg = b_gate.reshape(1, 2 * d)
    return pl.pallas_call(
        _merge_kernel,
        out_shape=(jax.ShapeDtypeStruct((n, d), F32), jax.ShapeDtypeStruct((n, d), BF16)),
        grid=(n // tm,),
        in_specs=[row(), row(), pl.BlockSpec((N_PAIRS, tm, LANES), lambda i: (0, i, 0)),
                  pl.BlockSpec((tm, d), lambda i: (i, C_GA // d)), pl.BlockSpec((tm, d), lambda i: (i, C_GB // d)),
                  pl.BlockSpec((1, d), lambda i: (0, 0)), pl.BlockSpec((1, d), lambda i: (0, 1)),
                  g1_s, sc_s, sh_s, const((1, d)), const((d, d)), const((d, d)), const((d, d))],
        out_specs=(row(), row()),
        compiler_params=_cparams(("parallel",)),
    )(x2, o_a, o_b, P, P, bg, bg, g1_a, sc_a, sh_a, nw2.reshape(1, d), wpa, wpb, wout)


def _top_exact(s, k):
    rows = lax.broadcasted_iota(I32, s.shape, 0).astype(F32)
    cur = s
    rank = jnp.full(s.shape, float(k), F32)
    vals = []
    for r in range(k):
        m = jnp.max(cur, axis=0, keepdims=True)
        first = jnp.min(jnp.where(cur == m, rows, 1e9), axis=0, keepdims=True)
        hit = rows == first
        vals.append(m)
        rank = jnp.where(hit, float(r), rank)
        cur = jnp.where(hit, -jnp.inf, cur)
    return vals, rank


def _top_fast(s, k):
    cur = s
    rank = jnp.full(s.shape, float(k), F32)
    vals = []
    for r in range(k):
        m = jnp.max(cur, axis=0, keepdims=True)
        hit = cur == m
        vals.append(m)
        rank = jnp.where(hit, float(r), rank)
        cur = jnp.where(hit, -jnp.inf, cur)
    taken = jnp.sum(jnp.where(rank < k, 1.0, 0.0), axis=0, keepdims=True)
    clean = jnp.max(jnp.abs(taken - k)) == 0.0
    return vals, rank, clean


def _top(s, k, vals_scr, rank_scr):
    nrow = s.shape[0]
    vals, rank, clean = _top_fast(s, k)
    vals_scr[...] = jnp.concatenate(vals, axis=0)
    rank_scr[0:nrow, :] = rank

    @pl.when(jnp.logical_not(clean))
    def _():
        vals_e, rank_e = _top_exact(s, k)
        vals_scr[...] = jnp.concatenate(vals_e, axis=0)
        rank_scr[0:nrow, :] = rank_e

    return vals_scr[...], rank_scr[0:nrow, :]


def _peer_sel_kernel(h_ref, wpqt_ref, kbd_ref, g_ref, cnt_ref, r2_ref, p2_ref, vals_scr, rank_scr):
    K = PEER_TOPK
    tm = h_ref.shape[0]
    qt = lax.dot_general(wpqt_ref[...], h_ref[...], NT, preferred_element_type=F32)
    st = jnp.dot(kbd_ref[...], qt.astype(BF16), preferred_element_type=F32)
    sub8 = lax.broadcasted_iota(I32, (8, tm), 0)
    neg = jnp.full((8, tm), -jnp.inf, F32)
    for hd in range(PEER_HEADS):
        r0 = hd * 2 * PEER_NKEYS
        s1 = st[r0:r0 + PEER_NKEYS]
        s2 = st[r0 + PEER_NKEYS:r0 + 2 * PEER_NKEYS]
        c1, rank1 = _top(s1, K, vals_scr, rank_scr)
        c2, rank2 = _top(s2, K, vals_scr, rank_scr)
        blocks = [c1[0:1] + c2, c1[1:2] + c2[0:8]]
        for k1 in range(2, 8):
            blocks.append(jnp.where(sub8 < K // (k1 + 1), c1[k1:k1 + 1] + c2[0:8], neg))
        blocks.append(c1[8:16] + c2[0:1])
        cand = jnp.concatenate(blocks, axis=0)
        _, crank = _top(cand, K, vals_scr, rank_scr)
        taken = crank < K
        z = jnp.sum(jnp.where(taken, jnp.exp(cand - (c1[0:1] + c2[0:1])), 0.0), axis=0, keepdims=True)
        tk = jnp.where(taken, 1.0, 0.0)
        per_k1 = [jnp.sum(tk[0:16], axis=0, keepdims=True)]
        per_k1 += [jnp.sum(tk[8 + 8 * k1:16 + 8 * k1], axis=0, keepdims=True) for k1 in range(1, 8)]
        cnt16 = jnp.concatenate(per_k1 + [tk[72:80]], axis=0)
        cnt = jnp.zeros(s1.shape, F32)
        for k1 in range(K):
            cnt = jnp.where(rank1 == float(k1), cnt16[k1:k1 + 1], cnt)
        g_ref[hd] = jnp.where(rank1 < K, jnp.exp(s1 - c1[0:1]) / z, 0.0)
        cnt_ref[hd] = cnt
        p2 = jnp.where(rank2 < K, jnp.exp(s2 - c2[0:1]), 0.0)
        cb = r2_ref.shape[-1]
        for tc in range(tm // cb):
            r2_ref[hd, tc] = rank2[:, tc * cb:(tc + 1) * cb]
            p2_ref[hd, tc] = p2[:, tc * cb:(tc + 1) * cb]


def _peer_select(h2, wpqt, kbd):
    n, d = h2.shape
    tm = 256 if n % 256 == 0 else n
    cb = min(LANES, tm)
    big = jax.ShapeDtypeStruct((PEER_HEADS, PEER_NKEYS, n), F32)
    blocked = jax.ShapeDtypeStruct((PEER_HEADS, n // cb, PEER_NKEYS, cb), F32)

    def blk():
        return pl.BlockSpec((PEER_HEADS, PEER_NKEYS, tm), lambda i: (0, 0, i))

    def blk4():
        return pl.BlockSpec((PEER_HEADS, tm // cb, PEER_NKEYS, cb), lambda i: (0, i, 0, 0))

    return pl.pallas_call(
        _peer_sel_kernel,
        out_shape=(big, big, blocked, blocked),
        grid=(n // tm,),
        in_specs=[pl.BlockSpec((tm, d), lambda i: (i, 0)),
                  pl.BlockSpec((d, d), lambda i: (0, 0)),
                  pl.BlockSpec((2 * d, d), lambda i: (0, 0))],
        out_specs=(blk(), blk(), blk4(), blk4()),
        scratch_shapes=[pltpu.VMEM((PEER_TOPK, tm), F32), pltpu.VMEM((PEER_NKEYS, tm), F32)],
        compiler_params=_cparams(("parallel",)),
    )(h2, wpqt, kbd)


def _gelu_tanh(x):
    return 0.5 * x * (1.0 + jnp.tanh(0.7978845608028654 * (x + 0.044715 * (x * x * x))))


def _peer_main_kernel(ni1, h_ref, x1_ref, g2_ref, u_ref, vt_ref, g_ref, cnt_ref, r2_ref, p2_ref, y_ref, acc, gate_s):
    j = pl.program_id(1)
    tm = h_ref.shape[0]

    @pl.when(j == 0)
    def _():
        acc[...] = jnp.zeros_like(acc)

    cb = r2_ref.shape[-1]
    h = h_ref[...]
    total = None
    for sb in range(ni1 // 2):
        for l in (2 * sb, 2 * sb + 1):
            for tc in range(tm // cb):
                ts = slice(tc * cb, (tc + 1) * cb)
                for rh in range(PEER_NKEYS // GATE_ROWS):
                    rs = slice(rh * GATE_ROWS, (rh + 1) * GATE_ROWS)
                    w = None
                    for hd in range(PEER_HEADS):
                        t = (jnp.where(r2_ref[hd, tc, rs, :] < cnt_ref[hd, l:l + 1, ts], p2_ref[hd, tc, rs, :], 0.0)
                             * g_ref[hd, l:l + 1, ts])
                        w = t if w is None else w + t
                    gate_s[tc, l * PEER_NKEYS + rh * GATE_ROWS:l * PEER_NKEYS + (rh + 1) * GATE_ROWS, :] = w
        rows = slice(sb * 2 * PEER_NKEYS, (sb + 1) * 2 * PEER_NKEYS)
        act = lax.dot_general(u_ref[rows, :], h, NT, preferred_element_type=F32)
        gate = jnp.concatenate([gate_s[tc, rows, :] for tc in range(tm // cb)], axis=1)
        coef = (gate * _gelu_tanh(act)).astype(BF16)
        part = jnp.dot(vt_ref[:, rows], coef, preferred_element_type=F32)
        total = part if total is None else total + part
    acc[...] += total

    @pl.when(j == pl.num_programs(1) - 1)
    def _():
        y_ref[...] = x1_ref[...] + g2_ref[...] * acc[...].T


def _peer_main(h2, x1, g2, u16, vt16, g, cnt, r2, p2, seq_len):
    n, d = h2.shape
    tm = _row_tile(n, seq_len, 512)
    ni1 = 8
    et = ni1 * PEER_NKEYS
    cb = r2.shape[-1]
    g2_a, g2_s = _seq_operand(g2, seq_len, tm)

    def row():
        return pl.BlockSpec((tm, d), lambda i, j: (i, 0))

    return pl.pallas_call(
        functools.partial(_peer_main_kernel, ni1),
        out_shape=jax.ShapeDtypeStruct((n, d), F32),
        grid=(n // tm, N_EXPERTS // et),
        in_specs=[row(), row(), g2_s,
                  pl.BlockSpec((et, d), lambda i, j: (j, 0)),
                  pl.BlockSpec((d, et), lambda i, j: (0, j)),
                  pl.BlockSpec((PEER_HEADS, ni1, tm), lambda i, j: (0, j, i)),
                  pl.BlockSpec((PEER_HEADS, ni1, tm), lambda i, j: (0, j, i)),
                  pl.BlockSpec((PEER_HEADS, tm // cb, PEER_NKEYS, cb), lambda i, j: (0, i, 0, 0)),
                  pl.BlockSpec((PEER_HEADS, tm // cb, PEER_NKEYS, cb), lambda i, j: (0, i, 0, 0))],
        out_specs=row(),
        scratch_shapes=[pltpu.VMEM((d, tm), F32), pltpu.VMEM((tm // cb, et, cb), F32)],
        compiler_params=_cparams(("parallel", "arbitrary")),
    )(h2, x1, g2_a, u16, vt16, g, cnt, r2, p2)


def _layer(x, mod, pos, shift_prev, s0, cache, lw):
    nb, t, d = x.shape
    n = nb * t
    sh1, sc1, g1, sh2, sc2, g2 = [mod[:, i * d:(i + 1) * d] for i in range(6)]
    x2 = x.reshape(n, d)
    P = _inproj(x2, sc1, sh1, lw['norm1_w'], lw['w_in16'], t)

    prev = _pack_rw(shift_prev).reshape(nb, 1, P_COLS)
    o_a, zf = _rwkv(P, nb, t, prev, lw['mu'], lw['w0'], lw['a0'], lw['k_k'], lw['k_a'], lw['r_k'], lw['lnx_w'],
                    lw['lnx_b'], lw['wup'], lw['aup'], lw['gup'], _state_to_pairs(s0))
    wkv = _pairs_to_state(zf)
    shift_last = _unpack_rw(P.reshape(nb, t, P_COLS)[:, -1, :])

    q16, k32, k16, v32, v16, qi16, kw32, ki2 = _dsa_prep(P, jnp.tile(pos, nb), lw['q_norm_w'], lw['k_norm_w'])
    if cache is None:
        o_b = _attn_prompt(q16, qi16, kw32, k16, v16, ki2, nb, t)
    else:
        ck, cv, cki = cache
        past = ck.shape[1]
        o_b = _attn_sample(q16, qi16, kw32, k16, v16, ki2, ck.reshape(nb, past, d), cv.reshape(nb, past, d), cki,
                           nb, t)

    x1, h2 = _merge(x2, o_a, o_b, P, lw['b_gate'], g1, sc2, sh2, lw['norm2_w'], lw['wpa'], lw['wpb'], lw['wout'], t)
    g, cnt, r2, p2 = _peer_select(h2, lw['wpqt'], lw['kbd'])
    y = _peer_main(h2, x1, g2, lw['u16'], lw['vt16'], g, cnt, r2, p2, t)

    k_new = k32.reshape(nb, t, N_HEADS, HEAD_DIM)
    v_new = v32.reshape(nb, t, N_HEADS, HEAD_DIM)
    ki_new = kw32[:, :IDX_DIM].reshape(nb, t, IDX_DIM)
    return y.reshape(nb, t, d), wkv, shift_last, k_new, v_new, ki_new


def _layer_weights(l, w_in, b_gate, mu_rw, w0, w_up, a0, a_up, g_up, k_k, k_a, r_k, lnx_w, lnx_b, q_norm_w, k_norm_w,
                   w_proj_a, w_proj_b, w_out, norm1_w, norm2_w, w_pq, peer_keys, peer_u, peer_v):
    d = D_MODEL
    zeros = lambda r: jnp.zeros((r, d), F32)
    keys = peer_keys[l].reshape(2 * PEER_HEADS, PEER_NKEYS, PEER_DHALF)
    eye = jnp.eye(2 * PEER_HEADS, dtype=F32)
    kbd = (eye[:, None, :, None] * keys[:, :, None, :]).reshape(2 * d, d)
    return {
        'w_in16': _pack_in(w_in[l]).astype(BF16), 'b_gate': b_gate[l], 'mu': _pack_rw(mu_rw[l]).reshape(1, P_COLS),
        'w0': w0[l], 'a0': a0[l], 'k_k': k_k[l], 'k_a': k_a[l], 'r_k': r_k[l].reshape(d), 'lnx_w': lnx_w[l],
        'lnx_b': lnx_b[l],
        'wup': jnp.concatenate([w_up[l], zeros(LANES - D_DECAY)], axis=0).astype(BF16),
        'aup': jnp.concatenate([zeros(D_DECAY), a_up[l]], axis=0).astype(BF16),
        'gup': jnp.concatenate([g_up[l], zeros(256 - D_GATE)], axis=0).astype(BF16),
        'q_norm_w': q_norm_w[l], 'k_norm_w': k_norm_w[l], 'norm1_w': norm1_w[l], 'norm2_w': norm2_w[l],
        'wpa': w_proj_a[l].astype(BF16), 'wpb': w_proj_b[l].astype(BF16), 'wout': w_out[l].astype(BF16),
        'wpqt': w_pq[l].T.astype(BF16), 'kbd': kbd.astype(BF16),
        'u16': peer_u[l].astype(BF16), 'vt16': peer_v[l].T.astype(BF16),
    }


def kernel(x_prompt, x_sample, c_prompt, c_sample, cache_k, cache_v, cache_kidx, state_wkv, state_shift, w_ada, b_ada,
           norm1_w, w_in, b_gate, mu_rw, w0, w_up, a0, a_up, g_up, k_k, k_a, r_k, lnx_w, lnx_b, q_norm_w, k_norm_w,
           w_proj_a, w_proj_b, w_out, norm2_w, w_pq, peer_keys, peer_u, peer_v):
    depth = w_in.shape[0]
    bp, tp = x_prompt.shape[:2]
    bs, ts = x_sample.shape[:2]
    past = cache_k.shape[2]
    dt = x_prompt.dtype
    pos_p = jnp.arange(tp, dtype=I32)
    pos_s = past + jnp.arange(ts, dtype=I32)
    zero_shift = jnp.zeros((bp, RW_IN), dt)
    zero_wkv = jnp.zeros((bp, N_HEADS, HEAD_DIM, HEAD_DIM), dt)
    c_all = jnp.concatenate([c_prompt, c_sample], axis=0)
    xp, xs = x_prompt, x_sample
    outs_p, outs_s = [], []
    for l in range(depth):
        lw = _layer_weights(l, w_in, b_gate, mu_rw, w0, w_up, a0, a_up, g_up, k_k, k_a, r_k, lnx_w, lnx_b, q_norm_w,
                            k_norm_w, w_proj_a, w_proj_b, w_out, norm1_w, norm2_w, w_pq, peer_keys, peer_u, peer_v)
        mod = _ada(c_all, w_ada[l], b_ada[l])
        xp, *rest_p = _layer(xp, mod[:bp], pos_p, zero_shift, zero_wkv, None, lw)
        xs, *rest_s = _layer(xs, mod[bp:], pos_s, state_shift[l], state_wkv[l],
                             (cache_k[l], cache_v[l], cache_kidx[l]), lw)
        outs_p.append(rest_p)
        outs_s.append(rest_s)
    stack = lambda outs, i: jnp.stack([o[i] for o in outs])
    return (xp, xs,
            stack(outs_p, 0), stack(outs_p, 1), stack(outs_p, 2), stack(outs_p, 3), stack(outs_p, 4),
            stack(outs_s, 0), stack(outs_s, 1), stack(outs_s, 2), stack(outs_s, 3), stack(outs_s, 4))
```

```python
import functools

import jax
import jax.numpy as jnp
from jax import lax
from jax.experimental import pallas as pl
from jax.experimental.pallas import tpu as pltpu

F32 = jnp.float32
BF16 = jnp.bfloat16
I32 = jnp.int32

LANES = 128
D_MODEL = 1024
EPS = 1e-6
GN_EPS = 64e-5
ROPE_THETA = 10000.0
CHUNK = 64
TOPK_MAX = 256
HEAD_DIM = 64
N_HEADS = D_MODEL // HEAD_DIM
N_PAIRS = N_HEADS // 2
IDX_HEADS = 8
IDX_DIM = 64
D_DECAY = 64
D_AAA = 64
D_GATE = 160
RW_IN = 3 * D_MODEL + D_DECAY + D_AAA + D_GATE
PEER_HEADS = 8
PEER_NKEYS = 128
PEER_TOPK = 16
PEER_DHALF = 64
N_EXPERTS = PEER_NKEYS * PEER_NKEYS
RW_CHUNK = 64
RW_INTERLEAVE = 8
RW_PASSES = (2, 1, 1, 1)
VMEM_LIMIT = 56 * 1024 * 1024
LOG2E = 1.4426950408889634
PEER_CAND = 80
GATE_ROWS = 32

C_R, C_K, C_V = 0, 1024, 2048
C_Q, C_KD, C_VD = 3072, 4096, 5120
C_GA, C_GB = 6144, 7168
C_QI = 8192
C_G = 8704
C_M = 8960
C_KW = 9088
P_COLS = 9216
IN_W = 9064

NT = (((1,), (1,)), ((), ()))
NN = (((1,), (0,)), ((), ()))


def _pack_in(w):
    z = lambda k: jnp.zeros(w.shape[:-1] + (k,), w.dtype)
    return jnp.concatenate([w[..., 0:3072], w[..., 3360:6432], w[..., 7016:9064], w[..., 6432:6944],
                            w[..., 3200:3360], z(256 - D_GATE), w[..., 3072:3200],
                            w[..., 6944:7016], z(LANES - IDX_DIM - IDX_HEADS)], axis=-1)


def _pack_rw(a):
    return _pack_in(jnp.concatenate([a, jnp.zeros(a.shape[:-1] + (IN_W - RW_IN,), a.dtype)], axis=-1))


def _unpack_rw(p):
    return jnp.concatenate([p[..., :3072], p[..., C_M:C_M + 128], p[..., C_G:C_G + D_GATE]], axis=-1)


def _split_bf16(x, n):
    parts = []
    r = x
    for _ in range(n):
        p = r.astype(BF16)
        parts.append(p)
        r = r - p.astype(F32)
    return parts


def _mm(a, b, pa=1, pb=1, dims=NN):
    aps = _split_bf16(a, pa) if a.dtype != BF16 else [a]
    bps = _split_bf16(b, pb) if b.dtype != BF16 else [b]
    order = max(len(aps), len(bps))
    out = None
    for i, ap in enumerate(aps):
        for j, bp in enumerate(bps):
            if i + j >= order:
                continue
            t = lax.dot_general(ap, bp, dims, preferred_element_type=F32)
            out = t if out is None else out + t
    return out


def _sigmoid(x):
    return 1.0 / (1.0 + jnp.exp(-x))


def _softplus(z):
    return jnp.maximum(z, 0.0) + jnp.log(1.0 + jnp.exp(-jnp.abs(z)))


def _cparams(sem):
    return pltpu.CompilerParams(dimension_semantics=sem, vmem_limit_bytes=VMEM_LIMIT)


def _ada_kernel(c_ref, w_ref, b_ref, o_ref):
    c = c_ref[...]
    s = c * _sigmoid(c)
    o_ref[...] = _mm(s, w_ref[...], 2, 2) + b_ref[...]


def _ada(c, w, b):
    m, d = c.shape
    n = w.shape[1]
    tn = 1024
    return pl.pallas_call(
        _ada_kernel,
        out_shape=jax.ShapeDtypeStruct((m, n), F32),
        grid=(n // tn,),
        in_specs=[pl.BlockSpec((m, d), lambda j: (0, 0)),
                  pl.BlockSpec((d, tn), lambda j: (0, j)),
                  pl.BlockSpec((1, tn), lambda j: (0, j))],
        out_specs=pl.BlockSpec((m, tn), lambda j: (0, j)),
        compiler_params=_cparams(("arbitrary",)),
    )(c, w, b.reshape(1, n))


def _seq_operand(vec, seq_len, tm):
    b, d = vec.shape
    if seq_len % tm == 0:
        per = seq_len // tm
        arr = vec.reshape(b, 1, d)
        spec = pl.BlockSpec((None, 1, d), lambda *g: (g[0] // per, 0, 0))
    else:
        assert tm % seq_len == 0
        arr = jnp.repeat(vec, seq_len, axis=0)
        spec = pl.BlockSpec((tm, d), lambda *g: (g[0], 0))
    return arr, spec


def _row_tile(n, seq_len, cap):
    tm = min(cap, n)
    while n % tm or (seq_len % tm and tm % seq_len):
        tm //= 2
    return tm


def _inproj_kernel(x_ref, sc_ref, sh_ref, nw_ref, w_ref, o_ref, h_scr):
    @pl.when(pl.program_id(1) == 0)
    def _():
        x = x_ref[...]
        y = x * lax.rsqrt(jnp.mean(x * x, axis=-1, keepdims=True) + EPS) * nw_ref[...]
        h_scr[...] = (y * (1.0 + sc_ref[...]) + sh_ref[...]).astype(BF16)

    o_ref[...] = jnp.dot(h_scr[...], w_ref[...], preferred_element_type=F32)


def _inproj(x2, sc, sh, nw, w16, seq_len):
    n, d = x2.shape
    tm = _row_tile(n, seq_len, 1024)
    tn = 1024
    sc_a, sc_s = _seq_operand(sc, seq_len, tm)
    sh_a, sh_s = _seq_operand(sh, seq_len, tm)
    return pl.pallas_call(
        _inproj_kernel,
        out_shape=jax.ShapeDtypeStruct((n, P_COLS), F32),
        grid=(n // tm, P_COLS // tn),
        in_specs=[pl.BlockSpec((tm, d), lambda i, j: (i, 0)), sc_s, sh_s,
                  pl.BlockSpec((1, d), lambda i, j: (0, 0)),
                  pl.BlockSpec((d, tn), lambda i, j: (0, j))],
        out_specs=pl.BlockSpec((tm, tn), lambda i, j: (i, j)),
        scratch_shapes=[pltpu.VMEM((tm, d), BF16)],
        compiler_params=_cparams(("parallel", "arbitrary")),
    )(x2, sc_a, sh_a, nw.reshape(1, d), w16)


def _lane_lo(shape):
    return lax.broadcasted_iota(I32, shape, len(shape) - 1) < HEAD_DIM


def _pair_sum(x):
    lo = _lane_lo(x.shape)
    s0 = jnp.sum(jnp.where(lo, x, 0.0), axis=-1, keepdims=True)
    s1 = jnp.sum(jnp.where(lo, 0.0, x), axis=-1, keepdims=True)
    return jnp.where(lo, s0, s1)


def _stack2(x):
    lo = _lane_lo(x.shape)
    return jnp.concatenate([jnp.where(lo, x, 0.0), jnp.where(lo, 0.0, x)], axis=0)


def _rwkv_kernel(t_real, npair, nchunk, pr, pk, pv, pg, pm, sr, sk, sv, sg, sm, mr, mk, mv, mg, mmu,
                 w0, a0, kkw, kaw, rkw, lnw, lnb, wup, aup, gup, z0, o_ref, zf_ref,
                 r_s, lw_s, k_s, v_s, a_s, b_s, y_s, bonus_s, g_s):
    C = RW_CHUNK
    t_pad = r_s.shape[1]

    def mix(p_ref, s_ref, m_ref):
        p = p_ref[...]
        prev = pltpu.roll(p, 1, 0)
        row = lax.broadcasted_iota(I32, p.shape, 0)
        prev = jnp.where(row == 0, s_ref[...], prev)
        return p + (prev - p) * m_ref[...]

    xg, xm = mix(pg, sg, mg), mix(pm, sm, mmu)
    th16, xm16, sg16 = jnp.tanh(xm).astype(BF16), xm.astype(BF16), _sigmoid(xg).astype(BF16)
    xr_all, xk_all, xv_all = mix(pr, sr, mr), mix(pk, sk, mk), mix(pv, sv, mv)

    def put(ref, pp, val):
        if t_pad > t_real:
            val = jnp.concatenate([val, jnp.zeros((t_pad - t_real, LANES), F32)], axis=0)
        ref[pp] = val

    for pp in range(npair):
        cs = slice(pp * LANES, (pp + 1) * LANES)
        xr, xk, xv = xr_all[:, cs], xk_all[:, cs], xv_all[:, cs]
        dw = jnp.dot(th16, wup[:, cs], preferred_element_type=F32)
        lw = -jnp.exp(-_softplus(-(w0[:, cs] + dw)) - 0.5)
        asig = _sigmoid(a0[:, cs] + jnp.dot(xm16, aup[:, cs], preferred_element_type=F32))
        g_s[pp] = jnp.dot(sg16, gup[:, cs], preferred_element_type=F32)
        kk = xk * kkw[:, cs]
        kk = kk * lax.rsqrt(_pair_sum(kk * kk) + 1e-12)
        kmod = xk * (1.0 + (asig - 1.0) * kaw[:, cs])
        bonus_s[pp] = _pair_sum(xr * kmod * rkw[:, cs]) * xv
        put(r_s, pp, xr)
        put(lw_s, pp, lw)
        put(k_s, pp, kmod)
        put(v_s, pp, xv)
        put(a_s, pp, -kk)
        put(b_s, pp, kk * asig)

    n2 = 2 * C
    ri = lax.broadcasted_iota(I32, (n2, n2), 0)
    ci = lax.broadcasted_iota(I32, (n2, n2), 1)
    same = (ri // C) == (ci // C)
    strict = same & ((ri % C) > (ci % C))
    incl = same & ((ri % C) >= (ci % C))
    eye = ri == ci
    eye_f = jnp.where(eye, 1.0, 0.0)
    tri = jnp.where(lax.broadcasted_iota(I32, (C, C), 0) >= lax.broadcasted_iota(I32, (C, C), 1), 1.0, 0.0
                    ).astype(BF16)
    zeros_sq = jnp.zeros((n2, LANES), F32)

    pg_, pi_, po_, ps_ = RW_PASSES

    def local(chains):
        each = lambda f, *cols: [f(*xs) for xs in zip(*cols)]
        lwc = [lw_s[pp, sl, :] for sl, pp in chains]
        cum = each(lambda l: _mm(tri, l, 1, 3), lwc)
        cum_last = each(lambda c: c[C - 1:C, :], cum)
        ec, eci = each(jnp.exp, cum), each(lambda c: jnp.exp(-c), cum)
        ecp = each(lambda c, l: jnp.exp(c - l), cum, lwc)
        ecl = each(lambda c, cl: jnp.exp(cl - c), cum, cum_last)
        a_c = [a_s[pp, sl, :] for sl, pp in chains]
        b_c = [b_s[pp, sl, :] for sl, pp in chains]
        k_c = [k_s[pp, sl, :] for sl, pp in chains]
        r_c = [r_s[pp, sl, :] for sl, pp in chains]
        As = each(lambda a, e: _stack2(a * e), a_c, ecp)
        Rs = each(lambda r, e: _stack2(r * e), r_c, ec)
        Bs = each(lambda b, e: _stack2(b * e), b_c, eci)
        Ks = each(lambda k, e: _stack2(k * e), k_c, eci)
        Bt = each(lambda b, e: _stack2(b * e), b_c, ecl)
        Kt = each(lambda k, e: _stack2(k * e), k_c, ecl)
        Vs = [_stack2(v_s[pp, sl, :]) for sl, pp in chains]

        G = each(lambda a, r, b, k: _mm(jnp.concatenate([a, r], axis=0), jnp.concatenate([b, k], axis=0),
                                        pg_, pg_, NT), As, Rs, Bs, Ks)
        a_ab = each(lambda g: jnp.where(strict, g[:n2, :n2], 0.0), G)
        a_ak = each(lambda g: jnp.where(strict, g[:n2, n2:], 0.0), G)
        a_rb = each(lambda g: jnp.where(incl, g[n2:, :n2], 0.0), G)
        a_rk = each(lambda g: jnp.where(incl, g[n2:, n2:], 0.0), G)

        lp = a_ab
        tm_ = each(lambda a: eye_f + a, a_ab)
        step = 2
        while step < C:
            lp = each(lambda l: _mm(l, l, pi_, pi_), lp)
            tm_ = each(lambda t, l: t + _mm(t, l, pi_, pi_), tm_, lp)
            step *= 2

        w1 = each(lambda a, v: _mm(a, v, po_, po_), a_ak, Vs)
        mu_ = each(lambda t, a, w: _mm(t, jnp.concatenate([a, w], axis=1), po_, po_), tm_, As, w1)
        rhs = each(lambda m, v: jnp.concatenate([m, jnp.concatenate([zeros_sq, v], axis=1)], axis=0), mu_, Vs)
        lhs = each(lambda rb, rk, b, k: jnp.concatenate([jnp.concatenate([rb, rk], axis=1),
                                                         jnp.concatenate([b.T, k.T], axis=1)], axis=0),
                   a_rb, a_rk, Bt, Kt)
        out2 = each(lambda l, r: _mm(l, r, po_, po_), lhs, rhs)
        m23 = each(lambda r, o, cl: jnp.concatenate([r + o[:n2, :LANES],
                                                     jnp.where(eye, jnp.exp(cl), 0.0) + o[n2:, :LANES]], axis=0),
                   Rs, out2, cum_last)
        return [(m, o[:n2, LANES:], o[n2:, LANES:]) for m, o in zip(m23, out2)]

    def step_chunks(i, zs):
        sls = [pl.ds(pl.multiple_of((i * nchunk + j) * C, C), C) for j in range(nchunk)]
        parts = local([(sl, pp) for sl in sls for pp in range(npair)])
        zs = list(zs)
        for j, sl in enumerate(sls):
            for pp in range(npair):
                m23, y_loc, z_loc = parts[j * npair + pp]
                yz = _mm(m23, zs[pp], ps_, ps_)
                y = yz[:n2] + y_loc
                y_s[pp, sl, :] = y[:C] + y[C:]
                zs[pp] = yz[n2:] + z_loc
        return tuple(zs)

    zs = lax.fori_loop(0, t_pad // (C * nchunk), step_chunks, tuple(z0[pp] for pp in range(npair)))
    for pp in range(npair):
        zf_ref[pp] = zs[pp]
        cs = slice(pp * LANES, (pp + 1) * LANES)
        y = y_s[pp, 0:t_real, :]
        mean = _pair_sum(y) * (1.0 / HEAD_DIM)
        dlt = y - mean
        var = _pair_sum(dlt * dlt) * (1.0 / HEAD_DIM)
        yn = dlt * lax.rsqrt(var + GN_EPS) * lnw[:, cs] + lnb[:, cs]
        o_ref[:, cs] = ((yn + bonus_s[pp]) * g_s[pp]).astype(o_ref.dtype)


def _rwkv(P, nb, t, prev, mu, w0, a0, k_k, k_a, r_k, lnx_w, lnx_b, wup, aup, gup, z0):
    t_pad = max(t, RW_CHUNK)
    assert t % 8 == 0 and t_pad % RW_CHUNK == 0
    n_chunks = t_pad // RW_CHUNK
    nchunk = min(RW_INTERLEAVE, n_chunks)
    npair = min(N_PAIRS, max(1, RW_INTERLEAVE // nchunk))
    wp = npair * LANES

    def cblk(c0, w, per_pair):
        return (lambda p: c0 // w + p) if per_pair else (lambda p: c0 // w)

    def pcol(c0, w, pp):
        f = cblk(c0, w, pp)
        return pl.BlockSpec((t, w), lambda b, p: (b, f(p)))

    def prevcol(c0, w, pp):
        f = cblk(c0, w, pp)
        return pl.BlockSpec((None, 1, w), lambda b, p: (b, 0, f(p)))

    def mucol(c0, w, pp):
        f = cblk(c0, w, pp)
        return pl.BlockSpec((1, w), lambda b, p: (0, f(p)))

    def hvec():
        return pl.BlockSpec((1, wp), lambda b, p: (0, p))

    cols = [(C_R, wp, True), (C_K, wp, True), (C_V, wp, True), (C_G, 256, False), (C_M, LANES, False)]
    in_specs = ([pcol(*c) for c in cols] + [prevcol(*c) for c in cols] + [mucol(*c) for c in cols]
                + [hvec() for _ in range(7)]
                + [pl.BlockSpec((LANES, wp), lambda b, p: (0, p)),
                   pl.BlockSpec((LANES, wp), lambda b, p: (0, p)),
                   pl.BlockSpec((256, wp), lambda b, p: (0, p)),
                   pl.BlockSpec((None, npair, LANES, LANES), lambda b, p: (b, p, 0, 0))])
    vecs = [v.reshape(1, D_MODEL) for v in (w0, a0, k_k, k_a, r_k, lnx_w, lnx_b)]
    o, zf = pl.pallas_call(
        functools.partial(_rwkv_kernel, t, npair, nchunk),
        out_shape=(jax.ShapeDtypeStruct((nb * t, D_MODEL), BF16),
                   jax.ShapeDtypeStruct((nb, N_PAIRS, LANES, LANES), F32)),
        grid=(nb, N_PAIRS // npair),
        in_specs=in_specs,
        out_specs=(pl.BlockSpec((t, wp), lambda b, p: (b, p)),
                   pl.BlockSpec((None, npair, LANES, LANES), lambda b, p: (b, p, 0, 0))),
        scratch_shapes=([pltpu.VMEM((npair, t_pad, LANES), F32) for _ in range(7)]
                        + [pltpu.VMEM((npair, t, LANES), F32) for _ in range(2)]),
        compiler_params=_cparams(("parallel", "arbitrary")),
    )(P, P, P, P, P, prev, prev, prev, prev, prev, mu, mu, mu, mu, mu, *vecs, wup, aup, gup, z0)
    return o, zf


def _state_to_pairs(s):
    nb = s.shape[0]
    zt = jnp.swapaxes(s, -1, -2).reshape(nb, N_PAIRS, 2, HEAD_DIM, HEAD_DIM)
    zero = jnp.zeros_like(zt[:, :, 0])
    top = jnp.concatenate([zt[:, :, 0], zero], axis=-1)
    bot = jnp.concatenate([zero, zt[:, :, 1]], axis=-1)
    return jnp.concatenate([top, bot], axis=-2)


def _pairs_to_state(z):
    nb = z.shape[0]
    h0 = z[:, :, :HEAD_DIM, :HEAD_DIM]
    h1 = z[:, :, HEAD_DIM:, HEAD_DIM:]
    s = jnp.stack([h0, h1], axis=2).reshape(nb, N_HEADS, HEAD_DIM, HEAD_DIM)
    return jnp.swapaxes(s, -1, -2)


def _rope(x, cos, sin_signed):
    w = x.shape[1]
    reps = w // LANES
    cw = jnp.concatenate([cos] * reps, axis=1) if reps > 1 else cos
    sw = jnp.concatenate([sin_signed] * reps, axis=1) if reps > 1 else sin_signed
    lane = lax.broadcasted_iota(I32, x.shape, 1)
    fwd = pltpu.roll(x, w - 32, 1)
    bwd = pltpu.roll(x, 32, 1)
    partner = jnp.where((lane % HEAD_DIM) < 32, fwd, bwd)
    return x * cw + partner * sw


def _head_rms(x, nw, e_dn, e_up):
    ms = _mm(x * x, e_dn, 2, 1) * (1.0 / HEAD_DIM)
    r = lax.rsqrt(ms + EPS)
    return x * _mm(r, e_up, 2, 1) * nw


def _dsa_prep_kernel(pq, pkd, pvd, pqi, pkw, cos_ref, sin_ref, qn, kn, edn, eup,
                     q16, k32, k16, v32, v16, qi16, kw32, ki2):
    cos, sin = cos_ref[...], sin_ref[...]
    e_dn, e_up = edn[...], eup[...]
    def put_pairs(ref, x):
        for p in range(N_PAIRS):
            ref[p] = x[:, p * LANES:(p + 1) * LANES].astype(ref.dtype)

    q = _rope(_head_rms(pq[...], qn[...], e_dn, e_up), cos, sin)
    put_pairs(q16, q * (HEAD_DIM ** -0.5 * LOG2E))
    k = _rope(_head_rms(pkd[...], kn[...], e_dn, e_up), cos, sin)
    k32[...] = k
    put_pairs(k16, k)
    v = pvd[...]
    v32[...] = v
    put_pairs(v16, v)
    qi16[...] = _rope(pqi[...], cos, sin).astype(BF16)
    kw = pkw[...]
    lane = lax.broadcasted_iota(I32, kw.shape, 1)
    wi_scale = (IDX_HEADS * IDX_DIM) ** -0.5
    kr = _rope(kw, cos, sin)
    kw32[...] = jnp.where(lane < IDX_DIM, kr, jnp.where(lane < IDX_DIM + IDX_HEADS, kw * wi_scale, 0.0))
    ki2[...] = jnp.where(lane < IDX_DIM, kr, pltpu.roll(kr, IDX_DIM, 1)).astype(BF16)


def _dsa_prep(P, pos_rows, q_norm_w, k_norm_w):
    n = P.shape[0]
    tm = 512 if n % 512 == 0 else n
    half = HEAD_DIM // 2
    inv = ROPE_THETA ** (-jnp.arange(half, dtype=F32) / half)
    ang = pos_rows.astype(F32)[:, None] * inv[None, :]
    cos = jnp.tile(jnp.cos(ang), (1, 4))
    sin = jnp.sin(ang)
    sin_signed = jnp.tile(jnp.concatenate([-sin, sin], axis=1), (1, 2))
    head_of = jnp.arange(D_MODEL) // HEAD_DIM
    e_dn = (head_of[:, None] == jnp.arange(LANES)[None, :]).astype(BF16)
    e_up = e_dn.T
    qn = jnp.tile(q_norm_w, N_HEADS).reshape(1, D_MODEL)
    kn = jnp.tile(k_norm_w, N_HEADS).reshape(1, D_MODEL)

    def col(c0, w):
        return pl.BlockSpec((tm, w), lambda i, c0=c0, w=w: (i, c0 // w))

    def row(w):
        return pl.BlockSpec((tm, w), lambda i: (i, 0))

    def const(shape):
        return pl.BlockSpec(shape, lambda i: (0, 0))

    pairs = jax.ShapeDtypeStruct((N_PAIRS, n, LANES), BF16)
    pair_spec = pl.BlockSpec((N_PAIRS, tm, LANES), lambda i: (0, i, 0))
    return pl.pallas_call(
        _dsa_prep_kernel,
        out_shape=(pairs, jax.ShapeDtypeStruct((n, D_MODEL), F32), pairs, jax.ShapeDtypeStruct((n, D_MODEL), F32),
                   pairs, jax.ShapeDtypeStruct((n, IDX_HEADS * IDX_DIM), BF16),
                   jax.ShapeDtypeStruct((n, LANES), F32), jax.ShapeDtypeStruct((n, LANES), BF16)),
        grid=(n // tm,),
        in_specs=[col(C_Q, 1024), col(C_KD, 1024), col(C_VD, 1024), col(C_QI, 512), col(C_KW, LANES),
                  row(LANES), row(LANES), const((1, D_MODEL)), const((1, D_MODEL)),
                  const((D_MODEL, LANES)), const((LANES, D_MODEL))],
        out_specs=(pair_spec, row(D_MODEL), pair_spec, row(D_MODEL), pair_spec, row(512), row(LANES), row(LANES)),
        compiler_params=_cparams(("parallel",)),
    )(P, P, P, P, P, cos, sin_signed, qn, kn, e_dn, e_up)


INT_MIN = -2 ** 31


def _sortable_key(score, admissible):
    bits = lax.bitcast_convert_type(score + 0.0, I32)
    key = jnp.where(bits < 0, bits ^ jnp.int32(0x7FFFFFFF), bits)
    return jnp.where(admissible, key, jnp.int32(INT_MIN))


def _index_scores(qi, wi, ki_list):
    outs = []
    for ki in ki_list:
        acc = None
        for h in range(IDX_HEADS):
            qpair = qi[:, (h // 2) * LANES:(h // 2 + 1) * LANES]
            lo = _lane_lo(qpair.shape)
            qh = jnp.where(lo if h % 2 == 0 else jnp.logical_not(lo), qpair, jnp.zeros_like(qpair))
            rel = lax.dot_general(qh, ki, NT, preferred_element_type=F32)
            term = wi[:, IDX_DIM + h:IDX_DIM + h + 1] * jnp.maximum(rel, 0.0)
            acc = term if acc is None else acc + term
        outs.append(acc)
    return outs


def _select_topk(keys, topk, bias_refs):
    tq = keys[0].shape[0]
    int_min = jnp.int32(INT_MIN)

    def write(masks):
        for ref, k, msk in zip(bias_refs, keys, masks):
            ref[:, 0:k.shape[1]] = jnp.where(msk, 0.0, -jnp.inf)

    def count(pred_list):
        tot = None
        for p in pred_list:
            c = jnp.sum(jnp.where(p, 1.0, 0.0), axis=-1, keepdims=True)
            tot = c if tot is None else tot + c
        return tot

    def bit_step(i, c):
        trial = c + jnp.left_shift(jnp.int32(1), 31 - i)
        cnt = count([k >= trial for k in keys])
        return jnp.where(cnt >= topk, trial, c)

    thr = lax.fori_loop(0, 32, bit_step, jnp.full((tq, 1), INT_MIN, I32))
    ge = [(k >= thr) & (k != int_min) for k in keys]
    write(ge)
    surplus = jnp.max(count(ge)) > topk

    @pl.when(surplus)
    def _():
        gt = [k > thr for k in keys]
        need = topk - count(gt)
        ties = [(k == thr) & (k != int_min) for k in keys]
        offs, idx = 0, []
        for k in keys:
            idx.append(lax.broadcasted_iota(I32, k.shape, 1) + offs)
            offs += k.shape[1]
        nbits = max(1, (offs - 1).bit_length() + 1)

        def idx_step(i, m):
            trial = m + jnp.left_shift(jnp.int32(1), nbits - 1 - i)
            cnt = count([t & (ix < trial) for t, ix in zip(ties, idx)])
            return jnp.where(cnt <= need, trial, m)

        cut = lax.fori_loop(0, nbits, idx_step, jnp.zeros((tq, 1), I32))
        write([g | (t & (ix < cut)) for g, t, ix in zip(gt, ties, idx)])


def _attend_pair(q_pair, k_list, v_list, bias_list):
    lo = _lane_lo(q_pair.shape)
    zero = jnp.zeros_like(q_pair)
    outs = []
    for qh in (jnp.where(lo, q_pair, zero), jnp.where(lo, zero, q_pair)):
        s_list = [lax.dot_general(qh, k, NT, preferred_element_type=F32) + b for k, b in zip(k_list, bias_list)]
        m = None
        for s in s_list:
            mx = jnp.max(s, axis=-1, keepdims=True)
            m = mx if m is None else jnp.maximum(m, mx)
        den, acc = None, None
        for s, v in zip(s_list, v_list):
            p = jnp.exp2(s - m)
            d = jnp.sum(p, axis=-1, keepdims=True)
            o = jnp.dot(p.astype(BF16), v, preferred_element_type=F32)
            den = d if den is None else den + d
            acc = o if acc is None else acc + o
        outs.append(acc / den)
    return jnp.where(_lane_lo(outs[0].shape), outs[0], outs[1])


def _attn_prompt_kernel(topk, ncase, q_ref, qi_ref, kw_ref, k_ref, v_ref, ki2_ref, o_ref, bias_s):
    tq = q_ref.shape[1]
    t = k_ref.shape[1]
    i = pl.program_id(1)
    lstep = t // ncase
    case = ((i + 1) * tq - 1) // lstep

    def run(L):
        score = _index_scores(qi_ref[...], kw_ref[...], [ki2_ref[0:L, :]])[0]
        qpos = i * tq + lax.broadcasted_iota(I32, (tq, L), 0)
        kpos = lax.broadcasted_iota(I32, (tq, L), 1)
        adm = (qpos // CHUNK) >= (kpos // CHUNK)
        _select_topk([_sortable_key(score, adm)], topk, [bias_s])

        def pair(p, carry):
            o = _attend_pair(q_ref[p], [k_ref[p, 0:L, :]], [v_ref[p, 0:L, :]], [bias_s[:, 0:L]])
            o_ref[p] = o.astype(o_ref.dtype)
            return carry

        lax.fori_loop(0, N_PAIRS, pair, 0)

    for c in range(ncase):
        pl.when(case == c)(functools.partial(run, (c + 1) * lstep))


def _attn_prompt(q16, qi16, kw32, k16, v16, ki2, nb, t):
    tq = min(256, t)
    topk = min(TOPK_MAX, t // 4)
    nq = t // tq
    ncase = min(4, nq)

    def qrow(w):
        return pl.BlockSpec((tq, w), lambda b, i: (b * nq + i, 0))

    def qpairs():
        return pl.BlockSpec((N_PAIRS, tq, LANES), lambda b, i: (0, b * nq + i, 0))

    def kpairs():
        return pl.BlockSpec((N_PAIRS, t, LANES), lambda b, i: (0, b, 0))

    return pl.pallas_call(
        functools.partial(_attn_prompt_kernel, topk, ncase),
        out_shape=jax.ShapeDtypeStruct((N_PAIRS, nb * t, LANES), BF16),
        grid=(nb, nq),
        in_specs=[qpairs(), qrow(512), qrow(LANES), kpairs(), kpairs(),
                  pl.BlockSpec((t, LANES), lambda b, i: (b, 0))],
        out_specs=qpairs(),
        scratch_shapes=[pltpu.VMEM((tq, t), F32)],
        compiler_params=_cparams(("parallel", "arbitrary")),
    )(q16, qi16, kw32, k16, v16, ki2)


def _attn_sample_kernel(topk, past, q_ref, qi_ref, kw_ref, ck_ref, cv_ref, cki2_ref, k_ref, v_ref, ki2_ref, o_ref,
                        biasc_s, biasn_s):
    ts = q_ref.shape[0]

    @pl.when(pl.program_id(1) == 0)
    def _():
        sc, sn = _index_scores(qi_ref[...], kw_ref[...], [cki2_ref[...], ki2_ref[...]])
        qpos = past + lax.broadcasted_iota(I32, (ts, 1), 0)
        kpos_c = lax.broadcasted_iota(I32, sc.shape, 1)
        kpos_n = past + lax.broadcasted_iota(I32, sn.shape, 1)
        keys = [_sortable_key(sc, (qpos // CHUNK) >= (kpos_c // CHUNK)),
                _sortable_key(sn, (qpos // CHUNK) >= (kpos_n // CHUNK))]
        _select_topk(keys, topk, [biasc_s, biasn_s])

    o_ref[...] = _attend_pair(q_ref[...], [ck_ref[...].astype(BF16), k_ref[...]],
                              [cv_ref[...].astype(BF16), v_ref[...]],
                              [biasc_s[...], biasn_s[...]]).astype(o_ref.dtype)


def _attn_sample(q16, qi16, kw32, k16, v16, ki2, cache_k, cache_v, cache_kidx, nb, ts):
    past = cache_k.shape[1]
    cki2 = jnp.concatenate([cache_kidx, cache_kidx], axis=-1).astype(BF16)
    topk = min(TOPK_MAX, (past + ts) // 4)

    def qrow(w):
        return pl.BlockSpec((ts, w), lambda b, p: (b, 0))

    def qpair():
        return pl.BlockSpec((None, ts, LANES), lambda b, p: (p, b, 0))

    def cache(pair):
        return pl.BlockSpec((None, past, LANES), (lambda b, p: (b, 0, p)) if pair else (lambda b, p: (b, 0, 0)))

    return pl.pallas_call(
        functools.partial(_attn_sample_kernel, topk, past),
        out_shape=jax.ShapeDtypeStruct((N_PAIRS, nb * ts, LANES), BF16),
        grid=(nb, N_PAIRS),
        in_specs=[qpair(), qrow(512), qrow(LANES), cache(True), cache(True), cache(False),
                  qpair(), qpair(), qrow(LANES)],
        out_specs=qpair(),
        scratch_shapes=[pltpu.VMEM((ts, past), F32), pltpu.VMEM((ts, ts), F32)],
        compiler_params=_cparams(("parallel", "arbitrary")),
    )(q16, qi16, kw32, cache_k, cache_v, cki2, k16, v16, ki2)


def _merge_kernel(x_ref, oa_ref, ob_ref, pga_ref, pgb_ref, bga_ref, bgb_ref, g1_ref, sc2_ref, sh2_ref, nw_ref,
                  wpa_ref, wpb_ref, wout_ref, x1_ref, h2_ref):
    ga = _sigmoid(pga_ref[...] + bga_ref[...])
    gb = _sigmoid(pgb_ref[...] + bgb_ref[...])
    ob = jnp.concatenate([ob_ref[p] for p in range(N_PAIRS)], axis=1)
    m = (ga * jnp.dot(oa_ref[...], wpa_ref[...], preferred_element_type=F32)
         + gb * jnp.dot(ob, wpb_ref[...], preferred_element_type=F32))
    x1 = x_ref[...] + g1_ref[...] * jnp.dot(m.astype(BF16), wout_ref[...], preferred_element_type=F32)
    x1_ref[...] = x1
    y = x1 * lax.rsqrt(jnp.mean(x1 * x1, axis=-1, keepdims=True) + EPS) * nw_ref[...]
    h2_ref[...] = (y * (1.0 + sc2_ref[...]) + sh2_ref[...]).astype(BF16)


def _merge(x2, o_a, o_b, P, b_gate, g1, sc2, sh2, nw2, wpa, wpb, wout, seq_len):
    n, d = x2.shape
    tm = _row_tile(n, seq_len, 512)
    g1_a, g1_s = _seq_operand(g1, seq_len, tm)
    sc_a, sc_s = _seq_operand(sc2, seq_len, tm)
    sh_a, sh_s = _seq_operand(sh2, seq_len, tm)

    def row():
        return pl.BlockSpec((tm, d), lambda i: (i, 0))

    def const(shape):
        return pl.BlockSpec(shape, lambda i: (0, 0))

    bg = b_gate.reshape(1, 2 * d)
    return pl.pallas_call(
        _merge_kernel,
        out_shape=(jax.ShapeDtypeStruct((n, d), F32), jax.ShapeDtypeStruct((n, d), BF16)),
        grid=(n // tm,),
        in_specs=[row(), row(), pl.BlockSpec((N_PAIRS, tm, LANES), lambda i: (0, i, 0)),
                  pl.BlockSpec((tm, d), lambda i: (i, C_GA // d)), pl.BlockSpec((tm, d), lambda i: (i, C_GB // d)),
                  pl.BlockSpec((1, d), lambda i: (0, 0)), pl.BlockSpec((1, d), lambda i: (0, 1)),
                  g1_s, sc_s, sh_s, const((1, d)), const((d, d)), const((d, d)), const((d, d))],
        out_specs=(row(), row()),
        compiler_params=_cparams(("parallel",)),
    )(x2, o_a, o_b, P, P, bg, bg, g1_a, sc_a, sh_a, nw2.reshape(1, d), wpa, wpb, wout)


def _top_exact(s, k):
    rows = lax.broadcasted_iota(I32, s.shape, 0).astype(F32)
    cur = s
    rank = jnp.full(s.shape, float(k), F32)
    vals = []
    for r in range(k):
        m = jnp.max(cur, axis=0, keepdims=True)
        first = jnp.min(jnp.where(cur == m, rows, 1e9), axis=0, keepdims=True)
        hit = rows == first
        vals.append(m)
        rank = jnp.where(hit, float(r), rank)
        cur = jnp.where(hit, -jnp.inf, cur)
    return vals, rank


def _top_fast(ss, k):
    curs = list(ss)
    ranks = [jnp.full(s.shape, float(k), F32) for s in ss]
    vals = [[] for _ in ss]
    for r in range(k):
        ms = [jnp.max(c, axis=0, keepdims=True) for c in curs]
        hits = [c == m for c, m in zip(curs, ms)]
        ranks = [jnp.where(h, float(r), rk) for h, rk in zip(hits, ranks)]
        curs = [jnp.where(h, -jnp.inf, c) for h, c in zip(hits, curs)]
        for v, m in zip(vals, ms):
            v.append(m)
    cleans = [jnp.max(jnp.abs(jnp.sum(jnp.where(rk < k, 1.0, 0.0), axis=0, keepdims=True) - k)) == 0.0
              for rk in ranks]
    return vals, ranks, cleans


def _top(ss, k, vals_scr, rank_scr):
    vals, ranks, cleans = _top_fast(ss, k)
    for i, s in enumerate(ss):
        vals_scr[i] = jnp.concatenate(vals[i], axis=0)
        rank_scr[i] = ranks[i]

        @pl.when(jnp.logical_not(cleans[i]))
        def _(i=i, s=s):
            vals_e, rank_e = _top_exact(s, k)
            vals_scr[i] = jnp.concatenate(vals_e, axis=0)
            rank_scr[i] = rank_e


def _peer_sel_kernel(h_ref, wpqt_ref, kbd_ref, g_ref, cnt_ref, r2_ref, p2_ref, s_scr, vals_scr, rank_scr,
                     cand_scr, cvals_scr, crank_scr):
    K = PEER_TOPK
    tm = h_ref.shape[0]
    qt = lax.dot_general(wpqt_ref[...], h_ref[...], NT, preferred_element_type=F32)
    s_scr[...] = jnp.dot(kbd_ref[...], qt.astype(BF16), preferred_element_type=F32
                         ).reshape(2 * PEER_HEADS, PEER_NKEYS, tm)
    sub8 = lax.broadcasted_iota(I32, (8, tm), 0)
    neg = jnp.full((8, tm), -jnp.inf, F32)
    _top([s_scr[r] for r in range(2 * PEER_HEADS)], K, vals_scr, rank_scr)
    for hd in range(PEER_HEADS):
        c1, c2 = vals_scr[2 * hd], vals_scr[2 * hd + 1]
        blocks = [c1[0:1] + c2, c1[1:2] + c2[0:8]]
        for k1 in range(2, 8):
            blocks.append(jnp.where(sub8 < K // (k1 + 1), c1[k1:k1 + 1] + c2[0:8], neg))
        blocks.append(c1[8:16] + c2[0:1])
        cand_scr[hd] = jnp.concatenate(blocks, axis=0)
    _top([cand_scr[hd] for hd in range(PEER_HEADS)], K, cvals_scr, crank_scr)
    for hd in range(PEER_HEADS):
        s1, s2 = s_scr[2 * hd], s_scr[2 * hd + 1]
        c1, c2 = vals_scr[2 * hd], vals_scr[2 * hd + 1]
        rank1, rank2 = rank_scr[2 * hd], rank_scr[2 * hd + 1]
        cand = cand_scr[hd]
        taken = crank_scr[hd] < K
        z = jnp.sum(jnp.where(taken, jnp.exp(cand - (c1[0:1] + c2[0:1])), 0.0), axis=0, keepdims=True)
        tk = jnp.where(taken, 1.0, 0.0)
        per_k1 = [jnp.sum(tk[0:16], axis=0, keepdims=True)]
        per_k1 += [jnp.sum(tk[8 + 8 * k1:16 + 8 * k1], axis=0, keepdims=True) for k1 in range(1, 8)]
        cnt16 = jnp.concatenate(per_k1 + [tk[72:80]], axis=0)
        cnt = jnp.zeros(s1.shape, F32)
        for k1 in range(K):
            cnt = jnp.where(rank1 == float(k1), cnt16[k1:k1 + 1], cnt)
        g_ref[hd] = jnp.where(rank1 < K, jnp.exp(s1 - c1[0:1]) / z, 0.0)
        cnt_ref[hd] = cnt
        p2 = jnp.where(rank2 < K, jnp.exp(s2 - c2[0:1]), 0.0)
        cb = r2_ref.shape[-1]
        for tc in range(tm // cb):
            r2_ref[hd, tc] = rank2[:, tc * cb:(tc + 1) * cb].astype(r2_ref.dtype)
            p2_ref[hd, tc] = p2[:, tc * cb:(tc + 1) * cb].astype(p2_ref.dtype)


def _peer_select(h2, wpqt, kbd):
    n, d = h2.shape
    tm = 256 if n % 256 == 0 else n
    cb = min(LANES, tm)
    big = jax.ShapeDtypeStruct((PEER_HEADS, PEER_NKEYS, n), F32)
    blocked = jax.ShapeDtypeStruct((PEER_HEADS, n // cb, PEER_NKEYS, cb), BF16)

    def blk():
        return pl.BlockSpec((PEER_HEADS, PEER_NKEYS, tm), lambda i: (0, 0, i))

    def blk4():
        return pl.BlockSpec((PEER_HEADS, tm // cb, PEER_NKEYS, cb), lambda i: (0, i, 0, 0))

    return pl.pallas_call(
        _peer_sel_kernel,
        out_shape=(big, big, blocked, blocked),
        grid=(n // tm,),
        in_specs=[pl.BlockSpec((tm, d), lambda i: (i, 0)),
                  pl.BlockSpec((d, d), lambda i: (0, 0)),
                  pl.BlockSpec((2 * d, d), lambda i: (0, 0))],
        out_specs=(blk(), blk(), blk4(), blk4()),
        scratch_shapes=[pltpu.VMEM((2 * PEER_HEADS, PEER_NKEYS, tm), F32),
                        pltpu.VMEM((2 * PEER_HEADS, PEER_TOPK, tm), F32),
                        pltpu.VMEM((2 * PEER_HEADS, PEER_NKEYS, tm), F32),
                        pltpu.VMEM((PEER_HEADS, PEER_CAND, tm), F32),
                        pltpu.VMEM((PEER_HEADS, PEER_TOPK, tm), F32),
                        pltpu.VMEM((PEER_HEADS, PEER_CAND, tm), F32)],
        compiler_params=_cparams(("parallel",)),
    )(h2, wpqt, kbd)


def _gelu_tanh(x):
    return 0.5 * x * (1.0 + jnp.tanh(0.7978845608028654 * (x + 0.044715 * (x * x * x))))


def _peer_main_kernel(ni1, h_ref, x1_ref, g2_ref, u_ref, vt_ref, g_ref, cnt_ref, r2_ref, p2_ref, y_ref, acc, gate_s):
    j = pl.program_id(1)
    tm = h_ref.shape[0]

    @pl.when(j == 0)
    def _():
        acc[...] = jnp.zeros_like(acc)

    cb = r2_ref.shape[-1]
    reps = PEER_NKEYS // 16
    zero = jnp.zeros((PEER_NKEYS, cb), BF16)
    for l in range(ni1):
        for tc in range(tm // cb):
            ts = slice(tc * cb, (tc + 1) * cb)
            w = None
            for hd in range(PEER_HEADS):
                c16 = jnp.broadcast_to(cnt_ref[hd, l:l + 1, ts], (16, cb)).astype(BF16)
                g16 = jnp.broadcast_to(g_ref[hd, l:l + 1, ts], (16, cb)).astype(BF16)
                t = (jnp.where(r2_ref[hd, tc] < jnp.concatenate([c16] * reps, axis=0), p2_ref[hd, tc], zero)
                     * jnp.concatenate([g16] * reps, axis=0))
                w = t if w is None else w + t
            gate_s[tc, l * PEER_NKEYS:(l + 1) * PEER_NKEYS, :] = w

    act = lax.dot_general(u_ref[...], h_ref[...], NT, preferred_element_type=F32)
    gate = jnp.concatenate([gate_s[tc] for tc in range(tm // cb)], axis=1)
    coef = gate * _gelu_tanh(act.astype(BF16))
    acc[...] += jnp.dot(vt_ref[...], coef, preferred_element_type=F32)

    @pl.when(j == pl.num_programs(1) - 1)
    def _():
        y_ref[...] = x1_ref[...] + g2_ref[...] * acc[...].T


def _peer_main(h2, x1, g2, u16, vt16, g, cnt, r2, p2, seq_len):
    n, d = h2.shape
    tm = _row_tile(n, seq_len, 512)
    ni1 = 8
    et = ni1 * PEER_NKEYS
    cb = r2.shape[-1]
    g2_a, g2_s = _seq_operand(g2, seq_len, tm)

    def row():
        return pl.BlockSpec((tm, d), lambda i, j: (i, 0))

    return pl.pallas_call(
        functools.partial(_peer_main_kernel, ni1),
        out_shape=jax.ShapeDtypeStruct((n, d), F32),
        grid=(n // tm, N_EXPERTS // et),
        in_specs=[row(), row(), g2_s,
                  pl.BlockSpec((et, d), lambda i, j: (j, 0)),
                  pl.BlockSpec((d, et), lambda i, j: (0, j)),
                  pl.BlockSpec((PEER_HEADS, ni1, tm), lambda i, j: (0, j, i)),
                  pl.BlockSpec((PEER_HEADS, ni1, tm), lambda i, j: (0, j, i)),
                  pl.BlockSpec((PEER_HEADS, tm // cb, PEER_NKEYS, cb), lambda i, j: (0, i, 0, 0)),
                  pl.BlockSpec((PEER_HEADS, tm // cb, PEER_NKEYS, cb), lambda i, j: (0, i, 0, 0))],
        out_specs=row(),
        scratch_shapes=[pltpu.VMEM((d, tm), F32), pltpu.VMEM((tm // cb, et, cb), BF16)],
        compiler_params=_cparams(("parallel", "arbitrary")),
    )(h2, x1, g2_a, u16, vt16, g, cnt, r2, p2)


def _layer(x, mod, pos, shift_prev, s0, cache, lw):
    nb, t, d = x.shape
    n = nb * t
    sh1, sc1, g1, sh2, sc2, g2 = [mod[:, i * d:(i + 1) * d] for i in range(6)]
    x2 = x.reshape(n, d)
    P = _inproj(x2, sc1, sh1, lw['norm1_w'], lw['w_in16'], t)

    prev = _pack_rw(shift_prev).reshape(nb, 1, P_COLS)
    o_a, zf = _rwkv(P, nb, t, prev, lw['mu'], lw['w0'], lw['a0'], lw['k_k'], lw['k_a'], lw['r_k'], lw['lnx_w'],
                    lw['lnx_b'], lw['wup'], lw['aup'], lw['gup'], _state_to_pairs(s0))
    wkv = _pairs_to_state(zf)
    shift_last = _unpack_rw(P.reshape(nb, t, P_COLS)[:, -1, :])

    q16, k32, k16, v32, v16, qi16, kw32, ki2 = _dsa_prep(P, jnp.tile(pos, nb), lw['q_norm_w'], lw['k_norm_w'])
    if cache is None:
        o_b = _attn_prompt(q16, qi16, kw32, k16, v16, ki2, nb, t)
    else:
        ck, cv, cki = cache
        past = ck.shape[1]
        o_b = _attn_sample(q16, qi16, kw32, k16, v16, ki2, ck.reshape(nb, past, d), cv.reshape(nb, past, d), cki,
                           nb, t)

    x1, h2 = _merge(x2, o_a, o_b, P, lw['b_gate'], g1, sc2, sh2, lw['norm2_w'], lw['wpa'], lw['wpb'], lw['wout'], t)
    g, cnt, r2, p2 = _peer_select(h2, lw['wpqt'], lw['kbd'])
    y = _peer_main(h2, x1, g2, lw['u16'], lw['vt16'], g, cnt, r2, p2, t)

    k_new = k32.reshape(nb, t, N_HEADS, HEAD_DIM)
    v_new = v32.reshape(nb, t, N_HEADS, HEAD_DIM)
    ki_new = kw32[:, :IDX_DIM].reshape(nb, t, IDX_DIM)
    return y.reshape(nb, t, d), wkv, shift_last, k_new, v_new, ki_new


def _layer_weights(l, w_in, b_gate, mu_rw, w0, w_up, a0, a_up, g_up, k_k, k_a, r_k, lnx_w, lnx_b, q_norm_w, k_norm_w,
                   w_proj_a, w_proj_b, w_out, norm1_w, norm2_w, w_pq, peer_keys, peer_u, peer_v):
    d = D_MODEL
    zeros = lambda r: jnp.zeros((r, d), F32)
    keys = peer_keys[l].reshape(2 * PEER_HEADS, PEER_NKEYS, PEER_DHALF)
    eye = jnp.eye(2 * PEER_HEADS, dtype=F32)
    kbd = (eye[:, None, :, None] * keys[:, :, None, :]).reshape(2 * d, d)
    return {
        'w_in16': _pack_in(w_in[l]).astype(BF16), 'b_gate': b_gate[l], 'mu': _pack_rw(mu_rw[l]).reshape(1, P_COLS),
        'w0': w0[l], 'a0': a0[l], 'k_k': k_k[l], 'k_a': k_a[l], 'r_k': r_k[l].reshape(d), 'lnx_w': lnx_w[l],
        'lnx_b': lnx_b[l],
        'wup': jnp.concatenate([w_up[l], zeros(LANES - D_DECAY)], axis=0).astype(BF16),
        'aup': jnp.concatenate([zeros(D_DECAY), a_up[l]], axis=0).astype(BF16),
        'gup': jnp.concatenate([g_up[l], zeros(256 - D_GATE)], axis=0).astype(BF16),
        'q_norm_w': q_norm_w[l], 'k_norm_w': k_norm_w[l], 'norm1_w': norm1_w[l], 'norm2_w': norm2_w[l],
        'wpa': w_proj_a[l].astype(BF16), 'wpb': w_proj_b[l].astype(BF16), 'wout': w_out[l].astype(BF16),
        'wpqt': w_pq[l].T.astype(BF16), 'kbd': kbd.astype(BF16),
        'u16': peer_u[l].astype(BF16), 'vt16': peer_v[l].T.astype(BF16),
    }


def kernel(x_prompt, x_sample, c_prompt, c_sample, cache_k, cache_v, cache_kidx, state_wkv, state_shift, w_ada, b_ada,
           norm1_w, w_in, b_gate, mu_rw, w0, w_up, a0, a_up, g_up, k_k, k_a, r_k, lnx_w, lnx_b, q_norm_w, k_norm_w,
           w_proj_a, w_proj_b, w_out, norm2_w, w_pq, peer_keys, peer_u, peer_v):
    depth = w_in.shape[0]
    bp, tp = x_prompt.shape[:2]
    bs, ts = x_sample.shape[:2]
    past = cache_k.shape[2]
    dt = x_prompt.dtype
    pos_p = jnp.arange(tp, dtype=I32)
    pos_s = past + jnp.arange(ts, dtype=I32)
    zero_shift = jnp.zeros((bp, RW_IN), dt)
    zero_wkv = jnp.zeros((bp, N_HEADS, HEAD_DIM, HEAD_DIM), dt)
    c_all = jnp.concatenate([c_prompt, c_sample], axis=0)
    xp, xs = x_prompt, x_sample
    outs_p, outs_s = [], []
    for l in range(depth):
        lw = _layer_weights(l, w_in, b_gate, mu_rw, w0, w_up, a0, a_up, g_up, k_k, k_a, r_k, lnx_w, lnx_b, q_norm_w,
                            k_norm_w, w_proj_a, w_proj_b, w_out, norm1_w, norm2_w, w_pq, peer_keys, peer_u, peer_v)
        mod = _ada(c_all, w_ada[l], b_ada[l])
        xp, *rest_p = _layer(xp, mod[:bp], pos_p, zero_shift, zero_wkv, None, lw)
        xs, *rest_s = _layer(xs, mod[bp:], pos_s, state_shift[l], state_wkv[l],
                             (cache_k[l], cache_v[l], cache_kidx[l]), lw)
        outs_p.append(rest_p)
        outs_s.append(rest_s)
    stack = lambda outs, i: jnp.stack([o[i] for o in outs])
    return (xp, xs,
            stack(outs_p, 0), stack(outs_p, 1), stack(outs_p, 2), stack(outs_p, 3), stack(outs_p, 4),
            stack(outs_s, 0), stack(outs_s, 1), stack(outs_s, 2), stack(outs_s, 3), stack(outs_s, 4))
```

```python
import functools

import jax
import jax.numpy as jnp
from jax import lax
from jax.experimental import pallas as pl
from jax.experimental.pallas import tpu as pltpu

F32 = jnp.float32
BF16 = jnp.bfloat16
I32 = jnp.int32

LANES = 128
D_MODEL = 1024
EPS = 1e-6
GN_EPS = 64e-5
ROPE_THETA = 10000.0
CHUNK = 64
TOPK_MAX = 256
HEAD_DIM = 64
N_HEADS = D_MODEL // HEAD_DIM
N_PAIRS = N_HEADS // 2
IDX_HEADS = 8
IDX_DIM = 64
D_DECAY = 64
D_AAA = 64
D_GATE = 160
RW_IN = 3 * D_MODEL + D_DECAY + D_AAA + D_GATE
PEER_HEADS = 8
PEER_NKEYS = 128
PEER_TOPK = 16
PEER_DHALF = 64
N_EXPERTS = PEER_NKEYS * PEER_NKEYS
RW_CHUNK = 64
RW_INTERLEAVE = 16
RW_PASSES = (2, 1, 1, 1)
VMEM_LIMIT = 56 * 1024 * 1024
LOG2E = 1.4426950408889634
PEER_CAND = 80
GATE_ROWS = 32

C_R, C_K, C_V = 0, 1024, 2048
C_Q, C_KD, C_VD = 3072, 4096, 5120
C_GA, C_GB = 6144, 7168
C_QI = 8192
C_G = 8704
C_M = 8960
C_KW = 9088
P_COLS = 9216
IN_W = 9064

NT = (((1,), (1,)), ((), ()))
NN = (((1,), (0,)), ((), ()))


def _pack_in(w):
    z = lambda k: jnp.zeros(w.shape[:-1] + (k,), w.dtype)
    return jnp.concatenate([w[..., 0:3072], w[..., 3360:6432], w[..., 7016:9064], w[..., 6432:6944],
                            w[..., 3200:3360], z(256 - D_GATE), w[..., 3072:3200],
                            w[..., 6944:7016], z(LANES - IDX_DIM - IDX_HEADS)], axis=-1)


def _pack_rw(a):
    return _pack_in(jnp.concatenate([a, jnp.zeros(a.shape[:-1] + (IN_W - RW_IN,), a.dtype)], axis=-1))


def _unpack_rw(p):
    return jnp.concatenate([p[..., :3072], p[..., C_M:C_M + 128], p[..., C_G:C_G + D_GATE]], axis=-1)


def _split_bf16(x, n):
    parts = []
    r = x
    for _ in range(n):
        p = r.astype(BF16)
        parts.append(p)
        r = r - p.astype(F32)
    return parts


def _mm(a, b, pa=1, pb=1, dims=NN):
    aps = _split_bf16(a, pa) if a.dtype != BF16 else [a]
    bps = _split_bf16(b, pb) if b.dtype != BF16 else [b]
    order = max(len(aps), len(bps))
    out = None
    for i, ap in enumerate(aps):
        for j, bp in enumerate(bps):
            if i + j >= order:
                continue
            t = lax.dot_general(ap, bp, dims, preferred_element_type=F32)
            out = t if out is None else out + t
    return out


def _sigmoid(x):
    return 1.0 / (1.0 + jnp.exp(-x))


def _softplus(z):
    return jnp.maximum(z, 0.0) + jnp.log(1.0 + jnp.exp(-jnp.abs(z)))


def _cparams(sem):
    return pltpu.CompilerParams(dimension_semantics=sem, vmem_limit_bytes=VMEM_LIMIT)


def _ada_kernel(c_ref, w_ref, b_ref, o_ref):
    c = c_ref[...]
    s = c * _sigmoid(c)
    o_ref[...] = _mm(s, w_ref[...], 2, 2) + b_ref[...]


def _ada(c, w, b):
    m, d = c.shape
    n = w.shape[1]
    tn = 1024
    return pl.pallas_call(
        _ada_kernel,
        out_shape=jax.ShapeDtypeStruct((m, n), F32),
        grid=(n // tn,),
        in_specs=[pl.BlockSpec((m, d), lambda j: (0, 0)),
                  pl.BlockSpec((d, tn), lambda j: (0, j)),
                  pl.BlockSpec((1, tn), lambda j: (0, j))],
        out_specs=pl.BlockSpec((m, tn), lambda j: (0, j)),
        compiler_params=_cparams(("arbitrary",)),
    )(c, w, b.reshape(1, n))


def _seq_operand(vec, seq_len, tm):
    b, d = vec.shape
    if seq_len % tm == 0:
        per = seq_len // tm
        arr = vec.reshape(b, 1, d)
        spec = pl.BlockSpec((None, 1, d), lambda *g: (g[0] // per, 0, 0))
    else:
        assert tm % seq_len == 0
        arr = jnp.repeat(vec, seq_len, axis=0)
        spec = pl.BlockSpec((tm, d), lambda *g: (g[0], 0))
    return arr, spec


def _row_tile(n, seq_len, cap):
    tm = min(cap, n)
    while n % tm or (seq_len % tm and tm % seq_len):
        tm //= 2
    return tm


def _inproj_kernel(x_ref, sc_ref, sh_ref, nw_ref, w_ref, o_ref, h_scr):
    @pl.when(pl.program_id(1) == 0)
    def _():
        x = x_ref[...]
        y = x * lax.rsqrt(jnp.mean(x * x, axis=-1, keepdims=True) + EPS) * nw_ref[...]
        h_scr[...] = (y * (1.0 + sc_ref[...]) + sh_ref[...]).astype(BF16)

    o_ref[...] = jnp.dot(h_scr[...], w_ref[...], preferred_element_type=F32)


def _inproj(x2, sc, sh, nw, w16, seq_len):
    n, d = x2.shape
    tm = _row_tile(n, seq_len, 1024)
    tn = 1024
    sc_a, sc_s = _seq_operand(sc, seq_len, tm)
    sh_a, sh_s = _seq_operand(sh, seq_len, tm)
    return pl.pallas_call(
        _inproj_kernel,
        out_shape=jax.ShapeDtypeStruct((n, P_COLS), F32),
        grid=(n // tm, P_COLS // tn),
        in_specs=[pl.BlockSpec((tm, d), lambda i, j: (i, 0)), sc_s, sh_s,
                  pl.BlockSpec((1, d), lambda i, j: (0, 0)),
                  pl.BlockSpec((d, tn), lambda i, j: (0, j))],
        out_specs=pl.BlockSpec((tm, tn), lambda i, j: (i, j)),
        scratch_shapes=[pltpu.VMEM((tm, d), BF16)],
        compiler_params=_cparams(("parallel", "arbitrary")),
    )(x2, sc_a, sh_a, nw.reshape(1, d), w16)


def _lane_lo(shape):
    return lax.broadcasted_iota(I32, shape, len(shape) - 1) < HEAD_DIM


def _pair_sum(x):
    lo = _lane_lo(x.shape)
    s0 = jnp.sum(jnp.where(lo, x, 0.0), axis=-1, keepdims=True)
    s1 = jnp.sum(jnp.where(lo, 0.0, x), axis=-1, keepdims=True)
    return jnp.where(lo, s0, s1)


def _stack2(x):
    lo = _lane_lo(x.shape)
    return jnp.concatenate([jnp.where(lo, x, 0.0), jnp.where(lo, 0.0, x)], axis=0)


def _rwkv_kernel(t_real, npair, nchunk, pr, pk, pv, pg, pm, sr, sk, sv, sg, sm, mr, mk, mv, mg, mmu,
                 w0, a0, kkw, kaw, rkw, lnw, lnb, wup, aup, gup, z0, o_ref, zf_ref,
                 r_s, lw_s, k_s, v_s, a_s, b_s, y_s, bonus_s, g_s):
    C = RW_CHUNK
    t_pad = r_s.shape[1]

    def mix(p_ref, s_ref, m_ref):
        p = p_ref[...]
        prev = pltpu.roll(p, 1, 0)
        row = lax.broadcasted_iota(I32, p.shape, 0)
        prev = jnp.where(row == 0, s_ref[...], prev)
        return p + (prev - p) * m_ref[...]

    xg, xm = mix(pg, sg, mg), mix(pm, sm, mmu)
    th16, xm16, sg16 = jnp.tanh(xm).astype(BF16), xm.astype(BF16), _sigmoid(xg).astype(BF16)
    xr_all, xk_all, xv_all = mix(pr, sr, mr), mix(pk, sk, mk), mix(pv, sv, mv)

    def put(ref, pp, val):
        if t_pad > t_real:
            val = jnp.concatenate([val, jnp.zeros((t_pad - t_real, LANES), F32)], axis=0)
        ref[pp] = val

    for pp in range(npair):
        cs = slice(pp * LANES, (pp + 1) * LANES)
        xr, xk, xv = xr_all[:, cs], xk_all[:, cs], xv_all[:, cs]
        dw = jnp.dot(th16, wup[:, cs], preferred_element_type=F32)
        lw = -jnp.exp(-_softplus(-(w0[:, cs] + dw)) - 0.5)
        asig = _sigmoid(a0[:, cs] + jnp.dot(xm16, aup[:, cs], preferred_element_type=F32))
        g_s[pp] = jnp.dot(sg16, gup[:, cs], preferred_element_type=F32)
        kk = xk * kkw[:, cs]
        kk = kk * lax.rsqrt(_pair_sum(kk * kk) + 1e-12)
        kmod = xk * (1.0 + (asig - 1.0) * kaw[:, cs])
        bonus_s[pp] = _pair_sum(xr * kmod * rkw[:, cs]) * xv
        put(r_s, pp, xr)
        put(lw_s, pp, lw)
        put(k_s, pp, kmod)
        put(v_s, pp, xv)
        put(a_s, pp, -kk)
        put(b_s, pp, kk * asig)

    n2 = 2 * C
    ri = lax.broadcasted_iota(I32, (n2, n2), 0)
    ci = lax.broadcasted_iota(I32, (n2, n2), 1)
    same = (ri // C) == (ci // C)
    strict = same & ((ri % C) > (ci % C))
    incl = same & ((ri % C) >= (ci % C))
    eye = ri == ci
    eye_f = jnp.where(eye, 1.0, 0.0)
    tri = jnp.where(lax.broadcasted_iota(I32, (C, C), 0) >= lax.broadcasted_iota(I32, (C, C), 1), 1.0, 0.0
                    ).astype(BF16)
    zeros_sq = jnp.zeros((n2, LANES), F32)

    pg_, pi_, po_, ps_ = RW_PASSES

    def local(chains):
        each = lambda f, *cols: [f(*xs) for xs in zip(*cols)]
        lwc = [lw_s[pp, sl, :] for sl, pp in chains]
        cum = each(lambda l: _mm(tri, l, 1, 3), lwc)
        cum_last = each(lambda c: c[C - 1:C, :], cum)
        ec, eci = each(jnp.exp, cum), each(lambda c: jnp.exp(-c), cum)
        ecp = each(lambda c, l: jnp.exp(c - l), cum, lwc)
        ecl = each(lambda c, cl: jnp.exp(cl - c), cum, cum_last)
        a_c = [a_s[pp, sl, :] for sl, pp in chains]
        b_c = [b_s[pp, sl, :] for sl, pp in chains]
        k_c = [k_s[pp, sl, :] for sl, pp in chains]
        r_c = [r_s[pp, sl, :] for sl, pp in chains]
        As = each(lambda a, e: _stack2(a * e), a_c, ecp)
        Rs = each(lambda r, e: _stack2(r * e), r_c, ec)
        Bs = each(lambda b, e: _stack2(b * e), b_c, eci)
        Ks = each(lambda k, e: _stack2(k * e), k_c, eci)
        Bt = each(lambda b, e: _stack2(b * e), b_c, ecl)
        Kt = each(lambda k, e: _stack2(k * e), k_c, ecl)
        Vs = [_stack2(v_s[pp, sl, :]) for sl, pp in chains]

        G = each(lambda a, r, b, k: _mm(jnp.concatenate([a, r], axis=0), jnp.concatenate([b, k], axis=0),
                                        pg_, pg_, NT), As, Rs, Bs, Ks)
        a_ab = each(lambda g: jnp.where(strict, g[:n2, :n2], 0.0), G)
        a_ak = each(lambda g: jnp.where(strict, g[:n2, n2:], 0.0), G)
        a_rb = each(lambda g: jnp.where(incl, g[n2:, :n2], 0.0), G)
        a_rk = each(lambda g: jnp.where(incl, g[n2:, n2:], 0.0), G)

        lp = a_ab
        tm_ = each(lambda a: eye_f + a, a_ab)
        step = 2
        while step < C:
            lp = each(lambda l: _mm(l, l, pi_, pi_), lp)
            tm_ = each(lambda t, l: t + _mm(t, l, pi_, pi_), tm_, lp)
            step *= 2

        w1 = each(lambda a, v: _mm(a, v, po_, po_), a_ak, Vs)
        mu_ = each(lambda t, a, w: _mm(t, jnp.concatenate([a, w], axis=1), po_, po_), tm_, As, w1)
        rhs = each(lambda m, v: jnp.concatenate([m, jnp.concatenate([zeros_sq, v], axis=1)], axis=0), mu_, Vs)
        lhs = each(lambda rb, rk, b, k: jnp.concatenate([jnp.concatenate([rb, rk], axis=1),
                                                         jnp.concatenate([b.T, k.T], axis=1)], axis=0),
                   a_rb, a_rk, Bt, Kt)
        out2 = each(lambda l, r: _mm(l, r, po_, po_), lhs, rhs)
        m23 = each(lambda r, o, cl: jnp.concatenate([r + o[:n2, :LANES],
                                                     jnp.where(eye, jnp.exp(cl), 0.0) + o[n2:, :LANES]], axis=0),
                   Rs, out2, cum_last)
        return [(m, o[:n2, LANES:], o[n2:, LANES:]) for m, o in zip(m23, out2)]

    def step_chunks(i, zs):
        sls = [pl.ds(pl.multiple_of((i * nchunk + j) * C, C), C) for j in range(nchunk)]
        parts = local([(sl, pp) for sl in sls for pp in range(npair)])
        zs = list(zs)
        for j, sl in enumerate(sls):
            for pp in range(npair):
                m23, y_loc, z_loc = parts[j * npair + pp]
                yz = _mm(m23, zs[pp], ps_, ps_)
                y = yz[:n2] + y_loc
                y_s[pp, sl, :] = y[:C] + y[C:]
                zs[pp] = yz[n2:] + z_loc
        return tuple(zs)

    zs = lax.fori_loop(0, t_pad // (C * nchunk), step_chunks, tuple(z0[pp] for pp in range(npair)))
    for pp in range(npair):
        zf_ref[pp] = zs[pp]
        cs = slice(pp * LANES, (pp + 1) * LANES)
        y = y_s[pp, 0:t_real, :]
        mean = _pair_sum(y) * (1.0 / HEAD_DIM)
        dlt = y - mean
        var = _pair_sum(dlt * dlt) * (1.0 / HEAD_DIM)
        yn = dlt * lax.rsqrt(var + GN_EPS) * lnw[:, cs] + lnb[:, cs]
        o_ref[:, cs] = ((yn + bonus_s[pp]) * g_s[pp]).astype(o_ref.dtype)


def _rwkv(P, nb, t, prev, mu, w0, a0, k_k, k_a, r_k, lnx_w, lnx_b, wup, aup, gup, z0):
    t_pad = max(t, RW_CHUNK)
    assert t % 8 == 0 and t_pad % RW_CHUNK == 0
    n_chunks = t_pad // RW_CHUNK
    nchunk = min(RW_INTERLEAVE, n_chunks)
    npair = min(N_PAIRS, max(1, RW_INTERLEAVE // nchunk))
    wp = npair * LANES

    def cblk(c0, w, per_pair):
        return (lambda p: c0 // w + p) if per_pair else (lambda p: c0 // w)

    def pcol(c0, w, pp):
        f = cblk(c0, w, pp)
        return pl.BlockSpec((t, w), lambda b, p: (b, f(p)))

    def prevcol(c0, w, pp):
        f = cblk(c0, w, pp)
        return pl.BlockSpec((None, 1, w), lambda b, p: (b, 0, f(p)))

    def mucol(c0, w, pp):
        f = cblk(c0, w, pp)
        return pl.BlockSpec((1, w), lambda b, p: (0, f(p)))

    def hvec():
        return pl.BlockSpec((1, wp), lambda b, p: (0, p))

    cols = [(C_R, wp, True), (C_K, wp, True), (C_V, wp, True), (C_G, 256, False), (C_M, LANES, False)]
    in_specs = ([pcol(*c) for c in cols] + [prevcol(*c) for c in cols] + [mucol(*c) for c in cols]
                + [hvec() for _ in range(7)]
                + [pl.BlockSpec((LANES, wp), lambda b, p: (0, p)),
                   pl.BlockSpec((LANES, wp), lambda b, p: (0, p)),
                   pl.BlockSpec((256, wp), lambda b, p: (0, p)),
                   pl.BlockSpec((None, npair, LANES, LANES), lambda b, p: (b, p, 0, 0))])
    vecs = [v.reshape(1, D_MODEL) for v in (w0, a0, k_k, k_a, r_k, lnx_w, lnx_b)]
    o, zf = pl.pallas_call(
        functools.partial(_rwkv_kernel, t, npair, nchunk),
        out_shape=(jax.ShapeDtypeStruct((nb * t, D_MODEL), BF16),
                   jax.ShapeDtypeStruct((nb, N_PAIRS, LANES, LANES), F32)),
        grid=(nb, N_PAIRS // npair),
        in_specs=in_specs,
        out_specs=(pl.BlockSpec((t, wp), lambda b, p: (b, p)),
                   pl.BlockSpec((None, npair, LANES, LANES), lambda b, p: (b, p, 0, 0))),
        scratch_shapes=([pltpu.VMEM((npair, t_pad, LANES), F32) for _ in range(7)]
                        + [pltpu.VMEM((npair, t, LANES), F32) for _ in range(2)]),
        compiler_params=_cparams(("parallel", "arbitrary")),
    )(P, P, P, P, P, prev, prev, prev, prev, prev, mu, mu, mu, mu, mu, *vecs, wup, aup, gup, z0)
    return o, zf


def _state_to_pairs(s):
    nb = s.shape[0]
    zt = jnp.swapaxes(s, -1, -2).reshape(nb, N_PAIRS, 2, HEAD_DIM, HEAD_DIM)
    zero = jnp.zeros_like(zt[:, :, 0])
    top = jnp.concatenate([zt[:, :, 0], zero], axis=-1)
    bot = jnp.concatenate([zero, zt[:, :, 1]], axis=-1)
    return jnp.concatenate([top, bot], axis=-2)


def _pairs_to_state(z):
    nb = z.shape[0]
    h0 = z[:, :, :HEAD_DIM, :HEAD_DIM]
    h1 = z[:, :, HEAD_DIM:, HEAD_DIM:]
    s = jnp.stack([h0, h1], axis=2).reshape(nb, N_HEADS, HEAD_DIM, HEAD_DIM)
    return jnp.swapaxes(s, -1, -2)


def _rope(x, cos, sin_signed):
    w = x.shape[1]
    reps = w // LANES
    cw = jnp.concatenate([cos] * reps, axis=1) if reps > 1 else cos
    sw = jnp.concatenate([sin_signed] * reps, axis=1) if reps > 1 else sin_signed
    lane = lax.broadcasted_iota(I32, x.shape, 1)
    fwd = pltpu.roll(x, w - 32, 1)
    bwd = pltpu.roll(x, 32, 1)
    partner = jnp.where((lane % HEAD_DIM) < 32, fwd, bwd)
    return x * cw + partner * sw


def _head_rms(x, nw, e_dn, e_up):
    ms = _mm(x * x, e_dn, 2, 1) * (1.0 / HEAD_DIM)
    r = lax.rsqrt(ms + EPS)
    return x * _mm(r, e_up, 2, 1) * nw


def _dsa_prep_kernel(pq, pkd, pvd, pqi, pkw, cos_ref, sin_ref, qn, kn, edn, eup,
                     q16, k32, k16, v32, v16, qi16, kw32, ki2):
    cos, sin = cos_ref[...], sin_ref[...]
    e_dn, e_up = edn[...], eup[...]
    def put_pairs(ref, x):
        for p in range(N_PAIRS):
            ref[p] = x[:, p * LANES:(p + 1) * LANES].astype(ref.dtype)

    q = _rope(_head_rms(pq[...], qn[...], e_dn, e_up), cos, sin)
    put_pairs(q16, q * (HEAD_DIM ** -0.5 * LOG2E))
    k = _rope(_head_rms(pkd[...], kn[...], e_dn, e_up), cos, sin)
    k32[...] = k
    put_pairs(k16, k)
    v = pvd[...]
    v32[...] = v
    put_pairs(v16, v)
    qi16[...] = _rope(pqi[...], cos, sin).astype(BF16)
    kw = pkw[...]
    lane = lax.broadcasted_iota(I32, kw.shape, 1)
    wi_scale = (IDX_HEADS * IDX_DIM) ** -0.5
    kr = _rope(kw, cos, sin)
    kw32[...] = jnp.where(lane < IDX_DIM, kr, jnp.where(lane < IDX_DIM + IDX_HEADS, kw * wi_scale, 0.0))
    ki2[...] = jnp.where(lane < IDX_DIM, kr, pltpu.roll(kr, IDX_DIM, 1)).astype(BF16)


def _dsa_prep(P, pos_rows, q_norm_w, k_norm_w):
    n = P.shape[0]
    tm = 512 if n % 512 == 0 else n
    half = HEAD_DIM // 2
    inv = ROPE_THETA ** (-jnp.arange(half, dtype=F32) / half)
    ang = pos_rows.astype(F32)[:, None] * inv[None, :]
    cos = jnp.tile(jnp.cos(ang), (1, 4))
    sin = jnp.sin(ang)
    sin_signed = jnp.tile(jnp.concatenate([-sin, sin], axis=1), (1, 2))
    head_of = jnp.arange(D_MODEL) // HEAD_DIM
    e_dn = (head_of[:, None] == jnp.arange(LANES)[None, :]).astype(BF16)
    e_up = e_dn.T
    qn = jnp.tile(q_norm_w, N_HEADS).reshape(1, D_MODEL)
    kn = jnp.tile(k_norm_w, N_HEADS).reshape(1, D_MODEL)

    def col(c0, w):
        return pl.BlockSpec((tm, w), lambda i, c0=c0, w=w: (i, c0 // w))

    def row(w):
        return pl.BlockSpec((tm, w), lambda i: (i, 0))

    def const(shape):
        return pl.BlockSpec(shape, lambda i: (0, 0))

    pairs = jax.ShapeDtypeStruct((N_PAIRS, n, LANES), BF16)
    pair_spec = pl.BlockSpec((N_PAIRS, tm, LANES), lambda i: (0, i, 0))
    return pl.pallas_call(
        _dsa_prep_kernel,
        out_shape=(pairs, jax.ShapeDtypeStruct((n, D_MODEL), F32), pairs, jax.ShapeDtypeStruct((n, D_MODEL), F32),
                   pairs, jax.ShapeDtypeStruct((n, IDX_HEADS * IDX_DIM), BF16),
                   jax.ShapeDtypeStruct((n, LANES), F32), jax.ShapeDtypeStruct((n, LANES), BF16)),
        grid=(n // tm,),
        in_specs=[col(C_Q, 1024), col(C_KD, 1024), col(C_VD, 1024), col(C_QI, 512), col(C_KW, LANES),
                  row(LANES), row(LANES), const((1, D_MODEL)), const((1, D_MODEL)),
                  const((D_MODEL, LANES)), const((LANES, D_MODEL))],
        out_specs=(pair_spec, row(D_MODEL), pair_spec, row(D_MODEL), pair_spec, row(512), row(LANES), row(LANES)),
        compiler_params=_cparams(("parallel",)),
    )(P, P, P, P, P, cos, sin_signed, qn, kn, e_dn, e_up)


INT_MIN = -2 ** 31


def _sortable_key(score, admissible):
    bits = lax.bitcast_convert_type(score + 0.0, I32)
    key = jnp.where(bits < 0, bits ^ jnp.int32(0x7FFFFFFF), bits)
    return jnp.where(admissible, key, jnp.int32(INT_MIN))


def _index_scores(qi, wi, ki_list):
    outs = []
    for ki in ki_list:
        acc = None
        for h in range(IDX_HEADS):
            qpair = qi[:, (h // 2) * LANES:(h // 2 + 1) * LANES]
            lo = _lane_lo(qpair.shape)
            qh = jnp.where(lo if h % 2 == 0 else jnp.logical_not(lo), qpair, jnp.zeros_like(qpair))
            rel = lax.dot_general(qh, ki, NT, preferred_element_type=F32)
            term = wi[:, IDX_DIM + h:IDX_DIM + h + 1] * jnp.maximum(rel, 0.0)
            acc = term if acc is None else acc + term
        outs.append(acc)
    return outs


def _select_topk(keys, topk, bias_refs):
    tq = keys[0].shape[0]
    int_min = jnp.int32(INT_MIN)

    def write(masks):
        for ref, k, msk in zip(bias_refs, keys, masks):
            ref[:, 0:k.shape[1]] = jnp.where(msk, 0.0, -jnp.inf)

    def count(pred_list):
        tot = None
        for p in pred_list:
            c = jnp.sum(jnp.where(p, 1.0, 0.0), axis=-1, keepdims=True)
            tot = c if tot is None else tot + c
        return tot

    few = count([k != int_min for k in keys]) <= topk

    def pending(cnt):
        return jnp.max(jnp.where(few | (cnt == topk), 0.0, 1.0))

    def bit_step(state):
        i, c, cnt, _ = state
        trial = c + jnp.left_shift(jnp.int32(1), 31 - i)
        cnt_t = count([k >= trial for k in keys])
        take = cnt_t >= topk
        cnt = jnp.where(take, cnt_t, cnt)
        return i + 1, jnp.where(take, trial, c), cnt, pending(cnt)

    cnt0 = jnp.full((tq, 1), float(sum(k.shape[1] for k in keys)), F32)
    _, thr, _, _ = lax.while_loop(lambda s: (s[0] < 32) & (s[3] > 0.0), bit_step,
                                  (jnp.int32(0), jnp.full((tq, 1), INT_MIN, I32), cnt0, pending(cnt0)))
    ge = [(k >= thr) & (k != int_min) for k in keys]
    write(ge)
    surplus = jnp.max(count(ge)) > topk

    @pl.when(surplus)
    def _():
        gt = [k > thr for k in keys]
        need = topk - count(gt)
        ties = [(k == thr) & (k != int_min) for k in keys]
        offs, idx = 0, []
        for k in keys:
            idx.append(lax.broadcasted_iota(I32, k.shape, 1) + offs)
            offs += k.shape[1]
        nbits = max(1, (offs - 1).bit_length() + 1)

        def idx_step(i, m):
            trial = m + jnp.left_shift(jnp.int32(1), nbits - 1 - i)
            cnt = count([t & (ix < trial) for t, ix in zip(ties, idx)])
            return jnp.where(cnt <= need, trial, m)

        cut = lax.fori_loop(0, nbits, idx_step, jnp.zeros((tq, 1), I32))
        write([g | (t & (ix < cut)) for g, t, ix in zip(gt, ties, idx)])


def _attend_pair(q_pair, k_list, v_list, bias_list):
    lo = _lane_lo(q_pair.shape)
    zero = jnp.zeros_like(q_pair)
    outs = []
    for qh in (jnp.where(lo, q_pair, zero), jnp.where(lo, zero, q_pair)):
        s_list = [lax.dot_general(qh, k, NT, preferred_element_type=F32) + b for k, b in zip(k_list, bias_list)]
        m = None
        for s in s_list:
            mx = jnp.max(s, axis=-1, keepdims=True)
            m = mx if m is None else jnp.maximum(m, mx)
        den, acc = None, None
        for s, v in zip(s_list, v_list):
            p = jnp.exp2(s - m)
            d = jnp.sum(p, axis=-1, keepdims=True)
            o = jnp.dot(p.astype(BF16), v, preferred_element_type=F32)
            den = d if den is None else den + d
            acc = o if acc is None else acc + o
        outs.append(acc / den)
    return jnp.where(_lane_lo(outs[0].shape), outs[0], outs[1])


def _attn_prompt_kernel(topk, ncase, q_ref, qi_ref, kw_ref, k_ref, v_ref, ki2_ref, o_ref, bias_s):
    tq = q_ref.shape[1]
    t = k_ref.shape[1]
    i = pl.program_id(1)
    lstep = t // ncase
    case = ((i + 1) * tq - 1) // lstep

    def run(L):
        score = _index_scores(qi_ref[...], kw_ref[...], [ki2_ref[0:L, :]])[0]
        qpos = i * tq + lax.broadcasted_iota(I32, (tq, L), 0)
        kpos = lax.broadcasted_iota(I32, (tq, L), 1)
        adm = (qpos // CHUNK) >= (kpos // CHUNK)
        _select_topk([_sortable_key(score, adm)], topk, [bias_s])

        def pair(p, carry):
            o = _attend_pair(q_ref[p], [k_ref[p, 0:L, :]], [v_ref[p, 0:L, :]], [bias_s[:, 0:L]])
            o_ref[p] = o.astype(o_ref.dtype)
            return carry

        lax.fori_loop(0, N_PAIRS, pair, 0)

    for c in range(ncase):
        pl.when(case == c)(functools.partial(run, (c + 1) * lstep))


def _attn_prompt(q16, qi16, kw32, k16, v16, ki2, nb, t):
    tq = min(256, t)
    topk = min(TOPK_MAX, t // 4)
    nq = t // tq
    ncase = min(4, nq)

    def qrow(w):
        return pl.BlockSpec((tq, w), lambda b, i: (b * nq + i, 0))

    def qpairs():
        return pl.BlockSpec((N_PAIRS, tq, LANES), lambda b, i: (0, b * nq + i, 0))

    def kpairs():
        return pl.BlockSpec((N_PAIRS, t, LANES), lambda b, i: (0, b, 0))

    return pl.pallas_call(
        functools.partial(_attn_prompt_kernel, topk, ncase),
        out_shape=jax.ShapeDtypeStruct((N_PAIRS, nb * t, LANES), BF16),
        grid=(nb, nq),
        in_specs=[qpairs(), qrow(512), qrow(LANES), kpairs(), kpairs(),
                  pl.BlockSpec((t, LANES), lambda b, i: (b, 0))],
        out_specs=qpairs(),
        scratch_shapes=[pltpu.VMEM((tq, t), F32)],
        compiler_params=_cparams(("parallel", "arbitrary")),
    )(q16, qi16, kw32, k16, v16, ki2)


def _attn_sample_kernel(topk, past, q_ref, qi_ref, kw_ref, ck_ref, cv_ref, cki2_ref, k_ref, v_ref, ki2_ref, o_ref,
                        biasc_s, biasn_s):
    ts = q_ref.shape[0]

    @pl.when(pl.program_id(1) == 0)
    def _():
        sc, sn = _index_scores(qi_ref[...], kw_ref[...], [cki2_ref[...], ki2_ref[...]])
        qpos = past + lax.broadcasted_iota(I32, (ts, 1), 0)
        kpos_c = lax.broadcasted_iota(I32, sc.shape, 1)
        kpos_n = past + lax.broadcasted_iota(I32, sn.shape, 1)
        keys = [_sortable_key(sc, (qpos // CHUNK) >= (kpos_c // CHUNK)),
                _sortable_key(sn, (qpos // CHUNK) >= (kpos_n // CHUNK))]
        _select_topk(keys, topk, [biasc_s, biasn_s])

    o_ref[...] = _attend_pair(q_ref[...], [ck_ref[...].astype(BF16), k_ref[...]],
                              [cv_ref[...].astype(BF16), v_ref[...]],
                              [biasc_s[...], biasn_s[...]]).astype(o_ref.dtype)


def _attn_sample(q16, qi16, kw32, k16, v16, ki2, cache_k, cache_v, cache_kidx, nb, ts):
    past = cache_k.shape[1]
    cki2 = jnp.concatenate([cache_kidx, cache_kidx], axis=-1).astype(BF16)
    topk = min(TOPK_MAX, (past + ts) // 4)

    def qrow(w):
        return pl.BlockSpec((ts, w), lambda b, p: (b, 0))

    def qpair():
        return pl.BlockSpec((None, ts, LANES), lambda b, p: (p, b, 0))

    def cache(pair):
        return pl.BlockSpec((None, past, LANES), (lambda b, p: (b, 0, p)) if pair else (lambda b, p: (b, 0, 0)))

    return pl.pallas_call(
        functools.partial(_attn_sample_kernel, topk, past),
        out_shape=jax.ShapeDtypeStruct((N_PAIRS, nb * ts, LANES), BF16),
        grid=(nb, N_PAIRS),
        in_specs=[qpair(), qrow(512), qrow(LANES), cache(True), cache(True), cache(False),
                  qpair(), qpair(), qrow(LANES)],
        out_specs=qpair(),
        scratch_shapes=[pltpu.VMEM((ts, past), F32), pltpu.VMEM((ts, ts), F32)],
        compiler_params=_cparams(("parallel", "arbitrary")),
    )(q16, qi16, kw32, cache_k, cache_v, cki2, k16, v16, ki2)


def _merge_kernel(x_ref, oa_ref, ob_ref, pga_ref, pgb_ref, bga_ref, bgb_ref, g1_ref, sc2_ref, sh2_ref, nw_ref,
                  wpa_ref, wpb_ref, wout_ref, x1_ref, h2_ref):
    ga = _sigmoid(pga_ref[...] + bga_ref[...])
    gb = _sigmoid(pgb_ref[...] + bgb_ref[...])
    ob = jnp.concatenate([ob_ref[p] for p in range(N_PAIRS)], axis=1)
    m = (ga * jnp.dot(oa_ref[...], wpa_ref[...], preferred_element_type=F32)
         + gb * jnp.dot(ob, wpb_ref[...], preferred_element_type=F32))
    x1 = x_ref[...] + g1_ref[...] * jnp.dot(m.astype(BF16), wout_ref[...], preferred_element_type=F32)
    x1_ref[...] = x1
    y = x1 * lax.rsqrt(jnp.mean(x1 * x1, axis=-1, keepdims=True) + EPS) * nw_ref[...]
    h2_ref[...] = (y * (1.0 + sc2_ref[...]) + sh2_ref[...]).astype(BF16)


def _merge(x2, o_a, o_b, P, b_gate, g1, sc2, sh2, nw2, wpa, wpb, wout, seq_len):
    n, d = x2.shape
    tm = _row_tile(n, seq_len, 512)
    g1_a, g1_s = _seq_operand(g1, seq_len, tm)
    sc_a, sc_s = _seq_operand(sc2, seq_len, tm)
    sh_a, sh_s = _seq_operand(sh2, seq_len, tm)

    def row():
        return pl.BlockSpec((tm, d), lambda i: (i, 0))

    def const(shape):
        return pl.BlockSpec(shape, lambda i: (0, 0))

    bg = b_gate.reshape(1, 2 * d)
    return pl.pallas_call(
        _merge_kernel,
        out_shape=(jax.ShapeDtypeStruct((n, d), F32), jax.ShapeDtypeStruct((n, d), BF16)),
        grid=(n // tm,),
        in_specs=[row(), row(), pl.BlockSpec((N_PAIRS, tm, LANES), lambda i: (0, i, 0)),
                  pl.BlockSpec((tm, d), lambda i: (i, C_GA // d)), pl.BlockSpec((tm, d), lambda i: (i, C_GB // d)),
                  pl.BlockSpec((1, d), lambda i: (0, 0)), pl.BlockSpec((1, d), lambda i: (0, 1)),
                  g1_s, sc_s, sh_s, const((1, d)), const((d, d)), const((d, d)), const((d, d))],
        out_specs=(row(), row()),
        compiler_params=_cparams(("parallel",)),
    )(x2, o_a, o_b, P, P, bg, bg, g1_a, sc_a, sh_a, nw2.reshape(1, d), wpa, wpb, wout)


def _top_exact(s, k):
    rows = lax.broadcasted_iota(I32, s.shape, 0).astype(F32)
    cur = s
    rank = jnp.full(s.shape, float(k), F32)
    vals = []
    for r in range(k):
        m = jnp.max(cur, axis=0, keepdims=True)
        first = jnp.min(jnp.where(cur == m, rows, 1e9), axis=0, keepdims=True)
        hit = rows == first
        vals.append(m)
        rank = jnp.where(hit, float(r), rank)
        cur = jnp.where(hit, -jnp.inf, cur)
    return vals, rank


def _top_fast(ss, k):
    curs = list(ss)
    ranks = [jnp.full(s.shape, float(k), F32) for s in ss]
    vals = [[] for _ in ss]
    for r in range(k):
        ms = [jnp.max(c, axis=0, keepdims=True) for c in curs]
        hits = [c == m for c, m in zip(curs, ms)]
        ranks = [jnp.where(h, float(r), rk) for h, rk in zip(hits, ranks)]
        curs = [jnp.where(h, -jnp.inf, c) for h, c in zip(hits, curs)]
        for v, m in zip(vals, ms):
            v.append(m)
    cleans = [jnp.max(jnp.abs(jnp.sum(jnp.where(rk < k, 1.0, 0.0), axis=0, keepdims=True) - k)) == 0.0
              for rk in ranks]
    return vals, ranks, cleans


def _top(ss, k, vals_scr, rank_scr):
    vals, ranks, cleans = _top_fast(ss, k)
    for i, s in enumerate(ss):
        vals_scr[i] = jnp.concatenate(vals[i], axis=0)
        rank_scr[i] = ranks[i]

        @pl.when(jnp.logical_not(cleans[i]))
        def _(i=i, s=s):
            vals_e, rank_e = _top_exact(s, k)
            vals_scr[i] = jnp.concatenate(vals_e, axis=0)
            rank_scr[i] = rank_e


def _peer_sel_kernel(h_ref, wpqt_ref, kbd_ref, g_ref, cnt_ref, r2_ref, p2_ref, s_scr, vals_scr, rank_scr,
                     cand_scr, cvals_scr, crank_scr):
    K = PEER_TOPK
    tm = h_ref.shape[0]
    qt = lax.dot_general(wpqt_ref[...], h_ref[...], NT, preferred_element_type=F32)
    s_scr[...] = jnp.dot(kbd_ref[...], qt.astype(BF16), preferred_element_type=F32
                         ).reshape(2 * PEER_HEADS, PEER_NKEYS, tm)
    sub8 = lax.broadcasted_iota(I32, (8, tm), 0)
    neg = jnp.full((8, tm), -jnp.inf, F32)
    _top([s_scr[r] for r in range(2 * PEER_HEADS)], K, vals_scr, rank_scr)
    for hd in range(PEER_HEADS):
        c1, c2 = vals_scr[2 * hd], vals_scr[2 * hd + 1]
        blocks = [c1[0:1] + c2, c1[1:2] + c2[0:8]]
        for k1 in range(2, 8):
            blocks.append(jnp.where(sub8 < K // (k1 + 1), c1[k1:k1 + 1] + c2[0:8], neg))
        blocks.append(c1[8:16] + c2[0:1])
        cand_scr[hd] = jnp.concatenate(blocks, axis=0)
    _top([cand_scr[hd] for hd in range(PEER_HEADS)], K, cvals_scr, crank_scr)
    for hd in range(PEER_HEADS):
        s1, s2 = s_scr[2 * hd], s_scr[2 * hd + 1]
        c1, c2 = vals_scr[2 * hd], vals_scr[2 * hd + 1]
        rank1, rank2 = rank_scr[2 * hd], rank_scr[2 * hd + 1]
        cand = cand_scr[hd]
        taken = crank_scr[hd] < K
        z = jnp.sum(jnp.where(taken, jnp.exp(cand - (c1[0:1] + c2[0:1])), 0.0), axis=0, keepdims=True)
        tk = jnp.where(taken, 1.0, 0.0)
        per_k1 = [jnp.sum(tk[0:16], axis=0, keepdims=True)]
        per_k1 += [jnp.sum(tk[8 + 8 * k1:16 + 8 * k1], axis=0, keepdims=True) for k1 in range(1, 8)]
        cnt16 = jnp.concatenate(per_k1 + [tk[72:80]], axis=0)
        cnt = jnp.zeros(s1.shape, F32)
        for k1 in range(K):
            cnt = jnp.where(rank1 == float(k1), cnt16[k1:k1 + 1], cnt)
        g_ref[hd] = jnp.where(rank1 < K, jnp.exp(s1 - c1[0:1]) / z, 0.0)
        cnt_ref[hd] = cnt
        p2 = jnp.where(rank2 < K, jnp.exp(s2 - c2[0:1]), 0.0)
        cb = r2_ref.shape[-1]
        for tc in range(tm // cb):
            r2_ref[hd, tc] = rank2[:, tc * cb:(tc + 1) * cb].astype(r2_ref.dtype)
            p2_ref[hd, tc] = p2[:, tc * cb:(tc + 1) * cb].astype(p2_ref.dtype)


def _peer_select(h2, wpqt, kbd):
    n, d = h2.shape
    tm = 256 if n % 256 == 0 else n
    cb = min(LANES, tm)
    big = jax.ShapeDtypeStruct((PEER_HEADS, PEER_NKEYS, n), F32)
    blocked = jax.ShapeDtypeStruct((PEER_HEADS, n // cb, PEER_NKEYS, cb), BF16)

    def blk():
        return pl.BlockSpec((PEER_HEADS, PEER_NKEYS, tm), lambda i: (0, 0, i))

    def blk4():
        return pl.BlockSpec((PEER_HEADS, tm // cb, PEER_NKEYS, cb), lambda i: (0, i, 0, 0))

    return pl.pallas_call(
        _peer_sel_kernel,
        out_shape=(big, big, blocked, blocked),
        grid=(n // tm,),
        in_specs=[pl.BlockSpec((tm, d), lambda i: (i, 0)),
                  pl.BlockSpec((d, d), lambda i: (0, 0)),
                  pl.BlockSpec((2 * d, d), lambda i: (0, 0))],
        out_specs=(blk(), blk(), blk4(), blk4()),
        scratch_shapes=[pltpu.VMEM((2 * PEER_HEADS, PEER_NKEYS, tm), F32),
                        pltpu.VMEM((2 * PEER_HEADS, PEER_TOPK, tm), F32),
                        pltpu.VMEM((2 * PEER_HEADS, PEER_NKEYS, tm), F32),
                        pltpu.VMEM((PEER_HEADS, PEER_CAND, tm), F32),
                        pltpu.VMEM((PEER_HEADS, PEER_TOPK, tm), F32),
                        pltpu.VMEM((PEER_HEADS, PEER_CAND, tm), F32)],
        compiler_params=_cparams(("parallel",)),
    )(h2, wpqt, kbd)


def _gelu_tanh(x):
    return 0.5 * x * (1.0 + jnp.tanh(0.7978845608028654 * (x + 0.044715 * (x * x * x))))


def _peer_main_kernel(ni1, h_ref, x1_ref, g2_ref, u_ref, vt_ref, g_ref, cnt_ref, r2_ref, p2_ref, y_ref, acc, gate_s):
    j = pl.program_id(1)
    tm = h_ref.shape[0]

    @pl.when(j == 0)
    def _():
        acc[...] = jnp.zeros_like(acc)

    cb = r2_ref.shape[-1]
    reps = PEER_NKEYS // 16
    zero = jnp.zeros((PEER_NKEYS, cb), BF16)
    for l in range(ni1):
        for tc in range(tm // cb):
            ts = slice(tc * cb, (tc + 1) * cb)
            w = None
            for hd in range(PEER_HEADS):
                c16 = jnp.broadcast_to(cnt_ref[hd, l:l + 1, ts], (16, cb)).astype(BF16)
                g16 = jnp.broadcast_to(g_ref[hd, l:l + 1, ts], (16, cb)).astype(BF16)
                t = (jnp.where(r2_ref[hd, tc] < jnp.concatenate([c16] * reps, axis=0), p2_ref[hd, tc], zero)
                     * jnp.concatenate([g16] * reps, axis=0))
                w = t if w is None else w + t
            gate_s[tc, l * PEER_NKEYS:(l + 1) * PEER_NKEYS, :] = w

    act = lax.dot_general(u_ref[...], h_ref[...], NT, preferred_element_type=F32)
    gate = jnp.concatenate([gate_s[tc] for tc in range(tm // cb)], axis=1)
    coef = gate * _gelu_tanh(act.astype(BF16))
    acc[...] += jnp.dot(vt_ref[...], coef, preferred_element_type=F32)

    @pl.when(j == pl.num_programs(1) - 1)
    def _():
        y_ref[...] = x1_ref[...] + g2_ref[...] * acc[...].T


def _peer_main(h2, x1, g2, u16, vt16, g, cnt, r2, p2, seq_len):
    n, d = h2.shape
    tm = _row_tile(n, seq_len, 512)
    ni1 = 16
    et = ni1 * PEER_NKEYS
    cb = r2.shape[-1]
    g2_a, g2_s = _seq_operand(g2, seq_len, tm)

    def row():
        return pl.BlockSpec((tm, d), lambda i, j: (i, 0))

    return pl.pallas_call(
        functools.partial(_peer_main_kernel, ni1),
        out_shape=jax.ShapeDtypeStruct((n, d), F32),
        grid=(n // tm, N_EXPERTS // et),
        in_specs=[row(), row(), g2_s,
                  pl.BlockSpec((et, d), lambda i, j: (j, 0)),
                  pl.BlockSpec((d, et), lambda i, j: (0, j)),
                  pl.BlockSpec((PEER_HEADS, ni1, tm), lambda i, j: (0, j, i)),
                  pl.BlockSpec((PEER_HEADS, ni1, tm), lambda i, j: (0, j, i)),
                  pl.BlockSpec((PEER_HEADS, tm // cb, PEER_NKEYS, cb), lambda i, j: (0, i, 0, 0)),
                  pl.BlockSpec((PEER_HEADS, tm // cb, PEER_NKEYS, cb), lambda i, j: (0, i, 0, 0))],
        out_specs=row(),
        scratch_shapes=[pltpu.VMEM((d, tm), F32), pltpu.VMEM((tm // cb, et, cb), BF16)],
        compiler_params=_cparams(("parallel", "arbitrary")),
    )(h2, x1, g2_a, u16, vt16, g, cnt, r2, p2)


def _layer(x, mod, pos, shift_prev, s0, cache, lw):
    nb, t, d = x.shape
    n = nb * t
    sh1, sc1, g1, sh2, sc2, g2 = [mod[:, i * d:(i + 1) * d] for i in range(6)]
    x2 = x.reshape(n, d)
    P = _inproj(x2, sc1, sh1, lw['norm1_w'], lw['w_in16'], t)

    prev = _pack_rw(shift_prev).reshape(nb, 1, P_COLS)
    o_a, zf = _rwkv(P, nb, t, prev, lw['mu'], lw['w0'], lw['a0'], lw['k_k'], lw['k_a'], lw['r_k'], lw['lnx_w'],
                    lw['lnx_b'], lw['wup'], lw['aup'], lw['gup'], _state_to_pairs(s0))
    wkv = _pairs_to_state(zf)
    shift_last = _unpack_rw(P.reshape(nb, t, P_COLS)[:, -1, :])

    q16, k32, k16, v32, v16, qi16, kw32, ki2 = _dsa_prep(P, jnp.tile(pos, nb), lw['q_norm_w'], lw['k_norm_w'])
    if cache is None:
        o_b = _attn_prompt(q16, qi16, kw32, k16, v16, ki2, nb, t)
    else:
        ck, cv, cki = cache
        past = ck.shape[1]
        o_b = _attn_sample(q16, qi16, kw32, k16, v16, ki2, ck.reshape(nb, past, d), cv.reshape(nb, past, d), cki,
                           nb, t)

    x1, h2 = _merge(x2, o_a, o_b, P, lw['b_gate'], g1, sc2, sh2, lw['norm2_w'], lw['wpa'], lw['wpb'], lw['wout'], t)
    g, cnt, r2, p2 = _peer_select(h2, lw['wpqt'], lw['kbd'])
    y = _peer_main(h2, x1, g2, lw['u16'], lw['vt16'], g, cnt, r2, p2, t)

    k_new = k32.reshape(nb, t, N_HEADS, HEAD_DIM)
    v_new = v32.reshape(nb, t, N_HEADS, HEAD_DIM)
    ki_new = kw32[:, :IDX_DIM].reshape(nb, t, IDX_DIM)
    return y.reshape(nb, t, d), wkv, shift_last, k_new, v_new, ki_new


def _layer_weights(l, w_in, b_gate, mu_rw, w0, w_up, a0, a_up, g_up, k_k, k_a, r_k, lnx_w, lnx_b, q_norm_w, k_norm_w,
                   w_proj_a, w_proj_b, w_out, norm1_w, norm2_w, w_pq, peer_keys, peer_u, peer_v):
    d = D_MODEL
    zeros = lambda r: jnp.zeros((r, d), F32)
    keys = peer_keys[l].reshape(2 * PEER_HEADS, PEER_NKEYS, PEER_DHALF)
    eye = jnp.eye(2 * PEER_HEADS, dtype=F32)
    kbd = (eye[:, None, :, None] * keys[:, :, None, :]).reshape(2 * d, d)
    return {
        'w_in16': _pack_in(w_in[l]).astype(BF16), 'b_gate': b_gate[l], 'mu': _pack_rw(mu_rw[l]).reshape(1, P_COLS),
        'w0': w0[l], 'a0': a0[l], 'k_k': k_k[l], 'k_a': k_a[l], 'r_k': r_k[l].reshape(d), 'lnx_w': lnx_w[l],
        'lnx_b': lnx_b[l],
        'wup': jnp.concatenate([w_up[l], zeros(LANES - D_DECAY)], axis=0).astype(BF16),
        'aup': jnp.concatenate([zeros(D_DECAY), a_up[l]], axis=0).astype(BF16),
        'gup': jnp.concatenate([g_up[l], zeros(256 - D_GATE)], axis=0).astype(BF16),
        'q_norm_w': q_norm_w[l], 'k_norm_w': k_norm_w[l], 'norm1_w': norm1_w[l], 'norm2_w': norm2_w[l],
        'wpa': w_proj_a[l].astype(BF16), 'wpb': w_proj_b[l].astype(BF16), 'wout': w_out[l].astype(BF16),
        'wpqt': w_pq[l].T.astype(BF16), 'kbd': kbd.astype(BF16),
        'u16': peer_u[l].astype(BF16), 'vt16': peer_v[l].T.astype(BF16),
    }


def kernel(x_prompt, x_sample, c_prompt, c_sample, cache_k, cache_v, cache_kidx, state_wkv, state_shift, w_ada, b_ada,
           norm1_w, w_in, b_gate, mu_rw, w0, w_up, a0, a_up, g_up, k_k, k_a, r_k, lnx_w, lnx_b, q_norm_w, k_norm_w,
           w_proj_a, w_proj_b, w_out, norm2_w, w_pq, peer_keys, peer_u, peer_v):
    depth = w_in.shape[0]
    bp, tp = x_prompt.shape[:2]
    bs, ts = x_sample.shape[:2]
    past = cache_k.shape[2]
    dt = x_prompt.dtype
    pos_p = jnp.arange(tp, dtype=I32)
    pos_s = past + jnp.arange(ts, dtype=I32)
    zero_shift = jnp.zeros((bp, RW_IN), dt)
    zero_wkv = jnp.zeros((bp, N_HEADS, HEAD_DIM, HEAD_DIM), dt)
    c_all = jnp.concatenate([c_prompt, c_sample], axis=0)
    xp, xs = x_prompt, x_sample
    outs_p, outs_s = [], []
    for l in range(depth):
        lw = _layer_weights(l, w_in, b_gate, mu_rw, w0, w_up, a0, a_up, g_up, k_k, k_a, r_k, lnx_w, lnx_b, q_norm_w,
                            k_norm_w, w_proj_a, w_proj_b, w_out, norm1_w, norm2_w, w_pq, peer_keys, peer_u, peer_v)
        mod = _ada(c_all, w_ada[l], b_ada[l])
        xp, *rest_p = _layer(xp, mod[:bp], pos_p, zero_shift, zero_wkv, None, lw)
        xs, *rest_s = _layer(xs, mod[bp:], pos_s, state_shift[l], state_wkv[l],
                             (cache_k[l], cache_v[l], cache_kidx[l]), lw)
        outs_p.append(rest_p)
        outs_s.append(rest_s)
    stack = lambda outs, i: jnp.stack([o[i] for o in outs])
    return (xp, xs,
            stack(outs_p, 0), stack(outs_p, 1), stack(outs_p, 2), stack(outs_p, 3), stack(outs_p, 4),
            stack(outs_s, 0), stack(outs_s, 1), stack(outs_s, 2), stack(outs_s, 3), stack(outs_s, 4))
```

```python
import functools

import jax
import jax.numpy as jnp
from jax import lax
from jax.experimental import pallas as pl
from jax.experimental.pallas import tpu as pltpu

F32 = jnp.float32
BF16 = jnp.bfloat16
I32 = jnp.int32

LANES = 128
D_MODEL = 1024
EPS = 1e-6
GN_EPS = 64e-5
ROPE_THETA = 10000.0
CHUNK = 64
TOPK_MAX = 256
HEAD_DIM = 64
N_HEADS = D_MODEL // HEAD_DIM
N_PAIRS = N_HEADS // 2
IDX_HEADS = 8
IDX_DIM = 64
D_DECAY = 64
D_AAA = 64
D_GATE = 160
RW_IN = 3 * D_MODEL + D_DECAY + D_AAA + D_GATE
PEER_HEADS = 8
PEER_NKEYS = 128
PEER_TOPK = 16
PEER_DHALF = 64
N_EXPERTS = PEER_NKEYS * PEER_NKEYS
RW_CHUNK = 64
RW_INTERLEAVE = 16
RW_PASSES = (2, 1, 1, 1)
VMEM_LIMIT = 56 * 1024 * 1024
LOG2E = 1.4426950408889634
PEER_CAND = 80
GATE_ROWS = 32

C_R, C_K, C_V = 0, 1024, 2048
C_Q, C_KD, C_VD = 3072, 4096, 5120
C_GA, C_GB = 6144, 7168
C_QI = 8192
C_G = 8704
C_M = 8960
C_KW = 9088
P_COLS = 9216
IN_W = 9064

NT = (((1,), (1,)), ((), ()))
NN = (((1,), (0,)), ((), ()))


def _pack_in(w):
    z = lambda k: jnp.zeros(w.shape[:-1] + (k,), w.dtype)
    return jnp.concatenate([w[..., 0:3072], w[..., 3360:6432], w[..., 7016:9064], w[..., 6432:6944],
                            w[..., 3200:3360], z(256 - D_GATE), w[..., 3072:3200],
                            w[..., 6944:7016], z(LANES - IDX_DIM - IDX_HEADS)], axis=-1)


def _pack_rw(a):
    return _pack_in(jnp.concatenate([a, jnp.zeros(a.shape[:-1] + (IN_W - RW_IN,), a.dtype)], axis=-1))


def _unpack_rw(p):
    return jnp.concatenate([p[..., :3072], p[..., C_M:C_M + 128], p[..., C_G:C_G + D_GATE]], axis=-1)


def _split_bf16(x, n):
    parts = []
    r = x
    for _ in range(n):
        p = r.astype(BF16)
        parts.append(p)
        r = r - p.astype(F32)
    return parts


def _mm(a, b, pa=1, pb=1, dims=NN):
    aps = _split_bf16(a, pa) if a.dtype != BF16 else [a]
    bps = _split_bf16(b, pb) if b.dtype != BF16 else [b]
    order = max(len(aps), len(bps))
    out = None
    for i, ap in enumerate(aps):
        for j, bp in enumerate(bps):
            if i + j >= order:
                continue
            t = lax.dot_general(ap, bp, dims, preferred_element_type=F32)
            out = t if out is None else out + t
    return out


def _sigmoid(x):
    return 1.0 / (1.0 + jnp.exp(-x))


def _softplus(z):
    return jnp.maximum(z, 0.0) + jnp.log(1.0 + jnp.exp(-jnp.abs(z)))


def _cparams(sem):
    return pltpu.CompilerParams(dimension_semantics=sem, vmem_limit_bytes=VMEM_LIMIT)


def _ada_kernel(c_ref, w_ref, b_ref, o_ref):
    c = c_ref[...]
    s = c * _sigmoid(c)
    o_ref[...] = _mm(s, w_ref[...], 2, 2) + b_ref[...]


def _ada(c, w, b):
    m, d = c.shape
    n = w.shape[1]
    tn = 1024
    return pl.pallas_call(
        _ada_kernel,
        out_shape=jax.ShapeDtypeStruct((m, n), F32),
        grid=(n // tn,),
        in_specs=[pl.BlockSpec((m, d), lambda j: (0, 0)),
                  pl.BlockSpec((d, tn), lambda j: (0, j)),
                  pl.BlockSpec((1, tn), lambda j: (0, j))],
        out_specs=pl.BlockSpec((m, tn), lambda j: (0, j)),
        compiler_params=_cparams(("arbitrary",)),
    )(c, w, b.reshape(1, n))


def _seq_operand(vec, seq_len, tm):
    b, d = vec.shape
    if seq_len % tm == 0:
        per = seq_len // tm
        arr = vec.reshape(b, 1, d)
        spec = pl.BlockSpec((None, 1, d), lambda *g: (g[0] // per, 0, 0))
    else:
        assert tm % seq_len == 0
        arr = jnp.repeat(vec, seq_len, axis=0)
        spec = pl.BlockSpec((tm, d), lambda *g: (g[0], 0))
    return arr, spec


def _row_tile(n, seq_len, cap):
    tm = min(cap, n)
    while n % tm or (seq_len % tm and tm % seq_len):
        tm //= 2
    return tm


def _inproj_kernel(x_ref, sc_ref, sh_ref, nw_ref, w_ref, o_ref, h_scr):
    @pl.when(pl.program_id(1) == 0)
    def _():
        x = x_ref[...]
        y = x * lax.rsqrt(jnp.mean(x * x, axis=-1, keepdims=True) + EPS) * nw_ref[...]
        h_scr[...] = (y * (1.0 + sc_ref[...]) + sh_ref[...]).astype(BF16)

    o_ref[...] = jnp.dot(h_scr[...], w_ref[...], preferred_element_type=F32)


def _inproj(x2, sc, sh, nw, w16, seq_len):
    n, d = x2.shape
    tm = _row_tile(n, seq_len, 1024)
    tn = 1024
    sc_a, sc_s = _seq_operand(sc, seq_len, tm)
    sh_a, sh_s = _seq_operand(sh, seq_len, tm)
    return pl.pallas_call(
        _inproj_kernel,
        out_shape=jax.ShapeDtypeStruct((n, P_COLS), F32),
        grid=(n // tm, P_COLS // tn),
        in_specs=[pl.BlockSpec((tm, d), lambda i, j: (i, 0)), sc_s, sh_s,
                  pl.BlockSpec((1, d), lambda i, j: (0, 0)),
                  pl.BlockSpec((d, tn), lambda i, j: (0, j))],
        out_specs=pl.BlockSpec((tm, tn), lambda i, j: (i, j)),
        scratch_shapes=[pltpu.VMEM((tm, d), BF16)],
        compiler_params=_cparams(("parallel", "arbitrary")),
    )(x2, sc_a, sh_a, nw.reshape(1, d), w16)


def _lane_lo(shape):
    return lax.broadcasted_iota(I32, shape, len(shape) - 1) < HEAD_DIM


def _pair_sum(x):
    lo = _lane_lo(x.shape)
    s0 = jnp.sum(jnp.where(lo, x, 0.0), axis=-1, keepdims=True)
    s1 = jnp.sum(jnp.where(lo, 0.0, x), axis=-1, keepdims=True)
    return jnp.where(lo, s0, s1)


def _stack2(x):
    lo = _lane_lo(x.shape)
    return jnp.concatenate([jnp.where(lo, x, 0.0), jnp.where(lo, 0.0, x)], axis=0)


def _rwkv_kernel(t_real, npair, nchunk, pr, pk, pv, pg, pm, sr, sk, sv, sg, sm, mr, mk, mv, mg, mmu,
                 w0, a0, kkw, kaw, rkw, lnw, lnb, wup, aup, gup, z0, o_ref, zf_ref,
                 r_s, lw_s, k_s, v_s, a_s, b_s, y_s, bonus_s, g_s):
    C = RW_CHUNK
    t_pad = r_s.shape[1]

    def mix(p_ref, s_ref, m_ref):
        p = p_ref[...]
        prev = pltpu.roll(p, 1, 0)
        row = lax.broadcasted_iota(I32, p.shape, 0)
        prev = jnp.where(row == 0, s_ref[...], prev)
        return p + (prev - p) * m_ref[...]

    xg, xm = mix(pg, sg, mg), mix(pm, sm, mmu)
    th16, xm16, sg16 = jnp.tanh(xm).astype(BF16), xm.astype(BF16), _sigmoid(xg).astype(BF16)
    xr_all, xk_all, xv_all = mix(pr, sr, mr), mix(pk, sk, mk), mix(pv, sv, mv)

    def put(ref, pp, val):
        if t_pad > t_real:
            val = jnp.concatenate([val, jnp.zeros((t_pad - t_real, LANES), F32)], axis=0)
        ref[pp] = val

    for pp in range(npair):
        cs = slice(pp * LANES, (pp + 1) * LANES)
        xr, xk, xv = xr_all[:, cs], xk_all[:, cs], xv_all[:, cs]
        dw = jnp.dot(th16, wup[:, cs], preferred_element_type=F32)
        lw = -jnp.exp(-_softplus(-(w0[:, cs] + dw)) - 0.5)
        asig = _sigmoid(a0[:, cs] + jnp.dot(xm16, aup[:, cs], preferred_element_type=F32))
        g_s[pp] = jnp.dot(sg16, gup[:, cs], preferred_element_type=F32)
        kk = xk * kkw[:, cs]
        kk = kk * lax.rsqrt(_pair_sum(kk * kk) + 1e-12)
        kmod = xk * (1.0 + (asig - 1.0) * kaw[:, cs])
        bonus_s[pp] = _pair_sum(xr * kmod * rkw[:, cs]) * xv
        put(r_s, pp, xr)
        put(lw_s, pp, lw)
        put(k_s, pp, kmod)
        put(v_s, pp, xv)
        put(a_s, pp, -kk)
        put(b_s, pp, kk * asig)

    n2 = 2 * C
    ri = lax.broadcasted_iota(I32, (n2, n2), 0)
    ci = lax.broadcasted_iota(I32, (n2, n2), 1)
    same = (ri // C) == (ci // C)
    strict = same & ((ri % C) > (ci % C))
    incl = same & ((ri % C) >= (ci % C))
    eye = ri == ci
    eye_f = jnp.where(eye, 1.0, 0.0)
    tri = jnp.where(lax.broadcasted_iota(I32, (C, C), 0) >= lax.broadcasted_iota(I32, (C, C), 1), 1.0, 0.0
                    ).astype(BF16)
    zeros_sq = jnp.zeros((n2, LANES), F32)

    pg_, pi_, po_, ps_ = RW_PASSES

    def local(chains):
        each = lambda f, *cols: [f(*xs) for xs in zip(*cols)]
        lwc = [lw_s[pp, sl, :] for sl, pp in chains]
        cum = each(lambda l: _mm(tri, l, 1, 3), lwc)
        cum_last = each(lambda c: c[C - 1:C, :], cum)
        ec, eci = each(jnp.exp, cum), each(lambda c: jnp.exp(-c), cum)
        ecp = each(lambda c, l: jnp.exp(c - l), cum, lwc)
        ecl = each(lambda c, cl: jnp.exp(cl - c), cum, cum_last)
        a_c = [a_s[pp, sl, :] for sl, pp in chains]
        b_c = [b_s[pp, sl, :] for sl, pp in chains]
        k_c = [k_s[pp, sl, :] for sl, pp in chains]
        r_c = [r_s[pp, sl, :] for sl, pp in chains]
        As = each(lambda a, e: _stack2(a * e), a_c, ecp)
        Rs = each(lambda r, e: _stack2(r * e), r_c, ec)
        Bs = each(lambda b, e: _stack2(b * e), b_c, eci)
        Ks = each(lambda k, e: _stack2(k * e), k_c, eci)
        Bt = each(lambda b, e: _stack2(b * e), b_c, ecl)
        Kt = each(lambda k, e: _stack2(k * e), k_c, ecl)
        Vs = [_stack2(v_s[pp, sl, :]) for sl, pp in chains]

        G = each(lambda a, r, b, k: _mm(jnp.concatenate([a, r], axis=0), jnp.concatenate([b, k], axis=0),
                                        pg_, pg_, NT), As, Rs, Bs, Ks)
        a_ab = each(lambda g: jnp.where(strict, g[:n2, :n2], 0.0), G)
        a_ak = each(lambda g: jnp.where(strict, g[:n2, n2:], 0.0), G)
        a_rb = each(lambda g: jnp.where(incl, g[n2:, :n2], 0.0), G)
        a_rk = each(lambda g: jnp.where(incl, g[n2:, n2:], 0.0), G)

        lp = a_ab
        tm_ = each(lambda a: eye_f + a, a_ab)
        step = 2
        while step < C:
            lp = each(lambda l: _mm(l, l, pi_, pi_), lp)
            tm_ = each(lambda t, l: t + _mm(t, l, pi_, pi_), tm_, lp)
            step *= 2

        w1 = each(lambda a, v: _mm(a, v, po_, po_), a_ak, Vs)
        mu_ = each(lambda t, a, w: _mm(t, jnp.concatenate([a, w], axis=1), po_, po_), tm_, As, w1)
        rhs = each(lambda m, v: jnp.concatenate([m, jnp.concatenate([zeros_sq, v], axis=1)], axis=0), mu_, Vs)
        lhs = each(lambda rb, rk, b, k: jnp.concatenate([jnp.concatenate([rb, rk], axis=1),
                                                         jnp.concatenate([b.T, k.T], axis=1)], axis=0),
                   a_rb, a_rk, Bt, Kt)
        out2 = each(lambda l, r: _mm(l, r, po_, po_), lhs, rhs)
        m23 = each(lambda r, o, cl: jnp.concatenate([r + o[:n2, :LANES],
                                                     jnp.where(eye, jnp.exp(cl), 0.0) + o[n2:, :LANES]], axis=0),
                   Rs, out2, cum_last)
        return [(m, o[:n2, LANES:], o[n2:, LANES:]) for m, o in zip(m23, out2)]

    def step_chunks(i, zs):
        sls = [pl.ds(pl.multiple_of((i * nchunk + j) * C, C), C) for j in range(nchunk)]
        parts = local([(sl, pp) for sl in sls for pp in range(npair)])
        zs = list(zs)
        for j, sl in enumerate(sls):
            for pp in range(npair):
                m23, y_loc, z_loc = parts[j * npair + pp]
                yz = _mm(m23, zs[pp], ps_, ps_)
                y = yz[:n2] + y_loc
                y_s[pp, sl, :] = y[:C] + y[C:]
                zs[pp] = yz[n2:] + z_loc
        return tuple(zs)

    zs = lax.fori_loop(0, t_pad // (C * nchunk), step_chunks, tuple(z0[pp] for pp in range(npair)))
    for pp in range(npair):
        zf_ref[pp] = zs[pp]
        cs = slice(pp * LANES, (pp + 1) * LANES)
        y = y_s[pp, 0:t_real, :]
        mean = _pair_sum(y) * (1.0 / HEAD_DIM)
        dlt = y - mean
        var = _pair_sum(dlt * dlt) * (1.0 / HEAD_DIM)
        yn = dlt * lax.rsqrt(var + GN_EPS) * lnw[:, cs] + lnb[:, cs]
        o_ref[:, cs] = ((yn + bonus_s[pp]) * g_s[pp]).astype(o_ref.dtype)


def _rwkv(P, nb, t, prev, mu, w0, a0, k_k, k_a, r_k, lnx_w, lnx_b, wup, aup, gup, z0):
    t_pad = max(t, RW_CHUNK)
    assert t % 8 == 0 and t_pad % RW_CHUNK == 0
    n_chunks = t_pad // RW_CHUNK
    nchunk = min(RW_INTERLEAVE, n_chunks)
    npair = min(N_PAIRS, max(1, RW_INTERLEAVE // nchunk))
    wp = npair * LANES

    def cblk(c0, w, per_pair):
        return (lambda p: c0 // w + p) if per_pair else (lambda p: c0 // w)

    def pcol(c0, w, pp):
        f = cblk(c0, w, pp)
        return pl.BlockSpec((t, w), lambda b, p: (b, f(p)))

    def prevcol(c0, w, pp):
        f = cblk(c0, w, pp)
        return pl.BlockSpec((None, 1, w), lambda b, p: (b, 0, f(p)))

    def mucol(c0, w, pp):
        f = cblk(c0, w, pp)
        return pl.BlockSpec((1, w), lambda b, p: (0, f(p)))

    def hvec():
        return pl.BlockSpec((1, wp), lambda b, p: (0, p))

    cols = [(C_R, wp, True), (C_K, wp, True), (C_V, wp, True), (C_G, 256, False), (C_M, LANES, False)]
    in_specs = ([pcol(*c) for c in cols] + [prevcol(*c) for c in cols] + [mucol(*c) for c in cols]
                + [hvec() for _ in range(7)]
                + [pl.BlockSpec((LANES, wp), lambda b, p: (0, p)),
                   pl.BlockSpec((LANES, wp), lambda b, p: (0, p)),
                   pl.BlockSpec((256, wp), lambda b, p: (0, p)),
                   pl.BlockSpec((None, npair, LANES, LANES), lambda b, p: (b, p, 0, 0))])
    vecs = [v.reshape(1, D_MODEL) for v in (w0, a0, k_k, k_a, r_k, lnx_w, lnx_b)]
    o, zf = pl.pallas_call(
        functools.partial(_rwkv_kernel, t, npair, nchunk),
        out_shape=(jax.ShapeDtypeStruct((nb * t, D_MODEL), BF16),
                   jax.ShapeDtypeStruct((nb, N_PAIRS, LANES, LANES), F32)),
        grid=(nb, N_PAIRS // npair),
        in_specs=in_specs,
        out_specs=(pl.BlockSpec((t, wp), lambda b, p: (b, p)),
                   pl.BlockSpec((None, npair, LANES, LANES), lambda b, p: (b, p, 0, 0))),
        scratch_shapes=([pltpu.VMEM((npair, t_pad, LANES), F32) for _ in range(7)]
                        + [pltpu.VMEM((npair, t, LANES), F32) for _ in range(2)]),
        compiler_params=_cparams(("parallel", "arbitrary")),
    )(P, P, P, P, P, prev, prev, prev, prev, prev, mu, mu, mu, mu, mu, *vecs, wup, aup, gup, z0)
    return o, zf


def _state_to_pairs(s):
    nb = s.shape[0]
    zt = jnp.swapaxes(s, -1, -2).reshape(nb, N_PAIRS, 2, HEAD_DIM, HEAD_DIM)
    zero = jnp.zeros_like(zt[:, :, 0])
    top = jnp.concatenate([zt[:, :, 0], zero], axis=-1)
    bot = jnp.concatenate([zero, zt[:, :, 1]], axis=-1)
    return jnp.concatenate([top, bot], axis=-2)


def _pairs_to_state(z):
    nb = z.shape[0]
    h0 = z[:, :, :HEAD_DIM, :HEAD_DIM]
    h1 = z[:, :, HEAD_DIM:, HEAD_DIM:]
    s = jnp.stack([h0, h1], axis=2).reshape(nb, N_HEADS, HEAD_DIM, HEAD_DIM)
    return jnp.swapaxes(s, -1, -2)


def _rope(x, cos, sin_signed):
    w = x.shape[1]
    reps = w // LANES
    cw = jnp.concatenate([cos] * reps, axis=1) if reps > 1 else cos
    sw = jnp.concatenate([sin_signed] * reps, axis=1) if reps > 1 else sin_signed
    lane = lax.broadcasted_iota(I32, x.shape, 1)
    fwd = pltpu.roll(x, w - 32, 1)
    bwd = pltpu.roll(x, 32, 1)
    partner = jnp.where((lane % HEAD_DIM) < 32, fwd, bwd)
    return x * cw + partner * sw


def _head_rms(x, nw, e_dn, e_up):
    ms = _mm(x * x, e_dn, 2, 1) * (1.0 / HEAD_DIM)
    r = lax.rsqrt(ms + EPS)
    return x * _mm(r, e_up, 2, 1) * nw


def _dsa_prep_kernel(pq, pkd, pvd, pqi, pkw, cos_ref, sin_ref, qn, kn, edn, eup,
                     q16, k32, k16, v32, v16, qi16, kw32, ki2):
    cos, sin = cos_ref[...], sin_ref[...]
    e_dn, e_up = edn[...], eup[...]
    def put_pairs(ref, x):
        for p in range(N_PAIRS):
            ref[p] = x[:, p * LANES:(p + 1) * LANES].astype(ref.dtype)

    q = _rope(_head_rms(pq[...], qn[...], e_dn, e_up), cos, sin)
    put_pairs(q16, q * (HEAD_DIM ** -0.5 * LOG2E))
    k = _rope(_head_rms(pkd[...], kn[...], e_dn, e_up), cos, sin)
    k32[...] = k
    put_pairs(k16, k)
    v = pvd[...]
    v32[...] = v
    put_pairs(v16, v)
    qi16[...] = _rope(pqi[...], cos, sin).astype(BF16)
    kw = pkw[...]
    lane = lax.broadcasted_iota(I32, kw.shape, 1)
    wi_scale = (IDX_HEADS * IDX_DIM) ** -0.5
    kr = _rope(kw, cos, sin)
    kw32[...] = jnp.where(lane < IDX_DIM, kr, jnp.where(lane < IDX_DIM + IDX_HEADS, kw * wi_scale, 0.0))
    ki2[...] = jnp.where(lane < IDX_DIM, kr, pltpu.roll(kr, IDX_DIM, 1)).astype(BF16)


def _dsa_prep(P, pos_rows, q_norm_w, k_norm_w):
    n = P.shape[0]
    tm = 512 if n % 512 == 0 else n
    half = HEAD_DIM // 2
    inv = ROPE_THETA ** (-jnp.arange(half, dtype=F32) / half)
    ang = pos_rows.astype(F32)[:, None] * inv[None, :]
    cos = jnp.tile(jnp.cos(ang), (1, 4))
    sin = jnp.sin(ang)
    sin_signed = jnp.tile(jnp.concatenate([-sin, sin], axis=1), (1, 2))
    head_of = jnp.arange(D_MODEL) // HEAD_DIM
    e_dn = (head_of[:, None] == jnp.arange(LANES)[None, :]).astype(BF16)
    e_up = e_dn.T
    qn = jnp.tile(q_norm_w, N_HEADS).reshape(1, D_MODEL)
    kn = jnp.tile(k_norm_w, N_HEADS).reshape(1, D_MODEL)

    def col(c0, w):
        return pl.BlockSpec((tm, w), lambda i, c0=c0, w=w: (i, c0 // w))

    def row(w):
        return pl.BlockSpec((tm, w), lambda i: (i, 0))

    def const(shape):
        return pl.BlockSpec(shape, lambda i: (0, 0))

    pairs = jax.ShapeDtypeStruct((N_PAIRS, n, LANES), BF16)
    pair_spec = pl.BlockSpec((N_PAIRS, tm, LANES), lambda i: (0, i, 0))
    return pl.pallas_call(
        _dsa_prep_kernel,
        out_shape=(pairs, jax.ShapeDtypeStruct((n, D_MODEL), F32), pairs, jax.ShapeDtypeStruct((n, D_MODEL), F32),
                   pairs, jax.ShapeDtypeStruct((n, IDX_HEADS * IDX_DIM), BF16),
                   jax.ShapeDtypeStruct((n, LANES), F32), jax.ShapeDtypeStruct((n, LANES), BF16)),
        grid=(n // tm,),
        in_specs=[col(C_Q, 1024), col(C_KD, 1024), col(C_VD, 1024), col(C_QI, 512), col(C_KW, LANES),
                  row(LANES), row(LANES), const((1, D_MODEL)), const((1, D_MODEL)),
                  const((D_MODEL, LANES)), const((LANES, D_MODEL))],
        out_specs=(pair_spec, row(D_MODEL), pair_spec, row(D_MODEL), pair_spec, row(512), row(LANES), row(LANES)),
        compiler_params=_cparams(("parallel",)),
    )(P, P, P, P, P, cos, sin_signed, qn, kn, e_dn, e_up)


BISECT_CAP = 320


def _index_scores(qi, wi, ki_list):
    outs = []
    for ki in ki_list:
        acc = None
        for h in range(IDX_HEADS):
            qpair = qi[:, (h // 2) * LANES:(h // 2 + 1) * LANES]
            lo = _lane_lo(qpair.shape)
            qh = jnp.where(lo if h % 2 == 0 else jnp.logical_not(lo), qpair, jnp.zeros_like(qpair))
            rel = lax.dot_general(qh, ki, NT, preferred_element_type=F32)
            term = wi[:, IDX_DIM + h:IDX_DIM + h + 1] * jnp.maximum(rel, 0.0)
            acc = term if acc is None else acc + term
        outs.append(acc)
    return outs


def _select_topk(keys, topk, bias_refs):
    tq = keys[0].shape[0]
    neg = -jnp.inf

    def write(masks):
        for ref, k, msk in zip(bias_refs, keys, masks):
            ref[:, 0:k.shape[1]] = jnp.where(msk, 0.0, neg)

    def count(pred_list):
        tot = None
        for p in pred_list:
            c = jnp.sum(jnp.where(p, 1.0, 0.0), axis=-1, keepdims=True)
            tot = c if tot is None else tot + c
        return tot

    def row_reduce(fn, vals):
        out = None
        for v in vals:
            r = fn(v, axis=-1, keepdims=True)
            out = r if out is None else (jnp.minimum(out, r) if fn is jnp.min else jnp.maximum(out, r))
        return out

    n_adm = count([k > neg for k in keys])
    few = n_adm <= topk
    lo0 = row_reduce(jnp.min, [jnp.where(k > neg, k, jnp.inf) for k in keys])
    hi0 = row_reduce(jnp.max, keys)

    def pending(cnt, adjacent):
        return jnp.max(jnp.where(few | (cnt == topk) | (adjacent > 0.0), 0.0, 1.0))

    def halve(state):
        i, lo, hi, cnt, adjacent, _ = state
        mid = 0.5 * lo + 0.5 * hi
        cnt_m = count([k >= mid for k in keys])
        adjacent = jnp.where((mid <= lo) | (mid >= hi), 1.0, adjacent)
        take = cnt_m >= topk
        lo = jnp.where(take, jnp.maximum(mid, lo), lo)
        hi = jnp.where(take, hi, jnp.minimum(mid, hi))
        cnt = jnp.where(take, cnt_m, cnt)
        return i + 1, lo, hi, cnt, adjacent, pending(cnt, adjacent)

    cnt_hi = count([k >= hi0 for k in keys])
    top_tied = cnt_hi >= topk
    lo0 = jnp.where(top_tied, hi0, lo0)
    cnt0 = jnp.where(top_tied, cnt_hi, n_adm)
    adj0 = jnp.where(top_tied | (lo0 >= hi0), 1.0, 0.0)
    state = (jnp.int32(0), lo0, hi0, cnt0, adj0, pending(cnt0, adj0))
    _, thr, _, _, _, _ = lax.while_loop(lambda s: (s[0] < BISECT_CAP) & (s[5] > 0.0), halve, state)
    ge = [(k >= thr) & (k > neg) for k in keys]
    write(ge)
    surplus = jnp.max(count(ge)) > topk

    @pl.when(surplus)
    def _():
        gt = [k > thr for k in keys]
        need = topk - count(gt)
        ties = [(k == thr) & (k > neg) for k in keys]
        offs, idx = 0, []
        for k in keys:
            idx.append(lax.broadcasted_iota(I32, k.shape, 1) + offs)
            offs += k.shape[1]
        nbits = max(1, (offs - 1).bit_length() + 1)

        def idx_step(i, m):
            trial = m + jnp.left_shift(jnp.int32(1), nbits - 1 - i)
            cnt = count([t & (ix < trial) for t, ix in zip(ties, idx)])
            return jnp.where(cnt <= need, trial, m)

        cut = lax.fori_loop(0, nbits, idx_step, jnp.zeros((tq, 1), I32))
        write([g | (t & (ix < cut)) for g, t, ix in zip(gt, ties, idx)])


def _attend_pair(q_pair, k_list, v_list, bias_list):
    lo = _lane_lo(q_pair.shape)
    zero = jnp.zeros_like(q_pair)
    outs = []
    for qh in (jnp.where(lo, q_pair, zero), jnp.where(lo, zero, q_pair)):
        s_list = [lax.dot_general(qh, k, NT, preferred_element_type=F32) + b for k, b in zip(k_list, bias_list)]
        m = None
        for s in s_list:
            mx = jnp.max(s, axis=-1, keepdims=True)
            m = mx if m is None else jnp.maximum(m, mx)
        den, acc = None, None
        for s, v in zip(s_list, v_list):
            p = jnp.exp2(s - m)
            d = jnp.sum(p, axis=-1, keepdims=True)
            o = jnp.dot(p.astype(BF16), v, preferred_element_type=F32)
            den = d if den is None else den + d
            acc = o if acc is None else acc + o
        outs.append(acc / den)
    return jnp.where(_lane_lo(outs[0].shape), outs[0], outs[1])


def _attn_prompt_kernel(topk, ncase, q_ref, qi_ref, kw_ref, k_ref, v_ref, ki2_ref, o_ref, bias_s):
    tq = q_ref.shape[1]
    t = k_ref.shape[1]
    i = pl.program_id(1)
    lstep = t // ncase
    case = ((i + 1) * tq - 1) // lstep

    def run(L):
        score = _index_scores(qi_ref[...], kw_ref[...], [ki2_ref[0:L, :]])[0]
        qpos = i * tq + lax.broadcasted_iota(I32, (tq, L), 0)
        kpos = lax.broadcasted_iota(I32, (tq, L), 1)
        adm = (qpos // CHUNK) >= (kpos // CHUNK)
        _select_topk([jnp.where(adm, score, -jnp.inf)], topk, [bias_s])

        def pair(p, carry):
            o = _attend_pair(q_ref[p], [k_ref[p, 0:L, :]], [v_ref[p, 0:L, :]], [bias_s[:, 0:L]])
            o_ref[p] = o.astype(o_ref.dtype)
            return carry

        lax.fori_loop(0, N_PAIRS, pair, 0)

    for c in range(ncase):
        pl.when(case == c)(functools.partial(run, (c + 1) * lstep))


def _attn_prompt(q16, qi16, kw32, k16, v16, ki2, nb, t):
    tq = min(256, t)
    topk = min(TOPK_MAX, t // 4)
    nq = t // tq
    ncase = min(4, nq)

    def qrow(w):
        return pl.BlockSpec((tq, w), lambda b, i: (b * nq + i, 0))

    def qpairs():
        return pl.BlockSpec((N_PAIRS, tq, LANES), lambda b, i: (0, b * nq + i, 0))

    def kpairs():
        return pl.BlockSpec((N_PAIRS, t, LANES), lambda b, i: (0, b, 0))

    return pl.pallas_call(
        functools.partial(_attn_prompt_kernel, topk, ncase),
        out_shape=jax.ShapeDtypeStruct((N_PAIRS, nb * t, LANES), BF16),
        grid=(nb, nq),
        in_specs=[qpairs(), qrow(512), qrow(LANES), kpairs(), kpairs(),
                  pl.BlockSpec((t, LANES), lambda b, i: (b, 0))],
        out_specs=qpairs(),
        scratch_shapes=[pltpu.VMEM((tq, t), F32)],
        compiler_params=_cparams(("parallel", "arbitrary")),
    )(q16, qi16, kw32, k16, v16, ki2)


def _attn_sample_kernel(topk, past, q_ref, qi_ref, kw_ref, ck_ref, cv_ref, cki2_ref, k_ref, v_ref, ki2_ref, o_ref,
                        biasc_s, biasn_s):
    ts = q_ref.shape[0]

    @pl.when(pl.program_id(1) == 0)
    def _():
        sc, sn = _index_scores(qi_ref[...], kw_ref[...], [cki2_ref[...], ki2_ref[...]])
        qpos = past + lax.broadcasted_iota(I32, (ts, 1), 0)
        kpos_c = lax.broadcasted_iota(I32, sc.shape, 1)
        kpos_n = past + lax.broadcasted_iota(I32, sn.shape, 1)
        keys = [jnp.where((qpos // CHUNK) >= (kpos_c // CHUNK), sc, -jnp.inf),
                jnp.where((qpos // CHUNK) >= (kpos_n // CHUNK), sn, -jnp.inf)]
        _select_topk(keys, topk, [biasc_s, biasn_s])

    o_ref[...] = _attend_pair(q_ref[...], [ck_ref[...].astype(BF16), k_ref[...]],
                              [cv_ref[...].astype(BF16), v_ref[...]],
                              [biasc_s[...], biasn_s[...]]).astype(o_ref.dtype)


def _attn_sample(q16, qi16, kw32, k16, v16, ki2, cache_k, cache_v, cache_kidx, nb, ts):
    past = cache_k.shape[1]
    cki2 = jnp.concatenate([cache_kidx, cache_kidx], axis=-1).astype(BF16)
    topk = min(TOPK_MAX, (past + ts) // 4)

    def qrow(w):
        return pl.BlockSpec((ts, w), lambda b, p: (b, 0))

    def qpair():
        return pl.BlockSpec((None, ts, LANES), lambda b, p: (p, b, 0))

    def cache(pair):
        return pl.BlockSpec((None, past, LANES), (lambda b, p: (b, 0, p)) if pair else (lambda b, p: (b, 0, 0)))

    return pl.pallas_call(
        functools.partial(_attn_sample_kernel, topk, past),
        out_shape=jax.ShapeDtypeStruct((N_PAIRS, nb * ts, LANES), BF16),
        grid=(nb, N_PAIRS),
        in_specs=[qpair(), qrow(512), qrow(LANES), cache(True), cache(True), cache(False),
                  qpair(), qpair(), qrow(LANES)],
        out_specs=qpair(),
        scratch_shapes=[pltpu.VMEM((ts, past), F32), pltpu.VMEM((ts, ts), F32)],
        compiler_params=_cparams(("parallel", "arbitrary")),
    )(q16, qi16, kw32, cache_k, cache_v, cki2, k16, v16, ki2)


def _merge_kernel(x_ref, oa_ref, ob_ref, pga_ref, pgb_ref, bga_ref, bgb_ref, g1_ref, sc2_ref, sh2_ref, nw_ref,
                  wpa_ref, wpb_ref, wout_ref, x1_ref, h2_ref):
    ga = _sigmoid(pga_ref[...] + bga_ref[...])
    gb = _sigmoid(pgb_ref[...] + bgb_ref[...])
    ob = jnp.concatenate([ob_ref[p] for p in range(N_PAIRS)], axis=1)
    m = (ga * jnp.dot(oa_ref[...], wpa_ref[...], preferred_element_type=F32)
         + gb * jnp.dot(ob, wpb_ref[...], preferred_element_type=F32))
    x1 = x_ref[...] + g1_ref[...] * jnp.dot(m.astype(BF16), wout_ref[...], preferred_element_type=F32)
    x1_ref[...] = x1
    y = x1 * lax.rsqrt(jnp.mean(x1 * x1, axis=-1, keepdims=True) + EPS) * nw_ref[...]
    h2_ref[...] = (y * (1.0 + sc2_ref[...]) + sh2_ref[...]).astype(BF16)


def _merge(x2, o_a, o_b, P, b_gate, g1, sc2, sh2, nw2, wpa, wpb, wout, seq_len):
    n, d = x2.shape
    tm = _row_tile(n, seq_len, 512)
    g1_a, g1_s = _seq_operand(g1, seq_len, tm)
    sc_a, sc_s = _seq_operand(sc2, seq_len, tm)
    sh_a, sh_s = _seq_operand(sh2, seq_len, tm)

    def row():
        return pl.BlockSpec((tm, d), lambda i: (i, 0))

    def const(shape):
        return pl.BlockSpec(shape, lambda i: (0, 0))

    bg = b_gate.reshape(1, 2 * d)
    return pl.pallas_call(
        _merge_kernel,
        out_shape=(jax.ShapeDtypeStruct((n, d), F32), jax.ShapeDtypeStruct((n, d), BF16)),
        grid=(n // tm,),
        in_specs=[row(), row(), pl.BlockSpec((N_PAIRS, tm, LANES), lambda i: (0, i, 0)),
                  pl.BlockSpec((tm, d), lambda i: (i, C_GA // d)), pl.BlockSpec((tm, d), lambda i: (i, C_GB // d)),
                  pl.BlockSpec((1, d), lambda i: (0, 0)), pl.BlockSpec((1, d), lambda i: (0, 1)),
                  g1_s, sc_s, sh_s, const((1, d)), const((d, d)), const((d, d)), const((d, d))],
        out_specs=(row(), row()),
        compiler_params=_cparams(("parallel",)),
    )(x2, o_a, o_b, P, P, bg, bg, g1_a, sc_a, sh_a, nw2.reshape(1, d), wpa, wpb, wout)


def _top_exact(s, k):
    rows = lax.broadcasted_iota(I32, s.shape, 0).astype(F32)
    cur = s
    rank = jnp.full(s.shape, float(k), F32)
    vals = []
    for r in range(k):
        m = jnp.max(cur, axis=0, keepdims=True)
        first = jnp.min(jnp.where(cur == m, rows, 1e9), axis=0, keepdims=True)
        hit = rows == first
        vals.append(m)
        rank = jnp.where(hit, float(r), rank)
        cur = jnp.where(hit, -jnp.inf, cur)
    return vals, rank


def _top_fast(ss, k):
    curs = list(ss)
    ranks = [jnp.full(s.shape, float(k), F32) for s in ss]
    vals = [[] for _ in ss]
    for r in range(k):
        ms = [jnp.max(c, axis=0, keepdims=True) for c in curs]
        hits = [c == m for c, m in zip(curs, ms)]
        ranks = [jnp.where(h, float(r), rk) for h, rk in zip(hits, ranks)]
        curs = [jnp.where(h, -jnp.inf, c) for h, c in zip(hits, curs)]
        for v, m in zip(vals, ms):
            v.append(m)
    cleans = [jnp.max(jnp.abs(jnp.sum(jnp.where(rk < k, 1.0, 0.0), axis=0, keepdims=True) - k)) == 0.0
              for rk in ranks]
    return vals, ranks, cleans


def _top(ss, k, vals_scr, rank_scr):
    vals, ranks, cleans = _top_fast(ss, k)
    for i, s in enumerate(ss):
        vals_scr[i] = jnp.concatenate(vals[i], axis=0)
        rank_scr[i] = ranks[i]

        @pl.when(jnp.logical_not(cleans[i]))
        def _(i=i, s=s):
            vals_e, rank_e = _top_exact(s, k)
            vals_scr[i] = jnp.concatenate(vals_e, axis=0)
            rank_scr[i] = rank_e


def _peer_sel_kernel(h_ref, wpqt_ref, kbd_ref, g_ref, cnt_ref, r2_ref, p2_ref, s_scr, vals_scr, rank_scr,
                     cand_scr, cvals_scr, crank_scr):
    K = PEER_TOPK
    tm = h_ref.shape[0]
    qt = lax.dot_general(wpqt_ref[...], h_ref[...], NT, preferred_element_type=F32)
    s_scr[...] = jnp.dot(kbd_ref[...], qt.astype(BF16), preferred_element_type=F32
                         ).reshape(2 * PEER_HEADS, PEER_NKEYS, tm)
    sub8 = lax.broadcasted_iota(I32, (8, tm), 0)
    neg = jnp.full((8, tm), -jnp.inf, F32)
    _top([s_scr[r] for r in range(2 * PEER_HEADS)], K, vals_scr, rank_scr)
    for hd in range(PEER_HEADS):
        c1, c2 = vals_scr[2 * hd], vals_scr[2 * hd + 1]
        blocks = [c1[0:1] + c2, c1[1:2] + c2[0:8]]
        for k1 in range(2, 8):
            blocks.append(jnp.where(sub8 < K // (k1 + 1), c1[k1:k1 + 1] + c2[0:8], neg))
        blocks.append(c1[8:16] + c2[0:1])
        cand_scr[hd] = jnp.concatenate(blocks, axis=0)
    _top([cand_scr[hd] for hd in range(PEER_HEADS)], K, cvals_scr, crank_scr)
    for hd in range(PEER_HEADS):
        s1, s2 = s_scr[2 * hd], s_scr[2 * hd + 1]
        c1, c2 = vals_scr[2 * hd], vals_scr[2 * hd + 1]
        rank1, rank2 = rank_scr[2 * hd], rank_scr[2 * hd + 1]
        cand = cand_scr[hd]
        taken = crank_scr[hd] < K
        z = jnp.sum(jnp.where(taken, jnp.exp(cand - (c1[0:1] + c2[0:1])), 0.0), axis=0, keepdims=True)
        tk = jnp.where(taken, 1.0, 0.0)
        per_k1 = [jnp.sum(tk[0:16], axis=0, keepdims=True)]
        per_k1 += [jnp.sum(tk[8 + 8 * k1:16 + 8 * k1], axis=0, keepdims=True) for k1 in range(1, 8)]
        cnt16 = jnp.concatenate(per_k1 + [tk[72:80]], axis=0)
        cnt = jnp.zeros(s1.shape, F32)
        for k1 in range(K):
            cnt = jnp.where(rank1 == float(k1), cnt16[k1:k1 + 1], cnt)
        g_ref[hd] = jnp.where(rank1 < K, jnp.exp(s1 - c1[0:1]) / z, 0.0)
        cnt_ref[hd] = cnt
        p2 = jnp.where(rank2 < K, jnp.exp(s2 - c2[0:1]), 0.0)
        cb = r2_ref.shape[-1]
        for tc in range(tm // cb):
            r2_ref[hd, tc] = rank2[:, tc * cb:(tc + 1) * cb].astype(r2_ref.dtype)
            p2_ref[hd, tc] = p2[:, tc * cb:(tc + 1) * cb].astype(p2_ref.dtype)


def _peer_select(h2, wpqt, kbd):
    n, d = h2.shape
    tm = 256 if n % 256 == 0 else n
    cb = min(LANES, tm)
    big = jax.ShapeDtypeStruct((PEER_HEADS, PEER_NKEYS, n), F32)
    blocked = jax.ShapeDtypeStruct((PEER_HEADS, n // cb, PEER_NKEYS, cb), BF16)

    def blk():
        return pl.BlockSpec((PEER_HEADS, PEER_NKEYS, tm), lambda i: (0, 0, i))

    def blk4():
        return pl.BlockSpec((PEER_HEADS, tm // cb, PEER_NKEYS, cb), lambda i: (0, i, 0, 0))

    return pl.pallas_call(
        _peer_sel_kernel,
        out_shape=(big, big, blocked, blocked),
        grid=(n // tm,),
        in_specs=[pl.BlockSpec((tm, d), lambda i: (i, 0)),
                  pl.BlockSpec((d, d), lambda i: (0, 0)),
                  pl.BlockSpec((2 * d, d), lambda i: (0, 0))],
        out_specs=(blk(), blk(), blk4(), blk4()),
        scratch_shapes=[pltpu.VMEM((2 * PEER_HEADS, PEER_NKEYS, tm), F32),
                        pltpu.VMEM((2 * PEER_HEADS, PEER_TOPK, tm), F32),
                        pltpu.VMEM((2 * PEER_HEADS, PEER_NKEYS, tm), F32),
                        pltpu.VMEM((PEER_HEADS, PEER_CAND, tm), F32),
                        pltpu.VMEM((PEER_HEADS, PEER_TOPK, tm), F32),
                        pltpu.VMEM((PEER_HEADS, PEER_CAND, tm), F32)],
        compiler_params=_cparams(("parallel",)),
    )(h2, wpqt, kbd)


def _gelu_tanh(x):
    return 0.5 * x * (1.0 + jnp.tanh(0.7978845608028654 * (x + 0.044715 * (x * x * x))))


def _peer_main_kernel(ni1, h_ref, x1_ref, g2_ref, u_ref, vt_ref, g_ref, cnt_ref, r2_ref, p2_ref, y_ref, acc, gate_s):
    j = pl.program_id(1)
    tm = h_ref.shape[0]

    @pl.when(j == 0)
    def _():
        acc[...] = jnp.zeros_like(acc)

    cb = r2_ref.shape[-1]
    reps = PEER_NKEYS // 16
    zero = jnp.zeros((PEER_NKEYS, cb), BF16)
    for l in range(ni1):
        for tc in range(tm // cb):
            ts = slice(tc * cb, (tc + 1) * cb)
            w = None
            for hd in range(PEER_HEADS):
                c16 = jnp.broadcast_to(cnt_ref[hd, l:l + 1, ts], (16, cb)).astype(BF16)
                g16 = jnp.broadcast_to(g_ref[hd, l:l + 1, ts], (16, cb)).astype(BF16)
                t = (jnp.where(r2_ref[hd, tc] < jnp.concatenate([c16] * reps, axis=0), p2_ref[hd, tc], zero)
                     * jnp.concatenate([g16] * reps, axis=0))
                w = t if w is None else w + t
            gate_s[tc, l * PEER_NKEYS:(l + 1) * PEER_NKEYS, :] = w

    act = lax.dot_general(u_ref[...], h_ref[...], NT, preferred_element_type=F32)
    gate = jnp.concatenate([gate_s[tc] for tc in range(tm // cb)], axis=1)
    coef = gate * _gelu_tanh(act.astype(BF16))
    acc[...] += jnp.dot(vt_ref[...], coef, preferred_element_type=F32)

    @pl.when(j == pl.num_programs(1) - 1)
    def _():
        y_ref[...] = x1_ref[...] + g2_ref[...] * acc[...].T


def _peer_main(h2, x1, g2, u16, vt16, g, cnt, r2, p2, seq_len):
    n, d = h2.shape
    tm = _row_tile(n, seq_len, 512)
    ni1 = 16
    et = ni1 * PEER_NKEYS
    cb = r2.shape[-1]
    g2_a, g2_s = _seq_operand(g2, seq_len, tm)

    def row():
        return pl.BlockSpec((tm, d), lambda i, j: (i, 0))

    return pl.pallas_call(
        functools.partial(_peer_main_kernel, ni1),
        out_shape=jax.ShapeDtypeStruct((n, d), F32),
        grid=(n // tm, N_EXPERTS // et),
        in_specs=[row(), row(), g2_s,
                  pl.BlockSpec((et, d), lambda i, j: (j, 0)),
                  pl.BlockSpec((d, et), lambda i, j: (0, j)),
                  pl.BlockSpec((PEER_HEADS, ni1, tm), lambda i, j: (0, j, i)),
                  pl.BlockSpec((PEER_HEADS, ni1, tm), lambda i, j: (0, j, i)),
                  pl.BlockSpec((PEER_HEADS, tm // cb, PEER_NKEYS, cb), lambda i, j: (0, i, 0, 0)),
                  pl.BlockSpec((PEER_HEADS, tm // cb, PEER_NKEYS, cb), lambda i, j: (0, i, 0, 0))],
        out_specs=row(),
        scratch_shapes=[pltpu.VMEM((d, tm), F32), pltpu.VMEM((tm // cb, et, cb), BF16)],
        compiler_params=_cparams(("parallel", "arbitrary")),
    )(h2, x1, g2_a, u16, vt16, g, cnt, r2, p2)


def _layer(x, mod, pos, shift_prev, s0, cache, lw):
    nb, t, d = x.shape
    n = nb * t
    sh1, sc1, g1, sh2, sc2, g2 = [mod[:, i * d:(i + 1) * d] for i in range(6)]
    x2 = x.reshape(n, d)
    P = _inproj(x2, sc1, sh1, lw['norm1_w'], lw['w_in16'], t)

    prev = _pack_rw(shift_prev).reshape(nb, 1, P_COLS)
    o_a, zf = _rwkv(P, nb, t, prev, lw['mu'], lw['w0'], lw['a0'], lw['k_k'], lw['k_a'], lw['r_k'], lw['lnx_w'],
                    lw['lnx_b'], lw['wup'], lw['aup'], lw['gup'], _state_to_pairs(s0))
    wkv = _pairs_to_state(zf)
    shift_last = _unpack_rw(P.reshape(nb, t, P_COLS)[:, -1, :])

    q16, k32, k16, v32, v16, qi16, kw32, ki2 = _dsa_prep(P, jnp.tile(pos, nb), lw['q_norm_w'], lw['k_norm_w'])
    if cache is None:
        o_b = _attn_prompt(q16, qi16, kw32, k16, v16, ki2, nb, t)
    else:
        ck, cv, cki = cache
        past = ck.shape[1]
        o_b = _attn_sample(q16, qi16, kw32, k16, v16, ki2, ck.reshape(nb, past, d), cv.reshape(nb, past, d), cki,
                           nb, t)

    x1, h2 = _merge(x2, o_a, o_b, P, lw['b_gate'], g1, sc2, sh2, lw['norm2_w'], lw['wpa'], lw['wpb'], lw['wout'], t)
    g, cnt, r2, p2 = _peer_select(h2, lw['wpqt'], lw['kbd'])
    y = _peer_main(h2, x1, g2, lw['u16'], lw['vt16'], g, cnt, r2, p2, t)

    k_new = k32.reshape(nb, t, N_HEADS, HEAD_DIM)
    v_new = v32.reshape(nb, t, N_HEADS, HEAD_DIM)
    ki_new = kw32[:, :IDX_DIM].reshape(nb, t, IDX_DIM)
    return y.reshape(nb, t, d), wkv, shift_last, k_new, v_new, ki_new


def _layer_weights(l, w_in, b_gate, mu_rw, w0, w_up, a0, a_up, g_up, k_k, k_a, r_k, lnx_w, lnx_b, q_norm_w, k_norm_w,
                   w_proj_a, w_proj_b, w_out, norm1_w, norm2_w, w_pq, peer_keys, peer_u, peer_v):
    d = D_MODEL
    zeros = lambda r: jnp.zeros((r, d), F32)
    keys = peer_keys[l].reshape(2 * PEER_HEADS, PEER_NKEYS, PEER_DHALF)
    eye = jnp.eye(2 * PEER_HEADS, dtype=F32)
    kbd = (eye[:, None, :, None] * keys[:, :, None, :]).reshape(2 * d, d)
    return {
        'w_in16': _pack_in(w_in[l]).astype(BF16), 'b_gate': b_gate[l], 'mu': _pack_rw(mu_rw[l]).reshape(1, P_COLS),
        'w0': w0[l], 'a0': a0[l], 'k_k': k_k[l], 'k_a': k_a[l], 'r_k': r_k[l].reshape(d), 'lnx_w': lnx_w[l],
        'lnx_b': lnx_b[l],
        'wup': jnp.concatenate([w_up[l], zeros(LANES - D_DECAY)], axis=0).astype(BF16),
        'aup': jnp.concatenate([zeros(D_DECAY), a_up[l]], axis=0).astype(BF16),
        'gup': jnp.concatenate([g_up[l], zeros(256 - D_GATE)], axis=0).astype(BF16),
        'q_norm_w': q_norm_w[l], 'k_norm_w': k_norm_w[l], 'norm1_w': norm1_w[l], 'norm2_w': norm2_w[l],
        'wpa': w_proj_a[l].astype(BF16), 'wpb': w_proj_b[l].astype(BF16), 'wout': w_out[l].astype(BF16),
        'wpqt': w_pq[l].T.astype(BF16), 'kbd': kbd.astype(BF16),
        'u16': peer_u[l].astype(BF16), 'vt16': peer_v[l].T.astype(BF16),
    }


def kernel(x_prompt, x_sample, c_prompt, c_sample, cache_k, cache_v, cache_kidx, state_wkv, state_shift, w_ada, b_ada,
           norm1_w, w_in, b_gate, mu_rw, w0, w_up, a0, a_up, g_up, k_k, k_a, r_k, lnx_w, lnx_b, q_norm_w, k_norm_w,
           w_proj_a, w_proj_b, w_out, norm2_w, w_pq, peer_keys, peer_u, peer_v):
    depth = w_in.shape[0]
    bp, tp = x_prompt.shape[:2]
    bs, ts = x_sample.shape[:2]
    past = cache_k.shape[2]
    dt = x_prompt.dtype
    pos_p = jnp.arange(tp, dtype=I32)
    pos_s = past + jnp.arange(ts, dtype=I32)
    zero_shift = jnp.zeros((bp, RW_IN), dt)
    zero_wkv = jnp.zeros((bp, N_HEADS, HEAD_DIM, HEAD_DIM), dt)
    c_all = jnp.concatenate([c_prompt, c_sample], axis=0)
    xp, xs = x_prompt, x_sample
    outs_p, outs_s = [], []
    for l in range(depth):
        lw = _layer_weights(l, w_in, b_gate, mu_rw, w0, w_up, a0, a_up, g_up, k_k, k_a, r_k, lnx_w, lnx_b, q_norm_w,
                            k_norm_w, w_proj_a, w_proj_b, w_out, norm1_w, norm2_w, w_pq, peer_keys, peer_u, peer_v)
        mod = _ada(c_all, w_ada[l], b_ada[l])
        xp, *rest_p = _layer(xp, mod[:bp], pos_p, zero_shift, zero_wkv, None, lw)
        xs, *rest_s = _layer(xs, mod[bp:], pos_s, state_shift[l], state_wkv[l],
                             (cache_k[l], cache_v[l], cache_kidx[l]), lw)
        outs_p.append(rest_p)
        outs_s.append(rest_s)
    stack = lambda outs, i: jnp.stack([o[i] for o in outs])
    return (xp, xs,
            stack(outs_p, 0), stack(outs_p, 1), stack(outs_p, 2), stack(outs_p, 3), stack(outs_p, 4),
            stack(outs_s, 0), stack(outs_s, 1), stack(outs_s, 2), stack(outs_s, 3), stack(outs_s, 4))
```

```python
import functools

import jax
import jax.numpy as jnp
from jax import lax
from jax.experimental import pallas as pl
from jax.experimental.pallas import tpu as pltpu

F32 = jnp.float32
BF16 = jnp.bfloat16
I32 = jnp.int32

LANES = 128
D_MODEL = 1024
EPS = 1e-6
GN_EPS = 64e-5
ROPE_THETA = 10000.0
CHUNK = 64
TOPK_MAX = 256
HEAD_DIM = 64
N_HEADS = D_MODEL // HEAD_DIM
N_PAIRS = N_HEADS // 2
IDX_HEADS = 8
IDX_DIM = 64
D_DECAY = 64
D_AAA = 64
D_GATE = 160
RW_IN = 3 * D_MODEL + D_DECAY + D_AAA + D_GATE
PEER_HEADS = 8
PEER_NKEYS = 128
PEER_TOPK = 16
PEER_DHALF = 64
N_EXPERTS = PEER_NKEYS * PEER_NKEYS
RW_CHUNK = 64
RW_INTERLEAVE = 16
RW_PASSES = (2, 1, 1, 1)
VMEM_LIMIT = 56 * 1024 * 1024
LOG2E = 1.4426950408889634
PEER_CAND = 80
GATE_ROWS = 32

C_R, C_K, C_V = 0, 1024, 2048
C_Q, C_KD, C_VD = 3072, 4096, 5120
C_GA, C_GB = 6144, 7168
C_QI = 8192
C_G = 8704
C_M = 8960
C_KW = 9088
P_COLS = 9216
IN_W = 9064

NT = (((1,), (1,)), ((), ()))
NN = (((1,), (0,)), ((), ()))


def _pack_in(w):
    z = lambda k: jnp.zeros(w.shape[:-1] + (k,), w.dtype)
    return jnp.concatenate([w[..., 0:3072], w[..., 3360:6432], w[..., 7016:9064], w[..., 6432:6944],
                            w[..., 3200:3360], z(256 - D_GATE), w[..., 3072:3200],
                            w[..., 6944:7016], z(LANES - IDX_DIM - IDX_HEADS)], axis=-1)


def _pack_rw(a):
    return _pack_in(jnp.concatenate([a, jnp.zeros(a.shape[:-1] + (IN_W - RW_IN,), a.dtype)], axis=-1))


def _unpack_rw(p):
    return jnp.concatenate([p[..., :3072], p[..., C_M:C_M + 128], p[..., C_G:C_G + D_GATE]], axis=-1)


def _split_bf16(x, n):
    parts = []
    r = x
    for _ in range(n):
        p = r.astype(BF16)
        parts.append(p)
        r = r - p.astype(F32)
    return parts


def _mm(a, b, pa=1, pb=1, dims=NN):
    aps = _split_bf16(a, pa) if a.dtype != BF16 else [a]
    bps = _split_bf16(b, pb) if b.dtype != BF16 else [b]
    order = max(len(aps), len(bps))
    out = None
    for i, ap in enumerate(aps):
        for j, bp in enumerate(bps):
            if i + j >= order:
                continue
            t = lax.dot_general(ap, bp, dims, preferred_element_type=F32)
            out = t if out is None else out + t
    return out


def _sigmoid(x):
    return 1.0 / (1.0 + jnp.exp(-x))


def _softplus(z):
    return jnp.maximum(z, 0.0) + jnp.log(1.0 + jnp.exp(-jnp.abs(z)))


def _cparams(sem):
    return pltpu.CompilerParams(dimension_semantics=sem, vmem_limit_bytes=VMEM_LIMIT)


def _ada_kernel(c_ref, w_ref, b_ref, o_ref):
    c = c_ref[...]
    s = c * _sigmoid(c)
    o_ref[...] = _mm(s, w_ref[...], 2, 2) + b_ref[...]


def _ada(c, w, b):
    m, d = c.shape
    n = w.shape[1]
    tn = 1024
    return pl.pallas_call(
        _ada_kernel,
        out_shape=jax.ShapeDtypeStruct((m, n), F32),
        grid=(n // tn,),
        in_specs=[pl.BlockSpec((m, d), lambda j: (0, 0)),
                  pl.BlockSpec((d, tn), lambda j: (0, j)),
                  pl.BlockSpec((1, tn), lambda j: (0, j))],
        out_specs=pl.BlockSpec((m, tn), lambda j: (0, j)),
        compiler_params=_cparams(("arbitrary",)),
    )(c, w, b.reshape(1, n))


def _seq_operand(vec, seq_len, tm):
    b, d = vec.shape
    if seq_len % tm == 0:
        per = seq_len // tm
        arr = vec.reshape(b, 1, d)
        spec = pl.BlockSpec((None, 1, d), lambda *g: (g[0] // per, 0, 0))
    else:
        assert tm % seq_len == 0
        arr = jnp.repeat(vec, seq_len, axis=0)
        spec = pl.BlockSpec((tm, d), lambda *g: (g[0], 0))
    return arr, spec


def _row_tile(n, seq_len, cap):
    tm = min(cap, n)
    while n % tm or (seq_len % tm and tm % seq_len):
        tm //= 2
    return tm


def _inproj_kernel(x_ref, sc_ref, sh_ref, nw_ref, w_ref, o_ref, h_scr):
    @pl.when(pl.program_id(1) == 0)
    def _():
        x = x_ref[...]
        y = x * lax.rsqrt(jnp.mean(x * x, axis=-1, keepdims=True) + EPS) * nw_ref[...]
        h_scr[...] = (y * (1.0 + sc_ref[...]) + sh_ref[...]).astype(BF16)

    o_ref[...] = jnp.dot(h_scr[...], w_ref[...], preferred_element_type=F32)


def _inproj(x2, sc, sh, nw, w16, seq_len):
    n, d = x2.shape
    tm = _row_tile(n, seq_len, 1024)
    tn = 1024
    sc_a, sc_s = _seq_operand(sc, seq_len, tm)
    sh_a, sh_s = _seq_operand(sh, seq_len, tm)
    return pl.pallas_call(
        _inproj_kernel,
        out_shape=jax.ShapeDtypeStruct((n, P_COLS), F32),
        grid=(n // tm, P_COLS // tn),
        in_specs=[pl.BlockSpec((tm, d), lambda i, j: (i, 0)), sc_s, sh_s,
                  pl.BlockSpec((1, d), lambda i, j: (0, 0)),
                  pl.BlockSpec((d, tn), lambda i, j: (0, j))],
        out_specs=pl.BlockSpec((tm, tn), lambda i, j: (i, j)),
        scratch_shapes=[pltpu.VMEM((tm, d), BF16)],
        compiler_params=_cparams(("parallel", "arbitrary")),
    )(x2, sc_a, sh_a, nw.reshape(1, d), w16)


def _lane_lo(shape):
    return lax.broadcasted_iota(I32, shape, len(shape) - 1) < HEAD_DIM


def _pair_sum(x):
    lo = _lane_lo(x.shape)
    s0 = jnp.sum(jnp.where(lo, x, 0.0), axis=-1, keepdims=True)
    s1 = jnp.sum(jnp.where(lo, 0.0, x), axis=-1, keepdims=True)
    return jnp.where(lo, s0, s1)


def _stack2(x):
    lo = _lane_lo(x.shape)
    return jnp.concatenate([jnp.where(lo, x, 0.0), jnp.where(lo, 0.0, x)], axis=0)


def _rwkv_kernel(t_real, npair, nchunk, pr, pk, pv, pg, pm, sr, sk, sv, sg, sm, mr, mk, mv, mg, mmu,
                 w0, a0, kkw, kaw, rkw, lnw, lnb, wup, aup, gup, z0, o_ref, zf_ref,
                 r_s, lw_s, k_s, v_s, a_s, b_s, y_s, bonus_s, g_s):
    C = RW_CHUNK
    t_pad = r_s.shape[1]

    def mix(p_ref, s_ref, m_ref):
        p = p_ref[...]
        prev = pltpu.roll(p, 1, 0)
        row = lax.broadcasted_iota(I32, p.shape, 0)
        prev = jnp.where(row == 0, s_ref[...], prev)
        return p + (prev - p) * m_ref[...]

    xg, xm = mix(pg, sg, mg), mix(pm, sm, mmu)
    th16, xm16, sg16 = jnp.tanh(xm).astype(BF16), xm.astype(BF16), _sigmoid(xg).astype(BF16)
    xr_all, xk_all, xv_all = mix(pr, sr, mr), mix(pk, sk, mk), mix(pv, sv, mv)

    def put(ref, pp, val):
        if t_pad > t_real:
            val = jnp.concatenate([val, jnp.zeros((t_pad - t_real, LANES), F32)], axis=0)
        ref[pp] = val

    for pp in range(npair):
        cs = slice(pp * LANES, (pp + 1) * LANES)
        xr, xk, xv = xr_all[:, cs], xk_all[:, cs], xv_all[:, cs]
        dw = jnp.dot(th16, wup[:, cs], preferred_element_type=F32)
        lw = -jnp.exp(-_softplus(-(w0[:, cs] + dw)) - 0.5)
        asig = _sigmoid(a0[:, cs] + jnp.dot(xm16, aup[:, cs], preferred_element_type=F32))
        g_s[pp] = jnp.dot(sg16, gup[:, cs], preferred_element_type=F32)
        kk = xk * kkw[:, cs]
        kk = kk * lax.rsqrt(_pair_sum(kk * kk) + 1e-12)
        kmod = xk * (1.0 + (asig - 1.0) * kaw[:, cs])
        bonus_s[pp] = _pair_sum(xr * kmod * rkw[:, cs]) * xv
        put(r_s, pp, xr)
        put(lw_s, pp, lw)
        put(k_s, pp, kmod)
        put(v_s, pp, xv)
        put(a_s, pp, -kk)
        put(b_s, pp, kk * asig)

    n2 = 2 * C
    ri = lax.broadcasted_iota(I32, (n2, n2), 0)
    ci = lax.broadcasted_iota(I32, (n2, n2), 1)
    same = (ri // C) == (ci // C)
    strict = same & ((ri % C) > (ci % C))
    incl = same & ((ri % C) >= (ci % C))
    eye = ri == ci
    eye_f = jnp.where(eye, 1.0, 0.0)
    tri = jnp.where(lax.broadcasted_iota(I32, (C, C), 0) >= lax.broadcasted_iota(I32, (C, C), 1), 1.0, 0.0
                    ).astype(BF16)
    zeros_sq = jnp.zeros((n2, LANES), F32)

    pg_, pi_, po_, ps_ = RW_PASSES

    def local(chains):
        each = lambda f, *cols: [f(*xs) for xs in zip(*cols)]
        lwc = [lw_s[pp, sl, :] for sl, pp in chains]
        cum = each(lambda l: _mm(tri, l, 1, 3), lwc)
        cum_last = each(lambda c: c[C - 1:C, :], cum)
        ec, eci = each(jnp.exp, cum), each(lambda c: jnp.exp(-c), cum)
        ecp = each(lambda c, l: jnp.exp(c - l), cum, lwc)
        ecl = each(lambda c, cl: jnp.exp(cl - c), cum, cum_last)
        a_c = [a_s[pp, sl, :] for sl, pp in chains]
        b_c = [b_s[pp, sl, :] for sl, pp in chains]
        k_c = [k_s[pp, sl, :] for sl, pp in chains]
        r_c = [r_s[pp, sl, :] for sl, pp in chains]
        As = each(lambda a, e: _stack2(a * e), a_c, ecp)
        Rs = each(lambda r, e: _stack2(r * e), r_c, ec)
        Bs = each(lambda b, e: _stack2(b * e), b_c, eci)
        Ks = each(lambda k, e: _stack2(k * e), k_c, eci)
        Bt = each(lambda b, e: _stack2(b * e), b_c, ecl)
        Kt = each(lambda k, e: _stack2(k * e), k_c, ecl)
        Vs = [_stack2(v_s[pp, sl, :]) for sl, pp in chains]

        G = each(lambda a, r, b, k: _mm(jnp.concatenate([a, r], axis=0), jnp.concatenate([b, k], axis=0),
                                        pg_, pg_, NT), As, Rs, Bs, Ks)
        a_ab = each(lambda g: jnp.where(strict, g[:n2, :n2], 0.0), G)
        a_ak = each(lambda g: jnp.where(strict, g[:n2, n2:], 0.0), G)
        a_rb = each(lambda g: jnp.where(incl, g[n2:, :n2], 0.0), G)
        a_rk = each(lambda g: jnp.where(incl, g[n2:, n2:], 0.0), G)

        lp = a_ab
        tm_ = each(lambda a: eye_f + a, a_ab)
        step = 2
        while step < C:
            lp = each(lambda l: _mm(l, l, pi_, pi_), lp)
            tm_ = each(lambda t, l: t + _mm(t, l, pi_, pi_), tm_, lp)
            step *= 2

        w1 = each(lambda a, v: _mm(a, v, po_, po_), a_ak, Vs)
        mu_ = each(lambda t, a, w: _mm(t, jnp.concatenate([a, w], axis=1), po_, po_), tm_, As, w1)
        rhs = each(lambda m, v: jnp.concatenate([m, jnp.concatenate([zeros_sq, v], axis=1)], axis=0), mu_, Vs)
        lhs = each(lambda rb, rk, b, k: jnp.concatenate([jnp.concatenate([rb, rk], axis=1),
                                                         jnp.concatenate([b.T, k.T], axis=1)], axis=0),
                   a_rb, a_rk, Bt, Kt)
        out2 = each(lambda l, r: _mm(l, r, po_, po_), lhs, rhs)
        m23 = each(lambda r, o, cl: jnp.concatenate([r + o[:n2, :LANES],
                                                     jnp.where(eye, jnp.exp(cl), 0.0) + o[n2:, :LANES]], axis=0),
                   Rs, out2, cum_last)
        return [(m, o[:n2, LANES:], o[n2:, LANES:]) for m, o in zip(m23, out2)]

    def step_chunks(i, zs):
        sls = [pl.ds(pl.multiple_of((i * nchunk + j) * C, C), C) for j in range(nchunk)]
        parts = local([(sl, pp) for sl in sls for pp in range(npair)])
        zs = list(zs)
        for j, sl in enumerate(sls):
            for pp in range(npair):
                m23, y_loc, z_loc = parts[j * npair + pp]
                yz = _mm(m23, zs[pp], ps_, ps_)
                y = yz[:n2] + y_loc
                y_s[pp, sl, :] = y[:C] + y[C:]
                zs[pp] = yz[n2:] + z_loc
        return tuple(zs)

    zs = lax.fori_loop(0, t_pad // (C * nchunk), step_chunks, tuple(z0[pp] for pp in range(npair)))
    for pp in range(npair):
        zf_ref[pp] = zs[pp]
        cs = slice(pp * LANES, (pp + 1) * LANES)
        y = y_s[pp, 0:t_real, :]
        mean = _pair_sum(y) * (1.0 / HEAD_DIM)
        dlt = y - mean
        var = _pair_sum(dlt * dlt) * (1.0 / HEAD_DIM)
        yn = dlt * lax.rsqrt(var + GN_EPS) * lnw[:, cs] + lnb[:, cs]
        o_ref[:, cs] = ((yn + bonus_s[pp]) * g_s[pp]).astype(o_ref.dtype)


def _rwkv(P, nb, t, prev, mu, w0, a0, k_k, k_a, r_k, lnx_w, lnx_b, wup, aup, gup, z0):
    t_pad = max(t, RW_CHUNK)
    assert t % 8 == 0 and t_pad % RW_CHUNK == 0
    n_chunks = t_pad // RW_CHUNK
    nchunk = min(RW_INTERLEAVE, n_chunks)
    npair = min(N_PAIRS, max(1, RW_INTERLEAVE // nchunk))
    wp = npair * LANES

    def cblk(c0, w, per_pair):
        return (lambda p: c0 // w + p) if per_pair else (lambda p: c0 // w)

    def pcol(c0, w, pp):
        f = cblk(c0, w, pp)
        return pl.BlockSpec((t, w), lambda b, p: (b, f(p)))

    def prevcol(c0, w, pp):
        f = cblk(c0, w, pp)
        return pl.BlockSpec((None, 1, w), lambda b, p: (b, 0, f(p)))

    def mucol(c0, w, pp):
        f = cblk(c0, w, pp)
        return pl.BlockSpec((1, w), lambda b, p: (0, f(p)))

    def hvec():
        return pl.BlockSpec((1, wp), lambda b, p: (0, p))

    cols = [(C_R, wp, True), (C_K, wp, True), (C_V, wp, True), (C_G, 256, False), (C_M, LANES, False)]
    in_specs = ([pcol(*c) for c in cols] + [prevcol(*c) for c in cols] + [mucol(*c) for c in cols]
                + [hvec() for _ in range(7)]
                + [pl.BlockSpec((LANES, wp), lambda b, p: (0, p)),
                   pl.BlockSpec((LANES, wp), lambda b, p: (0, p)),
                   pl.BlockSpec((256, wp), lambda b, p: (0, p)),
                   pl.BlockSpec((None, npair, LANES, LANES), lambda b, p: (b, p, 0, 0))])
    vecs = [v.reshape(1, D_MODEL) for v in (w0, a0, k_k, k_a, r_k, lnx_w, lnx_b)]
    o, zf = pl.pallas_call(
        functools.partial(_rwkv_kernel, t, npair, nchunk),
        out_shape=(jax.ShapeDtypeStruct((nb * t, D_MODEL), BF16),
                   jax.ShapeDtypeStruct((nb, N_PAIRS, LANES, LANES), F32)),
        grid=(nb, N_PAIRS // npair),
        in_specs=in_specs,
        out_specs=(pl.BlockSpec((t, wp), lambda b, p: (b, p)),
                   pl.BlockSpec((None, npair, LANES, LANES), lambda b, p: (b, p, 0, 0))),
        scratch_shapes=([pltpu.VMEM((npair, t_pad, LANES), F32) for _ in range(7)]
                        + [pltpu.VMEM((npair, t, LANES), F32) for _ in range(2)]),
        compiler_params=_cparams(("parallel", "arbitrary")),
    )(P, P, P, P, P, prev, prev, prev, prev, prev, mu, mu, mu, mu, mu, *vecs, wup, aup, gup, z0)
    return o, zf


def _state_to_pairs(s):
    nb = s.shape[0]
    zt = jnp.swapaxes(s, -1, -2).reshape(nb, N_PAIRS, 2, HEAD_DIM, HEAD_DIM)
    zero = jnp.zeros_like(zt[:, :, 0])
    top = jnp.concatenate([zt[:, :, 0], zero], axis=-1)
    bot = jnp.concatenate([zero, zt[:, :, 1]], axis=-1)
    return jnp.concatenate([top, bot], axis=-2)


def _pairs_to_state(z):
    nb = z.shape[0]
    h0 = z[:, :, :HEAD_DIM, :HEAD_DIM]
    h1 = z[:, :, HEAD_DIM:, HEAD_DIM:]
    s = jnp.stack([h0, h1], axis=2).reshape(nb, N_HEADS, HEAD_DIM, HEAD_DIM)
    return jnp.swapaxes(s, -1, -2)


def _rope(x, cos, sin_signed):
    w = x.shape[1]
    reps = w // LANES
    cw = jnp.concatenate([cos] * reps, axis=1) if reps > 1 else cos
    sw = jnp.concatenate([sin_signed] * reps, axis=1) if reps > 1 else sin_signed
    lane = lax.broadcasted_iota(I32, x.shape, 1)
    fwd = pltpu.roll(x, w - 32, 1)
    bwd = pltpu.roll(x, 32, 1)
    partner = jnp.where((lane % HEAD_DIM) < 32, fwd, bwd)
    return x * cw + partner * sw


def _head_rms(x, nw, e_dn, e_up):
    ms = _mm(x * x, e_dn, 2, 1) * (1.0 / HEAD_DIM)
    r = lax.rsqrt(ms + EPS)
    return x * _mm(r, e_up, 2, 1) * nw


def _dsa_prep_kernel(pq, pkd, pvd, pqi, pkw, cos_ref, sin_ref, qn, kn, edn, eup,
                     q16, k32, k16, v32, v16, qi16, kw32, ki2):
    cos, sin = cos_ref[...], sin_ref[...]
    e_dn, e_up = edn[...], eup[...]
    def put_pairs(ref, x):
        for p in range(N_PAIRS):
            ref[p] = x[:, p * LANES:(p + 1) * LANES].astype(ref.dtype)

    q = _rope(_head_rms(pq[...], qn[...], e_dn, e_up), cos, sin)
    put_pairs(q16, q * (HEAD_DIM ** -0.5 * LOG2E))
    k = _rope(_head_rms(pkd[...], kn[...], e_dn, e_up), cos, sin)
    k32[...] = k
    put_pairs(k16, k)
    v = pvd[...]
    v32[...] = v
    put_pairs(v16, v)
    qi16[...] = _rope(pqi[...], cos, sin).astype(BF16)
    kw = pkw[...]
    lane = lax.broadcasted_iota(I32, kw.shape, 1)
    wi_scale = (IDX_HEADS * IDX_DIM) ** -0.5
    kr = _rope(kw, cos, sin)
    kw32[...] = jnp.where(lane < IDX_DIM, kr, jnp.where(lane < IDX_DIM + IDX_HEADS, kw * wi_scale, 0.0))
    ki2[...] = jnp.where(lane < IDX_DIM, kr, pltpu.roll(kr, IDX_DIM, 1)).astype(BF16)


def _dsa_prep(P, pos_rows, q_norm_w, k_norm_w):
    n = P.shape[0]
    tm = 512 if n % 512 == 0 else n
    half = HEAD_DIM // 2
    inv = ROPE_THETA ** (-jnp.arange(half, dtype=F32) / half)
    ang = pos_rows.astype(F32)[:, None] * inv[None, :]
    cos = jnp.tile(jnp.cos(ang), (1, 4))
    sin = jnp.sin(ang)
    sin_signed = jnp.tile(jnp.concatenate([-sin, sin], axis=1), (1, 2))
    head_of = jnp.arange(D_MODEL) // HEAD_DIM
    e_dn = (head_of[:, None] == jnp.arange(LANES)[None, :]).astype(BF16)
    e_up = e_dn.T
    qn = jnp.tile(q_norm_w, N_HEADS).reshape(1, D_MODEL)
    kn = jnp.tile(k_norm_w, N_HEADS).reshape(1, D_MODEL)

    def col(c0, w):
        return pl.BlockSpec((tm, w), lambda i, c0=c0, w=w: (i, c0 // w))

    def row(w):
        return pl.BlockSpec((tm, w), lambda i: (i, 0))

    def const(shape):
        return pl.BlockSpec(shape, lambda i: (0, 0))

    pairs = jax.ShapeDtypeStruct((N_PAIRS, n, LANES), BF16)
    pair_spec = pl.BlockSpec((N_PAIRS, tm, LANES), lambda i: (0, i, 0))
    return pl.pallas_call(
        _dsa_prep_kernel,
        out_shape=(pairs, jax.ShapeDtypeStruct((n, D_MODEL), F32), pairs, jax.ShapeDtypeStruct((n, D_MODEL), F32),
                   pairs, jax.ShapeDtypeStruct((n, IDX_HEADS * IDX_DIM), BF16),
                   jax.ShapeDtypeStruct((n, LANES), F32), jax.ShapeDtypeStruct((n, LANES), BF16)),
        grid=(n // tm,),
        in_specs=[col(C_Q, 1024), col(C_KD, 1024), col(C_VD, 1024), col(C_QI, 512), col(C_KW, LANES),
                  row(LANES), row(LANES), const((1, D_MODEL)), const((1, D_MODEL)),
                  const((D_MODEL, LANES)), const((LANES, D_MODEL))],
        out_specs=(pair_spec, row(D_MODEL), pair_spec, row(D_MODEL), pair_spec, row(512), row(LANES), row(LANES)),
        compiler_params=_cparams(("parallel",)),
    )(P, P, P, P, P, cos, sin_signed, qn, kn, e_dn, e_up)


CODE_NEG_INF = -1 - 0x7F800000


def _index_scores(qi, wi, ki_list):
    outs = []
    for ki in ki_list:
        acc = None
        for h in range(IDX_HEADS):
            qpair = qi[:, (h // 2) * LANES:(h // 2 + 1) * LANES]
            lo = _lane_lo(qpair.shape)
            qh = jnp.where(lo if h % 2 == 0 else jnp.logical_not(lo), qpair, jnp.zeros_like(qpair))
            rel = lax.dot_general(qh, ki, NT, preferred_element_type=F32)
            term = wi[:, IDX_DIM + h:IDX_DIM + h + 1] * jnp.maximum(rel, 0.0)
            acc = term if acc is None else acc + term
        outs.append(acc)
    return outs


def _select_topk(keys, topk, bias_refs):
    tq = keys[0].shape[0]
    neg = -jnp.inf

    def write(masks):
        for ref, k, msk in zip(bias_refs, keys, masks):
            ref[:, 0:k.shape[1]] = jnp.where(msk, 0.0, neg)

    def count(pred_list):
        tot = None
        for p in pred_list:
            c = jnp.sum(jnp.where(p, 1.0, 0.0), axis=-1, keepdims=True)
            tot = c if tot is None else tot + c
        return tot

    def threshold(c):
        bits = jnp.where(c >= 0, c, c ^ jnp.int32(0x7FFFFFFF))
        return jnp.where(c < jnp.int32(CODE_NEG_INF), neg, lax.bitcast_convert_type(bits, F32))

    few = count([k > neg for k in keys]) <= topk

    def pending(cnt):
        return jnp.max(jnp.where(few | (cnt == topk), 0.0, 1.0))

    def bit_step(state):
        i, c, cnt, _ = state
        trial = c + jnp.left_shift(jnp.int32(1), 31 - i)
        cnt_t = count([k >= threshold(trial) for k in keys])
        take = cnt_t >= topk
        cnt = jnp.where(take, cnt_t, cnt)
        return i + 1, jnp.where(take, trial, c), cnt, pending(cnt)

    cnt0 = jnp.full((tq, 1), float(sum(k.shape[1] for k in keys)), F32)
    state = (jnp.int32(0), jnp.full((tq, 1), -2 ** 31, I32), cnt0, pending(cnt0))
    _, code, _, _ = lax.while_loop(lambda s: (s[0] < 32) & (s[3] > 0.0), bit_step, state)
    thr = threshold(code)
    ge = [(k >= thr) & (k > neg) for k in keys]
    write(ge)
    surplus = jnp.max(count(ge)) > topk

    @pl.when(surplus)
    def _():
        gt = [k > thr for k in keys]
        need = topk - count(gt)
        ties = [(k == thr) & (k > neg) for k in keys]
        offs, idx = 0, []
        for k in keys:
            idx.append(lax.broadcasted_iota(I32, k.shape, 1) + offs)
            offs += k.shape[1]
        nbits = max(1, (offs - 1).bit_length() + 1)

        def idx_step(i, m):
            trial = m + jnp.left_shift(jnp.int32(1), nbits - 1 - i)
            cnt = count([t & (ix < trial) for t, ix in zip(ties, idx)])
            return jnp.where(cnt <= need, trial, m)

        cut = lax.fori_loop(0, nbits, idx_step, jnp.zeros((tq, 1), I32))
        write([g | (t & (ix < cut)) for g, t, ix in zip(gt, ties, idx)])


def _attend_pair(q_pair, k_list, v_list, bias_list):
    lo = _lane_lo(q_pair.shape)
    zero = jnp.zeros_like(q_pair)
    outs = []
    for qh in (jnp.where(lo, q_pair, zero), jnp.where(lo, zero, q_pair)):
        s_list = [lax.dot_general(qh, k, NT, preferred_element_type=F32) + b for k, b in zip(k_list, bias_list)]
        m = None
        for s in s_list:
            mx = jnp.max(s, axis=-1, keepdims=True)
            m = mx if m is None else jnp.maximum(m, mx)
        den, acc = None, None
        for s, v in zip(s_list, v_list):
            p = jnp.exp2(s - m)
            d = jnp.sum(p, axis=-1, keepdims=True)
            o = jnp.dot(p.astype(BF16), v, preferred_element_type=F32)
            den = d if den is None else den + d
            acc = o if acc is None else acc + o
        outs.append(acc / den)
    return jnp.where(_lane_lo(outs[0].shape), outs[0], outs[1])


def _attn_prompt_kernel(topk, ncase, q_ref, qi_ref, kw_ref, k_ref, v_ref, ki2_ref, o_ref, bias_s):
    tq = q_ref.shape[1]
    t = k_ref.shape[1]
    i = pl.program_id(1)
    lstep = t // ncase
    case = ((i + 1) * tq - 1) // lstep

    def run(L):
        score = _index_scores(qi_ref[...], kw_ref[...], [ki2_ref[0:L, :]])[0]
        qpos = i * tq + lax.broadcasted_iota(I32, (tq, L), 0)
        kpos = lax.broadcasted_iota(I32, (tq, L), 1)
        adm = (qpos // CHUNK) >= (kpos // CHUNK)
        _select_topk([jnp.where(adm, score, -jnp.inf)], topk, [bias_s])

        def pair(p, carry):
            o = _attend_pair(q_ref[p], [k_ref[p, 0:L, :]], [v_ref[p, 0:L, :]], [bias_s[:, 0:L]])
            o_ref[p] = o.astype(o_ref.dtype)
            return carry

        lax.fori_loop(0, N_PAIRS, pair, 0)

    for c in range(ncase):
        pl.when(case == c)(functools.partial(run, (c + 1) * lstep))


def _attn_prompt(q16, qi16, kw32, k16, v16, ki2, nb, t):
    tq = min(256, t)
    topk = min(TOPK_MAX, t // 4)
    nq = t // tq
    ncase = min(4, nq)

    def qrow(w):
        return pl.BlockSpec((tq, w), lambda b, i: (b * nq + i, 0))

    def qpairs():
        return pl.BlockSpec((N_PAIRS, tq, LANES), lambda b, i: (0, b * nq + i, 0))

    def kpairs():
        return pl.BlockSpec((N_PAIRS, t, LANES), lambda b, i: (0, b, 0))

    return pl.pallas_call(
        functools.partial(_attn_prompt_kernel, topk, ncase),
        out_shape=jax.ShapeDtypeStruct((N_PAIRS, nb * t, LANES), BF16),
        grid=(nb, nq),
        in_specs=[qpairs(), qrow(512), qrow(LANES), kpairs(), kpairs(),
                  pl.BlockSpec((t, LANES), lambda b, i: (b, 0))],
        out_specs=qpairs(),
        scratch_shapes=[pltpu.VMEM((tq, t), F32)],
        compiler_params=_cparams(("parallel", "arbitrary")),
    )(q16, qi16, kw32, k16, v16, ki2)


def _attn_sample_kernel(topk, past, q_ref, qi_ref, kw_ref, ck_ref, cv_ref, cki2_ref, k_ref, v_ref, ki2_ref, o_ref,
                        biasc_s, biasn_s):
    ts = q_ref.shape[0]

    @pl.when(pl.program_id(1) == 0)
    def _():
        sc, sn = _index_scores(qi_ref[...], kw_ref[...], [cki2_ref[...], ki2_ref[...]])
        qpos = past + lax.broadcasted_iota(I32, (ts, 1), 0)
        kpos_c = lax.broadcasted_iota(I32, sc.shape, 1)
        kpos_n = past + lax.broadcasted_iota(I32, sn.shape, 1)
        keys = [jnp.where((qpos // CHUNK) >= (kpos_c // CHUNK), sc, -jnp.inf),
                jnp.where((qpos // CHUNK) >= (kpos_n // CHUNK), sn, -jnp.inf)]
        _select_topk(keys, topk, [biasc_s, biasn_s])

    o_ref[...] = _attend_pair(q_ref[...], [ck_ref[...].astype(BF16), k_ref[...]],
                              [cv_ref[...].astype(BF16), v_ref[...]],
                              [biasc_s[...], biasn_s[...]]).astype(o_ref.dtype)


def _attn_sample(q16, qi16, kw32, k16, v16, ki2, cache_k, cache_v, cache_kidx, nb, ts):
    past = cache_k.shape[1]
    cki2 = jnp.concatenate([cache_kidx, cache_kidx], axis=-1).astype(BF16)
    topk = min(TOPK_MAX, (past + ts) // 4)

    def qrow(w):
        return pl.BlockSpec((ts, w), lambda b, p: (b, 0))

    def qpair():
        return pl.BlockSpec((None, ts, LANES), lambda b, p: (p, b, 0))

    def cache(pair):
        return pl.BlockSpec((None, past, LANES), (lambda b, p: (b, 0, p)) if pair else (lambda b, p: (b, 0, 0)))

    return pl.pallas_call(
        functools.partial(_attn_sample_kernel, topk, past),
        out_shape=jax.ShapeDtypeStruct((N_PAIRS, nb * ts, LANES), BF16),
        grid=(nb, N_PAIRS),
        in_specs=[qpair(), qrow(512), qrow(LANES), cache(True), cache(True), cache(False),
                  qpair(), qpair(), qrow(LANES)],
        out_specs=qpair(),
        scratch_shapes=[pltpu.VMEM((ts, past), F32), pltpu.VMEM((ts, ts), F32)],
        compiler_params=_cparams(("parallel", "arbitrary")),
    )(q16, qi16, kw32, cache_k, cache_v, cki2, k16, v16, ki2)


def _merge_kernel(x_ref, oa_ref, ob_ref, pga_ref, pgb_ref, bga_ref, bgb_ref, g1_ref, sc2_ref, sh2_ref, nw_ref,
                  wpa_ref, wpb_ref, wout_ref, x1_ref, h2_ref):
    ga = _sigmoid(pga_ref[...] + bga_ref[...])
    gb = _sigmoid(pgb_ref[...] + bgb_ref[...])
    ob = jnp.concatenate([ob_ref[p] for p in range(N_PAIRS)], axis=1)
    m = (ga * jnp.dot(oa_ref[...], wpa_ref[...], preferred_element_type=F32)
         + gb * jnp.dot(ob, wpb_ref[...], preferred_element_type=F32))
    x1 = x_ref[...] + g1_ref[...] * jnp.dot(m.astype(BF16), wout_ref[...], preferred_element_type=F32)
    x1_ref[...] = x1
    y = x1 * lax.rsqrt(jnp.mean(x1 * x1, axis=-1, keepdims=True) + EPS) * nw_ref[...]
    h2_ref[...] = (y * (1.0 + sc2_ref[...]) + sh2_ref[...]).astype(BF16)


def _merge(x2, o_a, o_b, P, b_gate, g1, sc2, sh2, nw2, wpa, wpb, wout, seq_len):
    n, d = x2.shape
    tm = _row_tile(n, seq_len, 512)
    g1_a, g1_s = _seq_operand(g1, seq_len, tm)
    sc_a, sc_s = _seq_operand(sc2, seq_len, tm)
    sh_a, sh_s = _seq_operand(sh2, seq_len, tm)

    def row():
        return pl.BlockSpec((tm, d), lambda i: (i, 0))

    def const(shape):
        return pl.BlockSpec(shape, lambda i: (0, 0))

    bg = b_gate.reshape(1, 2 * d)
    return pl.pallas_call(
        _merge_kernel,
        out_shape=(jax.ShapeDtypeStruct((n, d), F32), jax.ShapeDtypeStruct((n, d), BF16)),
        grid=(n // tm,),
        in_specs=[row(), row(), pl.BlockSpec((N_PAIRS, tm, LANES), lambda i: (0, i, 0)),
                  pl.BlockSpec((tm, d), lambda i: (i, C_GA // d)), pl.BlockSpec((tm, d), lambda i: (i, C_GB // d)),
                  pl.BlockSpec((1, d), lambda i: (0, 0)), pl.BlockSpec((1, d), lambda i: (0, 1)),
                  g1_s, sc_s, sh_s, const((1, d)), const((d, d)), const((d, d)), const((d, d))],
        out_specs=(row(), row()),
        compiler_params=_cparams(("parallel",)),
    )(x2, o_a, o_b, P, P, bg, bg, g1_a, sc_a, sh_a, nw2.reshape(1, d), wpa, wpb, wout)


def _top_exact(s, k):
    rows = lax.broadcasted_iota(I32, s.shape, 0).astype(F32)
    cur = s
    rank = jnp.full(s.shape, float(k), F32)
    vals = []
    for r in range(k):
        m = jnp.max(cur, axis=0, keepdims=True)
        first = jnp.min(jnp.where(cur == m, rows, 1e9), axis=0, keepdims=True)
        hit = rows == first
        vals.append(m)
        rank = jnp.where(hit, float(r), rank)
        cur = jnp.where(hit, -jnp.inf, cur)
    return vals, rank


def _top_fast(ss, k):
    curs = list(ss)
    ranks = [jnp.full(s.shape, float(k), F32) for s in ss]
    vals = [[] for _ in ss]
    for r in range(k):
        ms = [jnp.max(c, axis=0, keepdims=True) for c in curs]
        hits = [c == m for c, m in zip(curs, ms)]
        ranks = [jnp.where(h, float(r), rk) for h, rk in zip(hits, ranks)]
        curs = [jnp.where(h, -jnp.inf, c) for h, c in zip(hits, curs)]
        for v, m in zip(vals, ms):
            v.append(m)
    cleans = [jnp.max(jnp.abs(jnp.sum(jnp.where(rk < k, 1.0, 0.0), axis=0, keepdims=True) - k)) == 0.0
              for rk in ranks]
    return vals, ranks, cleans


def _top(ss, k, vals_scr, rank_scr):
    vals, ranks, cleans = _top_fast(ss, k)
    for i, s in enumerate(ss):
        vals_scr[i] = jnp.concatenate(vals[i], axis=0)
        rank_scr[i] = ranks[i]

        @pl.when(jnp.logical_not(cleans[i]))
        def _(i=i, s=s):
            vals_e, rank_e = _top_exact(s, k)
            vals_scr[i] = jnp.concatenate(vals_e, axis=0)
            rank_scr[i] = rank_e


def _peer_sel_kernel(h_ref, wpqt_ref, kbd_ref, g_ref, cnt_ref, r2_ref, p2_ref, s_scr, vals_scr, rank_scr,
                     cand_scr, cvals_scr, crank_scr):
    K = PEER_TOPK
    tm = h_ref.shape[0]
    qt = lax.dot_general(wpqt_ref[...], h_ref[...], NT, preferred_element_type=F32)
    s_scr[...] = jnp.dot(kbd_ref[...], qt.astype(BF16), preferred_element_type=F32
                         ).reshape(2 * PEER_HEADS, PEER_NKEYS, tm)
    sub8 = lax.broadcasted_iota(I32, (8, tm), 0)
    neg = jnp.full((8, tm), -jnp.inf, F32)
    _top([s_scr[r] for r in range(2 * PEER_HEADS)], K, vals_scr, rank_scr)
    for hd in range(PEER_HEADS):
        c1, c2 = vals_scr[2 * hd], vals_scr[2 * hd + 1]
        blocks = [c1[0:1] + c2, c1[1:2] + c2[0:8]]
        for k1 in range(2, 8):
            blocks.append(jnp.where(sub8 < K // (k1 + 1), c1[k1:k1 + 1] + c2[0:8], neg))
        blocks.append(c1[8:16] + c2[0:1])
        cand_scr[hd] = jnp.concatenate(blocks, axis=0)
    _top([cand_scr[hd] for hd in range(PEER_HEADS)], K, cvals_scr, crank_scr)
    for hd in range(PEER_HEADS):
        s1, s2 = s_scr[2 * hd], s_scr[2 * hd + 1]
        c1, c2 = vals_scr[2 * hd], vals_scr[2 * hd + 1]
        rank1, rank2 = rank_scr[2 * hd], rank_scr[2 * hd + 1]
        cand = cand_scr[hd]
        taken = crank_scr[hd] < K
        z = jnp.sum(jnp.where(taken, jnp.exp(cand - (c1[0:1] + c2[0:1])), 0.0), axis=0, keepdims=True)
        tk = jnp.where(taken, 1.0, 0.0)
        per_k1 = [jnp.sum(tk[0:16], axis=0, keepdims=True)]
        per_k1 += [jnp.sum(tk[8 + 8 * k1:16 + 8 * k1], axis=0, keepdims=True) for k1 in range(1, 8)]
        cnt16 = jnp.concatenate(per_k1 + [tk[72:80]], axis=0)
        cnt = jnp.zeros(s1.shape, F32)
        for k1 in range(K):
            cnt = jnp.where(rank1 == float(k1), cnt16[k1:k1 + 1], cnt)
        g_ref[hd] = jnp.where(rank1 < K, jnp.exp(s1 - c1[0:1]) / z, 0.0)
        cnt_ref[hd] = cnt
        p2 = jnp.where(rank2 < K, jnp.exp(s2 - c2[0:1]), 0.0)
        cb = r2_ref.shape[-1]
        for tc in range(tm // cb):
            r2_ref[hd, tc] = rank2[:, tc * cb:(tc + 1) * cb].astype(r2_ref.dtype)
            p2_ref[hd, tc] = p2[:, tc * cb:(tc + 1) * cb].astype(p2_ref.dtype)


def _peer_select(h2, wpqt, kbd):
    n, d = h2.shape
    tm = 256 if n % 256 == 0 else n
    cb = min(LANES, tm)
    big = jax.ShapeDtypeStruct((PEER_HEADS, PEER_NKEYS, n), F32)
    blocked = jax.ShapeDtypeStruct((PEER_HEADS, n // cb, PEER_NKEYS, cb), BF16)

    def blk():
        return pl.BlockSpec((PEER_HEADS, PEER_NKEYS, tm), lambda i: (0, 0, i))

    def blk4():
        return pl.BlockSpec((PEER_HEADS, tm // cb, PEER_NKEYS, cb), lambda i: (0, i, 0, 0))

    return pl.pallas_call(
        _peer_sel_kernel,
        out_shape=(big, big, blocked, blocked),
        grid=(n // tm,),
        in_specs=[pl.BlockSpec((tm, d), lambda i: (i, 0)),
                  pl.BlockSpec((d, d), lambda i: (0, 0)),
                  pl.BlockSpec((2 * d, d), lambda i: (0, 0))],
        out_specs=(blk(), blk(), blk4(), blk4()),
        scratch_shapes=[pltpu.VMEM((2 * PEER_HEADS, PEER_NKEYS, tm), F32),
                        pltpu.VMEM((2 * PEER_HEADS, PEER_TOPK, tm), F32),
                        pltpu.VMEM((2 * PEER_HEADS, PEER_NKEYS, tm), F32),
                        pltpu.VMEM((PEER_HEADS, PEER_CAND, tm), F32),
                        pltpu.VMEM((PEER_HEADS, PEER_TOPK, tm), F32),
                        pltpu.VMEM((PEER_HEADS, PEER_CAND, tm), F32)],
        compiler_params=_cparams(("parallel",)),
    )(h2, wpqt, kbd)


def _gelu_tanh(x):
    return 0.5 * x * (1.0 + jnp.tanh(0.7978845608028654 * (x + 0.044715 * (x * x * x))))


def _peer_main_kernel(ni1, h_ref, x1_ref, g2_ref, u_ref, vt_ref, g_ref, cnt_ref, r2_ref, p2_ref, y_ref, acc, gate_s):
    j = pl.program_id(1)
    tm = h_ref.shape[0]

    @pl.when(j == 0)
    def _():
        acc[...] = jnp.zeros_like(acc)

    cb = r2_ref.shape[-1]
    reps = PEER_NKEYS // 16
    zero = jnp.zeros((PEER_NKEYS, cb), BF16)
    for l in range(ni1):
        for tc in range(tm // cb):
            ts = slice(tc * cb, (tc + 1) * cb)
            w = None
            for hd in range(PEER_HEADS):
                c16 = jnp.broadcast_to(cnt_ref[hd, l:l + 1, ts], (16, cb)).astype(BF16)
                g16 = jnp.broadcast_to(g_ref[hd, l:l + 1, ts], (16, cb)).astype(BF16)
                t = (jnp.where(r2_ref[hd, tc] < jnp.concatenate([c16] * reps, axis=0), p2_ref[hd, tc], zero)
                     * jnp.concatenate([g16] * reps, axis=0))
                w = t if w is None else w + t
            gate_s[tc, l * PEER_NKEYS:(l + 1) * PEER_NKEYS, :] = w

    act = lax.dot_general(u_ref[...], h_ref[...], NT, preferred_element_type=F32)
    gate = jnp.concatenate([gate_s[tc] for tc in range(tm // cb)], axis=1)
    coef = gate * _gelu_tanh(act.astype(BF16))
    acc[...] += jnp.dot(vt_ref[...], coef, preferred_element_type=F32)

    @pl.when(j == pl.num_programs(1) - 1)
    def _():
        y_ref[...] = x1_ref[...] + g2_ref[...] * acc[...].T


def _peer_main(h2, x1, g2, u16, vt16, g, cnt, r2, p2, seq_len):
    n, d = h2.shape
    tm = _row_tile(n, seq_len, 512)
    ni1 = 16
    et = ni1 * PEER_NKEYS
    cb = r2.shape[-1]
    g2_a, g2_s = _seq_operand(g2, seq_len, tm)

    def row():
        return pl.BlockSpec((tm, d), lambda i, j: (i, 0))

    return pl.pallas_call(
        functools.partial(_peer_main_kernel, ni1),
        out_shape=jax.ShapeDtypeStruct((n, d), F32),
        grid=(n // tm, N_EXPERTS // et),
        in_specs=[row(), row(), g2_s,
                  pl.BlockSpec((et, d), lambda i, j: (j, 0)),
                  pl.BlockSpec((d, et), lambda i, j: (0, j)),
                  pl.BlockSpec((PEER_HEADS, ni1, tm), lambda i, j: (0, j, i)),
                  pl.BlockSpec((PEER_HEADS, ni1, tm), lambda i, j: (0, j, i)),
                  pl.BlockSpec((PEER_HEADS, tm // cb, PEER_NKEYS, cb), lambda i, j: (0, i, 0, 0)),
                  pl.BlockSpec((PEER_HEADS, tm // cb, PEER_NKEYS, cb), lambda i, j: (0, i, 0, 0))],
        out_specs=row(),
        scratch_shapes=[pltpu.VMEM((d, tm), F32), pltpu.VMEM((tm // cb, et, cb), BF16)],
        compiler_params=_cparams(("parallel", "arbitrary")),
    )(h2, x1, g2_a, u16, vt16, g, cnt, r2, p2)


def _layer(x, mod, pos, shift_prev, s0, cache, lw):
    nb, t, d = x.shape
    n = nb * t
    sh1, sc1, g1, sh2, sc2, g2 = [mod[:, i * d:(i + 1) * d] for i in range(6)]
    x2 = x.reshape(n, d)
    P = _inproj(x2, sc1, sh1, lw['norm1_w'], lw['w_in16'], t)

    prev = _pack_rw(shift_prev).reshape(nb, 1, P_COLS)
    o_a, zf = _rwkv(P, nb, t, prev, lw['mu'], lw['w0'], lw['a0'], lw['k_k'], lw['k_a'], lw['r_k'], lw['lnx_w'],
                    lw['lnx_b'], lw['wup'], lw['aup'], lw['gup'], _state_to_pairs(s0))
    wkv = _pairs_to_state(zf)
    shift_last = _unpack_rw(P.reshape(nb, t, P_COLS)[:, -1, :])

    q16, k32, k16, v32, v16, qi16, kw32, ki2 = _dsa_prep(P, jnp.tile(pos, nb), lw['q_norm_w'], lw['k_norm_w'])
    if cache is None:
        o_b = _attn_prompt(q16, qi16, kw32, k16, v16, ki2, nb, t)
    else:
        ck, cv, cki = cache
        past = ck.shape[1]
        o_b = _attn_sample(q16, qi16, kw32, k16, v16, ki2, ck.reshape(nb, past, d), cv.reshape(nb, past, d), cki,
                           nb, t)

    x1, h2 = _merge(x2, o_a, o_b, P, lw['b_gate'], g1, sc2, sh2, lw['norm2_w'], lw['wpa'], lw['wpb'], lw['wout'], t)
    g, cnt, r2, p2 = _peer_select(h2, lw['wpqt'], lw['kbd'])
    y = _peer_main(h2, x1, g2, lw['u16'], lw['vt16'], g, cnt, r2, p2, t)

    k_new = k32.reshape(nb, t, N_HEADS, HEAD_DIM)
    v_new = v32.reshape(nb, t, N_HEADS, HEAD_DIM)
    ki_new = kw32[:, :IDX_DIM].reshape(nb, t, IDX_DIM)
    return y.reshape(nb, t, d), wkv, shift_last, k_new, v_new, ki_new


def _layer_weights(l, w_in, b_gate, mu_rw, w0, w_up, a0, a_up, g_up, k_k, k_a, r_k, lnx_w, lnx_b, q_norm_w, k_norm_w,
                   w_proj_a, w_proj_b, w_out, norm1_w, norm2_w, w_pq, peer_keys, peer_u, peer_v):
    d = D_MODEL
    zeros = lambda r: jnp.zeros((r, d), F32)
    keys = peer_keys[l].reshape(2 * PEER_HEADS, PEER_NKEYS, PEER_DHALF)
    eye = jnp.eye(2 * PEER_HEADS, dtype=F32)
    kbd = (eye[:, None, :, None] * keys[:, :, None, :]).reshape(2 * d, d)
    return {
        'w_in16': _pack_in(w_in[l]).astype(BF16), 'b_gate': b_gate[l], 'mu': _pack_rw(mu_rw[l]).reshape(1, P_COLS),
        'w0': w0[l], 'a0': a0[l], 'k_k': k_k[l], 'k_a': k_a[l], 'r_k': r_k[l].reshape(d), 'lnx_w': lnx_w[l],
        'lnx_b': lnx_b[l],
        'wup': jnp.concatenate([w_up[l], zeros(LANES - D_DECAY)], axis=0).astype(BF16),
        'aup': jnp.concatenate([zeros(D_DECAY), a_up[l]], axis=0).astype(BF16),
        'gup': jnp.concatenate([g_up[l], zeros(256 - D_GATE)], axis=0).astype(BF16),
        'q_norm_w': q_norm_w[l], 'k_norm_w': k_norm_w[l], 'norm1_w': norm1_w[l], 'norm2_w': norm2_w[l],
        'wpa': w_proj_a[l].astype(BF16), 'wpb': w_proj_b[l].astype(BF16), 'wout': w_out[l].astype(BF16),
        'wpqt': w_pq[l].T.astype(BF16), 'kbd': kbd.astype(BF16),
        'u16': peer_u[l].astype(BF16), 'vt16': peer_v[l].T.astype(BF16),
    }


def kernel(x_prompt, x_sample, c_prompt, c_sample, cache_k, cache_v, cache_kidx, state_wkv, state_shift, w_ada, b_ada,
           norm1_w, w_in, b_gate, mu_rw, w0, w_up, a0, a_up, g_up, k_k, k_a, r_k, lnx_w, lnx_b, q_norm_w, k_norm_w,
           w_proj_a, w_proj_b, w_out, norm2_w, w_pq, peer_keys, peer_u, peer_v):
    depth = w_in.shape[0]
    bp, tp = x_prompt.shape[:2]
    bs, ts = x_sample.shape[:2]
    past = cache_k.shape[2]
    dt = x_prompt.dtype
    pos_p = jnp.arange(tp, dtype=I32)
    pos_s = past + jnp.arange(ts, dtype=I32)
    zero_shift = jnp.zeros((bp, RW_IN), dt)
    zero_wkv = jnp.zeros((bp, N_HEADS, HEAD_DIM, HEAD_DIM), dt)
    c_all = jnp.concatenate([c_prompt, c_sample], axis=0)
    xp, xs = x_prompt, x_sample
    outs_p, outs_s = [], []
    for l in range(depth):
        lw = _layer_weights(l, w_in, b_gate, mu_rw, w0, w_up, a0, a_up, g_up, k_k, k_a, r_k, lnx_w, lnx_b, q_norm_w,
                            k_norm_w, w_proj_a, w_proj_b, w_out, norm1_w, norm2_w, w_pq, peer_keys, peer_u, peer_v)
        mod = _ada(c_all, w_ada[l], b_ada[l])
        xp, *rest_p = _layer(xp, mod[:bp], pos_p, zero_shift, zero_wkv, None, lw)
        xs, *rest_s = _layer(xs, mod[bp:], pos_s, state_shift[l], state_wkv[l],
                             (cache_k[l], cache_v[l], cache_kidx[l]), lw)
        outs_p.append(rest_p)
        outs_s.append(rest_s)
    stack = lambda outs, i: jnp.stack([o[i] for o in outs])
    return (xp, xs,
            stack(outs_p, 0), stack(outs_p, 1), stack(outs_p, 2), stack(outs_p, 3), stack(outs_p, 4),
            stack(outs_s, 0), stack(outs_s, 1), stack(outs_s, 2), stack(outs_s, 3), stack(outs_s, 4))
```

```python
import functools

import jax
import jax.numpy as jnp
from jax import lax
from jax.experimental import pallas as pl
from jax.experimental.pallas import tpu as pltpu

F32 = jnp.float32
BF16 = jnp.bfloat16
I32 = jnp.int32

LANES = 128
D_MODEL = 1024
EPS = 1e-6
GN_EPS = 64e-5
ROPE_THETA = 10000.0
CHUNK = 64
TOPK_MAX = 256
HEAD_DIM = 64
N_HEADS = D_MODEL // HEAD_DIM
N_PAIRS = N_HEADS // 2
IDX_HEADS = 8
IDX_DIM = 64
D_DECAY = 64
D_AAA = 64
D_GATE = 160
RW_IN = 3 * D_MODEL + D_DECAY + D_AAA + D_GATE
PEER_HEADS = 8
PEER_NKEYS = 128
PEER_TOPK = 16
PEER_DHALF = 64
N_EXPERTS = PEER_NKEYS * PEER_NKEYS
RW_CHUNK = 64
RW_INTERLEAVE = 16
RW_PAIRS_LONG = 2
RW_PASSES = (2, 1, 1, 1)
VMEM_LIMIT = 56 * 1024 * 1024
LOG2E = 1.4426950408889634
PEER_CAND = 80
GATE_ROWS = 32

C_R, C_K, C_V = 0, 1024, 2048
C_Q, C_KD, C_VD = 3072, 4096, 5120
C_GA, C_GB = 6144, 7168
C_QI = 8192
C_G = 8704
C_M = 8960
C_KW = 9088
P_COLS = 9216
IN_W = 9064

NT = (((1,), (1,)), ((), ()))
NN = (((1,), (0,)), ((), ()))


def _pack_in(w):
    z = lambda k: jnp.zeros(w.shape[:-1] + (k,), w.dtype)
    return jnp.concatenate([w[..., 0:3072], w[..., 3360:6432], w[..., 7016:9064], w[..., 6432:6944],
                            w[..., 3200:3360], z(256 - D_GATE), w[..., 3072:3200],
                            w[..., 6944:7016], z(LANES - IDX_DIM - IDX_HEADS)], axis=-1)


def _pack_rw(a):
    return _pack_in(jnp.concatenate([a, jnp.zeros(a.shape[:-1] + (IN_W - RW_IN,), a.dtype)], axis=-1))


def _unpack_rw(p):
    return jnp.concatenate([p[..., :3072], p[..., C_M:C_M + 128], p[..., C_G:C_G + D_GATE]], axis=-1)


def _split_bf16(x, n):
    parts = []
    r = x
    for _ in range(n):
        p = r.astype(BF16)
        parts.append(p)
        r = r - p.astype(F32)
    return parts


def _mm(a, b, pa=1, pb=1, dims=NN):
    aps = _split_bf16(a, pa) if a.dtype != BF16 else [a]
    bps = _split_bf16(b, pb) if b.dtype != BF16 else [b]
    order = max(len(aps), len(bps))
    out = None
    for i, ap in enumerate(aps):
        for j, bp in enumerate(bps):
            if i + j >= order:
                continue
            t = lax.dot_general(ap, bp, dims, preferred_element_type=F32)
            out = t if out is None else out + t
    return out


def _sigmoid(x):
    return 1.0 / (1.0 + jnp.exp(-x))


def _softplus(z):
    return jnp.maximum(z, 0.0) + jnp.log(1.0 + jnp.exp(-jnp.abs(z)))


def _cparams(sem):
    return pltpu.CompilerParams(dimension_semantics=sem, vmem_limit_bytes=VMEM_LIMIT)


def _ada_kernel(c_ref, w_ref, b_ref, o_ref):
    c = c_ref[...]
    s = c * _sigmoid(c)
    o_ref[...] = _mm(s, w_ref[...], 2, 2) + b_ref[...]


def _ada(c, w, b):
    m, d = c.shape
    n = w.shape[1]
    tn = 1024
    return pl.pallas_call(
        _ada_kernel,
        out_shape=jax.ShapeDtypeStruct((m, n), F32),
        grid=(n // tn,),
        in_specs=[pl.BlockSpec((m, d), lambda j: (0, 0)),
                  pl.BlockSpec((d, tn), lambda j: (0, j)),
                  pl.BlockSpec((1, tn), lambda j: (0, j))],
        out_specs=pl.BlockSpec((m, tn), lambda j: (0, j)),
        compiler_params=_cparams(("arbitrary",)),
    )(c, w, b.reshape(1, n))


def _seq_operand(vec, seq_len, tm):
    b, d = vec.shape
    if seq_len % tm == 0:
        per = seq_len // tm
        arr = vec.reshape(b, 1, d)
        spec = pl.BlockSpec((None, 1, d), lambda *g: (g[0] // per, 0, 0))
    else:
        assert tm % seq_len == 0
        arr = jnp.repeat(vec, seq_len, axis=0)
        spec = pl.BlockSpec((tm, d), lambda *g: (g[0], 0))
    return arr, spec


def _row_tile(n, seq_len, cap):
    tm = min(cap, n)
    while n % tm or (seq_len % tm and tm % seq_len):
        tm //= 2
    return tm


def _inproj_kernel(x_ref, sc_ref, sh_ref, nw_ref, w_ref, o_ref, h_scr):
    @pl.when(pl.program_id(1) == 0)
    def _():
        x = x_ref[...]
        y = x * lax.rsqrt(jnp.mean(x * x, axis=-1, keepdims=True) + EPS) * nw_ref[...]
        h_scr[...] = (y * (1.0 + sc_ref[...]) + sh_ref[...]).astype(BF16)

    o_ref[...] = jnp.dot(h_scr[...], w_ref[...], preferred_element_type=F32)


def _inproj(x2, sc, sh, nw, w16, seq_len):
    n, d = x2.shape
    tm = _row_tile(n, seq_len, 1024)
    tn = 1024
    sc_a, sc_s = _seq_operand(sc, seq_len, tm)
    sh_a, sh_s = _seq_operand(sh, seq_len, tm)
    return pl.pallas_call(
        _inproj_kernel,
        out_shape=jax.ShapeDtypeStruct((n, P_COLS), F32),
        grid=(n // tm, P_COLS // tn),
        in_specs=[pl.BlockSpec((tm, d), lambda i, j: (i, 0)), sc_s, sh_s,
                  pl.BlockSpec((1, d), lambda i, j: (0, 0)),
                  pl.BlockSpec((d, tn), lambda i, j: (0, j))],
        out_specs=pl.BlockSpec((tm, tn), lambda i, j: (i, j)),
        scratch_shapes=[pltpu.VMEM((tm, d), BF16)],
        compiler_params=_cparams(("parallel", "arbitrary")),
    )(x2, sc_a, sh_a, nw.reshape(1, d), w16)


def _lane_lo(shape):
    return lax.broadcasted_iota(I32, shape, len(shape) - 1) < HEAD_DIM


def _pair_sum(x):
    lo = _lane_lo(x.shape)
    s0 = jnp.sum(jnp.where(lo, x, 0.0), axis=-1, keepdims=True)
    s1 = jnp.sum(jnp.where(lo, 0.0, x), axis=-1, keepdims=True)
    return jnp.where(lo, s0, s1)


def _stack2(x):
    lo = _lane_lo(x.shape)
    return jnp.concatenate([jnp.where(lo, x, 0.0), jnp.where(lo, 0.0, x)], axis=0)


def _rwkv_kernel(t_real, npair, nchunk, pr, pk, pv, pg, pm, sr, sk, sv, sg, sm, mr, mk, mv, mg, mmu,
                 w0, a0, kkw, kaw, rkw, lnw, lnb, wup, aup, gup, z0, o_ref, zf_ref,
                 r_s, lw_s, k_s, v_s, a_s, b_s, y_s, bonus_s, g_s):
    C = RW_CHUNK
    t_pad = r_s.shape[1]

    def mix(p_ref, s_ref, m_ref):
        p = p_ref[...]
        prev = pltpu.roll(p, 1, 0)
        row = lax.broadcasted_iota(I32, p.shape, 0)
        prev = jnp.where(row == 0, s_ref[...], prev)
        return p + (prev - p) * m_ref[...]

    xg, xm = mix(pg, sg, mg), mix(pm, sm, mmu)
    th16, xm16, sg16 = jnp.tanh(xm).astype(BF16), xm.astype(BF16), _sigmoid(xg).astype(BF16)
    xr_all, xk_all, xv_all = mix(pr, sr, mr), mix(pk, sk, mk), mix(pv, sv, mv)

    def put(ref, pp, val):
        if t_pad > t_real:
            val = jnp.concatenate([val, jnp.zeros((t_pad - t_real, LANES), F32)], axis=0)
        ref[pp] = val

    for pp in range(npair):
        cs = slice(pp * LANES, (pp + 1) * LANES)
        xr, xk, xv = xr_all[:, cs], xk_all[:, cs], xv_all[:, cs]
        dw = jnp.dot(th16, wup[:, cs], preferred_element_type=F32)
        lw = -jnp.exp(-_softplus(-(w0[:, cs] + dw)) - 0.5)
        asig = _sigmoid(a0[:, cs] + jnp.dot(xm16, aup[:, cs], preferred_element_type=F32))
        g_s[pp] = jnp.dot(sg16, gup[:, cs], preferred_element_type=F32)
        kk = xk * kkw[:, cs]
        kk = kk * lax.rsqrt(_pair_sum(kk * kk) + 1e-12)
        kmod = xk * (1.0 + (asig - 1.0) * kaw[:, cs])
        bonus_s[pp] = _pair_sum(xr * kmod * rkw[:, cs]) * xv
        put(r_s, pp, xr)
        put(lw_s, pp, lw)
        put(k_s, pp, kmod)
        put(v_s, pp, xv)
        put(a_s, pp, -kk)
        put(b_s, pp, kk * asig)

    n2 = 2 * C
    ri = lax.broadcasted_iota(I32, (n2, n2), 0)
    ci = lax.broadcasted_iota(I32, (n2, n2), 1)
    same = (ri // C) == (ci // C)
    strict = same & ((ri % C) > (ci % C))
    incl = same & ((ri % C) >= (ci % C))
    eye = ri == ci
    eye_f = jnp.where(eye, 1.0, 0.0)
    tri = jnp.where(lax.broadcasted_iota(I32, (C, C), 0) >= lax.broadcasted_iota(I32, (C, C), 1), 1.0, 0.0
                    ).astype(BF16)
    zeros_sq = jnp.zeros((n2, LANES), F32)

    pg_, pi_, po_, ps_ = RW_PASSES

    def local(chains):
        each = lambda f, *cols: [f(*xs) for xs in zip(*cols)]
        lwc = [lw_s[pp, sl, :] for sl, pp in chains]
        cum = each(lambda l: _mm(tri, l, 1, 3), lwc)
        cum_last = each(lambda c: c[C - 1:C, :], cum)
        ec, eci = each(jnp.exp, cum), each(lambda c: jnp.exp(-c), cum)
        ecp = each(lambda c, l: jnp.exp(c - l), cum, lwc)
        ecl = each(lambda c, cl: jnp.exp(cl - c), cum, cum_last)
        a_c = [a_s[pp, sl, :] for sl, pp in chains]
        b_c = [b_s[pp, sl, :] for sl, pp in chains]
        k_c = [k_s[pp, sl, :] for sl, pp in chains]
        r_c = [r_s[pp, sl, :] for sl, pp in chains]
        As = each(lambda a, e: _stack2(a * e), a_c, ecp)
        Rs = each(lambda r, e: _stack2(r * e), r_c, ec)
        Bs = each(lambda b, e: _stack2(b * e), b_c, eci)
        Ks = each(lambda k, e: _stack2(k * e), k_c, eci)
        Bt = each(lambda b, e: _stack2(b * e), b_c, ecl)
        Kt = each(lambda k, e: _stack2(k * e), k_c, ecl)
        Vs = [_stack2(v_s[pp, sl, :]) for sl, pp in chains]

        G = each(lambda a, r, b, k: _mm(jnp.concatenate([a, r], axis=0), jnp.concatenate([b, k], axis=0),
                                        pg_, pg_, NT), As, Rs, Bs, Ks)
        a_ab = each(lambda g: jnp.where(strict, g[:n2, :n2], 0.0), G)
        a_ak = each(lambda g: jnp.where(strict, g[:n2, n2:], 0.0), G)
        a_rb = each(lambda g: jnp.where(incl, g[n2:, :n2], 0.0), G)
        a_rk = each(lambda g: jnp.where(incl, g[n2:, n2:], 0.0), G)

        lp = a_ab
        tm_ = each(lambda a: eye_f + a, a_ab)
        step = 2
        while step < C:
            lp = each(lambda l: _mm(l, l, pi_, pi_), lp)
            tm_ = each(lambda t, l: t + _mm(t, l, pi_, pi_), tm_, lp)
            step *= 2

        w1 = each(lambda a, v: _mm(a, v, po_, po_), a_ak, Vs)
        mu_ = each(lambda t, a, w: _mm(t, jnp.concatenate([a, w], axis=1), po_, po_), tm_, As, w1)
        rhs = each(lambda m, v: jnp.concatenate([m, jnp.concatenate([zeros_sq, v], axis=1)], axis=0), mu_, Vs)
        lhs = each(lambda rb, rk, b, k: jnp.concatenate([jnp.concatenate([rb, rk], axis=1),
                                                         jnp.concatenate([b.T, k.T], axis=1)], axis=0),
                   a_rb, a_rk, Bt, Kt)
        out2 = each(lambda l, r: _mm(l, r, po_, po_), lhs, rhs)
        m23 = each(lambda r, o, cl: jnp.concatenate([r + o[:n2, :LANES],
                                                     jnp.where(eye, jnp.exp(cl), 0.0) + o[n2:, :LANES]], axis=0),
                   Rs, out2, cum_last)
        return [(m, o[:n2, LANES:], o[n2:, LANES:]) for m, o in zip(m23, out2)]

    def step_chunks(i, zs):
        sls = [pl.ds(pl.multiple_of((i * nchunk + j) * C, C), C) for j in range(nchunk)]
        parts = local([(sl, pp) for sl in sls for pp in range(npair)])
        zs = list(zs)
        for j, sl in enumerate(sls):
            for pp in range(npair):
                m23, y_loc, z_loc = parts[j * npair + pp]
                yz = _mm(m23, zs[pp], ps_, ps_)
                y = yz[:n2] + y_loc
                y_s[pp, sl, :] = y[:C] + y[C:]
                zs[pp] = yz[n2:] + z_loc
        return tuple(zs)

    zs = lax.fori_loop(0, t_pad // (C * nchunk), step_chunks, tuple(z0[pp] for pp in range(npair)))
    for pp in range(npair):
        zf_ref[pp] = zs[pp]
        cs = slice(pp * LANES, (pp + 1) * LANES)
        y = y_s[pp, 0:t_real, :]
        mean = _pair_sum(y) * (1.0 / HEAD_DIM)
        dlt = y - mean
        var = _pair_sum(dlt * dlt) * (1.0 / HEAD_DIM)
        yn = dlt * lax.rsqrt(var + GN_EPS) * lnw[:, cs] + lnb[:, cs]
        o_ref[:, cs] = ((yn + bonus_s[pp]) * g_s[pp]).astype(o_ref.dtype)


def _rwkv(P, nb, t, prev, mu, w0, a0, k_k, k_a, r_k, lnx_w, lnx_b, wup, aup, gup, z0):
    t_pad = max(t, RW_CHUNK)
    assert t % 8 == 0 and t_pad % RW_CHUNK == 0
    n_chunks = t_pad // RW_CHUNK
    nchunk = min(RW_INTERLEAVE // RW_PAIRS_LONG, n_chunks)
    npair = min(N_PAIRS, max(1, RW_INTERLEAVE // nchunk))
    wp = npair * LANES

    def cblk(c0, w, per_pair):
        return (lambda p: c0 // w + p) if per_pair else (lambda p: c0 // w)

    def pcol(c0, w, pp):
        f = cblk(c0, w, pp)
        return pl.BlockSpec((t, w), lambda b, p: (b, f(p)))

    def prevcol(c0, w, pp):
        f = cblk(c0, w, pp)
        return pl.BlockSpec((None, 1, w), lambda b, p: (b, 0, f(p)))

    def mucol(c0, w, pp):
        f = cblk(c0, w, pp)
        return pl.BlockSpec((1, w), lambda b, p: (0, f(p)))

    def hvec():
        return pl.BlockSpec((1, wp), lambda b, p: (0, p))

    cols = [(C_R, wp, True), (C_K, wp, True), (C_V, wp, True), (C_G, 256, False), (C_M, LANES, False)]
    in_specs = ([pcol(*c) for c in cols] + [prevcol(*c) for c in cols] + [mucol(*c) for c in cols]
                + [hvec() for _ in range(7)]
                + [pl.BlockSpec((LANES, wp), lambda b, p: (0, p)),
                   pl.BlockSpec((LANES, wp), lambda b, p: (0, p)),
                   pl.BlockSpec((256, wp), lambda b, p: (0, p)),
                   pl.BlockSpec((None, npair, LANES, LANES), lambda b, p: (b, p, 0, 0))])
    vecs = [v.reshape(1, D_MODEL) for v in (w0, a0, k_k, k_a, r_k, lnx_w, lnx_b)]
    o, zf = pl.pallas_call(
        functools.partial(_rwkv_kernel, t, npair, nchunk),
        out_shape=(jax.ShapeDtypeStruct((nb * t, D_MODEL), BF16),
                   jax.ShapeDtypeStruct((nb, N_PAIRS, LANES, LANES), F32)),
        grid=(nb, N_PAIRS // npair),
        in_specs=in_specs,
        out_specs=(pl.BlockSpec((t, wp), lambda b, p: (b, p)),
                   pl.BlockSpec((None, npair, LANES, LANES), lambda b, p: (b, p, 0, 0))),
        scratch_shapes=([pltpu.VMEM((npair, t_pad, LANES), F32) for _ in range(7)]
                        + [pltpu.VMEM((npair, t, LANES), F32) for _ in range(2)]),
        compiler_params=_cparams(("parallel", "arbitrary")),
    )(P, P, P, P, P, prev, prev, prev, prev, prev, mu, mu, mu, mu, mu, *vecs, wup, aup, gup, z0)
    return o, zf


def _state_to_pairs(s):
    nb = s.shape[0]
    zt = jnp.swapaxes(s, -1, -2).reshape(nb, N_PAIRS, 2, HEAD_DIM, HEAD_DIM)
    zero = jnp.zeros_like(zt[:, :, 0])
    top = jnp.concatenate([zt[:, :, 0], zero], axis=-1)
    bot = jnp.concatenate([zero, zt[:, :, 1]], axis=-1)
    return jnp.concatenate([top, bot], axis=-2)


def _pairs_to_state(z):
    nb = z.shape[0]
    h0 = z[:, :, :HEAD_DIM, :HEAD_DIM]
    h1 = z[:, :, HEAD_DIM:, HEAD_DIM:]
    s = jnp.stack([h0, h1], axis=2).reshape(nb, N_HEADS, HEAD_DIM, HEAD_DIM)
    return jnp.swapaxes(s, -1, -2)


def _rope(x, cos, sin_signed):
    w = x.shape[1]
    reps = w // LANES
    cw = jnp.concatenate([cos] * reps, axis=1) if reps > 1 else cos
    sw = jnp.concatenate([sin_signed] * reps, axis=1) if reps > 1 else sin_signed
    lane = lax.broadcasted_iota(I32, x.shape, 1)
    fwd = pltpu.roll(x, w - 32, 1)
    bwd = pltpu.roll(x, 32, 1)
    partner = jnp.where((lane % HEAD_DIM) < 32, fwd, bwd)
    return x * cw + partner * sw


def _head_rms(x, nw, e_dn, e_up):
    ms = _mm(x * x, e_dn, 2, 1) * (1.0 / HEAD_DIM)
    r = lax.rsqrt(ms + EPS)
    return x * _mm(r, e_up, 2, 1) * nw


def _dsa_prep_kernel(pq, pkd, pvd, pqi, pkw, cos_ref, sin_ref, qn, kn, edn, eup,
                     q16, k32, k16, v32, v16, qi16, kw32, ki2):
    cos, sin = cos_ref[...], sin_ref[...]
    e_dn, e_up = edn[...], eup[...]
    def put_pairs(ref, x):
        for p in range(N_PAIRS):
            ref[p] = x[:, p * LANES:(p + 1) * LANES].astype(ref.dtype)

    q = _rope(_head_rms(pq[...], qn[...], e_dn, e_up), cos, sin)
    put_pairs(q16, q * (HEAD_DIM ** -0.5 * LOG2E))
    k = _rope(_head_rms(pkd[...], kn[...], e_dn, e_up), cos, sin)
    k32[...] = k
    put_pairs(k16, k)
    v = pvd[...]
    v32[...] = v
    put_pairs(v16, v)
    qi16[...] = _rope(pqi[...], cos, sin).astype(BF16)
    kw = pkw[...]
    lane = lax.broadcasted_iota(I32, kw.shape, 1)
    wi_scale = (IDX_HEADS * IDX_DIM) ** -0.5
    kr = _rope(kw, cos, sin)
    kw32[...] = jnp.where(lane < IDX_DIM, kr, jnp.where(lane < IDX_DIM + IDX_HEADS, kw * wi_scale, 0.0))
    ki2[...] = jnp.where(lane < IDX_DIM, kr, pltpu.roll(kr, IDX_DIM, 1)).astype(BF16)


def _dsa_prep(P, pos_rows, q_norm_w, k_norm_w):
    n = P.shape[0]
    tm = 512 if n % 512 == 0 else n
    half = HEAD_DIM // 2
    inv = ROPE_THETA ** (-jnp.arange(half, dtype=F32) / half)
    ang = pos_rows.astype(F32)[:, None] * inv[None, :]
    cos = jnp.tile(jnp.cos(ang), (1, 4))
    sin = jnp.sin(ang)
    sin_signed = jnp.tile(jnp.concatenate([-sin, sin], axis=1), (1, 2))
    head_of = jnp.arange(D_MODEL) // HEAD_DIM
    e_dn = (head_of[:, None] == jnp.arange(LANES)[None, :]).astype(BF16)
    e_up = e_dn.T
    qn = jnp.tile(q_norm_w, N_HEADS).reshape(1, D_MODEL)
    kn = jnp.tile(k_norm_w, N_HEADS).reshape(1, D_MODEL)

    def col(c0, w):
        return pl.BlockSpec((tm, w), lambda i, c0=c0, w=w: (i, c0 // w))

    def row(w):
        return pl.BlockSpec((tm, w), lambda i: (i, 0))

    def const(shape):
        return pl.BlockSpec(shape, lambda i: (0, 0))

    pairs = jax.ShapeDtypeStruct((N_PAIRS, n, LANES), BF16)
    pair_spec = pl.BlockSpec((N_PAIRS, tm, LANES), lambda i: (0, i, 0))
    return pl.pallas_call(
        _dsa_prep_kernel,
        out_shape=(pairs, jax.ShapeDtypeStruct((n, D_MODEL), F32), pairs, jax.ShapeDtypeStruct((n, D_MODEL), F32),
                   pairs, jax.ShapeDtypeStruct((n, IDX_HEADS * IDX_DIM), BF16),
                   jax.ShapeDtypeStruct((n, LANES), F32), jax.ShapeDtypeStruct((n, LANES), BF16)),
        grid=(n // tm,),
        in_specs=[col(C_Q, 1024), col(C_KD, 1024), col(C_VD, 1024), col(C_QI, 512), col(C_KW, LANES),
                  row(LANES), row(LANES), const((1, D_MODEL)), const((1, D_MODEL)),
                  const((D_MODEL, LANES)), const((LANES, D_MODEL))],
        out_specs=(pair_spec, row(D_MODEL), pair_spec, row(D_MODEL), pair_spec, row(512), row(LANES), row(LANES)),
        compiler_params=_cparams(("parallel",)),
    )(P, P, P, P, P, cos, sin_signed, qn, kn, e_dn, e_up)


CODE_NEG_INF = -1 - 0x7F800000


def _index_scores(qi, wi, ki_list):
    outs = []
    for ki in ki_list:
        acc = None
        for h in range(IDX_HEADS):
            qpair = qi[:, (h // 2) * LANES:(h // 2 + 1) * LANES]
            lo = _lane_lo(qpair.shape)
            qh = jnp.where(lo if h % 2 == 0 else jnp.logical_not(lo), qpair, jnp.zeros_like(qpair))
            rel = lax.dot_general(qh, ki, NT, preferred_element_type=F32)
            term = wi[:, IDX_DIM + h:IDX_DIM + h + 1] * jnp.maximum(rel, 0.0)
            acc = term if acc is None else acc + term
        outs.append(acc)
    return outs


def _select_topk(keys, topk, bias_refs):
    tq = keys[0].shape[0]
    neg = -jnp.inf

    def write(masks):
        for ref, k, msk in zip(bias_refs, keys, masks):
            ref[:, 0:k.shape[1]] = jnp.where(msk, 0.0, neg)

    def count(pred_list):
        tot = None
        for p in pred_list:
            c = jnp.sum(jnp.where(p, 1.0, 0.0), axis=-1, keepdims=True)
            tot = c if tot is None else tot + c
        return tot

    def threshold(c):
        bits = jnp.where(c >= 0, c, c ^ jnp.int32(0x7FFFFFFF))
        return jnp.where(c < jnp.int32(CODE_NEG_INF), neg, lax.bitcast_convert_type(bits, F32))

    few = count([k > neg for k in keys]) <= topk

    def pending(cnt):
        return jnp.max(jnp.where(few | (cnt == topk), 0.0, 1.0))

    def bit_step(state):
        i, c, cnt, _ = state
        trial = c + jnp.left_shift(jnp.int32(1), 31 - i)
        cnt_t = count([k >= threshold(trial) for k in keys])
        take = cnt_t >= topk
        cnt = jnp.where(take, cnt_t, cnt)
        return i + 1, jnp.where(take, trial, c), cnt, pending(cnt)

    cnt0 = jnp.full((tq, 1), float(sum(k.shape[1] for k in keys)), F32)
    state = (jnp.int32(0), jnp.full((tq, 1), -2 ** 31, I32), cnt0, pending(cnt0))
    _, code, _, _ = lax.while_loop(lambda s: (s[0] < 32) & (s[3] > 0.0), bit_step, state)
    thr = threshold(code)
    ge = [(k >= thr) & (k > neg) for k in keys]
    write(ge)
    surplus = jnp.max(count(ge)) > topk

    @pl.when(surplus)
    def _():
        gt = [k > thr for k in keys]
        need = topk - count(gt)
        ties = [(k == thr) & (k > neg) for k in keys]
        offs, idx = 0, []
        for k in keys:
            idx.append(lax.broadcasted_iota(I32, k.shape, 1) + offs)
            offs += k.shape[1]
        nbits = max(1, (offs - 1).bit_length() + 1)

        def idx_step(i, m):
            trial = m + jnp.left_shift(jnp.int32(1), nbits - 1 - i)
            cnt = count([t & (ix < trial) for t, ix in zip(ties, idx)])
            return jnp.where(cnt <= need, trial, m)

        cut = lax.fori_loop(0, nbits, idx_step, jnp.zeros((tq, 1), I32))
        write([g | (t & (ix < cut)) for g, t, ix in zip(gt, ties, idx)])


def _attend_pair(q_pair, k_list, v_list, bias_list):
    lo = _lane_lo(q_pair.shape)
    zero = jnp.zeros_like(q_pair)
    outs = []
    for qh in (jnp.where(lo, q_pair, zero), jnp.where(lo, zero, q_pair)):
        s_list = [lax.dot_general(qh, k, NT, preferred_element_type=F32) + b for k, b in zip(k_list, bias_list)]
        m = None
        for s in s_list:
            mx = jnp.max(s, axis=-1, keepdims=True)
            m = mx if m is None else jnp.maximum(m, mx)
        den, acc = None, None
        for s, v in zip(s_list, v_list):
            p = jnp.exp2(s - m)
            d = jnp.sum(p, axis=-1, keepdims=True)
            o = jnp.dot(p.astype(BF16), v, preferred_element_type=F32)
            den = d if den is None else den + d
            acc = o if acc is None else acc + o
        outs.append(acc / den)
    return jnp.where(_lane_lo(outs[0].shape), outs[0], outs[1])


def _attn_prompt_kernel(topk, ncase, q_ref, qi_ref, kw_ref, k_ref, v_ref, ki2_ref, o_ref, bias_s):
    tq = q_ref.shape[1]
    t = k_ref.shape[1]
    i = pl.program_id(1)
    lstep = t // ncase
    case = ((i + 1) * tq - 1) // lstep

    def run(L):
        score = _index_scores(qi_ref[...], kw_ref[...], [ki2_ref[0:L, :]])[0]
        qpos = i * tq + lax.broadcasted_iota(I32, (tq, L), 0)
        kpos = lax.broadcasted_iota(I32, (tq, L), 1)
        adm = (qpos // CHUNK) >= (kpos // CHUNK)
        _select_topk([jnp.where(adm, score, -jnp.inf)], topk, [bias_s])

        def pair(p, carry):
            o = _attend_pair(q_ref[p], [k_ref[p, 0:L, :]], [v_ref[p, 0:L, :]], [bias_s[:, 0:L]])
            o_ref[p] = o.astype(o_ref.dtype)
            return carry

        lax.fori_loop(0, N_PAIRS, pair, 0)

    for c in range(ncase):
        pl.when(case == c)(functools.partial(run, (c + 1) * lstep))


def _attn_prompt(q16, qi16, kw32, k16, v16, ki2, nb, t):
    tq = min(256, t)
    topk = min(TOPK_MAX, t // 4)
    nq = t // tq
    ncase = min(4, nq)

    def qrow(w):
        return pl.BlockSpec((tq, w), lambda b, i: (b * nq + i, 0))

    def qpairs():
        return pl.BlockSpec((N_PAIRS, tq, LANES), lambda b, i: (0, b * nq + i, 0))

    def kpairs():
        return pl.BlockSpec((N_PAIRS, t, LANES), lambda b, i: (0, b, 0))

    return pl.pallas_call(
        functools.partial(_attn_prompt_kernel, topk, ncase),
        out_shape=jax.ShapeDtypeStruct((N_PAIRS, nb * t, LANES), BF16),
        grid=(nb, nq),
        in_specs=[qpairs(), qrow(512), qrow(LANES), kpairs(), kpairs(),
                  pl.BlockSpec((t, LANES), lambda b, i: (b, 0))],
        out_specs=qpairs(),
        scratch_shapes=[pltpu.VMEM((tq, t), F32)],
        compiler_params=_cparams(("parallel", "arbitrary")),
    )(q16, qi16, kw32, k16, v16, ki2)


def _attn_sample_kernel(topk, past, q_ref, qi_ref, kw_ref, ck_ref, cv_ref, cki2_ref, k_ref, v_ref, ki2_ref, o_ref,
                        biasc_s, biasn_s):
    ts = q_ref.shape[0]

    @pl.when(pl.program_id(1) == 0)
    def _():
        sc, sn = _index_scores(qi_ref[...], kw_ref[...], [cki2_ref[...], ki2_ref[...]])
        qpos = past + lax.broadcasted_iota(I32, (ts, 1), 0)
        kpos_c = lax.broadcasted_iota(I32, sc.shape, 1)
        kpos_n = past + lax.broadcasted_iota(I32, sn.shape, 1)
        keys = [jnp.where((qpos // CHUNK) >= (kpos_c // CHUNK), sc, -jnp.inf),
                jnp.where((qpos // CHUNK) >= (kpos_n // CHUNK), sn, -jnp.inf)]
        _select_topk(keys, topk, [biasc_s, biasn_s])

    o_ref[...] = _attend_pair(q_ref[...], [ck_ref[...].astype(BF16), k_ref[...]],
                              [cv_ref[...].astype(BF16), v_ref[...]],
                              [biasc_s[...], biasn_s[...]]).astype(o_ref.dtype)


def _attn_sample(q16, qi16, kw32, k16, v16, ki2, cache_k, cache_v, cache_kidx, nb, ts):
    past = cache_k.shape[1]
    cki2 = jnp.concatenate([cache_kidx, cache_kidx], axis=-1).astype(BF16)
    topk = min(TOPK_MAX, (past + ts) // 4)

    def qrow(w):
        return pl.BlockSpec((ts, w), lambda b, p: (b, 0))

    def qpair():
        return pl.BlockSpec((None, ts, LANES), lambda b, p: (p, b, 0))

    def cache(pair):
        return pl.BlockSpec((None, past, LANES), (lambda b, p: (b, 0, p)) if pair else (lambda b, p: (b, 0, 0)))

    return pl.pallas_call(
        functools.partial(_attn_sample_kernel, topk, past),
        out_shape=jax.ShapeDtypeStruct((N_PAIRS, nb * ts, LANES), BF16),
        grid=(nb, N_PAIRS),
        in_specs=[qpair(), qrow(512), qrow(LANES), cache(True), cache(True), cache(False),
                  qpair(), qpair(), qrow(LANES)],
        out_specs=qpair(),
        scratch_shapes=[pltpu.VMEM((ts, past), F32), pltpu.VMEM((ts, ts), F32)],
        compiler_params=_cparams(("parallel", "arbitrary")),
    )(q16, qi16, kw32, cache_k, cache_v, cki2, k16, v16, ki2)


def _merge_kernel(x_ref, oa_ref, ob_ref, pga_ref, pgb_ref, bga_ref, bgb_ref, g1_ref, sc2_ref, sh2_ref, nw_ref,
                  wpa_ref, wpb_ref, wout_ref, x1_ref, h2_ref):
    ga = _sigmoid(pga_ref[...] + bga_ref[...])
    gb = _sigmoid(pgb_ref[...] + bgb_ref[...])
    ob = jnp.concatenate([ob_ref[p] for p in range(N_PAIRS)], axis=1)
    m = (ga * jnp.dot(oa_ref[...], wpa_ref[...], preferred_element_type=F32)
         + gb * jnp.dot(ob, wpb_ref[...], preferred_element_type=F32))
    x1 = x_ref[...] + g1_ref[...] * jnp.dot(m.astype(BF16), wout_ref[...], preferred_element_type=F32)
    x1_ref[...] = x1
    y = x1 * lax.rsqrt(jnp.mean(x1 * x1, axis=-1, keepdims=True) + EPS) * nw_ref[...]
    h2_ref[...] = (y * (1.0 + sc2_ref[...]) + sh2_ref[...]).astype(BF16)


def _merge(x2, o_a, o_b, P, b_gate, g1, sc2, sh2, nw2, wpa, wpb, wout, seq_len):
    n, d = x2.shape
    tm = _row_tile(n, seq_len, 512)
    g1_a, g1_s = _seq_operand(g1, seq_len, tm)
    sc_a, sc_s = _seq_operand(sc2, seq_len, tm)
    sh_a, sh_s = _seq_operand(sh2, seq_len, tm)

    def row():
        return pl.BlockSpec((tm, d), lambda i: (i, 0))

    def const(shape):
        return pl.BlockSpec(shape, lambda i: (0, 0))

    bg = b_gate.reshape(1, 2 * d)
    return pl.pallas_call(
        _merge_kernel,
        out_shape=(jax.ShapeDtypeStruct((n, d), F32), jax.ShapeDtypeStruct((n, d), BF16)),
        grid=(n // tm,),
        in_specs=[row(), row(), pl.BlockSpec((N_PAIRS, tm, LANES), lambda i: (0, i, 0)),
                  pl.BlockSpec((tm, d), lambda i: (i, C_GA // d)), pl.BlockSpec((tm, d), lambda i: (i, C_GB // d)),
                  pl.BlockSpec((1, d), lambda i: (0, 0)), pl.BlockSpec((1, d), lambda i: (0, 1)),
                  g1_s, sc_s, sh_s, const((1, d)), const((d, d)), const((d, d)), const((d, d))],
        out_specs=(row(), row()),
        compiler_params=_cparams(("parallel",)),
    )(x2, o_a, o_b, P, P, bg, bg, g1_a, sc_a, sh_a, nw2.reshape(1, d), wpa, wpb, wout)


def _top_exact(s, k):
    rows = lax.broadcasted_iota(I32, s.shape, 0).astype(F32)
    cur = s
    rank = jnp.full(s.shape, float(k), F32)
    vals = []
    for r in range(k):
        m = jnp.max(cur, axis=0, keepdims=True)
        first = jnp.min(jnp.where(cur == m, rows, 1e9), axis=0, keepdims=True)
        hit = rows == first
        vals.append(m)
        rank = jnp.where(hit, float(r), rank)
        cur = jnp.where(hit, -jnp.inf, cur)
    return vals, rank


def _top_fast(ss, k):
    curs = list(ss)
    ranks = [jnp.full(s.shape, float(k), F32) for s in ss]
    vals = [[] for _ in ss]
    for r in range(k):
        ms = [jnp.max(c, axis=0, keepdims=True) for c in curs]
        hits = [c == m for c, m in zip(curs, ms)]
        ranks = [jnp.where(h, float(r), rk) for h, rk in zip(hits, ranks)]
        curs = [jnp.where(h, -jnp.inf, c) for h, c in zip(hits, curs)]
        for v, m in zip(vals, ms):
            v.append(m)
    cleans = [jnp.max(jnp.abs(jnp.sum(jnp.where(rk < k, 1.0, 0.0), axis=0, keepdims=True) - k)) == 0.0
              for rk in ranks]
    return vals, ranks, cleans


def _top(ss, k, vals_scr, rank_scr):
    vals, ranks, cleans = _top_fast(ss, k)
    for i, s in enumerate(ss):
        vals_scr[i] = jnp.concatenate(vals[i], axis=0)
        rank_scr[i] = ranks[i]

        @pl.when(jnp.logical_not(cleans[i]))
        def _(i=i, s=s):
            vals_e, rank_e = _top_exact(s, k)
            vals_scr[i] = jnp.concatenate(vals_e, axis=0)
            rank_scr[i] = rank_e


def _peer_sel_kernel(h_ref, wpqt_ref, kbd_ref, g_ref, cnt_ref, r2_ref, p2_ref, s_scr, vals_scr, rank_scr,
                     cand_scr, cvals_scr, crank_scr):
    K = PEER_TOPK
    tm = h_ref.shape[0]
    qt = lax.dot_general(wpqt_ref[...], h_ref[...], NT, preferred_element_type=F32)
    s_scr[...] = jnp.dot(kbd_ref[...], qt.astype(BF16), preferred_element_type=F32
                         ).reshape(2 * PEER_HEADS, PEER_NKEYS, tm)
    sub8 = lax.broadcasted_iota(I32, (8, tm), 0)
    neg = jnp.full((8, tm), -jnp.inf, F32)
    _top([s_scr[r] for r in range(2 * PEER_HEADS)], K, vals_scr, rank_scr)
    for hd in range(PEER_HEADS):
        c1, c2 = vals_scr[2 * hd], vals_scr[2 * hd + 1]
        blocks = [c1[0:1] + c2, c1[1:2] + c2[0:8]]
        for k1 in range(2, 8):
            blocks.append(jnp.where(sub8 < K // (k1 + 1), c1[k1:k1 + 1] + c2[0:8], neg))
        blocks.append(c1[8:16] + c2[0:1])
        cand_scr[hd] = jnp.concatenate(blocks, axis=0)
    _top([cand_scr[hd] for hd in range(PEER_HEADS)], K, cvals_scr, crank_scr)
    for hd in range(PEER_HEADS):
        s1, s2 = s_scr[2 * hd], s_scr[2 * hd + 1]
        c1, c2 = vals_scr[2 * hd], vals_scr[2 * hd + 1]
        rank1, rank2 = rank_scr[2 * hd], rank_scr[2 * hd + 1]
        cand = cand_scr[hd]
        taken = crank_scr[hd] < K
        z = jnp.sum(jnp.where(taken, jnp.exp(cand - (c1[0:1] + c2[0:1])), 0.0), axis=0, keepdims=True)
        tk = jnp.where(taken, 1.0, 0.0)
        per_k1 = [jnp.sum(tk[0:16], axis=0, keepdims=True)]
        per_k1 += [jnp.sum(tk[8 + 8 * k1:16 + 8 * k1], axis=0, keepdims=True) for k1 in range(1, 8)]
        cnt16 = jnp.concatenate(per_k1 + [tk[72:80]], axis=0)
        cnt = jnp.zeros(s1.shape, F32)
        for k1 in range(K):
            cnt = jnp.where(rank1 == float(k1), cnt16[k1:k1 + 1], cnt)
        g_ref[hd] = jnp.where(rank1 < K, jnp.exp(s1 - c1[0:1]) / z, 0.0)
        cnt_ref[hd] = cnt
        p2 = jnp.where(rank2 < K, jnp.exp(s2 - c2[0:1]), 0.0)
        cb = r2_ref.shape[-1]
        for tc in range(tm // cb):
            r2_ref[hd, tc] = rank2[:, tc * cb:(tc + 1) * cb].astype(r2_ref.dtype)
            p2_ref[hd, tc] = p2[:, tc * cb:(tc + 1) * cb].astype(p2_ref.dtype)


def _peer_select(h2, wpqt, kbd):
    n, d = h2.shape
    tm = 256 if n % 256 == 0 else n
    cb = min(LANES, tm)
    big = jax.ShapeDtypeStruct((PEER_HEADS, PEER_NKEYS, n), F32)
    blocked = jax.ShapeDtypeStruct((PEER_HEADS, n // cb, PEER_NKEYS, cb), BF16)

    def blk():
        return pl.BlockSpec((PEER_HEADS, PEER_NKEYS, tm), lambda i: (0, 0, i))

    def blk4():
        return pl.BlockSpec((PEER_HEADS, tm // cb, PEER_NKEYS, cb), lambda i: (0, i, 0, 0))

    return pl.pallas_call(
        _peer_sel_kernel,
        out_shape=(big, big, blocked, blocked),
        grid=(n // tm,),
        in_specs=[pl.BlockSpec((tm, d), lambda i: (i, 0)),
                  pl.BlockSpec((d, d), lambda i: (0, 0)),
                  pl.BlockSpec((2 * d, d), lambda i: (0, 0))],
        out_specs=(blk(), blk(), blk4(), blk4()),
        scratch_shapes=[pltpu.VMEM((2 * PEER_HEADS, PEER_NKEYS, tm), F32),
                        pltpu.VMEM((2 * PEER_HEADS, PEER_TOPK, tm), F32),
                        pltpu.VMEM((2 * PEER_HEADS, PEER_NKEYS, tm), F32),
                        pltpu.VMEM((PEER_HEADS, PEER_CAND, tm), F32),
                        pltpu.VMEM((PEER_HEADS, PEER_TOPK, tm), F32),
                        pltpu.VMEM((PEER_HEADS, PEER_CAND, tm), F32)],
        compiler_params=_cparams(("parallel",)),
    )(h2, wpqt, kbd)


def _gelu_tanh(x):
    return 0.5 * x * (1.0 + jnp.tanh(0.7978845608028654 * (x + 0.044715 * (x * x * x))))


def _peer_main_kernel(ni1, h_ref, x1_ref, g2_ref, u_ref, vt_ref, g_ref, cnt_ref, r2_ref, p2_ref, y_ref, acc, gate_s):
    j = pl.program_id(1)
    tm = h_ref.shape[0]

    @pl.when(j == 0)
    def _():
        acc[...] = jnp.zeros_like(acc)

    cb = r2_ref.shape[-1]
    reps = PEER_NKEYS // 16
    zero = jnp.zeros((PEER_NKEYS, cb), BF16)
    for l in range(ni1):
        for tc in range(tm // cb):
            ts = slice(tc * cb, (tc + 1) * cb)
            w = None
            for hd in range(PEER_HEADS):
                c16 = jnp.broadcast_to(cnt_ref[hd, l:l + 1, ts], (16, cb)).astype(BF16)
                g16 = jnp.broadcast_to(g_ref[hd, l:l + 1, ts], (16, cb)).astype(BF16)
                t = (jnp.where(r2_ref[hd, tc] < jnp.concatenate([c16] * reps, axis=0), p2_ref[hd, tc], zero)
                     * jnp.concatenate([g16] * reps, axis=0))
                w = t if w is None else w + t
            gate_s[tc, l * PEER_NKEYS:(l + 1) * PEER_NKEYS, :] = w

    act = lax.dot_general(u_ref[...], h_ref[...], NT, preferred_element_type=F32)
    gate = jnp.concatenate([gate_s[tc] for tc in range(tm // cb)], axis=1)
    coef = gate * _gelu_tanh(act.astype(BF16))
    acc[...] += jnp.dot(vt_ref[...], coef, preferred_element_type=F32)

    @pl.when(j == pl.num_programs(1) - 1)
    def _():
        y_ref[...] = x1_ref[...] + g2_ref[...] * acc[...].T


def _peer_main(h2, x1, g2, u16, vt16, g, cnt, r2, p2, seq_len):
    n, d = h2.shape
    tm = _row_tile(n, seq_len, 512)
    ni1 = 16
    et = ni1 * PEER_NKEYS
    cb = r2.shape[-1]
    g2_a, g2_s = _seq_operand(g2, seq_len, tm)

    def row():
        return pl.BlockSpec((tm, d), lambda i, j: (i, 0))

    return pl.pallas_call(
        functools.partial(_peer_main_kernel, ni1),
        out_shape=jax.ShapeDtypeStruct((n, d), F32),
        grid=(n // tm, N_EXPERTS // et),
        in_specs=[row(), row(), g2_s,
                  pl.BlockSpec((et, d), lambda i, j: (j, 0)),
                  pl.BlockSpec((d, et), lambda i, j: (0, j)),
                  pl.BlockSpec((PEER_HEADS, ni1, tm), lambda i, j: (0, j, i)),
                  pl.BlockSpec((PEER_HEADS, ni1, tm), lambda i, j: (0, j, i)),
                  pl.BlockSpec((PEER_HEADS, tm // cb, PEER_NKEYS, cb), lambda i, j: (0, i, 0, 0)),
                  pl.BlockSpec((PEER_HEADS, tm // cb, PEER_NKEYS, cb), lambda i, j: (0, i, 0, 0))],
        out_specs=row(),
        scratch_shapes=[pltpu.VMEM((d, tm), F32), pltpu.VMEM((tm // cb, et, cb), BF16)],
        compiler_params=_cparams(("parallel", "arbitrary")),
    )(h2, x1, g2_a, u16, vt16, g, cnt, r2, p2)


def _layer(x, mod, pos, shift_prev, s0, cache, lw):
    nb, t, d = x.shape
    n = nb * t
    sh1, sc1, g1, sh2, sc2, g2 = [mod[:, i * d:(i + 1) * d] for i in range(6)]
    x2 = x.reshape(n, d)
    P = _inproj(x2, sc1, sh1, lw['norm1_w'], lw['w_in16'], t)

    prev = _pack_rw(shift_prev).reshape(nb, 1, P_COLS)
    o_a, zf = _rwkv(P, nb, t, prev, lw['mu'], lw['w0'], lw['a0'], lw['k_k'], lw['k_a'], lw['r_k'], lw['lnx_w'],
                    lw['lnx_b'], lw['wup'], lw['aup'], lw['gup'], _state_to_pairs(s0))
    wkv = _pairs_to_state(zf)
    shift_last = _unpack_rw(P.reshape(nb, t, P_COLS)[:, -1, :])

    q16, k32, k16, v32, v16, qi16, kw32, ki2 = _dsa_prep(P, jnp.tile(pos, nb), lw['q_norm_w'], lw['k_norm_w'])
    if cache is None:
        o_b = _attn_prompt(q16, qi16, kw32, k16, v16, ki2, nb, t)
    else:
        ck, cv, cki = cache
        past = ck.shape[1]
        o_b = _attn_sample(q16, qi16, kw32, k16, v16, ki2, ck.reshape(nb, past, d), cv.reshape(nb, past, d), cki,
                           nb, t)

    x1, h2 = _merge(x2, o_a, o_b, P, lw['b_gate'], g1, sc2, sh2, lw['norm2_w'], lw['wpa'], lw['wpb'], lw['wout'], t)
    g, cnt, r2, p2 = _peer_select(h2, lw['wpqt'], lw['kbd'])
    y = _peer_main(h2, x1, g2, lw['u16'], lw['vt16'], g, cnt, r2, p2, t)

    k_new = k32.reshape(nb, t, N_HEADS, HEAD_DIM)
    v_new = v32.reshape(nb, t, N_HEADS, HEAD_DIM)
    ki_new = kw32[:, :IDX_DIM].reshape(nb, t, IDX_DIM)
    return y.reshape(nb, t, d), wkv, shift_last, k_new, v_new, ki_new


def _layer_weights(l, w_in, b_gate, mu_rw, w0, w_up, a0, a_up, g_up, k_k, k_a, r_k, lnx_w, lnx_b, q_norm_w, k_norm_w,
                   w_proj_a, w_proj_b, w_out, norm1_w, norm2_w, w_pq, peer_keys, peer_u, peer_v):
    d = D_MODEL
    zeros = lambda r: jnp.zeros((r, d), F32)
    keys = peer_keys[l].reshape(2 * PEER_HEADS, PEER_NKEYS, PEER_DHALF)
    eye = jnp.eye(2 * PEER_HEADS, dtype=F32)
    kbd = (eye[:, None, :, None] * keys[:, :, None, :]).reshape(2 * d, d)
    return {
        'w_in16': _pack_in(w_in[l]).astype(BF16), 'b_gate': b_gate[l], 'mu': _pack_rw(mu_rw[l]).reshape(1, P_COLS),
        'w0': w0[l], 'a0': a0[l], 'k_k': k_k[l], 'k_a': k_a[l], 'r_k': r_k[l].reshape(d), 'lnx_w': lnx_w[l],
        'lnx_b': lnx_b[l],
        'wup': jnp.concatenate([w_up[l], zeros(LANES - D_DECAY)], axis=0).astype(BF16),
        'aup': jnp.concatenate([zeros(D_DECAY), a_up[l]], axis=0).astype(BF16),
        'gup': jnp.concatenate([g_up[l], zeros(256 - D_GATE)], axis=0).astype(BF16),
        'q_norm_w': q_norm_w[l], 'k_norm_w': k_norm_w[l], 'norm1_w': norm1_w[l], 'norm2_w': norm2_w[l],
        'wpa': w_proj_a[l].astype(BF16), 'wpb': w_proj_b[l].astype(BF16), 'wout': w_out[l].astype(BF16),
        'wpqt': w_pq[l].T.astype(BF16), 'kbd': kbd.astype(BF16),
        'u16': peer_u[l].astype(BF16), 'vt16': peer_v[l].T.astype(BF16),
    }


def kernel(x_prompt, x_sample, c_prompt, c_sample, cache_k, cache_v, cache_kidx, state_wkv, state_shift, w_ada, b_ada,
           norm1_w, w_in, b_gate, mu_rw, w0, w_up, a0, a_up, g_up, k_k, k_a, r_k, lnx_w, lnx_b, q_norm_w, k_norm_w,
           w_proj_a, w_proj_b, w_out, norm2_w, w_pq, peer_keys, peer_u, peer_v):
    depth = w_in.shape[0]
    bp, tp = x_prompt.shape[:2]
    bs, ts = x_sample.shape[:2]
    past = cache_k.shape[2]
    dt = x_prompt.dtype
    pos_p = jnp.arange(tp, dtype=I32)
    pos_s = past + jnp.arange(ts, dtype=I32)
    zero_shift = jnp.zeros((bp, RW_IN), dt)
    zero_wkv = jnp.zeros((bp, N_HEADS, HEAD_DIM, HEAD_DIM), dt)
    c_all = jnp.concatenate([c_prompt, c_sample], axis=0)
    xp, xs = x_prompt, x_sample
    outs_p, outs_s = [], []
    for l in range(depth):
        lw = _layer_weights(l, w_in, b_gate, mu_rw, w0, w_up, a0, a_up, g_up, k_k, k_a, r_k, lnx_w, lnx_b, q_norm_w,
                            k_norm_w, w_proj_a, w_proj_b, w_out, norm1_w, norm2_w, w_pq, peer_keys, peer_u, peer_v)
        mod = _ada(c_all, w_ada[l], b_ada[l])
        xp, *rest_p = _layer(xp, mod[:bp], pos_p, zero_shift, zero_wkv, None, lw)
        xs, *rest_s = _layer(xs, mod[bp:], pos_s, state_shift[l], state_wkv[l],
                             (cache_k[l], cache_v[l], cache_kidx[l]), lw)
        outs_p.append(rest_p)
        outs_s.append(rest_s)
    stack = lambda outs, i: jnp.stack([o[i] for o in outs])
    return (xp, xs,
            stack(outs_p, 0), stack(outs_p, 1), stack(outs_p, 2), stack(outs_p, 3), stack(outs_p, 4),
            stack(outs_s, 0), stack(outs_s, 1), stack(outs_s, 2), stack(outs_s, 3), stack(outs_s, 4))
```

```python
import functools

import jax
import jax.numpy as jnp
from jax import lax
from jax.experimental import pallas as pl
from jax.experimental.pallas import tpu as pltpu

F32 = jnp.float32
BF16 = jnp.bfloat16
I32 = jnp.int32

LANES = 128
D_MODEL = 1024
EPS = 1e-6
GN_EPS = 64e-5
ROPE_THETA = 10000.0
CHUNK = 64
TOPK_MAX = 256
HEAD_DIM = 64
N_HEADS = D_MODEL // HEAD_DIM
N_PAIRS = N_HEADS // 2
IDX_HEADS = 8
IDX_DIM = 64
D_DECAY = 64
D_AAA = 64
D_GATE = 160
RW_IN = 3 * D_MODEL + D_DECAY + D_AAA + D_GATE
PEER_HEADS = 8
PEER_NKEYS = 128
PEER_TOPK = 16
PEER_DHALF = 64
N_EXPERTS = PEER_NKEYS * PEER_NKEYS
RW_CHUNK = 64
RW_INTERLEAVE = 16
RW_PAIRS_LONG = 2
RW_PASSES = (2, 1, 1, 1)
VMEM_LIMIT = 56 * 1024 * 1024
LOG2E = 1.4426950408889634
PEER_CAND = 80
GATE_ROWS = 32

C_R, C_K, C_V = 0, 1024, 2048
C_Q, C_KD, C_VD = 3072, 4096, 5120
C_GA, C_GB = 6144, 7168
C_QI = 8192
C_G = 8704
C_M = 8960
C_KW = 9088
P_COLS = 9216
IN_W = 9064

NT = (((1,), (1,)), ((), ()))
NN = (((1,), (0,)), ((), ()))


def _pack_in(w):
    z = lambda k: jnp.zeros(w.shape[:-1] + (k,), w.dtype)
    return jnp.concatenate([w[..., 0:3072], w[..., 3360:6432], w[..., 7016:9064], w[..., 6432:6944],
                            w[..., 3200:3360], z(256 - D_GATE), w[..., 3072:3200],
                            w[..., 6944:7016], z(LANES - IDX_DIM - IDX_HEADS)], axis=-1)


def _pack_rw(a):
    return _pack_in(jnp.concatenate([a, jnp.zeros(a.shape[:-1] + (IN_W - RW_IN,), a.dtype)], axis=-1))


def _unpack_rw(p):
    return jnp.concatenate([p[..., :3072], p[..., C_M:C_M + 128], p[..., C_G:C_G + D_GATE]], axis=-1)


def _split_bf16(x, n):
    parts = []
    r = x
    for _ in range(n):
        p = r.astype(BF16)
        parts.append(p)
        r = r - p.astype(F32)
    return parts


def _mm(a, b, pa=1, pb=1, dims=NN):
    aps = _split_bf16(a, pa) if a.dtype != BF16 else [a]
    bps = _split_bf16(b, pb) if b.dtype != BF16 else [b]
    order = max(len(aps), len(bps))
    out = None
    for i, ap in enumerate(aps):
        for j, bp in enumerate(bps):
            if i + j >= order:
                continue
            t = lax.dot_general(ap, bp, dims, preferred_element_type=F32)
            out = t if out is None else out + t
    return out


def _sigmoid(x):
    return 1.0 / (1.0 + jnp.exp(-x))


def _softplus(z):
    return jnp.maximum(z, 0.0) + jnp.log(1.0 + jnp.exp(-jnp.abs(z)))


def _cparams(sem):
    return pltpu.CompilerParams(dimension_semantics=sem, vmem_limit_bytes=VMEM_LIMIT)


def _ada_kernel(c_ref, w_ref, b_ref, o_ref):
    c = c_ref[...]
    s = c * _sigmoid(c)
    o_ref[...] = _mm(s, w_ref[...], 2, 2) + b_ref[...]


def _ada(c, w, b):
    m, d = c.shape
    n = w.shape[1]
    tn = 1024
    return pl.pallas_call(
        _ada_kernel,
        out_shape=jax.ShapeDtypeStruct((m, n), F32),
        grid=(n // tn,),
        in_specs=[pl.BlockSpec((m, d), lambda j: (0, 0)),
                  pl.BlockSpec((d, tn), lambda j: (0, j)),
                  pl.BlockSpec((1, tn), lambda j: (0, j))],
        out_specs=pl.BlockSpec((m, tn), lambda j: (0, j)),
        compiler_params=_cparams(("arbitrary",)),
    )(c, w, b.reshape(1, n))


def _seq_operand(vec, seq_len, tm):
    b, d = vec.shape
    if seq_len % tm == 0:
        per = seq_len // tm
        arr = vec.reshape(b, 1, d)
        spec = pl.BlockSpec((None, 1, d), lambda *g: (g[0] // per, 0, 0))
    else:
        assert tm % seq_len == 0
        arr = jnp.repeat(vec, seq_len, axis=0)
        spec = pl.BlockSpec((tm, d), lambda *g: (g[0], 0))
    return arr, spec


def _row_tile(n, seq_len, cap):
    tm = min(cap, n)
    while n % tm or (seq_len % tm and tm % seq_len):
        tm //= 2
    return tm


def _inproj_kernel(x_ref, sc_ref, sh_ref, nw_ref, w_ref, o_ref, h_scr):
    @pl.when(pl.program_id(1) == 0)
    def _():
        x = x_ref[...]
        y = x * lax.rsqrt(jnp.mean(x * x, axis=-1, keepdims=True) + EPS) * nw_ref[...]
        h_scr[...] = (y * (1.0 + sc_ref[...]) + sh_ref[...]).astype(BF16)

    o_ref[...] = jnp.dot(h_scr[...], w_ref[...], preferred_element_type=F32)


def _inproj(x2, sc, sh, nw, w16, seq_len):
    n, d = x2.shape
    tm = _row_tile(n, seq_len, 1024)
    tn = 1024
    sc_a, sc_s = _seq_operand(sc, seq_len, tm)
    sh_a, sh_s = _seq_operand(sh, seq_len, tm)
    return pl.pallas_call(
        _inproj_kernel,
        out_shape=jax.ShapeDtypeStruct((n, P_COLS), F32),
        grid=(n // tm, P_COLS // tn),
        in_specs=[pl.BlockSpec((tm, d), lambda i, j: (i, 0)), sc_s, sh_s,
                  pl.BlockSpec((1, d), lambda i, j: (0, 0)),
                  pl.BlockSpec((d, tn), lambda i, j: (0, j))],
        out_specs=pl.BlockSpec((tm, tn), lambda i, j: (i, j)),
        scratch_shapes=[pltpu.VMEM((tm, d), BF16)],
        compiler_params=_cparams(("parallel", "arbitrary")),
    )(x2, sc_a, sh_a, nw.reshape(1, d), w16)


def _lane_lo(shape):
    return lax.broadcasted_iota(I32, shape, len(shape) - 1) < HEAD_DIM


def _pair_sum(x):
    lo = _lane_lo(x.shape)
    s0 = jnp.sum(jnp.where(lo, x, 0.0), axis=-1, keepdims=True)
    s1 = jnp.sum(jnp.where(lo, 0.0, x), axis=-1, keepdims=True)
    return jnp.where(lo, s0, s1)


def _stack2(x):
    lo = _lane_lo(x.shape)
    return jnp.concatenate([jnp.where(lo, x, 0.0), jnp.where(lo, 0.0, x)], axis=0)


def _rwkv_kernel(t_real, npair, nchunk, pr, pk, pv, pg, pm, sr, sk, sv, sg, sm, mr, mk, mv, mg, mmu,
                 w0, a0, kkw, kaw, rkw, lnw, lnb, wup, aup, gup, z0, o_ref, zf_ref,
                 r_s, lw_s, k_s, v_s, a_s, b_s, y_s, bonus_s, g_s):
    C = RW_CHUNK
    t_pad = r_s.shape[1]

    def mix(p_ref, s_ref, m_ref):
        p = p_ref[...]
        prev = pltpu.roll(p, 1, 0)
        row = lax.broadcasted_iota(I32, p.shape, 0)
        prev = jnp.where(row == 0, s_ref[...], prev)
        return p + (prev - p) * m_ref[...]

    xg, xm = mix(pg, sg, mg), mix(pm, sm, mmu)
    th16, xm16, sg16 = jnp.tanh(xm).astype(BF16), xm.astype(BF16), _sigmoid(xg).astype(BF16)
    xr_all, xk_all, xv_all = mix(pr, sr, mr), mix(pk, sk, mk), mix(pv, sv, mv)

    def put(ref, pp, val):
        if t_pad > t_real:
            val = jnp.concatenate([val, jnp.zeros((t_pad - t_real, LANES), F32)], axis=0)
        ref[pp] = val

    for pp in range(npair):
        cs = slice(pp * LANES, (pp + 1) * LANES)
        xr, xk, xv = xr_all[:, cs], xk_all[:, cs], xv_all[:, cs]
        dw = jnp.dot(th16, wup[:, cs], preferred_element_type=F32)
        lw = -jnp.exp(-_softplus(-(w0[:, cs] + dw)) - 0.5)
        asig = _sigmoid(a0[:, cs] + jnp.dot(xm16, aup[:, cs], preferred_element_type=F32))
        g_s[pp] = jnp.dot(sg16, gup[:, cs], preferred_element_type=F32)
        kk = xk * kkw[:, cs]
        kk = kk * lax.rsqrt(_pair_sum(kk * kk) + 1e-12)
        kmod = xk * (1.0 + (asig - 1.0) * kaw[:, cs])
        bonus_s[pp] = _pair_sum(xr * kmod * rkw[:, cs]) * xv
        put(r_s, pp, xr)
        put(lw_s, pp, lw)
        put(k_s, pp, kmod)
        put(v_s, pp, xv)
        put(a_s, pp, -kk)
        put(b_s, pp, kk * asig)

    n2 = 2 * C
    ri = lax.broadcasted_iota(I32, (n2, n2), 0)
    ci = lax.broadcasted_iota(I32, (n2, n2), 1)
    same = (ri // C) == (ci // C)
    strict = same & ((ri % C) > (ci % C))
    incl = same & ((ri % C) >= (ci % C))
    eye = ri == ci
    eye_f = jnp.where(eye, 1.0, 0.0)
    tri = jnp.where(lax.broadcasted_iota(I32, (C, C), 0) >= lax.broadcasted_iota(I32, (C, C), 1), 1.0, 0.0
                    ).astype(BF16)
    zeros_sq = jnp.zeros((n2, LANES), F32)

    pg_, pi_, po_, ps_ = RW_PASSES

    def local(chains):
        each = lambda f, *cols: [f(*xs) for xs in zip(*cols)]
        lwc = [lw_s[pp, sl, :] for sl, pp in chains]
        cum = each(lambda l: _mm(tri, l, 1, 3), lwc)
        cum_last = each(lambda c: c[C - 1:C, :], cum)
        ec, eci = each(jnp.exp, cum), each(lambda c: jnp.exp(-c), cum)
        ecp = each(lambda c, l: jnp.exp(c - l), cum, lwc)
        ecl = each(lambda c, cl: jnp.exp(cl - c), cum, cum_last)
        a_c = [a_s[pp, sl, :] for sl, pp in chains]
        b_c = [b_s[pp, sl, :] for sl, pp in chains]
        k_c = [k_s[pp, sl, :] for sl, pp in chains]
        r_c = [r_s[pp, sl, :] for sl, pp in chains]
        As = each(lambda a, e: _stack2(a * e), a_c, ecp)
        Rs = each(lambda r, e: _stack2(r * e), r_c, ec)
        Bs = each(lambda b, e: _stack2(b * e), b_c, eci)
        Ks = each(lambda k, e: _stack2(k * e), k_c, eci)
        Bt = each(lambda b, e: _stack2(b * e), b_c, ecl)
        Kt = each(lambda k, e: _stack2(k * e), k_c, ecl)
        Vs = [_stack2(v_s[pp, sl, :]) for sl, pp in chains]

        G = each(lambda a, r, b, k: _mm(jnp.concatenate([a, r], axis=0), jnp.concatenate([b, k], axis=0),
                                        pg_, pg_, NT), As, Rs, Bs, Ks)
        a_ab = each(lambda g: jnp.where(strict, g[:n2, :n2], 0.0), G)
        a_ak = each(lambda g: jnp.where(strict, g[:n2, n2:], 0.0), G)
        a_rb = each(lambda g: jnp.where(incl, g[n2:, :n2], 0.0), G)
        a_rk = each(lambda g: jnp.where(incl, g[n2:, n2:], 0.0), G)

        lp = a_ab
        tm_ = each(lambda a: eye_f + a, a_ab)
        step = 2
        while step < C:
            lp = each(lambda l: _mm(l, l, pi_, pi_), lp)
            tm_ = each(lambda t, l: t + _mm(t, l, pi_, pi_), tm_, lp)
            step *= 2

        w1 = each(lambda a, v: _mm(a, v, po_, po_), a_ak, Vs)
        mu_ = each(lambda t, a, w: _mm(t, jnp.concatenate([a, w], axis=1), po_, po_), tm_, As, w1)
        rhs = each(lambda m, v: jnp.concatenate([m, jnp.concatenate([zeros_sq, v], axis=1)], axis=0), mu_, Vs)
        lhs = each(lambda rb, rk, b, k: jnp.concatenate([jnp.concatenate([rb, rk], axis=1),
                                                         jnp.concatenate([b.T, k.T], axis=1)], axis=0),
                   a_rb, a_rk, Bt, Kt)
        out2 = each(lambda l, r: _mm(l, r, po_, po_), lhs, rhs)
        m23 = each(lambda r, o, cl: jnp.concatenate([r + o[:n2, :LANES],
                                                     jnp.where(eye, jnp.exp(cl), 0.0) + o[n2:, :LANES]], axis=0),
                   Rs, out2, cum_last)
        return [(m, o[:n2, LANES:], o[n2:, LANES:]) for m, o in zip(m23, out2)]

    def step_chunks(i, zs):
        sls = [pl.ds(pl.multiple_of((i * nchunk + j) * C, C), C) for j in range(nchunk)]
        parts = local([(sl, pp) for sl in sls for pp in range(npair)])
        zs = list(zs)
        for j, sl in enumerate(sls):
            for pp in range(npair):
                m23, y_loc, z_loc = parts[j * npair + pp]
                yz = _mm(m23, zs[pp], ps_, ps_)
                y = yz[:n2] + y_loc
                y_s[pp, sl, :] = y[:C] + y[C:]
                zs[pp] = yz[n2:] + z_loc
        return tuple(zs)

    zs = lax.fori_loop(0, t_pad // (C * nchunk), step_chunks, tuple(z0[pp] for pp in range(npair)))
    for pp in range(npair):
        zf_ref[pp] = zs[pp]
        cs = slice(pp * LANES, (pp + 1) * LANES)
        y = y_s[pp, 0:t_real, :]
        mean = _pair_sum(y) * (1.0 / HEAD_DIM)
        dlt = y - mean
        var = _pair_sum(dlt * dlt) * (1.0 / HEAD_DIM)
        yn = dlt * lax.rsqrt(var + GN_EPS) * lnw[:, cs] + lnb[:, cs]
        o_ref[:, cs] = ((yn + bonus_s[pp]) * g_s[pp]).astype(o_ref.dtype)


def _rwkv(P, nb, t, prev, mu, w0, a0, k_k, k_a, r_k, lnx_w, lnx_b, wup, aup, gup, z0):
    t_pad = max(t, RW_CHUNK)
    assert t % 8 == 0 and t_pad % RW_CHUNK == 0
    n_chunks = t_pad // RW_CHUNK
    nchunk = min(RW_INTERLEAVE // RW_PAIRS_LONG, n_chunks)
    npair = min(N_PAIRS, max(1, RW_INTERLEAVE // nchunk))
    wp = npair * LANES

    def cblk(c0, w, per_pair):
        return (lambda p: c0 // w + p) if per_pair else (lambda p: c0 // w)

    def pcol(c0, w, pp):
        f = cblk(c0, w, pp)
        return pl.BlockSpec((t, w), lambda b, p: (b, f(p)))

    def prevcol(c0, w, pp):
        f = cblk(c0, w, pp)
        return pl.BlockSpec((None, 1, w), lambda b, p: (b, 0, f(p)))

    def mucol(c0, w, pp):
        f = cblk(c0, w, pp)
        return pl.BlockSpec((1, w), lambda b, p: (0, f(p)))

    def hvec():
        return pl.BlockSpec((1, wp), lambda b, p: (0, p))

    cols = [(C_R, wp, True), (C_K, wp, True), (C_V, wp, True), (C_G, 256, False), (C_M, LANES, False)]
    in_specs = ([pcol(*c) for c in cols] + [prevcol(*c) for c in cols] + [mucol(*c) for c in cols]
                + [hvec() for _ in range(7)]
                + [pl.BlockSpec((LANES, wp), lambda b, p: (0, p)),
                   pl.BlockSpec((LANES, wp), lambda b, p: (0, p)),
                   pl.BlockSpec((256, wp), lambda b, p: (0, p)),
                   pl.BlockSpec((None, npair, LANES, LANES), lambda b, p: (b, p, 0, 0))])
    vecs = [v.reshape(1, D_MODEL) for v in (w0, a0, k_k, k_a, r_k, lnx_w, lnx_b)]
    o, zf = pl.pallas_call(
        functools.partial(_rwkv_kernel, t, npair, nchunk),
        out_shape=(jax.ShapeDtypeStruct((nb * t, D_MODEL), BF16),
                   jax.ShapeDtypeStruct((nb, N_PAIRS, LANES, LANES), F32)),
        grid=(nb, N_PAIRS // npair),
        in_specs=in_specs,
        out_specs=(pl.BlockSpec((t, wp), lambda b, p: (b, p)),
                   pl.BlockSpec((None, npair, LANES, LANES), lambda b, p: (b, p, 0, 0))),
        scratch_shapes=([pltpu.VMEM((npair, t_pad, LANES), F32) for _ in range(7)]
                        + [pltpu.VMEM((npair, t, LANES), F32) for _ in range(2)]),
        compiler_params=_cparams(("parallel", "arbitrary")),
    )(P, P, P, P, P, prev, prev, prev, prev, prev, mu, mu, mu, mu, mu, *vecs, wup, aup, gup, z0)
    return o, zf


def _state_to_pairs(s):
    nb = s.shape[0]
    zt = jnp.swapaxes(s, -1, -2).reshape(nb, N_PAIRS, 2, HEAD_DIM, HEAD_DIM)
    zero = jnp.zeros_like(zt[:, :, 0])
    top = jnp.concatenate([zt[:, :, 0], zero], axis=-1)
    bot = jnp.concatenate([zero, zt[:, :, 1]], axis=-1)
    return jnp.concatenate([top, bot], axis=-2)


def _pairs_to_state(z):
    nb = z.shape[0]
    h0 = z[:, :, :HEAD_DIM, :HEAD_DIM]
    h1 = z[:, :, HEAD_DIM:, HEAD_DIM:]
    s = jnp.stack([h0, h1], axis=2).reshape(nb, N_HEADS, HEAD_DIM, HEAD_DIM)
    return jnp.swapaxes(s, -1, -2)


def _rope(x, cos, sin_signed):
    w = x.shape[1]
    reps = w // LANES
    cw = jnp.concatenate([cos] * reps, axis=1) if reps > 1 else cos
    sw = jnp.concatenate([sin_signed] * reps, axis=1) if reps > 1 else sin_signed
    lane = lax.broadcasted_iota(I32, x.shape, 1)
    fwd = pltpu.roll(x, w - 32, 1)
    bwd = pltpu.roll(x, 32, 1)
    partner = jnp.where((lane % HEAD_DIM) < 32, fwd, bwd)
    return x * cw + partner * sw


def _head_rms(x, nw, e_dn, e_up):
    ms = _mm(x * x, e_dn, 2, 1) * (1.0 / HEAD_DIM)
    r = lax.rsqrt(ms + EPS)
    return x * _mm(r, e_up, 2, 1) * nw


def _dsa_prep_kernel(pq, pkd, pvd, pqi, pkw, cos_ref, sin_ref, qn, kn, edn, eup,
                     q16, k32, k16, v32, v16, qi16, kw32, ki2):
    cos, sin = cos_ref[...], sin_ref[...]
    e_dn, e_up = edn[...], eup[...]
    def put_pairs(ref, x):
        for p in range(N_PAIRS):
            ref[p] = x[:, p * LANES:(p + 1) * LANES].astype(ref.dtype)

    q = _rope(_head_rms(pq[...], qn[...], e_dn, e_up), cos, sin)
    put_pairs(q16, q * (HEAD_DIM ** -0.5 * LOG2E))
    k = _rope(_head_rms(pkd[...], kn[...], e_dn, e_up), cos, sin)
    k32[...] = k
    put_pairs(k16, k)
    v = pvd[...]
    v32[...] = v
    put_pairs(v16, v)
    qi16[...] = _rope(pqi[...], cos, sin).astype(BF16)
    kw = pkw[...]
    lane = lax.broadcasted_iota(I32, kw.shape, 1)
    wi_scale = (IDX_HEADS * IDX_DIM) ** -0.5
    kr = _rope(kw, cos, sin)
    kw32[...] = jnp.where(lane < IDX_DIM, kr, jnp.where(lane < IDX_DIM + IDX_HEADS, kw * wi_scale, 0.0))
    ki2[...] = jnp.where(lane < IDX_DIM, kr, pltpu.roll(kr, IDX_DIM, 1)).astype(BF16)


def _dsa_prep(P, pos_rows, q_norm_w, k_norm_w):
    n = P.shape[0]
    tm = 512 if n % 512 == 0 else n
    half = HEAD_DIM // 2
    inv = ROPE_THETA ** (-jnp.arange(half, dtype=F32) / half)
    ang = pos_rows.astype(F32)[:, None] * inv[None, :]
    cos = jnp.tile(jnp.cos(ang), (1, 4))
    sin = jnp.sin(ang)
    sin_signed = jnp.tile(jnp.concatenate([-sin, sin], axis=1), (1, 2))
    head_of = jnp.arange(D_MODEL) // HEAD_DIM
    e_dn = (head_of[:, None] == jnp.arange(LANES)[None, :]).astype(BF16)
    e_up = e_dn.T
    qn = jnp.tile(q_norm_w, N_HEADS).reshape(1, D_MODEL)
    kn = jnp.tile(k_norm_w, N_HEADS).reshape(1, D_MODEL)

    def col(c0, w):
        return pl.BlockSpec((tm, w), lambda i, c0=c0, w=w: (i, c0 // w))

    def row(w):
        return pl.BlockSpec((tm, w), lambda i: (i, 0))

    def const(shape):
        return pl.BlockSpec(shape, lambda i: (0, 0))

    pairs = jax.ShapeDtypeStruct((N_PAIRS, n, LANES), BF16)
    pair_spec = pl.BlockSpec((N_PAIRS, tm, LANES), lambda i: (0, i, 0))
    return pl.pallas_call(
        _dsa_prep_kernel,
        out_shape=(pairs, jax.ShapeDtypeStruct((n, D_MODEL), F32), pairs, jax.ShapeDtypeStruct((n, D_MODEL), F32),
                   pairs, jax.ShapeDtypeStruct((n, IDX_HEADS * IDX_DIM), BF16),
                   jax.ShapeDtypeStruct((n, LANES), F32), jax.ShapeDtypeStruct((n, LANES), BF16)),
        grid=(n // tm,),
        in_specs=[col(C_Q, 1024), col(C_KD, 1024), col(C_VD, 1024), col(C_QI, 512), col(C_KW, LANES),
                  row(LANES), row(LANES), const((1, D_MODEL)), const((1, D_MODEL)),
                  const((D_MODEL, LANES)), const((LANES, D_MODEL))],
        out_specs=(pair_spec, row(D_MODEL), pair_spec, row(D_MODEL), pair_spec, row(512), row(LANES), row(LANES)),
        compiler_params=_cparams(("parallel",)),
    )(P, P, P, P, P, cos, sin_signed, qn, kn, e_dn, e_up)


CODE_NEG_INF = -1 - 0x7F800000
ATTN_PAIRS = 2
SAMPLE_PAIRS = 4


def _index_scores(qi, wi, ki_list):
    outs = []
    for ki in ki_list:
        acc = None
        for h in range(IDX_HEADS):
            qpair = qi[:, (h // 2) * LANES:(h // 2 + 1) * LANES]
            lo = _lane_lo(qpair.shape)
            qh = jnp.where(lo if h % 2 == 0 else jnp.logical_not(lo), qpair, jnp.zeros_like(qpair))
            rel = lax.dot_general(qh, ki, NT, preferred_element_type=F32)
            term = wi[:, IDX_DIM + h:IDX_DIM + h + 1] * jnp.maximum(rel, 0.0)
            acc = term if acc is None else acc + term
        outs.append(acc)
    return outs


def _select_topk(keys, topk, bias_refs):
    tq = keys[0].shape[0]
    neg = -jnp.inf

    def write(masks):
        for ref, k, msk in zip(bias_refs, keys, masks):
            ref[:, 0:k.shape[1]] = jnp.where(msk, 0.0, neg)

    def count(pred_list):
        tot = None
        for p in pred_list:
            c = jnp.sum(jnp.where(p, 1.0, 0.0), axis=-1, keepdims=True)
            tot = c if tot is None else tot + c
        return tot

    def threshold(c):
        bits = jnp.where(c >= 0, c, c ^ jnp.int32(0x7FFFFFFF))
        return jnp.where(c < jnp.int32(CODE_NEG_INF), neg, lax.bitcast_convert_type(bits, F32))

    few = count([k > neg for k in keys]) <= topk

    def pending(cnt):
        return jnp.max(jnp.where(few | (cnt == topk), 0.0, 1.0))

    def bit_step(state):
        i, c, cnt, _ = state
        trial = c + jnp.left_shift(jnp.int32(1), 31 - i)
        cnt_t = count([k >= threshold(trial) for k in keys])
        take = cnt_t >= topk
        cnt = jnp.where(take, cnt_t, cnt)
        return i + 1, jnp.where(take, trial, c), cnt, pending(cnt)

    cnt0 = jnp.full((tq, 1), float(sum(k.shape[1] for k in keys)), F32)
    state = (jnp.int32(0), jnp.full((tq, 1), -2 ** 31, I32), cnt0, pending(cnt0))
    _, code, _, _ = lax.while_loop(lambda s: (s[0] < 32) & (s[3] > 0.0), bit_step, state)
    thr = threshold(code)
    ge = [(k >= thr) & (k > neg) for k in keys]
    write(ge)
    surplus = jnp.max(count(ge)) > topk

    @pl.when(surplus)
    def _():
        gt = [k > thr for k in keys]
        need = topk - count(gt)
        ties = [(k == thr) & (k > neg) for k in keys]
        offs, idx = 0, []
        for k in keys:
            idx.append(lax.broadcasted_iota(I32, k.shape, 1) + offs)
            offs += k.shape[1]
        nbits = max(1, (offs - 1).bit_length() + 1)

        def idx_step(i, m):
            trial = m + jnp.left_shift(jnp.int32(1), nbits - 1 - i)
            cnt = count([t & (ix < trial) for t, ix in zip(ties, idx)])
            return jnp.where(cnt <= need, trial, m)

        cut = lax.fori_loop(0, nbits, idx_step, jnp.zeros((tq, 1), I32))
        write([g | (t & (ix < cut)) for g, t, ix in zip(gt, ties, idx)])


def _attend_pairs(q_pairs, k_lists, v_lists, bias_list):
    lo = _lane_lo(q_pairs[0].shape)
    zero = jnp.zeros_like(q_pairs[0])
    heads = []
    for pi, q in enumerate(q_pairs):
        heads += [(jnp.where(lo, q, zero), pi), (jnp.where(lo, zero, q), pi)]
    s = [[lax.dot_general(qh, k, NT, preferred_element_type=F32) + b for k, b in zip(k_lists[pi], bias_list)]
         for qh, pi in heads]
    m = []
    for sh in s:
        mh = None
        for sj in sh:
            mx = jnp.max(sj, axis=-1, keepdims=True)
            mh = mx if mh is None else jnp.maximum(mh, mx)
        m.append(mh)
    p = [[jnp.exp2(sj - mh) for sj in sh] for sh, mh in zip(s, m)]
    den = [functools.reduce(lambda a, b: a + b, [jnp.sum(pj, axis=-1, keepdims=True) for pj in ph]) for ph in p]
    acc = [functools.reduce(lambda a, b: a + b,
                            [jnp.dot(pj.astype(BF16), v, preferred_element_type=F32)
                             for pj, v in zip(ph, v_lists[pi])])
           for ph, (_, pi) in zip(p, heads)]
    outs = [a / d for a, d in zip(acc, den)]
    return [jnp.where(lo, outs[2 * i], outs[2 * i + 1]) for i in range(len(q_pairs))]


def _attn_prompt_kernel(topk, ncase, q_ref, qi_ref, kw_ref, k_ref, v_ref, ki2_ref, o_ref, bias_s):
    tq = q_ref.shape[1]
    t = k_ref.shape[1]
    i = pl.program_id(1)
    lstep = t // ncase
    case = ((i + 1) * tq - 1) // lstep

    def run(L):
        score = _index_scores(qi_ref[...], kw_ref[...], [ki2_ref[0:L, :]])[0]
        qpos = i * tq + lax.broadcasted_iota(I32, (tq, L), 0)
        kpos = lax.broadcasted_iota(I32, (tq, L), 1)
        adm = (qpos // CHUNK) >= (kpos // CHUNK)
        _select_topk([jnp.where(adm, score, -jnp.inf)], topk, [bias_s])

        def pairs(g, carry):
            ps = [ATTN_PAIRS * g + j for j in range(ATTN_PAIRS)]
            outs = _attend_pairs([q_ref[p] for p in ps], [[k_ref[p, 0:L, :]] for p in ps],
                                 [[v_ref[p, 0:L, :]] for p in ps], [bias_s[:, 0:L]])
            for p, o in zip(ps, outs):
                o_ref[p] = o.astype(o_ref.dtype)
            return carry

        lax.fori_loop(0, N_PAIRS // ATTN_PAIRS, pairs, 0)

    for c in range(ncase):
        pl.when(case == c)(functools.partial(run, (c + 1) * lstep))


def _attn_prompt(q16, qi16, kw32, k16, v16, ki2, nb, t):
    tq = min(256, t)
    topk = min(TOPK_MAX, t // 4)
    nq = t // tq
    ncase = min(4, nq)

    def qrow(w):
        return pl.BlockSpec((tq, w), lambda b, i: (b * nq + i, 0))

    def qpairs():
        return pl.BlockSpec((N_PAIRS, tq, LANES), lambda b, i: (0, b * nq + i, 0))

    def kpairs():
        return pl.BlockSpec((N_PAIRS, t, LANES), lambda b, i: (0, b, 0))

    return pl.pallas_call(
        functools.partial(_attn_prompt_kernel, topk, ncase),
        out_shape=jax.ShapeDtypeStruct((N_PAIRS, nb * t, LANES), BF16),
        grid=(nb, nq),
        in_specs=[qpairs(), qrow(512), qrow(LANES), kpairs(), kpairs(),
                  pl.BlockSpec((t, LANES), lambda b, i: (b, 0))],
        out_specs=qpairs(),
        scratch_shapes=[pltpu.VMEM((tq, t), F32)],
        compiler_params=_cparams(("parallel", "arbitrary")),
    )(q16, qi16, kw32, k16, v16, ki2)


def _attn_sample_kernel(topk, past, q_ref, qi_ref, kw_ref, ck_ref, cv_ref, cki2_ref, k_ref, v_ref, ki2_ref, o_ref,
                        biasc_s, biasn_s):
    npairs, ts = q_ref.shape[0], q_ref.shape[1]

    @pl.when(pl.program_id(1) == 0)
    def _():
        sc, sn = _index_scores(qi_ref[...], kw_ref[...], [cki2_ref[...], ki2_ref[...]])
        qpos = past + lax.broadcasted_iota(I32, (ts, 1), 0)
        kpos_c = lax.broadcasted_iota(I32, sc.shape, 1)
        kpos_n = past + lax.broadcasted_iota(I32, sn.shape, 1)
        keys = [jnp.where((qpos // CHUNK) >= (kpos_c // CHUNK), sc, -jnp.inf),
                jnp.where((qpos // CHUNK) >= (kpos_n // CHUNK), sn, -jnp.inf)]
        _select_topk(keys, topk, [biasc_s, biasn_s])

    lanes = [slice(p * LANES, (p + 1) * LANES) for p in range(npairs)]
    outs = _attend_pairs([q_ref[p] for p in range(npairs)],
                         [[ck_ref[:, cs].astype(BF16), k_ref[p]] for p, cs in enumerate(lanes)],
                         [[cv_ref[:, cs].astype(BF16), v_ref[p]] for p, cs in enumerate(lanes)],
                         [biasc_s[...], biasn_s[...]])
    for p, o in enumerate(outs):
        o_ref[p] = o.astype(o_ref.dtype)


def _attn_sample(q16, qi16, kw32, k16, v16, ki2, cache_k, cache_v, cache_kidx, nb, ts):
    past = cache_k.shape[1]
    cki2 = jnp.concatenate([cache_kidx, cache_kidx], axis=-1).astype(BF16)
    topk = min(TOPK_MAX, (past + ts) // 4)

    def qrow(w):
        return pl.BlockSpec((ts, w), lambda b, p: (b, 0))

    sp = SAMPLE_PAIRS

    def qpair():
        return pl.BlockSpec((sp, ts, LANES), lambda b, p: (p, b, 0))

    def cache(pair):
        if pair:
            return pl.BlockSpec((None, past, sp * LANES), lambda b, p: (b, 0, p))
        return pl.BlockSpec((None, past, LANES), lambda b, p: (b, 0, 0))

    return pl.pallas_call(
        functools.partial(_attn_sample_kernel, topk, past),
        out_shape=jax.ShapeDtypeStruct((N_PAIRS, nb * ts, LANES), BF16),
        grid=(nb, N_PAIRS // sp),
        in_specs=[qpair(), qrow(512), qrow(LANES), cache(True), cache(True), cache(False),
                  qpair(), qpair(), qrow(LANES)],
        out_specs=qpair(),
        scratch_shapes=[pltpu.VMEM((ts, past), F32), pltpu.VMEM((ts, ts), F32)],
        compiler_params=_cparams(("parallel", "arbitrary")),
    )(q16, qi16, kw32, cache_k, cache_v, cki2, k16, v16, ki2)


def _merge_kernel(x_ref, oa_ref, ob_ref, pga_ref, pgb_ref, bga_ref, bgb_ref, g1_ref, sc2_ref, sh2_ref, nw_ref,
                  wpa_ref, wpb_ref, wout_ref, x1_ref, h2_ref):
    ga = _sigmoid(pga_ref[...] + bga_ref[...])
    gb = _sigmoid(pgb_ref[...] + bgb_ref[...])
    ob = jnp.concatenate([ob_ref[p] for p in range(N_PAIRS)], axis=1)
    m = (ga * jnp.dot(oa_ref[...], wpa_ref[...], preferred_element_type=F32)
         + gb * jnp.dot(ob, wpb_ref[...], preferred_element_type=F32))
    x1 = x_ref[...] + g1_ref[...] * jnp.dot(m.astype(BF16), wout_ref[...], preferred_element_type=F32)
    x1_ref[...] = x1
    y = x1 * lax.rsqrt(jnp.mean(x1 * x1, axis=-1, keepdims=True) + EPS) * nw_ref[...]
    h2_ref[...] = (y * (1.0 + sc2_ref[...]) + sh2_ref[...]).astype(BF16)


def _merge(x2, o_a, o_b, P, b_gate, g1, sc2, sh2, nw2, wpa, wpb, wout, seq_len):
    n, d = x2.shape
    tm = _row_tile(n, seq_len, 512)
    g1_a, g1_s = _seq_operand(g1, seq_len, tm)
    sc_a, sc_s = _seq_operand(sc2, seq_len, tm)
    sh_a, sh_s = _seq_operand(sh2, seq_len, tm)

    def row():
        return pl.BlockSpec((tm, d), lambda i: (i, 0))

    def const(shape):
        return pl.BlockSpec(shape, lambda i: (0, 0))

    bg = b_gate.reshape(1, 2 * d)
    return pl.pallas_call(
        _merge_kernel,
        out_shape=(jax.ShapeDtypeStruct((n, d), F32), jax.ShapeDtypeStruct((n, d), BF16)),
        grid=(n // tm,),
        in_specs=[row(), row(), pl.BlockSpec((N_PAIRS, tm, LANES), lambda i: (0, i, 0)),
                  pl.BlockSpec((tm, d), lambda i: (i, C_GA // d)), pl.BlockSpec((tm, d), lambda i: (i, C_GB // d)),
                  pl.BlockSpec((1, d), lambda i: (0, 0)), pl.BlockSpec((1, d), lambda i: (0, 1)),
                  g1_s, sc_s, sh_s, const((1, d)), const((d, d)), const((d, d)), const((d, d))],
        out_specs=(row(), row()),
        compiler_params=_cparams(("parallel",)),
    )(x2, o_a, o_b, P, P, bg, bg, g1_a, sc_a, sh_a, nw2.reshape(1, d), wpa, wpb, wout)


def _top_exact(s, k):
    rows = lax.broadcasted_iota(I32, s.shape, 0).astype(F32)
    cur = s
    rank = jnp.full(s.shape, float(k), F32)
    vals = []
    for r in range(k):
        m = jnp.max(cur, axis=0, keepdims=True)
        first = jnp.min(jnp.where(cur == m, rows, 1e9), axis=0, keepdims=True)
        hit = rows == first
        vals.append(m)
        rank = jnp.where(hit, float(r), rank)
        cur = jnp.where(hit, -jnp.inf, cur)
    return vals, rank


def _top_fast(ss, k):
    curs = list(ss)
    ranks = [jnp.full(s.shape, float(k), F32) for s in ss]
    vals = [[] for _ in ss]
    for r in range(k):
        ms = [jnp.max(c, axis=0, keepdims=True) for c in curs]
        hits = [c == m for c, m in zip(curs, ms)]
        ranks = [jnp.where(h, float(r), rk) for h, rk in zip(hits, ranks)]
        curs = [jnp.where(h, -jnp.inf, c) for h, c in zip(hits, curs)]
        for v, m in zip(vals, ms):
            v.append(m)
    cleans = [jnp.max(jnp.abs(jnp.sum(jnp.where(rk < k, 1.0, 0.0), axis=0, keepdims=True) - k)) == 0.0
              for rk in ranks]
    return vals, ranks, cleans


def _top(ss, k, vals_scr, rank_scr):
    vals, ranks, cleans = _top_fast(ss, k)
    for i, s in enumerate(ss):
        vals_scr[i] = jnp.concatenate(vals[i], axis=0)
        rank_scr[i] = ranks[i]

        @pl.when(jnp.logical_not(cleans[i]))
        def _(i=i, s=s):
            vals_e, rank_e = _top_exact(s, k)
            vals_scr[i] = jnp.concatenate(vals_e, axis=0)
            rank_scr[i] = rank_e


def _peer_sel_kernel(h_ref, wpqt_ref, kbd_ref, g_ref, cnt_ref, r2_ref, p2_ref, s_scr, vals_scr, rank_scr,
                     cand_scr, cvals_scr, crank_scr):
    K = PEER_TOPK
    tm = h_ref.shape[0]
    qt = lax.dot_general(wpqt_ref[...], h_ref[...], NT, preferred_element_type=F32)
    s_scr[...] = jnp.dot(kbd_ref[...], qt.astype(BF16), preferred_element_type=F32
                         ).reshape(2 * PEER_HEADS, PEER_NKEYS, tm)
    sub8 = lax.broadcasted_iota(I32, (8, tm), 0)
    neg = jnp.full((8, tm), -jnp.inf, F32)
    _top([s_scr[r] for r in range(2 * PEER_HEADS)], K, vals_scr, rank_scr)
    for hd in range(PEER_HEADS):
        c1, c2 = vals_scr[2 * hd], vals_scr[2 * hd + 1]
        blocks = [c1[0:1] + c2, c1[1:2] + c2[0:8]]
        for k1 in range(2, 8):
            blocks.append(jnp.where(sub8 < K // (k1 + 1), c1[k1:k1 + 1] + c2[0:8], neg))
        blocks.append(c1[8:16] + c2[0:1])
        cand_scr[hd] = jnp.concatenate(blocks, axis=0)
    _top([cand_scr[hd] for hd in range(PEER_HEADS)], K, cvals_scr, crank_scr)
    for hd in range(PEER_HEADS):
        s1, s2 = s_scr[2 * hd], s_scr[2 * hd + 1]
        c1, c2 = vals_scr[2 * hd], vals_scr[2 * hd + 1]
        rank1, rank2 = rank_scr[2 * hd], rank_scr[2 * hd + 1]
        cand = cand_scr[hd]
        taken = crank_scr[hd] < K
        z = jnp.sum(jnp.where(taken, jnp.exp(cand - (c1[0:1] + c2[0:1])), 0.0), axis=0, keepdims=True)
        tk = jnp.where(taken, 1.0, 0.0)
        per_k1 = [jnp.sum(tk[0:16], axis=0, keepdims=True)]
        per_k1 += [jnp.sum(tk[8 + 8 * k1:16 + 8 * k1], axis=0, keepdims=True) for k1 in range(1, 8)]
        cnt16 = jnp.concatenate(per_k1 + [tk[72:80]], axis=0)
        cnt = jnp.zeros(s1.shape, F32)
        for k1 in range(K):
            cnt = jnp.where(rank1 == float(k1), cnt16[k1:k1 + 1], cnt)
        g_ref[hd] = jnp.where(rank1 < K, jnp.exp(s1 - c1[0:1]) / z, 0.0)
        cnt_ref[hd] = cnt
        p2 = jnp.where(rank2 < K, jnp.exp(s2 - c2[0:1]), 0.0)
        cb = r2_ref.shape[-1]
        for tc in range(tm // cb):
            r2_ref[hd, tc] = rank2[:, tc * cb:(tc + 1) * cb].astype(r2_ref.dtype)
            p2_ref[hd, tc] = p2[:, tc * cb:(tc + 1) * cb].astype(p2_ref.dtype)


def _peer_select(h2, wpqt, kbd):
    n, d = h2.shape
    tm = 256 if n % 256 == 0 else n
    cb = min(LANES, tm)
    big = jax.ShapeDtypeStruct((PEER_HEADS, PEER_NKEYS, n), F32)
    blocked = jax.ShapeDtypeStruct((PEER_HEADS, n // cb, PEER_NKEYS, cb), BF16)

    def blk():
        return pl.BlockSpec((PEER_HEADS, PEER_NKEYS, tm), lambda i: (0, 0, i))

    def blk4():
        return pl.BlockSpec((PEER_HEADS, tm // cb, PEER_NKEYS, cb), lambda i: (0, i, 0, 0))

    return pl.pallas_call(
        _peer_sel_kernel,
        out_shape=(big, big, blocked, blocked),
        grid=(n // tm,),
        in_specs=[pl.BlockSpec((tm, d), lambda i: (i, 0)),
                  pl.BlockSpec((d, d), lambda i: (0, 0)),
                  pl.BlockSpec((2 * d, d), lambda i: (0, 0))],
        out_specs=(blk(), blk(), blk4(), blk4()),
        scratch_shapes=[pltpu.VMEM((2 * PEER_HEADS, PEER_NKEYS, tm), F32),
                        pltpu.VMEM((2 * PEER_HEADS, PEER_TOPK, tm), F32),
                        pltpu.VMEM((2 * PEER_HEADS, PEER_NKEYS, tm), F32),
                        pltpu.VMEM((PEER_HEADS, PEER_CAND, tm), F32),
                        pltpu.VMEM((PEER_HEADS, PEER_TOPK, tm), F32),
                        pltpu.VMEM((PEER_HEADS, PEER_CAND, tm), F32)],
        compiler_params=_cparams(("parallel",)),
    )(h2, wpqt, kbd)


def _gelu_tanh(x):
    return 0.5 * x * (1.0 + jnp.tanh(0.7978845608028654 * (x + 0.044715 * (x * x * x))))


def _peer_main_kernel(ni1, h_ref, x1_ref, g2_ref, u_ref, vt_ref, g_ref, cnt_ref, r2_ref, p2_ref, y_ref, acc, gate_s):
    j = pl.program_id(1)
    tm = h_ref.shape[0]

    @pl.when(j == 0)
    def _():
        acc[...] = jnp.zeros_like(acc)

    cb = r2_ref.shape[-1]
    reps = PEER_NKEYS // 16
    zero = jnp.zeros((PEER_NKEYS, cb), BF16)
    for l in range(ni1):
        for tc in range(tm // cb):
            ts = slice(tc * cb, (tc + 1) * cb)
            w = None
            for hd in range(PEER_HEADS):
                c16 = jnp.broadcast_to(cnt_ref[hd, l:l + 1, ts], (16, cb)).astype(BF16)
                g16 = jnp.broadcast_to(g_ref[hd, l:l + 1, ts], (16, cb)).astype(BF16)
                t = (jnp.where(r2_ref[hd, tc] < jnp.concatenate([c16] * reps, axis=0), p2_ref[hd, tc], zero)
                     * jnp.concatenate([g16] * reps, axis=0))
                w = t if w is None else w + t
            gate_s[tc, l * PEER_NKEYS:(l + 1) * PEER_NKEYS, :] = w

    act = lax.dot_general(u_ref[...], h_ref[...], NT, preferred_element_type=F32)
    gate = jnp.concatenate([gate_s[tc] for tc in range(tm // cb)], axis=1)
    coef = gate * _gelu_tanh(act.astype(BF16))
    acc[...] += jnp.dot(vt_ref[...], coef, preferred_element_type=F32)

    @pl.when(j == pl.num_programs(1) - 1)
    def _():
        y_ref[...] = x1_ref[...] + g2_ref[...] * acc[...].T


def _peer_main(h2, x1, g2, u16, vt16, g, cnt, r2, p2, seq_len):
    n, d = h2.shape
    tm = _row_tile(n, seq_len, 512)
    ni1 = 16
    et = ni1 * PEER_NKEYS
    cb = r2.shape[-1]
    g2_a, g2_s = _seq_operand(g2, seq_len, tm)

    def row():
        return pl.BlockSpec((tm, d), lambda i, j: (i, 0))

    return pl.pallas_call(
        functools.partial(_peer_main_kernel, ni1),
        out_shape=jax.ShapeDtypeStruct((n, d), F32),
        grid=(n // tm, N_EXPERTS // et),
        in_specs=[row(), row(), g2_s,
                  pl.BlockSpec((et, d), lambda i, j: (j, 0)),
                  pl.BlockSpec((d, et), lambda i, j: (0, j)),
                  pl.BlockSpec((PEER_HEADS, ni1, tm), lambda i, j: (0, j, i)),
                  pl.BlockSpec((PEER_HEADS, ni1, tm), lambda i, j: (0, j, i)),
                  pl.BlockSpec((PEER_HEADS, tm // cb, PEER_NKEYS, cb), lambda i, j: (0, i, 0, 0)),
                  pl.BlockSpec((PEER_HEADS, tm // cb, PEER_NKEYS, cb), lambda i, j: (0, i, 0, 0))],
        out_specs=row(),
        scratch_shapes=[pltpu.VMEM((d, tm), F32), pltpu.VMEM((tm // cb, et, cb), BF16)],
        compiler_params=_cparams(("parallel", "arbitrary")),
    )(h2, x1, g2_a, u16, vt16, g, cnt, r2, p2)


def _layer(x, mod, pos, shift_prev, s0, cache, lw):
    nb, t, d = x.shape
    n = nb * t
    sh1, sc1, g1, sh2, sc2, g2 = [mod[:, i * d:(i + 1) * d] for i in range(6)]
    x2 = x.reshape(n, d)
    P = _inproj(x2, sc1, sh1, lw['norm1_w'], lw['w_in16'], t)

    prev = _pack_rw(shift_prev).reshape(nb, 1, P_COLS)
    o_a, zf = _rwkv(P, nb, t, prev, lw['mu'], lw['w0'], lw['a0'], lw['k_k'], lw['k_a'], lw['r_k'], lw['lnx_w'],
                    lw['lnx_b'], lw['wup'], lw['aup'], lw['gup'], _state_to_pairs(s0))
    wkv = _pairs_to_state(zf)
    shift_last = _unpack_rw(P.reshape(nb, t, P_COLS)[:, -1, :])

    q16, k32, k16, v32, v16, qi16, kw32, ki2 = _dsa_prep(P, jnp.tile(pos, nb), lw['q_norm_w'], lw['k_norm_w'])
    if cache is None:
        o_b = _attn_prompt(q16, qi16, kw32, k16, v16, ki2, nb, t)
    else:
        ck, cv, cki = cache
        past = ck.shape[1]
        o_b = _attn_sample(q16, qi16, kw32, k16, v16, ki2, ck.reshape(nb, past, d), cv.reshape(nb, past, d), cki,
                           nb, t)

    x1, h2 = _merge(x2, o_a, o_b, P, lw['b_gate'], g1, sc2, sh2, lw['norm2_w'], lw['wpa'], lw['wpb'], lw['wout'], t)
    g, cnt, r2, p2 = _peer_select(h2, lw['wpqt'], lw['kbd'])
    y = _peer_main(h2, x1, g2, lw['u16'], lw['vt16'], g, cnt, r2, p2, t)

    k_new = k32.reshape(nb, t, N_HEADS, HEAD_DIM)
    v_new = v32.reshape(nb, t, N_HEADS, HEAD_DIM)
    ki_new = kw32[:, :IDX_DIM].reshape(nb, t, IDX_DIM)
    return y.reshape(nb, t, d), wkv, shift_last, k_new, v_new, ki_new


def _layer_weights(l, w_in, b_gate, mu_rw, w0, w_up, a0, a_up, g_up, k_k, k_a, r_k, lnx_w, lnx_b, q_norm_w, k_norm_w,
                   w_proj_a, w_proj_b, w_out, norm1_w, norm2_w, w_pq, peer_keys, peer_u, peer_v):
    d = D_MODEL
    zeros = lambda r: jnp.zeros((r, d), F32)
    keys = peer_keys[l].reshape(2 * PEER_HEADS, PEER_NKEYS, PEER_DHALF)
    eye = jnp.eye(2 * PEER_HEADS, dtype=F32)
    kbd = (eye[:, None, :, None] * keys[:, :, None, :]).reshape(2 * d, d)
    return {
        'w_in16': _pack_in(w_in[l]).astype(BF16), 'b_gate': b_gate[l], 'mu': _pack_rw(mu_rw[l]).reshape(1, P_COLS),
        'w0': w0[l], 'a0': a0[l], 'k_k': k_k[l], 'k_a': k_a[l], 'r_k': r_k[l].reshape(d), 'lnx_w': lnx_w[l],
        'lnx_b': lnx_b[l],
        'wup': jnp.concatenate([w_up[l], zeros(LANES - D_DECAY)], axis=0).astype(BF16),
        'aup': jnp.concatenate([zeros(D_DECAY), a_up[l]], axis=0).astype(BF16),
        'gup': jnp.concatenate([g_up[l], zeros(256 - D_GATE)], axis=0).astype(BF16),
        'q_norm_w': q_norm_w[l], 'k_norm_w': k_norm_w[l], 'norm1_w': norm1_w[l], 'norm2_w': norm2_w[l],
        'wpa': w_proj_a[l].astype(BF16), 'wpb': w_proj_b[l].astype(BF16), 'wout': w_out[l].astype(BF16),
        'wpqt': w_pq[l].T.astype(BF16), 'kbd': kbd.astype(BF16),
        'u16': peer_u[l].astype(BF16), 'vt16': peer_v[l].T.astype(BF16),
    }


def kernel(x_prompt, x_sample, c_prompt, c_sample, cache_k, cache_v, cache_kidx, state_wkv, state_shift, w_ada, b_ada,
           norm1_w, w_in, b_gate, mu_rw, w0, w_up, a0, a_up, g_up, k_k, k_a, r_k, lnx_w, lnx_b, q_norm_w, k_norm_w,
           w_proj_a, w_proj_b, w_out, norm2_w, w_pq, peer_keys, peer_u, peer_v):
    depth = w_in.shape[0]
    bp, tp = x_prompt.shape[:2]
    bs, ts = x_sample.shape[:2]
    past = cache_k.shape[2]
    dt = x_prompt.dtype
    pos_p = jnp.arange(tp, dtype=I32)
    pos_s = past + jnp.arange(ts, dtype=I32)
    zero_shift = jnp.zeros((bp, RW_IN), dt)
    zero_wkv = jnp.zeros((bp, N_HEADS, HEAD_DIM, HEAD_DIM), dt)
    c_all = jnp.concatenate([c_prompt, c_sample], axis=0)
    xp, xs = x_prompt, x_sample
    outs_p, outs_s = [], []
    for l in range(depth):
        lw = _layer_weights(l, w_in, b_gate, mu_rw, w0, w_up, a0, a_up, g_up, k_k, k_a, r_k, lnx_w, lnx_b, q_norm_w,
                            k_norm_w, w_proj_a, w_proj_b, w_out, norm1_w, norm2_w, w_pq, peer_keys, peer_u, peer_v)
        mod = _ada(c_all, w_ada[l], b_ada[l])
        xp, *rest_p = _layer(xp, mod[:bp], pos_p, zero_shift, zero_wkv, None, lw)
        xs, *rest_s = _layer(xs, mod[bp:], pos_s, state_shift[l], state_wkv[l],
                             (cache_k[l], cache_v[l], cache_kidx[l]), lw)
        outs_p.append(rest_p)
        outs_s.append(rest_s)
    stack = lambda outs, i: jnp.stack([o[i] for o in outs])
    return (xp, xs,
            stack(outs_p, 0), stack(outs_p, 1), stack(outs_p, 2), stack(outs_p, 3), stack(outs_p, 4),
            stack(outs_s, 0), stack(outs_s, 1), stack(outs_s, 2), stack(outs_s, 3), stack(outs_s, 4))
```

```python
import functools

import jax
import jax.numpy as jnp
from jax import lax
from jax.experimental import pallas as pl
from jax.experimental.pallas import tpu as pltpu

F32 = jnp.float32
BF16 = jnp.bfloat16
I32 = jnp.int32

LANES = 128
D_MODEL = 1024
EPS = 1e-6
GN_EPS = 64e-5
ROPE_THETA = 10000.0
CHUNK = 64
TOPK_MAX = 256
HEAD_DIM = 64
N_HEADS = D_MODEL // HEAD_DIM
N_PAIRS = N_HEADS // 2
IDX_HEADS = 8
IDX_DIM = 64
D_DECAY = 64
D_AAA = 64
D_GATE = 160
RW_IN = 3 * D_MODEL + D_DECAY + D_AAA + D_GATE
PEER_HEADS = 8
PEER_NKEYS = 128
PEER_TOPK = 16
PEER_DHALF = 64
N_EXPERTS = PEER_NKEYS * PEER_NKEYS
RW_CHUNK = 64
RW_INTERLEAVE = 16
RW_PAIRS_LONG = 2
RW_PASSES = (2, 1, 1, 1, 1)
VMEM_LIMIT = 56 * 1024 * 1024
LOG2E = 1.4426950408889634
PEER_CAND = 80
GATE_ROWS = 32

C_R, C_K, C_V = 0, 1024, 2048
C_Q, C_KD, C_VD = 3072, 4096, 5120
C_GA, C_GB = 6144, 7168
C_QI = 8192
C_G = 8704
C_M = 8960
C_KW = 9088
P_COLS = 9216
IN_W = 9064

NT = (((1,), (1,)), ((), ()))
NN = (((1,), (0,)), ((), ()))


def _pack_in(w):
    z = lambda k: jnp.zeros(w.shape[:-1] + (k,), w.dtype)
    return jnp.concatenate([w[..., 0:3072], w[..., 3360:6432], w[..., 7016:9064], w[..., 6432:6944],
                            w[..., 3200:3360], z(256 - D_GATE), w[..., 3072:3200],
                            w[..., 6944:7016], z(LANES - IDX_DIM - IDX_HEADS)], axis=-1)


def _pack_rw(a):
    return _pack_in(jnp.concatenate([a, jnp.zeros(a.shape[:-1] + (IN_W - RW_IN,), a.dtype)], axis=-1))


def _unpack_rw(p):
    return jnp.concatenate([p[..., :3072], p[..., C_M:C_M + 128], p[..., C_G:C_G + D_GATE]], axis=-1)


def _split_bf16(x, n):
    parts = []
    r = x
    for _ in range(n):
        p = r.astype(BF16)
        parts.append(p)
        r = r - p.astype(F32)
    return parts


def _mm(a, b, pa=1, pb=1, dims=NN):
    aps = _split_bf16(a, pa) if a.dtype != BF16 else [a]
    bps = _split_bf16(b, pb) if b.dtype != BF16 else [b]
    order = max(len(aps), len(bps))
    out = None
    for i, ap in enumerate(aps):
        for j, bp in enumerate(bps):
            if i + j >= order:
                continue
            t = lax.dot_general(ap, bp, dims, preferred_element_type=F32)
            out = t if out is None else out + t
    return out


def _sigmoid(x):
    return 1.0 / (1.0 + jnp.exp(-x))


def _softplus(z):
    return jnp.maximum(z, 0.0) + jnp.log(1.0 + jnp.exp(-jnp.abs(z)))


def _cparams(sem):
    return pltpu.CompilerParams(dimension_semantics=sem, vmem_limit_bytes=VMEM_LIMIT)


def _ada_kernel(c_ref, w_ref, b_ref, o_ref):
    c = c_ref[...]
    s = c * _sigmoid(c)
    o_ref[...] = _mm(s, w_ref[...], 2, 2) + b_ref[...]


def _ada(c, w, b):
    m, d = c.shape
    n = w.shape[1]
    tn = 1024
    return pl.pallas_call(
        _ada_kernel,
        out_shape=jax.ShapeDtypeStruct((m, n), F32),
        grid=(n // tn,),
        in_specs=[pl.BlockSpec((m, d), lambda j: (0, 0)),
                  pl.BlockSpec((d, tn), lambda j: (0, j)),
                  pl.BlockSpec((1, tn), lambda j: (0, j))],
        out_specs=pl.BlockSpec((m, tn), lambda j: (0, j)),
        compiler_params=_cparams(("arbitrary",)),
    )(c, w, b.reshape(1, n))


def _seq_operand(vec, seq_len, tm):
    b, d = vec.shape
    if seq_len % tm == 0:
        per = seq_len // tm
        arr = vec.reshape(b, 1, d)
        spec = pl.BlockSpec((None, 1, d), lambda *g: (g[0] // per, 0, 0))
    else:
        assert tm % seq_len == 0
        arr = jnp.repeat(vec, seq_len, axis=0)
        spec = pl.BlockSpec((tm, d), lambda *g: (g[0], 0))
    return arr, spec


def _row_tile(n, seq_len, cap):
    tm = min(cap, n)
    while n % tm or (seq_len % tm and tm % seq_len):
        tm //= 2
    return tm


def _inproj_kernel(x_ref, sc_ref, sh_ref, nw_ref, w_ref, o_ref, h_scr):
    @pl.when(pl.program_id(1) == 0)
    def _():
        x = x_ref[...]
        y = x * lax.rsqrt(jnp.mean(x * x, axis=-1, keepdims=True) + EPS) * nw_ref[...]
        h_scr[...] = (y * (1.0 + sc_ref[...]) + sh_ref[...]).astype(BF16)

    o_ref[...] = jnp.dot(h_scr[...], w_ref[...], preferred_element_type=F32)


def _inproj(x2, sc, sh, nw, w16, seq_len):
    n, d = x2.shape
    tm = _row_tile(n, seq_len, 1024)
    tn = 1024
    sc_a, sc_s = _seq_operand(sc, seq_len, tm)
    sh_a, sh_s = _seq_operand(sh, seq_len, tm)
    return pl.pallas_call(
        _inproj_kernel,
        out_shape=jax.ShapeDtypeStruct((n, P_COLS), F32),
        grid=(n // tm, P_COLS // tn),
        in_specs=[pl.BlockSpec((tm, d), lambda i, j: (i, 0)), sc_s, sh_s,
                  pl.BlockSpec((1, d), lambda i, j: (0, 0)),
                  pl.BlockSpec((d, tn), lambda i, j: (0, j))],
        out_specs=pl.BlockSpec((tm, tn), lambda i, j: (i, j)),
        scratch_shapes=[pltpu.VMEM((tm, d), BF16)],
        compiler_params=_cparams(("parallel", "arbitrary")),
    )(x2, sc_a, sh_a, nw.reshape(1, d), w16)


def _lane_lo(shape):
    return lax.broadcasted_iota(I32, shape, len(shape) - 1) < HEAD_DIM


def _pair_sum(x):
    lo = _lane_lo(x.shape)
    s0 = jnp.sum(jnp.where(lo, x, 0.0), axis=-1, keepdims=True)
    s1 = jnp.sum(jnp.where(lo, 0.0, x), axis=-1, keepdims=True)
    return jnp.where(lo, s0, s1)


def _stack2(x):
    lo = _lane_lo(x.shape)
    return jnp.concatenate([jnp.where(lo, x, 0.0), jnp.where(lo, 0.0, x)], axis=0)


def _rwkv_kernel(t_real, npair, nchunk, pr, pk, pv, pg, pm, sr, sk, sv, sg, sm, mr, mk, mv, mg, mmu,
                 w0, a0, kkw, kaw, rkw, lnw, lnb, wup, aup, gup, z0, o_ref, zf_ref,
                 r_s, lw_s, k_s, v_s, a_s, b_s, y_s, bonus_s, g_s):
    C = RW_CHUNK
    t_pad = r_s.shape[1]

    def mix(p_ref, s_ref, m_ref):
        p = p_ref[...]
        prev = pltpu.roll(p, 1, 0)
        row = lax.broadcasted_iota(I32, p.shape, 0)
        prev = jnp.where(row == 0, s_ref[...], prev)
        return p + (prev - p) * m_ref[...]

    xg, xm = mix(pg, sg, mg), mix(pm, sm, mmu)
    th16, xm16, sg16 = jnp.tanh(xm).astype(BF16), xm.astype(BF16), _sigmoid(xg).astype(BF16)
    xr_all, xk_all, xv_all = mix(pr, sr, mr), mix(pk, sk, mk), mix(pv, sv, mv)

    def put(ref, pp, val):
        if t_pad > t_real:
            val = jnp.concatenate([val, jnp.zeros((t_pad - t_real, LANES), F32)], axis=0)
        ref[pp] = val

    for pp in range(npair):
        cs = slice(pp * LANES, (pp + 1) * LANES)
        xr, xk, xv = xr_all[:, cs], xk_all[:, cs], xv_all[:, cs]
        dw = jnp.dot(th16, wup[:, cs], preferred_element_type=F32)
        lw = -jnp.exp(-_softplus(-(w0[:, cs] + dw)) - 0.5)
        asig = _sigmoid(a0[:, cs] + jnp.dot(xm16, aup[:, cs], preferred_element_type=F32))
        g_s[pp] = jnp.dot(sg16, gup[:, cs], preferred_element_type=F32)
        kk = xk * kkw[:, cs]
        kk = kk * lax.rsqrt(_pair_sum(kk * kk) + 1e-12)
        kmod = xk * (1.0 + (asig - 1.0) * kaw[:, cs])
        bonus_s[pp] = _pair_sum(xr * kmod * rkw[:, cs]) * xv
        put(r_s, pp, xr)
        put(lw_s, pp, lw)
        put(k_s, pp, kmod)
        put(v_s, pp, xv)
        put(a_s, pp, -kk)
        put(b_s, pp, kk * asig)

    n2 = 2 * C
    ri = lax.broadcasted_iota(I32, (n2, n2), 0)
    ci = lax.broadcasted_iota(I32, (n2, n2), 1)
    same = (ri // C) == (ci // C)
    strict = same & ((ri % C) > (ci % C))
    incl = same & ((ri % C) >= (ci % C))
    eye = ri == ci
    eye_f = jnp.where(eye, 1.0, 0.0)
    tri = jnp.where(lax.broadcasted_iota(I32, (C, C), 0) >= lax.broadcasted_iota(I32, (C, C), 1), 1.0, 0.0
                    ).astype(BF16)
    zeros_sq = jnp.zeros((n2, LANES), F32)

    pc_, pg_, pi_, po_, ps_ = RW_PASSES

    def local(chains):
        each = lambda f, *cols: [f(*xs) for xs in zip(*cols)]
        lwc = [lw_s[pp, sl, :] for sl, pp in chains]
        cum = each(lambda l: _mm(tri, l, 1, pc_), lwc)
        cum_last = each(lambda c: c[C - 1:C, :], cum)
        ec, eci = each(jnp.exp, cum), each(lambda c: jnp.exp(-c), cum)
        ecp = each(lambda c, l: jnp.exp(c - l), cum, lwc)
        ecl = each(lambda c, cl: jnp.exp(cl - c), cum, cum_last)
        a_c = [a_s[pp, sl, :] for sl, pp in chains]
        b_c = [b_s[pp, sl, :] for sl, pp in chains]
        k_c = [k_s[pp, sl, :] for sl, pp in chains]
        r_c = [r_s[pp, sl, :] for sl, pp in chains]
        As = each(lambda a, e: _stack2(a * e), a_c, ecp)
        Rs = each(lambda r, e: _stack2(r * e), r_c, ec)
        Bs = each(lambda b, e: _stack2(b * e), b_c, eci)
        Ks = each(lambda k, e: _stack2(k * e), k_c, eci)
        Bt = each(lambda b, e: _stack2(b * e), b_c, ecl)
        Kt = each(lambda k, e: _stack2(k * e), k_c, ecl)
        Vs = [_stack2(v_s[pp, sl, :]) for sl, pp in chains]

        G = each(lambda a, r, b, k: _mm(jnp.concatenate([a, r], axis=0), jnp.concatenate([b, k], axis=0),
                                        pg_, pg_, NT), As, Rs, Bs, Ks)
        a_ab = each(lambda g: jnp.where(strict, g[:n2, :n2], 0.0), G)
        a_ak = each(lambda g: jnp.where(strict, g[:n2, n2:], 0.0), G)
        a_rb = each(lambda g: jnp.where(incl, g[n2:, :n2], 0.0), G)
        a_rk = each(lambda g: jnp.where(incl, g[n2:, n2:], 0.0), G)

        lp = a_ab
        tm_ = each(lambda a: eye_f + a, a_ab)
        step = 2
        while step < C:
            lp = each(lambda l: _mm(l, l, pi_, pi_), lp)
            tm_ = each(lambda t, l: t + _mm(t, l, pi_, pi_), tm_, lp)
            step *= 2

        w1 = each(lambda a, v: _mm(a, v, po_, po_), a_ak, Vs)
        mu_ = each(lambda t, a, w: _mm(t, jnp.concatenate([a, w], axis=1), po_, po_), tm_, As, w1)
        rhs = each(lambda m, v: jnp.concatenate([m, jnp.concatenate([zeros_sq, v], axis=1)], axis=0), mu_, Vs)
        lhs = each(lambda rb, rk, b, k: jnp.concatenate([jnp.concatenate([rb, rk], axis=1),
                                                         jnp.concatenate([b.T, k.T], axis=1)], axis=0),
                   a_rb, a_rk, Bt, Kt)
        out2 = each(lambda l, r: _mm(l, r, po_, po_), lhs, rhs)
        m23 = each(lambda r, o, cl: jnp.concatenate([r + o[:n2, :LANES],
                                                     jnp.where(eye, jnp.exp(cl), 0.0) + o[n2:, :LANES]], axis=0),
                   Rs, out2, cum_last)
        return [(m, o[:n2, LANES:], o[n2:, LANES:]) for m, o in zip(m23, out2)]

    def step_chunks(i, zs):
        sls = [pl.ds(pl.multiple_of((i * nchunk + j) * C, C), C) for j in range(nchunk)]
        parts = local([(sl, pp) for sl in sls for pp in range(npair)])
        zs = list(zs)
        for j, sl in enumerate(sls):
            for pp in range(npair):
                m23, y_loc, z_loc = parts[j * npair + pp]
                yz = _mm(m23, zs[pp], ps_, ps_)
                y = yz[:n2] + y_loc
                y_s[pp, sl, :] = y[:C] + y[C:]
                zs[pp] = yz[n2:] + z_loc
        return tuple(zs)

    zs = lax.fori_loop(0, t_pad // (C * nchunk), step_chunks, tuple(z0[pp] for pp in range(npair)))
    for pp in range(npair):
        zf_ref[pp] = zs[pp]
        cs = slice(pp * LANES, (pp + 1) * LANES)
        y = y_s[pp, 0:t_real, :]
        mean = _pair_sum(y) * (1.0 / HEAD_DIM)
        dlt = y - mean
        var = _pair_sum(dlt * dlt) * (1.0 / HEAD_DIM)
        yn = dlt * lax.rsqrt(var + GN_EPS) * lnw[:, cs] + lnb[:, cs]
        o_ref[:, cs] = ((yn + bonus_s[pp]) * g_s[pp]).astype(o_ref.dtype)


def _rwkv(P, nb, t, prev, mu, w0, a0, k_k, k_a, r_k, lnx_w, lnx_b, wup, aup, gup, z0):
    t_pad = max(t, RW_CHUNK)
    assert t % 8 == 0 and t_pad % RW_CHUNK == 0
    n_chunks = t_pad // RW_CHUNK
    nchunk = min(RW_INTERLEAVE // RW_PAIRS_LONG, n_chunks)
    npair = min(N_PAIRS, max(1, RW_INTERLEAVE // nchunk))
    wp = npair * LANES

    def cblk(c0, w, per_pair):
        return (lambda p: c0 // w + p) if per_pair else (lambda p: c0 // w)

    def pcol(c0, w, pp):
        f = cblk(c0, w, pp)
        return pl.BlockSpec((t, w), lambda b, p: (b, f(p)))

    def prevcol(c0, w, pp):
        f = cblk(c0, w, pp)
        return pl.BlockSpec((None, 1, w), lambda b, p: (b, 0, f(p)))

    def mucol(c0, w, pp):
        f = cblk(c0, w, pp)
        return pl.BlockSpec((1, w), lambda b, p: (0, f(p)))

    def hvec():
        return pl.BlockSpec((1, wp), lambda b, p: (0, p))

    cols = [(C_R, wp, True), (C_K, wp, True), (C_V, wp, True), (C_G, 256, False), (C_M, LANES, False)]
    in_specs = ([pcol(*c) for c in cols] + [prevcol(*c) for c in cols] + [mucol(*c) for c in cols]
                + [hvec() for _ in range(7)]
                + [pl.BlockSpec((LANES, wp), lambda b, p: (0, p)),
                   pl.BlockSpec((LANES, wp), lambda b, p: (0, p)),
                   pl.BlockSpec((256, wp), lambda b, p: (0, p)),
                   pl.BlockSpec((None, npair, LANES, LANES), lambda b, p: (b, p, 0, 0))])
    vecs = [v.reshape(1, D_MODEL) for v in (w0, a0, k_k, k_a, r_k, lnx_w, lnx_b)]
    o, zf = pl.pallas_call(
        functools.partial(_rwkv_kernel, t, npair, nchunk),
        out_shape=(jax.ShapeDtypeStruct((nb * t, D_MODEL), BF16),
                   jax.ShapeDtypeStruct((nb, N_PAIRS, LANES, LANES), F32)),
        grid=(nb, N_PAIRS // npair),
        in_specs=in_specs,
        out_specs=(pl.BlockSpec((t, wp), lambda b, p: (b, p)),
                   pl.BlockSpec((None, npair, LANES, LANES), lambda b, p: (b, p, 0, 0))),
        scratch_shapes=([pltpu.VMEM((npair, t_pad, LANES), F32) for _ in range(7)]
                        + [pltpu.VMEM((npair, t, LANES), F32) for _ in range(2)]),
        compiler_params=_cparams(("parallel", "arbitrary")),
    )(P, P, P, P, P, prev, prev, prev, prev, prev, mu, mu, mu, mu, mu, *vecs, wup, aup, gup, z0)
    return o, zf


def _state_to_pairs(s):
    nb = s.shape[0]
    zt = jnp.swapaxes(s, -1, -2).reshape(nb, N_PAIRS, 2, HEAD_DIM, HEAD_DIM)
    zero = jnp.zeros_like(zt[:, :, 0])
    top = jnp.concatenate([zt[:, :, 0], zero], axis=-1)
    bot = jnp.concatenate([zero, zt[:, :, 1]], axis=-1)
    return jnp.concatenate([top, bot], axis=-2)


def _pairs_to_state(z):
    nb = z.shape[0]
    h0 = z[:, :, :HEAD_DIM, :HEAD_DIM]
    h1 = z[:, :, HEAD_DIM:, HEAD_DIM:]
    s = jnp.stack([h0, h1], axis=2).reshape(nb, N_HEADS, HEAD_DIM, HEAD_DIM)
    return jnp.swapaxes(s, -1, -2)


def _rope(x, cos, sin_signed):
    w = x.shape[1]
    reps = w // LANES
    cw = jnp.concatenate([cos] * reps, axis=1) if reps > 1 else cos
    sw = jnp.concatenate([sin_signed] * reps, axis=1) if reps > 1 else sin_signed
    lane = lax.broadcasted_iota(I32, x.shape, 1)
    fwd = pltpu.roll(x, w - 32, 1)
    bwd = pltpu.roll(x, 32, 1)
    partner = jnp.where((lane % HEAD_DIM) < 32, fwd, bwd)
    return x * cw + partner * sw


def _head_rms(x, nw, e_dn, e_up):
    ms = _mm(x * x, e_dn, 2, 1) * (1.0 / HEAD_DIM)
    r = lax.rsqrt(ms + EPS)
    return x * _mm(r, e_up, 2, 1) * nw


def _dsa_prep_kernel(pq, pkd, pvd, pqi, pkw, cos_ref, sin_ref, qn, kn, edn, eup,
                     q16, k32, k16, v32, v16, qi16, kw32, ki2):
    cos, sin = cos_ref[...], sin_ref[...]
    e_dn, e_up = edn[...], eup[...]
    def put_pairs(ref, x):
        for p in range(N_PAIRS):
            ref[p] = x[:, p * LANES:(p + 1) * LANES].astype(ref.dtype)

    q = _rope(_head_rms(pq[...], qn[...], e_dn, e_up), cos, sin)
    put_pairs(q16, q * (HEAD_DIM ** -0.5 * LOG2E))
    k = _rope(_head_rms(pkd[...], kn[...], e_dn, e_up), cos, sin)
    k32[...] = k
    put_pairs(k16, k)
    v = pvd[...]
    v32[...] = v
    put_pairs(v16, v)
    qi16[...] = _rope(pqi[...], cos, sin).astype(BF16)
    kw = pkw[...]
    lane = lax.broadcasted_iota(I32, kw.shape, 1)
    wi_scale = (IDX_HEADS * IDX_DIM) ** -0.5
    kr = _rope(kw, cos, sin)
    kw32[...] = jnp.where(lane < IDX_DIM, kr, jnp.where(lane < IDX_DIM + IDX_HEADS, kw * wi_scale, 0.0))
    ki2[...] = jnp.where(lane < IDX_DIM, kr, pltpu.roll(kr, IDX_DIM, 1)).astype(BF16)


def _dsa_prep(P, pos_rows, q_norm_w, k_norm_w):
    n = P.shape[0]
    tm = 512 if n % 512 == 0 else n
    half = HEAD_DIM // 2
    inv = ROPE_THETA ** (-jnp.arange(half, dtype=F32) / half)
    ang = pos_rows.astype(F32)[:, None] * inv[None, :]
    cos = jnp.tile(jnp.cos(ang), (1, 4))
    sin = jnp.sin(ang)
    sin_signed = jnp.tile(jnp.concatenate([-sin, sin], axis=1), (1, 2))
    head_of = jnp.arange(D_MODEL) // HEAD_DIM
    e_dn = (head_of[:, None] == jnp.arange(LANES)[None, :]).astype(BF16)
    e_up = e_dn.T
    qn = jnp.tile(q_norm_w, N_HEADS).reshape(1, D_MODEL)
    kn = jnp.tile(k_norm_w, N_HEADS).reshape(1, D_MODEL)

    def col(c0, w):
        return pl.BlockSpec((tm, w), lambda i, c0=c0, w=w: (i, c0 // w))

    def row(w):
        return pl.BlockSpec((tm, w), lambda i: (i, 0))

    def const(shape):
        return pl.BlockSpec(shape, lambda i: (0, 0))

    pairs = jax.ShapeDtypeStruct((N_PAIRS, n, LANES), BF16)
    pair_spec = pl.BlockSpec((N_PAIRS, tm, LANES), lambda i: (0, i, 0))
    return pl.pallas_call(
        _dsa_prep_kernel,
        out_shape=(pairs, jax.ShapeDtypeStruct((n, D_MODEL), F32), pairs, jax.ShapeDtypeStruct((n, D_MODEL), F32),
                   pairs, jax.ShapeDtypeStruct((n, IDX_HEADS * IDX_DIM), BF16),
                   jax.ShapeDtypeStruct((n, LANES), F32), jax.ShapeDtypeStruct((n, LANES), BF16)),
        grid=(n // tm,),
        in_specs=[col(C_Q, 1024), col(C_KD, 1024), col(C_VD, 1024), col(C_QI, 512), col(C_KW, LANES),
                  row(LANES), row(LANES), const((1, D_MODEL)), const((1, D_MODEL)),
                  const((D_MODEL, LANES)), const((LANES, D_MODEL))],
        out_specs=(pair_spec, row(D_MODEL), pair_spec, row(D_MODEL), pair_spec, row(512), row(LANES), row(LANES)),
        compiler_params=_cparams(("parallel",)),
    )(P, P, P, P, P, cos, sin_signed, qn, kn, e_dn, e_up)


CODE_NEG_INF = -1 - 0x7F800000
ATTN_TQ = 256
ATTN_PAIRS = 2
SAMPLE_PAIRS = 4


def _index_scores(qi, wi, ki_list):
    outs = []
    for ki in ki_list:
        acc = None
        for h in range(IDX_HEADS):
            qpair = qi[:, (h // 2) * LANES:(h // 2 + 1) * LANES]
            lo = _lane_lo(qpair.shape)
            qh = jnp.where(lo if h % 2 == 0 else jnp.logical_not(lo), qpair, jnp.zeros_like(qpair))
            rel = lax.dot_general(qh, ki, NT, preferred_element_type=F32)
            term = wi[:, IDX_DIM + h:IDX_DIM + h + 1] * jnp.maximum(rel, 0.0)
            acc = term if acc is None else acc + term
        outs.append(acc)
    return outs


def _select_topk(keys, topk, bias_refs):
    tq = keys[0].shape[0]
    neg = -jnp.inf

    def write(masks):
        for ref, k, msk in zip(bias_refs, keys, masks):
            ref[:, 0:k.shape[1]] = jnp.where(msk, 0.0, neg)

    def count(pred_list):
        tot = None
        for p in pred_list:
            c = jnp.sum(jnp.where(p, 1.0, 0.0), axis=-1, keepdims=True)
            tot = c if tot is None else tot + c
        return tot

    def threshold(c):
        bits = jnp.where(c >= 0, c, c ^ jnp.int32(0x7FFFFFFF))
        return jnp.where(c < jnp.int32(CODE_NEG_INF), neg, lax.bitcast_convert_type(bits, F32))

    few = count([k > neg for k in keys]) <= topk

    def pending(cnt):
        return jnp.max(jnp.where(few | (cnt == topk), 0.0, 1.0))

    def bit_step(state):
        i, c, cnt, _ = state
        trial = c + jnp.left_shift(jnp.int32(1), 31 - i)
        cnt_t = count([k >= threshold(trial) for k in keys])
        take = cnt_t >= topk
        cnt = jnp.where(take, cnt_t, cnt)
        return i + 1, jnp.where(take, trial, c), cnt, pending(cnt)

    cnt0 = jnp.full((tq, 1), float(sum(k.shape[1] for k in keys)), F32)
    state = (jnp.int32(0), jnp.full((tq, 1), -2 ** 31, I32), cnt0, pending(cnt0))
    _, code, _, _ = lax.while_loop(lambda s: (s[0] < 32) & (s[3] > 0.0), bit_step, state)
    thr = threshold(code)
    ge = [(k >= thr) & (k > neg) for k in keys]
    write(ge)
    surplus = jnp.max(count(ge)) > topk

    @pl.when(surplus)
    def _():
        gt = [k > thr for k in keys]
        need = topk - count(gt)
        ties = [(k == thr) & (k > neg) for k in keys]
        offs, idx = 0, []
        for k in keys:
            idx.append(lax.broadcasted_iota(I32, k.shape, 1) + offs)
            offs += k.shape[1]
        nbits = max(1, (offs - 1).bit_length() + 1)

        def idx_step(i, m):
            trial = m + jnp.left_shift(jnp.int32(1), nbits - 1 - i)
            cnt = count([t & (ix < trial) for t, ix in zip(ties, idx)])
            return jnp.where(cnt <= need, trial, m)

        cut = lax.fori_loop(0, nbits, idx_step, jnp.zeros((tq, 1), I32))
        write([g | (t & (ix < cut)) for g, t, ix in zip(gt, ties, idx)])


def _attend_pairs(q_pairs, k_lists, v_lists, bias_list):
    lo = _lane_lo(q_pairs[0].shape)
    zero = jnp.zeros_like(q_pairs[0])
    heads = []
    for pi, q in enumerate(q_pairs):
        heads += [(jnp.where(lo, q, zero), pi), (jnp.where(lo, zero, q), pi)]
    s = [[lax.dot_general(qh, k, NT, preferred_element_type=F32) + b for k, b in zip(k_lists[pi], bias_list)]
         for qh, pi in heads]
    m = []
    for sh in s:
        mh = None
        for sj in sh:
            mx = jnp.max(sj, axis=-1, keepdims=True)
            mh = mx if mh is None else jnp.maximum(mh, mx)
        m.append(mh)
    p = [[jnp.exp2(sj - mh) for sj in sh] for sh, mh in zip(s, m)]
    den = [functools.reduce(lambda a, b: a + b, [jnp.sum(pj, axis=-1, keepdims=True) for pj in ph]) for ph in p]
    acc = [functools.reduce(lambda a, b: a + b,
                            [jnp.dot(pj.astype(BF16), v, preferred_element_type=F32)
                             for pj, v in zip(ph, v_lists[pi])])
           for ph, (_, pi) in zip(p, heads)]
    outs = [a / d for a, d in zip(acc, den)]
    return [jnp.where(lo, outs[2 * i], outs[2 * i + 1]) for i in range(len(q_pairs))]


def _attn_prompt_kernel(topk, ncase, q_ref, qi_ref, kw_ref, k_ref, v_ref, ki2_ref, o_ref, bias_s):
    tq = q_ref.shape[1]
    t = k_ref.shape[1]
    i = pl.program_id(1)
    lstep = t // ncase
    case = ((i + 1) * tq - 1) // lstep

    def run(L):
        score = _index_scores(qi_ref[...], kw_ref[...], [ki2_ref[0:L, :]])[0]
        qpos = i * tq + lax.broadcasted_iota(I32, (tq, L), 0)
        kpos = lax.broadcasted_iota(I32, (tq, L), 1)
        adm = (qpos // CHUNK) >= (kpos // CHUNK)
        _select_topk([jnp.where(adm, score, -jnp.inf)], topk, [bias_s])

        def pairs(g, carry):
            ps = [ATTN_PAIRS * g + j for j in range(ATTN_PAIRS)]
            outs = _attend_pairs([q_ref[p] for p in ps], [[k_ref[p, 0:L, :]] for p in ps],
                                 [[v_ref[p, 0:L, :]] for p in ps], [bias_s[:, 0:L]])
            for p, o in zip(ps, outs):
                o_ref[p] = o.astype(o_ref.dtype)
            return carry

        lax.fori_loop(0, N_PAIRS // ATTN_PAIRS, pairs, 0)

    for c in range(ncase):
        pl.when(case == c)(functools.partial(run, (c + 1) * lstep))


def _attn_prompt(q16, qi16, kw32, k16, v16, ki2, nb, t):
    tq = min(ATTN_TQ, t)
    topk = min(TOPK_MAX, t // 4)
    nq = t // tq
    ncase = min(4, nq)

    def qrow(w):
        return pl.BlockSpec((tq, w), lambda b, i: (b * nq + i, 0))

    def qpairs():
        return pl.BlockSpec((N_PAIRS, tq, LANES), lambda b, i: (0, b * nq + i, 0))

    def kpairs():
        return pl.BlockSpec((N_PAIRS, t, LANES), lambda b, i: (0, b, 0))

    return pl.pallas_call(
        functools.partial(_attn_prompt_kernel, topk, ncase),
        out_shape=jax.ShapeDtypeStruct((N_PAIRS, nb * t, LANES), BF16),
        grid=(nb, nq),
        in_specs=[qpairs(), qrow(512), qrow(LANES), kpairs(), kpairs(),
                  pl.BlockSpec((t, LANES), lambda b, i: (b, 0))],
        out_specs=qpairs(),
        scratch_shapes=[pltpu.VMEM((tq, t), F32)],
        compiler_params=_cparams(("parallel", "arbitrary")),
    )(q16, qi16, kw32, k16, v16, ki2)


def _attn_sample_kernel(topk, past, q_ref, qi_ref, kw_ref, ck_ref, cv_ref, cki2_ref, k_ref, v_ref, ki2_ref, o_ref,
                        biasc_s, biasn_s):
    npairs, ts = q_ref.shape[0], q_ref.shape[1]

    @pl.when(pl.program_id(1) == 0)
    def _():
        sc, sn = _index_scores(qi_ref[...], kw_ref[...], [cki2_ref[...], ki2_ref[...]])
        qpos = past + lax.broadcasted_iota(I32, (ts, 1), 0)
        kpos_c = lax.broadcasted_iota(I32, sc.shape, 1)
        kpos_n = past + lax.broadcasted_iota(I32, sn.shape, 1)
        keys = [jnp.where((qpos // CHUNK) >= (kpos_c // CHUNK), sc, -jnp.inf),
                jnp.where((qpos // CHUNK) >= (kpos_n // CHUNK), sn, -jnp.inf)]
        _select_topk(keys, topk, [biasc_s, biasn_s])

    lanes = [slice(p * LANES, (p + 1) * LANES) for p in range(npairs)]
    outs = _attend_pairs([q_ref[p] for p in range(npairs)],
                         [[ck_ref[:, cs].astype(BF16), k_ref[p]] for p, cs in enumerate(lanes)],
                         [[cv_ref[:, cs].astype(BF16), v_ref[p]] for p, cs in enumerate(lanes)],
                         [biasc_s[...], biasn_s[...]])
    for p, o in enumerate(outs):
        o_ref[p] = o.astype(o_ref.dtype)


def _attn_sample(q16, qi16, kw32, k16, v16, ki2, cache_k, cache_v, cache_kidx, nb, ts):
    past = cache_k.shape[1]
    cki2 = jnp.concatenate([cache_kidx, cache_kidx], axis=-1).astype(BF16)
    topk = min(TOPK_MAX, (past + ts) // 4)

    def qrow(w):
        return pl.BlockSpec((ts, w), lambda b, p: (b, 0))

    sp = SAMPLE_PAIRS

    def qpair():
        return pl.BlockSpec((sp, ts, LANES), lambda b, p: (p, b, 0))

    def cache(pair):
        if pair:
            return pl.BlockSpec((None, past, sp * LANES), lambda b, p: (b, 0, p))
        return pl.BlockSpec((None, past, LANES), lambda b, p: (b, 0, 0))

    return pl.pallas_call(
        functools.partial(_attn_sample_kernel, topk, past),
        out_shape=jax.ShapeDtypeStruct((N_PAIRS, nb * ts, LANES), BF16),
        grid=(nb, N_PAIRS // sp),
        in_specs=[qpair(), qrow(512), qrow(LANES), cache(True), cache(True), cache(False),
                  qpair(), qpair(), qrow(LANES)],
        out_specs=qpair(),
        scratch_shapes=[pltpu.VMEM((ts, past), F32), pltpu.VMEM((ts, ts), F32)],
        compiler_params=_cparams(("parallel", "arbitrary")),
    )(q16, qi16, kw32, cache_k, cache_v, cki2, k16, v16, ki2)


def _merge_kernel(x_ref, oa_ref, ob_ref, pga_ref, pgb_ref, bga_ref, bgb_ref, g1_ref, sc2_ref, sh2_ref, nw_ref,
                  wpa_ref, wpb_ref, wout_ref, x1_ref, h2_ref):
    ga = _sigmoid(pga_ref[...] + bga_ref[...])
    gb = _sigmoid(pgb_ref[...] + bgb_ref[...])
    ob = jnp.concatenate([ob_ref[p] for p in range(N_PAIRS)], axis=1)
    m = (ga * jnp.dot(oa_ref[...], wpa_ref[...], preferred_element_type=F32)
         + gb * jnp.dot(ob, wpb_ref[...], preferred_element_type=F32))
    x1 = x_ref[...] + g1_ref[...] * jnp.dot(m.astype(BF16), wout_ref[...], preferred_element_type=F32)
    x1_ref[...] = x1
    y = x1 * lax.rsqrt(jnp.mean(x1 * x1, axis=-1, keepdims=True) + EPS) * nw_ref[...]
    h2_ref[...] = (y * (1.0 + sc2_ref[...]) + sh2_ref[...]).astype(BF16)


def _merge(x2, o_a, o_b, P, b_gate, g1, sc2, sh2, nw2, wpa, wpb, wout, seq_len):
    n, d = x2.shape
    tm = _row_tile(n, seq_len, 512)
    g1_a, g1_s = _seq_operand(g1, seq_len, tm)
    sc_a, sc_s = _seq_operand(sc2, seq_len, tm)
    sh_a, sh_s = _seq_operand(sh2, seq_len, tm)

    def row():
        return pl.BlockSpec((tm, d), lambda i: (i, 0))

    def const(shape):
        return pl.BlockSpec(shape, lambda i: (0, 0))

    bg = b_gate.reshape(1, 2 * d)
    return pl.pallas_call(
        _merge_kernel,
        out_shape=(jax.ShapeDtypeStruct((n, d), F32), jax.ShapeDtypeStruct((n, d), BF16)),
        grid=(n // tm,),
        in_specs=[row(), row(), pl.BlockSpec((N_PAIRS, tm, LANES), lambda i: (0, i, 0)),
                  pl.BlockSpec((tm, d), lambda i: (i, C_GA // d)), pl.BlockSpec((tm, d), lambda i: (i, C_GB // d)),
                  pl.BlockSpec((1, d), lambda i: (0, 0)), pl.BlockSpec((1, d), lambda i: (0, 1)),
                  g1_s, sc_s, sh_s, const((1, d)), const((d, d)), const((d, d)), const((d, d))],
        out_specs=(row(), row()),
        compiler_params=_cparams(("parallel",)),
    )(x2, o_a, o_b, P, P, bg, bg, g1_a, sc_a, sh_a, nw2.reshape(1, d), wpa, wpb, wout)


def _top_exact(s, k):
    rows = lax.broadcasted_iota(I32, s.shape, 0).astype(F32)
    cur = s
    rank = jnp.full(s.shape, float(k), F32)
    vals = []
    for r in range(k):
        m = jnp.max(cur, axis=0, keepdims=True)
        first = jnp.min(jnp.where(cur == m, rows, 1e9), axis=0, keepdims=True)
        hit = rows == first
        vals.append(m)
        rank = jnp.where(hit, float(r), rank)
        cur = jnp.where(hit, -jnp.inf, cur)
    return vals, rank


def _top_fast(ss, k):
    curs = list(ss)
    ranks = [jnp.full(s.shape, float(k), F32) for s in ss]
    vals = [[] for _ in ss]
    for r in range(k):
        ms = [jnp.max(c, axis=0, keepdims=True) for c in curs]
        hits = [c == m for c, m in zip(curs, ms)]
        ranks = [jnp.where(h, float(r), rk) for h, rk in zip(hits, ranks)]
        curs = [jnp.where(h, -jnp.inf, c) for h, c in zip(hits, curs)]
        for v, m in zip(vals, ms):
            v.append(m)
    cleans = [jnp.max(jnp.abs(jnp.sum(jnp.where(rk < k, 1.0, 0.0), axis=0, keepdims=True) - k)) == 0.0
              for rk in ranks]
    return vals, ranks, cleans


def _top(ss, k, vals_scr, rank_scr):
    vals, ranks, cleans = _top_fast(ss, k)
    for i, s in enumerate(ss):
        vals_scr[i] = jnp.concatenate(vals[i], axis=0)
        rank_scr[i] = ranks[i]

        @pl.when(jnp.logical_not(cleans[i]))
        def _(i=i, s=s):
            vals_e, rank_e = _top_exact(s, k)
            vals_scr[i] = jnp.concatenate(vals_e, axis=0)
            rank_scr[i] = rank_e


def _peer_sel_kernel(h_ref, wpqt_ref, kbd_ref, g_ref, cnt_ref, r2_ref, p2_ref, s_scr, vals_scr, rank_scr,
                     cand_scr, cvals_scr, crank_scr):
    K = PEER_TOPK
    tm = h_ref.shape[0]
    qt = lax.dot_general(wpqt_ref[...], h_ref[...], NT, preferred_element_type=F32)
    s_scr[...] = jnp.dot(kbd_ref[...], qt.astype(BF16), preferred_element_type=F32
                         ).reshape(2 * PEER_HEADS, PEER_NKEYS, tm)
    sub8 = lax.broadcasted_iota(I32, (8, tm), 0)
    neg = jnp.full((8, tm), -jnp.inf, F32)
    _top([s_scr[r] for r in range(2 * PEER_HEADS)], K, vals_scr, rank_scr)
    for hd in range(PEER_HEADS):
        c1, c2 = vals_scr[2 * hd], vals_scr[2 * hd + 1]
        blocks = [c1[0:1] + c2, c1[1:2] + c2[0:8]]
        for k1 in range(2, 8):
            blocks.append(jnp.where(sub8 < K // (k1 + 1), c1[k1:k1 + 1] + c2[0:8], neg))
        blocks.append(c1[8:16] + c2[0:1])
        cand_scr[hd] = jnp.concatenate(blocks, axis=0)
    _top([cand_scr[hd] for hd in range(PEER_HEADS)], K, cvals_scr, crank_scr)
    for hd in range(PEER_HEADS):
        s1, s2 = s_scr[2 * hd], s_scr[2 * hd + 1]
        c1, c2 = vals_scr[2 * hd], vals_scr[2 * hd + 1]
        rank1, rank2 = rank_scr[2 * hd], rank_scr[2 * hd + 1]
        cand = cand_scr[hd]
        taken = crank_scr[hd] < K
        z = jnp.sum(jnp.where(taken, jnp.exp(cand - (c1[0:1] + c2[0:1])), 0.0), axis=0, keepdims=True)
        tk = jnp.where(taken, 1.0, 0.0)
        per_k1 = [jnp.sum(tk[0:16], axis=0, keepdims=True)]
        per_k1 += [jnp.sum(tk[8 + 8 * k1:16 + 8 * k1], axis=0, keepdims=True) for k1 in range(1, 8)]
        cnt16 = jnp.concatenate(per_k1 + [tk[72:80]], axis=0)
        cnt = jnp.zeros(s1.shape, F32)
        for k1 in range(K):
            cnt = jnp.where(rank1 == float(k1), cnt16[k1:k1 + 1], cnt)
        g_ref[hd] = jnp.where(rank1 < K, jnp.exp(s1 - c1[0:1]) / z, 0.0)
        cnt_ref[hd] = cnt
        p2 = jnp.where(rank2 < K, jnp.exp(s2 - c2[0:1]), 0.0)
        cb = r2_ref.shape[-1]
        for tc in range(tm // cb):
            r2_ref[hd, tc] = rank2[:, tc * cb:(tc + 1) * cb].astype(r2_ref.dtype)
            p2_ref[hd, tc] = p2[:, tc * cb:(tc + 1) * cb].astype(p2_ref.dtype)


def _peer_select(h2, wpqt, kbd):
    n, d = h2.shape
    tm = 256 if n % 256 == 0 else n
    cb = min(LANES, tm)
    big = jax.ShapeDtypeStruct((PEER_HEADS, PEER_NKEYS, n), F32)
    blocked = jax.ShapeDtypeStruct((PEER_HEADS, n // cb, PEER_NKEYS, cb), BF16)

    def blk():
        return pl.BlockSpec((PEER_HEADS, PEER_NKEYS, tm), lambda i: (0, 0, i))

    def blk4():
        return pl.BlockSpec((PEER_HEADS, tm // cb, PEER_NKEYS, cb), lambda i: (0, i, 0, 0))

    return pl.pallas_call(
        _peer_sel_kernel,
        out_shape=(big, big, blocked, blocked),
        grid=(n // tm,),
        in_specs=[pl.BlockSpec((tm, d), lambda i: (i, 0)),
                  pl.BlockSpec((d, d), lambda i: (0, 0)),
                  pl.BlockSpec((2 * d, d), lambda i: (0, 0))],
        out_specs=(blk(), blk(), blk4(), blk4()),
        scratch_shapes=[pltpu.VMEM((2 * PEER_HEADS, PEER_NKEYS, tm), F32),
                        pltpu.VMEM((2 * PEER_HEADS, PEER_TOPK, tm), F32),
                        pltpu.VMEM((2 * PEER_HEADS, PEER_NKEYS, tm), F32),
                        pltpu.VMEM((PEER_HEADS, PEER_CAND, tm), F32),
                        pltpu.VMEM((PEER_HEADS, PEER_TOPK, tm), F32),
                        pltpu.VMEM((PEER_HEADS, PEER_CAND, tm), F32)],
        compiler_params=_cparams(("parallel",)),
    )(h2, wpqt, kbd)


def _gelu_tanh(x):
    return 0.5 * x * (1.0 + jnp.tanh(0.7978845608028654 * (x + 0.044715 * (x * x * x))))


def _peer_main_kernel(ni1, h_ref, x1_ref, g2_ref, u_ref, vt_ref, g_ref, cnt_ref, r2_ref, p2_ref, y_ref, acc, gate_s):
    j = pl.program_id(1)
    tm = h_ref.shape[0]

    @pl.when(j == 0)
    def _():
        acc[...] = jnp.zeros_like(acc)

    cb = r2_ref.shape[-1]
    reps = PEER_NKEYS // 16
    zero = jnp.zeros((PEER_NKEYS, cb), BF16)
    for l in range(ni1):
        for tc in range(tm // cb):
            ts = slice(tc * cb, (tc + 1) * cb)
            w = None
            for hd in range(PEER_HEADS):
                c16 = jnp.broadcast_to(cnt_ref[hd, l:l + 1, ts], (16, cb)).astype(BF16)
                g16 = jnp.broadcast_to(g_ref[hd, l:l + 1, ts], (16, cb)).astype(BF16)
                t = (jnp.where(r2_ref[hd, tc] < jnp.concatenate([c16] * reps, axis=0), p2_ref[hd, tc], zero)
                     * jnp.concatenate([g16] * reps, axis=0))
                w = t if w is None else w + t
            gate_s[tc, l * PEER_NKEYS:(l + 1) * PEER_NKEYS, :] = w

    act = lax.dot_general(u_ref[...], h_ref[...], NT, preferred_element_type=F32)
    gate = jnp.concatenate([gate_s[tc] for tc in range(tm // cb)], axis=1)
    coef = gate * _gelu_tanh(act.astype(BF16))
    acc[...] += jnp.dot(vt_ref[...], coef, preferred_element_type=F32)

    @pl.when(j == pl.num_programs(1) - 1)
    def _():
        y_ref[...] = x1_ref[...] + g2_ref[...] * acc[...].T


def _peer_main(h2, x1, g2, u16, vt16, g, cnt, r2, p2, seq_len):
    n, d = h2.shape
    tm = _row_tile(n, seq_len, 512)
    ni1 = 16
    et = ni1 * PEER_NKEYS
    cb = r2.shape[-1]
    g2_a, g2_s = _seq_operand(g2, seq_len, tm)

    def row():
        return pl.BlockSpec((tm, d), lambda i, j: (i, 0))

    return pl.pallas_call(
        functools.partial(_peer_main_kernel, ni1),
        out_shape=jax.ShapeDtypeStruct((n, d), F32),
        grid=(n // tm, N_EXPERTS // et),
        in_specs=[row(), row(), g2_s,
                  pl.BlockSpec((et, d), lambda i, j: (j, 0)),
                  pl.BlockSpec((d, et), lambda i, j: (0, j)),
                  pl.BlockSpec((PEER_HEADS, ni1, tm), lambda i, j: (0, j, i)),
                  pl.BlockSpec((PEER_HEADS, ni1, tm), lambda i, j: (0, j, i)),
                  pl.BlockSpec((PEER_HEADS, tm // cb, PEER_NKEYS, cb), lambda i, j: (0, i, 0, 0)),
                  pl.BlockSpec((PEER_HEADS, tm // cb, PEER_NKEYS, cb), lambda i, j: (0, i, 0, 0))],
        out_specs=row(),
        scratch_shapes=[pltpu.VMEM((d, tm), F32), pltpu.VMEM((tm // cb, et, cb), BF16)],
        compiler_params=_cparams(("parallel", "arbitrary")),
    )(h2, x1, g2_a, u16, vt16, g, cnt, r2, p2)


def _layer(x, mod, pos, shift_prev, s0, cache, lw):
    nb, t, d = x.shape
    n = nb * t
    sh1, sc1, g1, sh2, sc2, g2 = [mod[:, i * d:(i + 1) * d] for i in range(6)]
    x2 = x.reshape(n, d)
    P = _inproj(x2, sc1, sh1, lw['norm1_w'], lw['w_in16'], t)

    prev = _pack_rw(shift_prev).reshape(nb, 1, P_COLS)
    o_a, zf = _rwkv(P, nb, t, prev, lw['mu'], lw['w0'], lw['a0'], lw['k_k'], lw['k_a'], lw['r_k'], lw['lnx_w'],
                    lw['lnx_b'], lw['wup'], lw['aup'], lw['gup'], _state_to_pairs(s0))
    wkv = _pairs_to_state(zf)
    shift_last = _unpack_rw(P.reshape(nb, t, P_COLS)[:, -1, :])

    q16, k32, k16, v32, v16, qi16, kw32, ki2 = _dsa_prep(P, jnp.tile(pos, nb), lw['q_norm_w'], lw['k_norm_w'])
    if cache is None:
        o_b = _attn_prompt(q16, qi16, kw32, k16, v16, ki2, nb, t)
    else:
        ck, cv, cki = cache
        past = ck.shape[1]
        o_b = _attn_sample(q16, qi16, kw32, k16, v16, ki2, ck.reshape(nb, past, d), cv.reshape(nb, past, d), cki,
                           nb, t)

    x1, h2 = _merge(x2, o_a, o_b, P, lw['b_gate'], g1, sc2, sh2, lw['norm2_w'], lw['wpa'], lw['wpb'], lw['wout'], t)
    g, cnt, r2, p2 = _peer_select(h2, lw['wpqt'], lw['kbd'])
    y = _peer_main(h2, x1, g2, lw['u16'], lw['vt16'], g, cnt, r2, p2, t)

    k_new = k32.reshape(nb, t, N_HEADS, HEAD_DIM)
    v_new = v32.reshape(nb, t, N_HEADS, HEAD_DIM)
    ki_new = kw32[:, :IDX_DIM].reshape(nb, t, IDX_DIM)
    return y.reshape(nb, t, d), wkv, shift_last, k_new, v_new, ki_new


def _layer_weights(l, w_in, b_gate, mu_rw, w0, w_up, a0, a_up, g_up, k_k, k_a, r_k, lnx_w, lnx_b, q_norm_w, k_norm_w,
                   w_proj_a, w_proj_b, w_out, norm1_w, norm2_w, w_pq, peer_keys, peer_u, peer_v):
    d = D_MODEL
    zeros = lambda r: jnp.zeros((r, d), F32)
    keys = peer_keys[l].reshape(2 * PEER_HEADS, PEER_NKEYS, PEER_DHALF)
    eye = jnp.eye(2 * PEER_HEADS, dtype=F32)
    kbd = (eye[:, None, :, None] * keys[:, :, None, :]).reshape(2 * d, d)
    return {
        'w_in16': _pack_in(w_in[l]).astype(BF16), 'b_gate': b_gate[l], 'mu': _pack_rw(mu_rw[l]).reshape(1, P_COLS),
        'w0': w0[l], 'a0': a0[l], 'k_k': k_k[l], 'k_a': k_a[l], 'r_k': r_k[l].reshape(d), 'lnx_w': lnx_w[l],
        'lnx_b': lnx_b[l],
        'wup': jnp.concatenate([w_up[l], zeros(LANES - D_DECAY)], axis=0).astype(BF16),
        'aup': jnp.concatenate([zeros(D_DECAY), a_up[l]], axis=0).astype(BF16),
        'gup': jnp.concatenate([g_up[l], zeros(256 - D_GATE)], axis=0).astype(BF16),
        'q_norm_w': q_norm_w[l], 'k_norm_w': k_norm_w[l], 'norm1_w': norm1_w[l], 'norm2_w': norm2_w[l],
        'wpa': w_proj_a[l].astype(BF16), 'wpb': w_proj_b[l].astype(BF16), 'wout': w_out[l].astype(BF16),
        'wpqt': w_pq[l].T.astype(BF16), 'kbd': kbd.astype(BF16),
        'u16': peer_u[l].astype(BF16), 'vt16': peer_v[l].T.astype(BF16),
    }


def kernel(x_prompt, x_sample, c_prompt, c_sample, cache_k, cache_v, cache_kidx, state_wkv, state_shift, w_ada, b_ada,
           norm1_w, w_in, b_gate, mu_rw, w0, w_up, a0, a_up, g_up, k_k, k_a, r_k, lnx_w, lnx_b, q_norm_w, k_norm_w,
           w_proj_a, w_proj_b, w_out, norm2_w, w_pq, peer_keys, peer_u, peer_v):
    depth = w_in.shape[0]
    bp, tp = x_prompt.shape[:2]
    bs, ts = x_sample.shape[:2]
    past = cache_k.shape[2]
    dt = x_prompt.dtype
    pos_p = jnp.arange(tp, dtype=I32)
    pos_s = past + jnp.arange(ts, dtype=I32)
    zero_shift = jnp.zeros((bp, RW_IN), dt)
    zero_wkv = jnp.zeros((bp, N_HEADS, HEAD_DIM, HEAD_DIM), dt)
    c_all = jnp.concatenate([c_prompt, c_sample], axis=0)
    xp, xs = x_prompt, x_sample
    outs_p, outs_s = [], []
    for l in range(depth):
        lw = _layer_weights(l, w_in, b_gate, mu_rw, w0, w_up, a0, a_up, g_up, k_k, k_a, r_k, lnx_w, lnx_b, q_norm_w,
                            k_norm_w, w_proj_a, w_proj_b, w_out, norm1_w, norm2_w, w_pq, peer_keys, peer_u, peer_v)
        mod = _ada(c_all, w_ada[l], b_ada[l])
        xp, *rest_p = _layer(xp, mod[:bp], pos_p, zero_shift, zero_wkv, None, lw)
        xs, *rest_s = _layer(xs, mod[bp:], pos_s, state_shift[l], state_wkv[l],
                             (cache_k[l], cache_v[l], cache_kidx[l]), lw)
        outs_p.append(rest_p)
        outs_s.append(rest_s)
    stack = lambda outs, i: jnp.stack([o[i] for o in outs])
    return (xp, xs,
            stack(outs_p, 0), stack(outs_p, 1), stack(outs_p, 2), stack(outs_p, 3), stack(outs_p, 4),
            stack(outs_s, 0), stack(outs_s, 1), stack(outs_s, 2), stack(outs_s, 3), stack(outs_s, 4))
```

```python
import functools

import jax
import jax.numpy as jnp
from jax import lax
from jax.experimental import pallas as pl
from jax.experimental.pallas import tpu as pltpu

F32 = jnp.float32
BF16 = jnp.bfloat16
I32 = jnp.int32

LANES = 128
D_MODEL = 1024
EPS = 1e-6
GN_EPS = 64e-5
ROPE_THETA = 10000.0
CHUNK = 64
TOPK_MAX = 256
HEAD_DIM = 64
N_HEADS = D_MODEL // HEAD_DIM
N_PAIRS = N_HEADS // 2
IDX_HEADS = 8
IDX_DIM = 64
D_DECAY = 64
D_AAA = 64
D_GATE = 160
RW_IN = 3 * D_MODEL + D_DECAY + D_AAA + D_GATE
PEER_HEADS = 8
PEER_NKEYS = 128
PEER_TOPK = 16
PEER_DHALF = 64
N_EXPERTS = PEER_NKEYS * PEER_NKEYS
RW_CHUNK = 64
RW_INTERLEAVE = 16
RW_PAIRS_LONG = 2
RW_PASSES = (2, 1, 1, 1, 1)
VMEM_LIMIT = 56 * 1024 * 1024
LOG2E = 1.4426950408889634
PEER_CAND = 80
GATE_ROWS = 32

C_R, C_K, C_V = 0, 1024, 2048
C_Q, C_KD, C_VD = 3072, 4096, 5120
C_GA, C_GB = 6144, 7168
C_QI = 8192
C_G = 8704
C_M = 8960
C_KW = 9088
P_COLS = 9216
IN_W = 9064

NT = (((1,), (1,)), ((), ()))
NN = (((1,), (0,)), ((), ()))


def _pack_in(w):
    z = lambda k: jnp.zeros(w.shape[:-1] + (k,), w.dtype)
    return jnp.concatenate([w[..., 0:3072], w[..., 3360:6432], w[..., 7016:9064], w[..., 6432:6944],
                            w[..., 3200:3360], z(256 - D_GATE), w[..., 3072:3200],
                            w[..., 6944:7016], z(LANES - IDX_DIM - IDX_HEADS)], axis=-1)


def _pack_rw(a):
    return _pack_in(jnp.concatenate([a, jnp.zeros(a.shape[:-1] + (IN_W - RW_IN,), a.dtype)], axis=-1))


def _unpack_rw(p):
    return jnp.concatenate([p[..., :3072], p[..., C_M:C_M + 128], p[..., C_G:C_G + D_GATE]], axis=-1)


def _split_bf16(x, n):
    parts = []
    r = x
    for _ in range(n):
        p = r.astype(BF16)
        parts.append(p)
        r = r - p.astype(F32)
    return parts


def _mm(a, b, pa=1, pb=1, dims=NN):
    aps = _split_bf16(a, pa) if a.dtype != BF16 else [a]
    bps = _split_bf16(b, pb) if b.dtype != BF16 else [b]
    order = max(len(aps), len(bps))
    out = None
    for i, ap in enumerate(aps):
        for j, bp in enumerate(bps):
            if i + j >= order:
                continue
            t = lax.dot_general(ap, bp, dims, preferred_element_type=F32)
            out = t if out is None else out + t
    return out


def _sigmoid(x):
    return 1.0 / (1.0 + jnp.exp(-x))


def _softplus(z):
    return jnp.maximum(z, 0.0) + jnp.log(1.0 + jnp.exp(-jnp.abs(z)))


def _cparams(sem):
    return pltpu.CompilerParams(dimension_semantics=sem, vmem_limit_bytes=VMEM_LIMIT)


def _ada_kernel(c_ref, w_ref, b_ref, o_ref):
    c = c_ref[...]
    s = c * _sigmoid(c)
    o_ref[...] = _mm(s, w_ref[...], 2, 2) + b_ref[...]


def _ada(c, w, b):
    m, d = c.shape
    n = w.shape[1]
    tn = 1024
    return pl.pallas_call(
        _ada_kernel,
        out_shape=jax.ShapeDtypeStruct((m, n), F32),
        grid=(n // tn,),
        in_specs=[pl.BlockSpec((m, d), lambda j: (0, 0)),
                  pl.BlockSpec((d, tn), lambda j: (0, j)),
                  pl.BlockSpec((1, tn), lambda j: (0, j))],
        out_specs=pl.BlockSpec((m, tn), lambda j: (0, j)),
        compiler_params=_cparams(("arbitrary",)),
    )(c, w, b.reshape(1, n))


def _seq_operand(vec, seq_len, tm):
    b, d = vec.shape
    if seq_len % tm == 0:
        per = seq_len // tm
        arr = vec.reshape(b, 1, d)
        spec = pl.BlockSpec((None, 1, d), lambda *g: (g[0] // per, 0, 0))
    else:
        assert tm % seq_len == 0
        arr = jnp.repeat(vec, seq_len, axis=0)
        spec = pl.BlockSpec((tm, d), lambda *g: (g[0], 0))
    return arr, spec


def _row_tile(n, seq_len, cap):
    tm = min(cap, n)
    while n % tm or (seq_len % tm and tm % seq_len):
        tm //= 2
    return tm


def _inproj_kernel(x_ref, sc_ref, sh_ref, nw_ref, w_ref, o_ref, h_scr):
    @pl.when(pl.program_id(1) == 0)
    def _():
        x = x_ref[...]
        y = x * lax.rsqrt(jnp.mean(x * x, axis=-1, keepdims=True) + EPS) * nw_ref[...]
        h_scr[...] = (y * (1.0 + sc_ref[...]) + sh_ref[...]).astype(BF16)

    o_ref[...] = jnp.dot(h_scr[...], w_ref[...], preferred_element_type=F32)


def _inproj(x2, sc, sh, nw, w16, seq_len):
    n, d = x2.shape
    tm = _row_tile(n, seq_len, 1024)
    tn = 1024
    sc_a, sc_s = _seq_operand(sc, seq_len, tm)
    sh_a, sh_s = _seq_operand(sh, seq_len, tm)
    return pl.pallas_call(
        _inproj_kernel,
        out_shape=jax.ShapeDtypeStruct((n, P_COLS), F32),
        grid=(n // tm, P_COLS // tn),
        in_specs=[pl.BlockSpec((tm, d), lambda i, j: (i, 0)), sc_s, sh_s,
                  pl.BlockSpec((1, d), lambda i, j: (0, 0)),
                  pl.BlockSpec((d, tn), lambda i, j: (0, j))],
        out_specs=pl.BlockSpec((tm, tn), lambda i, j: (i, j)),
        scratch_shapes=[pltpu.VMEM((tm, d), BF16)],
        compiler_params=_cparams(("parallel", "arbitrary")),
    )(x2, sc_a, sh_a, nw.reshape(1, d), w16)


def _lane_lo(shape):
    return lax.broadcasted_iota(I32, shape, len(shape) - 1) < HEAD_DIM


def _pair_sum(x):
    lo = _lane_lo(x.shape)
    s0 = jnp.sum(jnp.where(lo, x, 0.0), axis=-1, keepdims=True)
    s1 = jnp.sum(jnp.where(lo, 0.0, x), axis=-1, keepdims=True)
    return jnp.where(lo, s0, s1)


def _stack2(x):
    lo = _lane_lo(x.shape)
    return jnp.concatenate([jnp.where(lo, x, 0.0), jnp.where(lo, 0.0, x)], axis=0)


def _rwkv_kernel(t_real, npair, nchunk, pr, pk, pv, pg, pm, sr, sk, sv, sg, sm, mr, mk, mv, mg, mmu,
                 w0, a0, kkw, kaw, rkw, lnw, lnb, wup, aup, gup, z0, o_ref, zf_ref,
                 r_s, lw_s, k_s, v_s, a_s, b_s, y_s, bonus_s, g_s):
    C = RW_CHUNK
    t_pad = r_s.shape[1]

    def mix(p_ref, s_ref, m_ref):
        p = p_ref[...]
        prev = pltpu.roll(p, 1, 0)
        row = lax.broadcasted_iota(I32, p.shape, 0)
        prev = jnp.where(row == 0, s_ref[...], prev)
        return p + (prev - p) * m_ref[...]

    xg, xm = mix(pg, sg, mg), mix(pm, sm, mmu)
    th16, xm16, sg16 = jnp.tanh(xm).astype(BF16), xm.astype(BF16), _sigmoid(xg).astype(BF16)
    xr_all, xk_all, xv_all = mix(pr, sr, mr), mix(pk, sk, mk), mix(pv, sv, mv)

    def put(ref, pp, val):
        if t_pad > t_real:
            val = jnp.concatenate([val, jnp.zeros((t_pad - t_real, LANES), F32)], axis=0)
        ref[pp] = val

    for pp in range(npair):
        cs = slice(pp * LANES, (pp + 1) * LANES)
        xr, xk, xv = xr_all[:, cs], xk_all[:, cs], xv_all[:, cs]
        dw = jnp.dot(th16, wup[:, cs], preferred_element_type=F32)
        lw = -jnp.exp(-_softplus(-(w0[:, cs] + dw)) - 0.5)
        asig = _sigmoid(a0[:, cs] + jnp.dot(xm16, aup[:, cs], preferred_element_type=F32))
        g_s[pp] = jnp.dot(sg16, gup[:, cs], preferred_element_type=F32)
        kk = xk * kkw[:, cs]
        kk = kk * lax.rsqrt(_pair_sum(kk * kk) + 1e-12)
        kmod = xk * (1.0 + (asig - 1.0) * kaw[:, cs])
        bonus_s[pp] = _pair_sum(xr * kmod * rkw[:, cs]) * xv
        put(r_s, pp, xr)
        put(lw_s, pp, lw)
        put(k_s, pp, kmod)
        put(v_s, pp, xv)
        put(a_s, pp, -kk)
        put(b_s, pp, kk * asig)

    n2 = 2 * C
    ri = lax.broadcasted_iota(I32, (n2, n2), 0)
    ci = lax.broadcasted_iota(I32, (n2, n2), 1)
    same = (ri // C) == (ci // C)
    strict = same & ((ri % C) > (ci % C))
    incl = same & ((ri % C) >= (ci % C))
    eye = ri == ci
    eye_f = jnp.where(eye, 1.0, 0.0)
    tri = jnp.where(lax.broadcasted_iota(I32, (C, C), 0) >= lax.broadcasted_iota(I32, (C, C), 1), 1.0, 0.0
                    ).astype(BF16)
    zeros_sq = jnp.zeros((n2, LANES), F32)

    pc_, pg_, pi_, po_, ps_ = RW_PASSES

    def local(chains):
        each = lambda f, *cols: [f(*xs) for xs in zip(*cols)]
        lwc = [lw_s[pp, sl, :] for sl, pp in chains]
        cum = each(lambda l: _mm(tri, l, 1, pc_), lwc)
        cum_last = each(lambda c: c[C - 1:C, :], cum)
        ec, eci = each(jnp.exp, cum), each(lambda c: jnp.exp(-c), cum)
        ecp = each(lambda c, l: jnp.exp(c - l), cum, lwc)
        ecl = each(lambda c, cl: jnp.exp(cl - c), cum, cum_last)
        a_c = [a_s[pp, sl, :] for sl, pp in chains]
        b_c = [b_s[pp, sl, :] for sl, pp in chains]
        k_c = [k_s[pp, sl, :] for sl, pp in chains]
        r_c = [r_s[pp, sl, :] for sl, pp in chains]
        As = each(lambda a, e: _stack2(a * e), a_c, ecp)
        Rs = each(lambda r, e: _stack2(r * e), r_c, ec)
        Bs = each(lambda b, e: _stack2(b * e), b_c, eci)
        Ks = each(lambda k, e: _stack2(k * e), k_c, eci)
        Bt = each(lambda b, e: _stack2(b * e), b_c, ecl)
        Kt = each(lambda k, e: _stack2(k * e), k_c, ecl)
        Vs = [_stack2(v_s[pp, sl, :]) for sl, pp in chains]

        G = each(lambda a, r, b, k: _mm(jnp.concatenate([a, r], axis=0), jnp.concatenate([b, k], axis=0),
                                        pg_, pg_, NT), As, Rs, Bs, Ks)
        a_ab = each(lambda g: jnp.where(strict, g[:n2, :n2], 0.0), G)
        a_ak = each(lambda g: jnp.where(strict, g[:n2, n2:], 0.0), G)
        a_rb = each(lambda g: jnp.where(incl, g[n2:, :n2], 0.0), G)
        a_rk = each(lambda g: jnp.where(incl, g[n2:, n2:], 0.0), G)

        lp = a_ab
        tm_ = each(lambda a: eye_f + a, a_ab)
        step = 2
        while step < C:
            lp = each(lambda l: _mm(l, l, pi_, pi_), lp)
            tm_ = each(lambda t, l: t + _mm(t, l, pi_, pi_), tm_, lp)
            step *= 2

        w1 = each(lambda a, v: _mm(a, v, po_, po_), a_ak, Vs)
        mu_ = each(lambda t, a, w: _mm(t, jnp.concatenate([a, w], axis=1), po_, po_), tm_, As, w1)
        rhs = each(lambda m, v: jnp.concatenate([m, jnp.concatenate([zeros_sq, v], axis=1)], axis=0), mu_, Vs)
        lhs = each(lambda rb, rk, b, k: jnp.concatenate([jnp.concatenate([rb, rk], axis=1),
                                                         jnp.concatenate([b.T, k.T], axis=1)], axis=0),
                   a_rb, a_rk, Bt, Kt)
        out2 = each(lambda l, r: _mm(l, r, po_, po_), lhs, rhs)
        m23 = each(lambda r, o, cl: jnp.concatenate([r + o[:n2, :LANES],
                                                     jnp.where(eye, jnp.exp(cl), 0.0) + o[n2:, :LANES]], axis=0),
                   Rs, out2, cum_last)
        return [(m, o[:n2, LANES:], o[n2:, LANES:]) for m, o in zip(m23, out2)]

    def step_chunks(i, zs):
        sls = [pl.ds(pl.multiple_of((i * nchunk + j) * C, C), C) for j in range(nchunk)]
        parts = local([(sl, pp) for sl in sls for pp in range(npair)])
        zs = list(zs)
        for j, sl in enumerate(sls):
            for pp in range(npair):
                m23, y_loc, z_loc = parts[j * npair + pp]
                yz = _mm(m23, zs[pp], ps_, ps_)
                y = yz[:n2] + y_loc
                y_s[pp, sl, :] = y[:C] + y[C:]
                zs[pp] = yz[n2:] + z_loc
        return tuple(zs)

    zs = lax.fori_loop(0, t_pad // (C * nchunk), step_chunks, tuple(z0[pp] for pp in range(npair)))
    for pp in range(npair):
        zf_ref[pp] = zs[pp]
        cs = slice(pp * LANES, (pp + 1) * LANES)
        y = y_s[pp, 0:t_real, :]
        mean = _pair_sum(y) * (1.0 / HEAD_DIM)
        dlt = y - mean
        var = _pair_sum(dlt * dlt) * (1.0 / HEAD_DIM)
        yn = dlt * lax.rsqrt(var + GN_EPS) * lnw[:, cs] + lnb[:, cs]
        o_ref[:, cs] = ((yn + bonus_s[pp]) * g_s[pp]).astype(o_ref.dtype)


def _rwkv(P, nb, t, prev, mu, w0, a0, k_k, k_a, r_k, lnx_w, lnx_b, wup, aup, gup, z0):
    t_pad = max(t, RW_CHUNK)
    assert t % 8 == 0 and t_pad % RW_CHUNK == 0
    n_chunks = t_pad // RW_CHUNK
    nchunk = min(RW_INTERLEAVE // RW_PAIRS_LONG, n_chunks)
    npair = min(N_PAIRS, max(1, RW_INTERLEAVE // nchunk))
    wp = npair * LANES

    def cblk(c0, w, per_pair):
        return (lambda p: c0 // w + p) if per_pair else (lambda p: c0 // w)

    def pcol(c0, w, pp):
        f = cblk(c0, w, pp)
        return pl.BlockSpec((t, w), lambda b, p: (b, f(p)))

    def prevcol(c0, w, pp):
        f = cblk(c0, w, pp)
        return pl.BlockSpec((None, 1, w), lambda b, p: (b, 0, f(p)))

    def mucol(c0, w, pp):
        f = cblk(c0, w, pp)
        return pl.BlockSpec((1, w), lambda b, p: (0, f(p)))

    def hvec():
        return pl.BlockSpec((1, wp), lambda b, p: (0, p))

    cols = [(C_R, wp, True), (C_K, wp, True), (C_V, wp, True), (C_G, 256, False), (C_M, LANES, False)]
    in_specs = ([pcol(*c) for c in cols] + [prevcol(*c) for c in cols] + [mucol(*c) for c in cols]
                + [hvec() for _ in range(7)]
                + [pl.BlockSpec((LANES, wp), lambda b, p: (0, p)),
                   pl.BlockSpec((LANES, wp), lambda b, p: (0, p)),
                   pl.BlockSpec((256, wp), lambda b, p: (0, p)),
                   pl.BlockSpec((None, npair, LANES, LANES), lambda b, p: (b, p, 0, 0))])
    vecs = [v.reshape(1, D_MODEL) for v in (w0, a0, k_k, k_a, r_k, lnx_w, lnx_b)]
    o, zf = pl.pallas_call(
        functools.partial(_rwkv_kernel, t, npair, nchunk),
        out_shape=(jax.ShapeDtypeStruct((nb * t, D_MODEL), BF16),
                   jax.ShapeDtypeStruct((nb, N_PAIRS, LANES, LANES), F32)),
        grid=(nb, N_PAIRS // npair),
        in_specs=in_specs,
        out_specs=(pl.BlockSpec((t, wp), lambda b, p: (b, p)),
                   pl.BlockSpec((None, npair, LANES, LANES), lambda b, p: (b, p, 0, 0))),
        scratch_shapes=([pltpu.VMEM((npair, t_pad, LANES), F32) for _ in range(7)]
                        + [pltpu.VMEM((npair, t, LANES), F32) for _ in range(2)]),
        compiler_params=_cparams(("parallel", "arbitrary")),
    )(P, P, P, P, P, prev, prev, prev, prev, prev, mu, mu, mu, mu, mu, *vecs, wup, aup, gup, z0)
    return o, zf


def _state_to_pairs(s):
    nb = s.shape[0]
    zt = jnp.swapaxes(s, -1, -2).reshape(nb, N_PAIRS, 2, HEAD_DIM, HEAD_DIM)
    zero = jnp.zeros_like(zt[:, :, 0])
    top = jnp.concatenate([zt[:, :, 0], zero], axis=-1)
    bot = jnp.concatenate([zero, zt[:, :, 1]], axis=-1)
    return jnp.concatenate([top, bot], axis=-2)


def _pairs_to_state(z):
    nb = z.shape[0]
    h0 = z[:, :, :HEAD_DIM, :HEAD_DIM]
    h1 = z[:, :, HEAD_DIM:, HEAD_DIM:]
    s = jnp.stack([h0, h1], axis=2).reshape(nb, N_HEADS, HEAD_DIM, HEAD_DIM)
    return jnp.swapaxes(s, -1, -2)


def _rope(x, cos, sin_signed):
    w = x.shape[1]
    reps = w // LANES
    cw = jnp.concatenate([cos] * reps, axis=1) if reps > 1 else cos
    sw = jnp.concatenate([sin_signed] * reps, axis=1) if reps > 1 else sin_signed
    lane = lax.broadcasted_iota(I32, x.shape, 1)
    fwd = pltpu.roll(x, w - 32, 1)
    bwd = pltpu.roll(x, 32, 1)
    partner = jnp.where((lane % HEAD_DIM) < 32, fwd, bwd)
    return x * cw + partner * sw


def _head_rms(x, nw, e_dn, e_up):
    ms = _mm(x * x, e_dn, 2, 1) * (1.0 / HEAD_DIM)
    r = lax.rsqrt(ms + EPS)
    return x * _mm(r, e_up, 2, 1) * nw


def _dsa_prep_kernel(pq, pkd, pvd, pqi, pkw, cos_ref, sin_ref, qn, kn, edn, eup,
                     q16, k32, k16, v32, v16, qi16, kw32, ki2):
    cos, sin = cos_ref[...], sin_ref[...]
    e_dn, e_up = edn[...], eup[...]
    def put_pairs(ref, x):
        for p in range(N_PAIRS):
            ref[p] = x[:, p * LANES:(p + 1) * LANES].astype(ref.dtype)

    q = _rope(_head_rms(pq[...], qn[...], e_dn, e_up), cos, sin)
    put_pairs(q16, q * (HEAD_DIM ** -0.5 * LOG2E))
    k = _rope(_head_rms(pkd[...], kn[...], e_dn, e_up), cos, sin)
    k32[...] = k
    put_pairs(k16, k)
    v = pvd[...]
    v32[...] = v
    put_pairs(v16, v)
    qi16[...] = _rope(pqi[...], cos, sin).astype(BF16)
    kw = pkw[...]
    lane = lax.broadcasted_iota(I32, kw.shape, 1)
    wi_scale = (IDX_HEADS * IDX_DIM) ** -0.5
    kr = _rope(kw, cos, sin)
    kw32[...] = jnp.where(lane < IDX_DIM, kr, jnp.where(lane < IDX_DIM + IDX_HEADS, kw * wi_scale, 0.0))
    ki2[...] = jnp.where(lane < IDX_DIM, kr, pltpu.roll(kr, IDX_DIM, 1)).astype(BF16)


def _dsa_prep(P, pos_rows, q_norm_w, k_norm_w):
    n = P.shape[0]
    tm = 512 if n % 512 == 0 else n
    half = HEAD_DIM // 2
    inv = ROPE_THETA ** (-jnp.arange(half, dtype=F32) / half)
    ang = pos_rows.astype(F32)[:, None] * inv[None, :]
    cos = jnp.tile(jnp.cos(ang), (1, 4))
    sin = jnp.sin(ang)
    sin_signed = jnp.tile(jnp.concatenate([-sin, sin], axis=1), (1, 2))
    head_of = jnp.arange(D_MODEL) // HEAD_DIM
    e_dn = (head_of[:, None] == jnp.arange(LANES)[None, :]).astype(BF16)
    e_up = e_dn.T
    qn = jnp.tile(q_norm_w, N_HEADS).reshape(1, D_MODEL)
    kn = jnp.tile(k_norm_w, N_HEADS).reshape(1, D_MODEL)

    def col(c0, w):
        return pl.BlockSpec((tm, w), lambda i, c0=c0, w=w: (i, c0 // w))

    def row(w):
        return pl.BlockSpec((tm, w), lambda i: (i, 0))

    def const(shape):
        return pl.BlockSpec(shape, lambda i: (0, 0))

    pairs = jax.ShapeDtypeStruct((N_PAIRS, n, LANES), BF16)
    pair_spec = pl.BlockSpec((N_PAIRS, tm, LANES), lambda i: (0, i, 0))
    return pl.pallas_call(
        _dsa_prep_kernel,
        out_shape=(pairs, jax.ShapeDtypeStruct((n, D_MODEL), F32), pairs, jax.ShapeDtypeStruct((n, D_MODEL), F32),
                   pairs, jax.ShapeDtypeStruct((n, IDX_HEADS * IDX_DIM), BF16),
                   jax.ShapeDtypeStruct((n, LANES), F32), jax.ShapeDtypeStruct((n, LANES), BF16)),
        grid=(n // tm,),
        in_specs=[col(C_Q, 1024), col(C_KD, 1024), col(C_VD, 1024), col(C_QI, 512), col(C_KW, LANES),
                  row(LANES), row(LANES), const((1, D_MODEL)), const((1, D_MODEL)),
                  const((D_MODEL, LANES)), const((LANES, D_MODEL))],
        out_specs=(pair_spec, row(D_MODEL), pair_spec, row(D_MODEL), pair_spec, row(512), row(LANES), row(LANES)),
        compiler_params=_cparams(("parallel",)),
    )(P, P, P, P, P, cos, sin_signed, qn, kn, e_dn, e_up)


CODE_NEG_INF = -1 - 0x7F800000
ATTN_TQ = 256
ATTN_PAIRS = 2
ATTN_CASES = 8
SAMPLE_PAIRS = 4


def _index_scores(qi, wi, ki_list):
    outs = []
    for ki in ki_list:
        acc = None
        for h in range(IDX_HEADS):
            qpair = qi[:, (h // 2) * LANES:(h // 2 + 1) * LANES]
            lo = _lane_lo(qpair.shape)
            qh = jnp.where(lo if h % 2 == 0 else jnp.logical_not(lo), qpair, jnp.zeros_like(qpair))
            rel = lax.dot_general(qh, ki, NT, preferred_element_type=F32)
            term = wi[:, IDX_DIM + h:IDX_DIM + h + 1] * jnp.maximum(rel, 0.0)
            acc = term if acc is None else acc + term
        outs.append(acc)
    return outs


def _select_topk(keys, topk, bias_refs):
    tq = keys[0].shape[0]
    neg = -jnp.inf

    def write(masks):
        for ref, k, msk in zip(bias_refs, keys, masks):
            ref[:, 0:k.shape[1]] = jnp.where(msk, 0.0, neg)

    def count(pred_list):
        tot = None
        for p in pred_list:
            c = jnp.sum(jnp.where(p, 1.0, 0.0), axis=-1, keepdims=True)
            tot = c if tot is None else tot + c
        return tot

    def threshold(c):
        bits = jnp.where(c >= 0, c, c ^ jnp.int32(0x7FFFFFFF))
        return jnp.where(c < jnp.int32(CODE_NEG_INF), neg, lax.bitcast_convert_type(bits, F32))

    few = count([k > neg for k in keys]) <= topk

    def pending(cnt):
        return jnp.max(jnp.where(few | (cnt == topk), 0.0, 1.0))

    def bit_step(state):
        i, c, cnt, _ = state
        trial = c + jnp.left_shift(jnp.int32(1), 31 - i)
        cnt_t = count([k >= threshold(trial) for k in keys])
        take = cnt_t >= topk
        cnt = jnp.where(take, cnt_t, cnt)
        return i + 1, jnp.where(take, trial, c), cnt, pending(cnt)

    cnt0 = jnp.full((tq, 1), float(sum(k.shape[1] for k in keys)), F32)
    state = (jnp.int32(0), jnp.full((tq, 1), -2 ** 31, I32), cnt0, pending(cnt0))
    _, code, _, _ = lax.while_loop(lambda s: (s[0] < 32) & (s[3] > 0.0), bit_step, state)
    thr = threshold(code)
    ge = [(k >= thr) & (k > neg) for k in keys]
    write(ge)
    surplus = jnp.max(count(ge)) > topk

    @pl.when(surplus)
    def _():
        gt = [k > thr for k in keys]
        need = topk - count(gt)
        ties = [(k == thr) & (k > neg) for k in keys]
        offs, idx = 0, []
        for k in keys:
            idx.append(lax.broadcasted_iota(I32, k.shape, 1) + offs)
            offs += k.shape[1]
        nbits = max(1, (offs - 1).bit_length() + 1)

        def idx_step(i, m):
            trial = m + jnp.left_shift(jnp.int32(1), nbits - 1 - i)
            cnt = count([t & (ix < trial) for t, ix in zip(ties, idx)])
            return jnp.where(cnt <= need, trial, m)

        cut = lax.fori_loop(0, nbits, idx_step, jnp.zeros((tq, 1), I32))
        write([g | (t & (ix < cut)) for g, t, ix in zip(gt, ties, idx)])


def _attend_pairs(q_pairs, k_lists, v_lists, bias_list):
    lo = _lane_lo(q_pairs[0].shape)
    zero = jnp.zeros_like(q_pairs[0])
    heads = []
    for pi, q in enumerate(q_pairs):
        heads += [(jnp.where(lo, q, zero), pi), (jnp.where(lo, zero, q), pi)]
    s = [[lax.dot_general(qh, k, NT, preferred_element_type=F32) + b for k, b in zip(k_lists[pi], bias_list)]
         for qh, pi in heads]
    m = []
    for sh in s:
        mh = None
        for sj in sh:
            mx = jnp.max(sj, axis=-1, keepdims=True)
            mh = mx if mh is None else jnp.maximum(mh, mx)
        m.append(mh)
    p = [[jnp.exp2(sj - mh) for sj in sh] for sh, mh in zip(s, m)]
    den = [functools.reduce(lambda a, b: a + b, [jnp.sum(pj, axis=-1, keepdims=True) for pj in ph]) for ph in p]
    acc = [functools.reduce(lambda a, b: a + b,
                            [jnp.dot(pj.astype(BF16), v, preferred_element_type=F32)
                             for pj, v in zip(ph, v_lists[pi])])
           for ph, (_, pi) in zip(p, heads)]
    outs = [a / d for a, d in zip(acc, den)]
    return [jnp.where(lo, outs[2 * i], outs[2 * i + 1]) for i in range(len(q_pairs))]


def _attn_prompt_kernel(topk, ncase, q_ref, qi_ref, kw_ref, k_ref, v_ref, ki2_ref, o_ref, bias_s):
    tq = q_ref.shape[1]
    t = k_ref.shape[1]
    i = pl.program_id(1)
    lstep = t // ncase
    case = ((i + 1) * tq - 1) // lstep

    def run(L):
        score = _index_scores(qi_ref[...], kw_ref[...], [ki2_ref[0:L, :]])[0]
        qpos = i * tq + lax.broadcasted_iota(I32, (tq, L), 0)
        kpos = lax.broadcasted_iota(I32, (tq, L), 1)
        adm = (qpos // CHUNK) >= (kpos // CHUNK)
        _select_topk([jnp.where(adm, score, -jnp.inf)], topk, [bias_s])

        def pairs(g, carry):
            ps = [ATTN_PAIRS * g + j for j in range(ATTN_PAIRS)]
            outs = _attend_pairs([q_ref[p] for p in ps], [[k_ref[p, 0:L, :]] for p in ps],
                                 [[v_ref[p, 0:L, :]] for p in ps], [bias_s[:, 0:L]])
            for p, o in zip(ps, outs):
                o_ref[p] = o.astype(o_ref.dtype)
            return carry

        lax.fori_loop(0, N_PAIRS // ATTN_PAIRS, pairs, 0)

    for c in range(ncase):
        pl.when(case == c)(functools.partial(run, (c + 1) * lstep))


def _attn_prompt(q16, qi16, kw32, k16, v16, ki2, nb, t):
    tq = min(ATTN_TQ, t)
    topk = min(TOPK_MAX, t // 4)
    nq = t // tq
    ncase = min(ATTN_CASES, nq)

    def qrow(w):
        return pl.BlockSpec((tq, w), lambda b, i: (b * nq + i, 0))

    def qpairs():
        return pl.BlockSpec((N_PAIRS, tq, LANES), lambda b, i: (0, b * nq + i, 0))

    def kpairs():
        return pl.BlockSpec((N_PAIRS, t, LANES), lambda b, i: (0, b, 0))

    return pl.pallas_call(
        functools.partial(_attn_prompt_kernel, topk, ncase),
        out_shape=jax.ShapeDtypeStruct((N_PAIRS, nb * t, LANES), BF16),
        grid=(nb, nq),
        in_specs=[qpairs(), qrow(512), qrow(LANES), kpairs(), kpairs(),
                  pl.BlockSpec((t, LANES), lambda b, i: (b, 0))],
        out_specs=qpairs(),
        scratch_shapes=[pltpu.VMEM((tq, t), F32)],
        compiler_params=_cparams(("parallel", "arbitrary")),
    )(q16, qi16, kw32, k16, v16, ki2)


def _attn_sample_kernel(topk, past, q_ref, qi_ref, kw_ref, ck_ref, cv_ref, cki2_ref, k_ref, v_ref, ki2_ref, o_ref,
                        biasc_s, biasn_s):
    npairs, ts = q_ref.shape[0], q_ref.shape[1]

    @pl.when(pl.program_id(1) == 0)
    def _():
        sc, sn = _index_scores(qi_ref[...], kw_ref[...], [cki2_ref[...], ki2_ref[...]])
        qpos = past + lax.broadcasted_iota(I32, (ts, 1), 0)
        kpos_c = lax.broadcasted_iota(I32, sc.shape, 1)
        kpos_n = past + lax.broadcasted_iota(I32, sn.shape, 1)
        keys = [jnp.where((qpos // CHUNK) >= (kpos_c // CHUNK), sc, -jnp.inf),
                jnp.where((qpos // CHUNK) >= (kpos_n // CHUNK), sn, -jnp.inf)]
        _select_topk(keys, topk, [biasc_s, biasn_s])

    lanes = [slice(p * LANES, (p + 1) * LANES) for p in range(npairs)]
    outs = _attend_pairs([q_ref[p] for p in range(npairs)],
                         [[ck_ref[:, cs].astype(BF16), k_ref[p]] for p, cs in enumerate(lanes)],
                         [[cv_ref[:, cs].astype(BF16), v_ref[p]] for p, cs in enumerate(lanes)],
                         [biasc_s[...], biasn_s[...]])
    for p, o in enumerate(outs):
        o_ref[p] = o.astype(o_ref.dtype)


def _attn_sample(q16, qi16, kw32, k16, v16, ki2, cache_k, cache_v, cache_kidx, nb, ts):
    past = cache_k.shape[1]
    cki2 = jnp.concatenate([cache_kidx, cache_kidx], axis=-1).astype(BF16)
    topk = min(TOPK_MAX, (past + ts) // 4)

    def qrow(w):
        return pl.BlockSpec((ts, w), lambda b, p: (b, 0))

    sp = SAMPLE_PAIRS

    def qpair():
        return pl.BlockSpec((sp, ts, LANES), lambda b, p: (p, b, 0))

    def cache(pair):
        if pair:
            return pl.BlockSpec((None, past, sp * LANES), lambda b, p: (b, 0, p))
        return pl.BlockSpec((None, past, LANES), lambda b, p: (b, 0, 0))

    return pl.pallas_call(
        functools.partial(_attn_sample_kernel, topk, past),
        out_shape=jax.ShapeDtypeStruct((N_PAIRS, nb * ts, LANES), BF16),
        grid=(nb, N_PAIRS // sp),
        in_specs=[qpair(), qrow(512), qrow(LANES), cache(True), cache(True), cache(False),
                  qpair(), qpair(), qrow(LANES)],
        out_specs=qpair(),
        scratch_shapes=[pltpu.VMEM((ts, past), F32), pltpu.VMEM((ts, ts), F32)],
        compiler_params=_cparams(("parallel", "arbitrary")),
    )(q16, qi16, kw32, cache_k, cache_v, cki2, k16, v16, ki2)


def _merge_kernel(x_ref, oa_ref, ob_ref, pga_ref, pgb_ref, bga_ref, bgb_ref, g1_ref, sc2_ref, sh2_ref, nw_ref,
                  wpa_ref, wpb_ref, wout_ref, x1_ref, h2_ref):
    ga = _sigmoid(pga_ref[...] + bga_ref[...])
    gb = _sigmoid(pgb_ref[...] + bgb_ref[...])
    ob = jnp.concatenate([ob_ref[p] for p in range(N_PAIRS)], axis=1)
    m = (ga * jnp.dot(oa_ref[...], wpa_ref[...], preferred_element_type=F32)
         + gb * jnp.dot(ob, wpb_ref[...], preferred_element_type=F32))
    x1 = x_ref[...] + g1_ref[...] * jnp.dot(m.astype(BF16), wout_ref[...], preferred_element_type=F32)
    x1_ref[...] = x1
    y = x1 * lax.rsqrt(jnp.mean(x1 * x1, axis=-1, keepdims=True) + EPS) * nw_ref[...]
    h2_ref[...] = (y * (1.0 + sc2_ref[...]) + sh2_ref[...]).astype(BF16)


def _merge(x2, o_a, o_b, P, b_gate, g1, sc2, sh2, nw2, wpa, wpb, wout, seq_len):
    n, d = x2.shape
    tm = _row_tile(n, seq_len, 512)
    g1_a, g1_s = _seq_operand(g1, seq_len, tm)
    sc_a, sc_s = _seq_operand(sc2, seq_len, tm)
    sh_a, sh_s = _seq_operand(sh2, seq_len, tm)

    def row():
        return pl.BlockSpec((tm, d), lambda i: (i, 0))

    def const(shape):
        return pl.BlockSpec(shape, lambda i: (0, 0))

    bg = b_gate.reshape(1, 2 * d)
    return pl.pallas_call(
        _merge_kernel,
        out_shape=(jax.ShapeDtypeStruct((n, d), F32), jax.ShapeDtypeStruct((n, d), BF16)),
        grid=(n // tm,),
        in_specs=[row(), row(), pl.BlockSpec((N_PAIRS, tm, LANES), lambda i: (0, i, 0)),
                  pl.BlockSpec((tm, d), lambda i: (i, C_GA // d)), pl.BlockSpec((tm, d), lambda i: (i, C_GB // d)),
                  pl.BlockSpec((1, d), lambda i: (0, 0)), pl.BlockSpec((1, d), lambda i: (0, 1)),
                  g1_s, sc_s, sh_s, const((1, d)), const((d, d)), const((d, d)), const((d, d))],
        out_specs=(row(), row()),
        compiler_params=_cparams(("parallel",)),
    )(x2, o_a, o_b, P, P, bg, bg, g1_a, sc_a, sh_a, nw2.reshape(1, d), wpa, wpb, wout)


def _top_exact(s, k):
    rows = lax.broadcasted_iota(I32, s.shape, 0).astype(F32)
    cur = s
    rank = jnp.full(s.shape, float(k), F32)
    vals = []
    for r in range(k):
        m = jnp.max(cur, axis=0, keepdims=True)
        first = jnp.min(jnp.where(cur == m, rows, 1e9), axis=0, keepdims=True)
        hit = rows == first
        vals.append(m)
        rank = jnp.where(hit, float(r), rank)
        cur = jnp.where(hit, -jnp.inf, cur)
    return vals, rank


def _top_fast(ss, k):
    curs = list(ss)
    ranks = [jnp.full(s.shape, float(k), F32) for s in ss]
    vals = [[] for _ in ss]
    for r in range(k):
        ms = [jnp.max(c, axis=0, keepdims=True) for c in curs]
        hits = [c == m for c, m in zip(curs, ms)]
        ranks = [jnp.where(h, float(r), rk) for h, rk in zip(hits, ranks)]
        curs = [jnp.where(h, -jnp.inf, c) for h, c in zip(hits, curs)]
        for v, m in zip(vals, ms):
            v.append(m)
    cleans = [jnp.max(jnp.abs(jnp.sum(jnp.where(rk < k, 1.0, 0.0), axis=0, keepdims=True) - k)) == 0.0
              for rk in ranks]
    return vals, ranks, cleans


def _top(ss, k, vals_scr, rank_scr):
    vals, ranks, cleans = _top_fast(ss, k)
    for i, s in enumerate(ss):
        vals_scr[i] = jnp.concatenate(vals[i], axis=0)
        rank_scr[i] = ranks[i]

        @pl.when(jnp.logical_not(cleans[i]))
        def _(i=i, s=s):
            vals_e, rank_e = _top_exact(s, k)
            vals_scr[i] = jnp.concatenate(vals_e, axis=0)
            rank_scr[i] = rank_e


def _peer_sel_kernel(h_ref, wpqt_ref, kbd_ref, g_ref, cnt_ref, r2_ref, p2_ref, s_scr, vals_scr, rank_scr,
                     cand_scr, cvals_scr, crank_scr):
    K = PEER_TOPK
    tm = h_ref.shape[0]
    qt = lax.dot_general(wpqt_ref[...], h_ref[...], NT, preferred_element_type=F32)
    s_scr[...] = jnp.dot(kbd_ref[...], qt.astype(BF16), preferred_element_type=F32
                         ).reshape(2 * PEER_HEADS, PEER_NKEYS, tm)
    sub8 = lax.broadcasted_iota(I32, (8, tm), 0)
    neg = jnp.full((8, tm), -jnp.inf, F32)
    _top([s_scr[r] for r in range(2 * PEER_HEADS)], K, vals_scr, rank_scr)
    for hd in range(PEER_HEADS):
        c1, c2 = vals_scr[2 * hd], vals_scr[2 * hd + 1]
        blocks = [c1[0:1] + c2, c1[1:2] + c2[0:8]]
        for k1 in range(2, 8):
            blocks.append(jnp.where(sub8 < K // (k1 + 1), c1[k1:k1 + 1] + c2[0:8], neg))
        blocks.append(c1[8:16] + c2[0:1])
        cand_scr[hd] = jnp.concatenate(blocks, axis=0)
    _top([cand_scr[hd] for hd in range(PEER_HEADS)], K, cvals_scr, crank_scr)
    for hd in range(PEER_HEADS):
        s1, s2 = s_scr[2 * hd], s_scr[2 * hd + 1]
        c1, c2 = vals_scr[2 * hd], vals_scr[2 * hd + 1]
        rank1, rank2 = rank_scr[2 * hd], rank_scr[2 * hd + 1]
        cand = cand_scr[hd]
        taken = crank_scr[hd] < K
        z = jnp.sum(jnp.where(taken, jnp.exp(cand - (c1[0:1] + c2[0:1])), 0.0), axis=0, keepdims=True)
        tk = jnp.where(taken, 1.0, 0.0)
        per_k1 = [jnp.sum(tk[0:16], axis=0, keepdims=True)]
        per_k1 += [jnp.sum(tk[8 + 8 * k1:16 + 8 * k1], axis=0, keepdims=True) for k1 in range(1, 8)]
        cnt16 = jnp.concatenate(per_k1 + [tk[72:80]], axis=0)
        cnt = jnp.zeros(s1.shape, F32)
        for k1 in range(K):
            cnt = jnp.where(rank1 == float(k1), cnt16[k1:k1 + 1], cnt)
        g_ref[hd] = jnp.where(rank1 < K, jnp.exp(s1 - c1[0:1]) / z, 0.0)
        cnt_ref[hd] = cnt
        p2 = jnp.where(rank2 < K, jnp.exp(s2 - c2[0:1]), 0.0)
        cb = r2_ref.shape[-1]
        for tc in range(tm // cb):
            r2_ref[hd, tc] = rank2[:, tc * cb:(tc + 1) * cb].astype(r2_ref.dtype)
            p2_ref[hd, tc] = p2[:, tc * cb:(tc + 1) * cb].astype(p2_ref.dtype)


def _peer_select(h2, wpqt, kbd):
    n, d = h2.shape
    tm = 256 if n % 256 == 0 else n
    cb = min(LANES, tm)
    big = jax.ShapeDtypeStruct((PEER_HEADS, PEER_NKEYS, n), F32)
    blocked = jax.ShapeDtypeStruct((PEER_HEADS, n // cb, PEER_NKEYS, cb), BF16)

    def blk():
        return pl.BlockSpec((PEER_HEADS, PEER_NKEYS, tm), lambda i: (0, 0, i))

    def blk4():
        return pl.BlockSpec((PEER_HEADS, tm // cb, PEER_NKEYS, cb), lambda i: (0, i, 0, 0))

    return pl.pallas_call(
        _peer_sel_kernel,
        out_shape=(big, big, blocked, blocked),
        grid=(n // tm,),
        in_specs=[pl.BlockSpec((tm, d), lambda i: (i, 0)),
                  pl.BlockSpec((d, d), lambda i: (0, 0)),
                  pl.BlockSpec((2 * d, d), lambda i: (0, 0))],
        out_specs=(blk(), blk(), blk4(), blk4()),
        scratch_shapes=[pltpu.VMEM((2 * PEER_HEADS, PEER_NKEYS, tm), F32),
                        pltpu.VMEM((2 * PEER_HEADS, PEER_TOPK, tm), F32),
                        pltpu.VMEM((2 * PEER_HEADS, PEER_NKEYS, tm), F32),
                        pltpu.VMEM((PEER_HEADS, PEER_CAND, tm), F32),
                        pltpu.VMEM((PEER_HEADS, PEER_TOPK, tm), F32),
                        pltpu.VMEM((PEER_HEADS, PEER_CAND, tm), F32)],
        compiler_params=_cparams(("parallel",)),
    )(h2, wpqt, kbd)


def _gelu_tanh(x):
    return 0.5 * x * (1.0 + jnp.tanh(0.7978845608028654 * (x + 0.044715 * (x * x * x))))


def _peer_main_kernel(ni1, h_ref, x1_ref, g2_ref, u_ref, vt_ref, g_ref, cnt_ref, r2_ref, p2_ref, y_ref, acc, gate_s):
    j = pl.program_id(1)
    tm = h_ref.shape[0]

    @pl.when(j == 0)
    def _():
        acc[...] = jnp.zeros_like(acc)

    cb = r2_ref.shape[-1]
    reps = PEER_NKEYS // 16
    zero = jnp.zeros((PEER_NKEYS, cb), BF16)
    for l in range(ni1):
        for tc in range(tm // cb):
            ts = slice(tc * cb, (tc + 1) * cb)
            w = None
            for hd in range(PEER_HEADS):
                c16 = jnp.broadcast_to(cnt_ref[hd, l:l + 1, ts], (16, cb)).astype(BF16)
                g16 = jnp.broadcast_to(g_ref[hd, l:l + 1, ts], (16, cb)).astype(BF16)
                t = (jnp.where(r2_ref[hd, tc] < jnp.concatenate([c16] * reps, axis=0), p2_ref[hd, tc], zero)
                     * jnp.concatenate([g16] * reps, axis=0))
                w = t if w is None else w + t
            gate_s[tc, l * PEER_NKEYS:(l + 1) * PEER_NKEYS, :] = w

    act = lax.dot_general(u_ref[...], h_ref[...], NT, preferred_element_type=F32)
    gate = jnp.concatenate([gate_s[tc] for tc in range(tm // cb)], axis=1)
    coef = gate * _gelu_tanh(act.astype(BF16))
    acc[...] += jnp.dot(vt_ref[...], coef, preferred_element_type=F32)

    @pl.when(j == pl.num_programs(1) - 1)
    def _():
        y_ref[...] = x1_ref[...] + g2_ref[...] * acc[...].T


def _peer_main(h2, x1, g2, u16, vt16, g, cnt, r2, p2, seq_len):
    n, d = h2.shape
    tm = _row_tile(n, seq_len, 512)
    ni1 = 16
    et = ni1 * PEER_NKEYS
    cb = r2.shape[-1]
    g2_a, g2_s = _seq_operand(g2, seq_len, tm)

    def row():
        return pl.BlockSpec((tm, d), lambda i, j: (i, 0))

    return pl.pallas_call(
        functools.partial(_peer_main_kernel, ni1),
        out_shape=jax.ShapeDtypeStruct((n, d), F32),
        grid=(n // tm, N_EXPERTS // et),
        in_specs=[row(), row(), g2_s,
                  pl.BlockSpec((et, d), lambda i, j: (j, 0)),
                  pl.BlockSpec((d, et), lambda i, j: (0, j)),
                  pl.BlockSpec((PEER_HEADS, ni1, tm), lambda i, j: (0, j, i)),
                  pl.BlockSpec((PEER_HEADS, ni1, tm), lambda i, j: (0, j, i)),
                  pl.BlockSpec((PEER_HEADS, tm // cb, PEER_NKEYS, cb), lambda i, j: (0, i, 0, 0)),
                  pl.BlockSpec((PEER_HEADS, tm // cb, PEER_NKEYS, cb), lambda i, j: (0, i, 0, 0))],
        out_specs=row(),
        scratch_shapes=[pltpu.VMEM((d, tm), F32), pltpu.VMEM((tm // cb, et, cb), BF16)],
        compiler_params=_cparams(("parallel", "arbitrary")),
    )(h2, x1, g2_a, u16, vt16, g, cnt, r2, p2)


def _layer(x, mod, pos, shift_prev, s0, cache, lw):
    nb, t, d = x.shape
    n = nb * t
    sh1, sc1, g1, sh2, sc2, g2 = [mod[:, i * d:(i + 1) * d] for i in range(6)]
    x2 = x.reshape(n, d)
    P = _inproj(x2, sc1, sh1, lw['norm1_w'], lw['w_in16'], t)

    prev = _pack_rw(shift_prev).reshape(nb, 1, P_COLS)
    o_a, zf = _rwkv(P, nb, t, prev, lw['mu'], lw['w0'], lw['a0'], lw['k_k'], lw['k_a'], lw['r_k'], lw['lnx_w'],
                    lw['lnx_b'], lw['wup'], lw['aup'], lw['gup'], _state_to_pairs(s0))
    wkv = _pairs_to_state(zf)
    shift_last = _unpack_rw(P.reshape(nb, t, P_COLS)[:, -1, :])

    q16, k32, k16, v32, v16, qi16, kw32, ki2 = _dsa_prep(P, jnp.tile(pos, nb), lw['q_norm_w'], lw['k_norm_w'])
    if cache is None:
        o_b = _attn_prompt(q16, qi16, kw32, k16, v16, ki2, nb, t)
    else:
        ck, cv, cki = cache
        past = ck.shape[1]
        o_b = _attn_sample(q16, qi16, kw32, k16, v16, ki2, ck.reshape(nb, past, d), cv.reshape(nb, past, d), cki,
                           nb, t)

    x1, h2 = _merge(x2, o_a, o_b, P, lw['b_gate'], g1, sc2, sh2, lw['norm2_w'], lw['wpa'], lw['wpb'], lw['wout'], t)
    g, cnt, r2, p2 = _peer_select(h2, lw['wpqt'], lw['kbd'])
    y = _peer_main(h2, x1, g2, lw['u16'], lw['vt16'], g, cnt, r2, p2, t)

    k_new = k32.reshape(nb, t, N_HEADS, HEAD_DIM)
    v_new = v32.reshape(nb, t, N_HEADS, HEAD_DIM)
    ki_new = kw32[:, :IDX_DIM].reshape(nb, t, IDX_DIM)
    return y.reshape(nb, t, d), wkv, shift_last, k_new, v_new, ki_new


def _layer_weights(l, w_in, b_gate, mu_rw, w0, w_up, a0, a_up, g_up, k_k, k_a, r_k, lnx_w, lnx_b, q_norm_w, k_norm_w,
                   w_proj_a, w_proj_b, w_out, norm1_w, norm2_w, w_pq, peer_keys, peer_u, peer_v):
    d = D_MODEL
    zeros = lambda r: jnp.zeros((r, d), F32)
    keys = peer_keys[l].reshape(2 * PEER_HEADS, PEER_NKEYS, PEER_DHALF)
    eye = jnp.eye(2 * PEER_HEADS, dtype=F32)
    kbd = (eye[:, None, :, None] * keys[:, :, None, :]).reshape(2 * d, d)
    return {
        'w_in16': _pack_in(w_in[l]).astype(BF16), 'b_gate': b_gate[l], 'mu': _pack_rw(mu_rw[l]).reshape(1, P_COLS),
        'w0': w0[l], 'a0': a0[l], 'k_k': k_k[l], 'k_a': k_a[l], 'r_k': r_k[l].reshape(d), 'lnx_w': lnx_w[l],
        'lnx_b': lnx_b[l],
        'wup': jnp.concatenate([w_up[l], zeros(LANES - D_DECAY)], axis=0).astype(BF16),
        'aup': jnp.concatenate([zeros(D_DECAY), a_up[l]], axis=0).astype(BF16),
        'gup': jnp.concatenate([g_up[l], zeros(256 - D_GATE)], axis=0).astype(BF16),
        'q_norm_w': q_norm_w[l], 'k_norm_w': k_norm_w[l], 'norm1_w': norm1_w[l], 'norm2_w': norm2_w[l],
        'wpa': w_proj_a[l].astype(BF16), 'wpb': w_proj_b[l].astype(BF16), 'wout': w_out[l].astype(BF16),
        'wpqt': w_pq[l].T.astype(BF16), 'kbd': kbd.astype(BF16),
        'u16': peer_u[l].astype(BF16), 'vt16': peer_v[l].T.astype(BF16),
    }


def kernel(x_prompt, x_sample, c_prompt, c_sample, cache_k, cache_v, cache_kidx, state_wkv, state_shift, w_ada, b_ada,
           norm1_w, w_in, b_gate, mu_rw, w0, w_up, a0, a_up, g_up, k_k, k_a, r_k, lnx_w, lnx_b, q_norm_w, k_norm_w,
           w_proj_a, w_proj_b, w_out, norm2_w, w_pq, peer_keys, peer_u, peer_v):
    depth = w_in.shape[0]
    bp, tp = x_prompt.shape[:2]
    bs, ts = x_sample.shape[:2]
    past = cache_k.shape[2]
    dt = x_prompt.dtype
    pos_p = jnp.arange(tp, dtype=I32)
    pos_s = past + jnp.arange(ts, dtype=I32)
    zero_shift = jnp.zeros((bp, RW_IN), dt)
    zero_wkv = jnp.zeros((bp, N_HEADS, HEAD_DIM, HEAD_DIM), dt)
    c_all = jnp.concatenate([c_prompt, c_sample], axis=0)
    xp, xs = x_prompt, x_sample
    outs_p, outs_s = [], []
    for l in range(depth):
        lw = _layer_weights(l, w_in, b_gate, mu_rw, w0, w_up, a0, a_up, g_up, k_k, k_a, r_k, lnx_w, lnx_b, q_norm_w,
                            k_norm_w, w_proj_a, w_proj_b, w_out, norm1_w, norm2_w, w_pq, peer_keys, peer_u, peer_v)
        mod = _ada(c_all, w_ada[l], b_ada[l])
        xp, *rest_p = _layer(xp, mod[:bp], pos_p, zero_shift, zero_wkv, None, lw)
        xs, *rest_s = _layer(xs, mod[bp:], pos_s, state_shift[l], state_wkv[l],
                             (cache_k[l], cache_v[l], cache_kidx[l]), lw)
        outs_p.append(rest_p)
        outs_s.append(rest_s)
    stack = lambda outs, i: jnp.stack([o[i] for o in outs])
    return (xp, xs,
            stack(outs_p, 0), stack(outs_p, 1), stack(outs_p, 2), stack(outs_p, 3), stack(outs_p, 4),
            stack(outs_s, 0), stack(outs_s, 1), stack(outs_s, 2), stack(outs_s, 3), stack(outs_s, 4))
```

```python
import functools

import jax
import jax.numpy as jnp
from jax import lax
from jax.experimental import pallas as pl
from jax.experimental.pallas import tpu as pltpu

F32 = jnp.float32
BF16 = jnp.bfloat16
I32 = jnp.int32

LANES = 128
D_MODEL = 1024
EPS = 1e-6
GN_EPS = 64e-5
ROPE_THETA = 10000.0
CHUNK = 64
TOPK_MAX = 256
HEAD_DIM = 64
N_HEADS = D_MODEL // HEAD_DIM
N_PAIRS = N_HEADS // 2
IDX_HEADS = 8
IDX_DIM = 64
D_DECAY = 64
D_AAA = 64
D_GATE = 160
RW_IN = 3 * D_MODEL + D_DECAY + D_AAA + D_GATE
PEER_HEADS = 8
PEER_NKEYS = 128
PEER_TOPK = 16
PEER_DHALF = 64
N_EXPERTS = PEER_NKEYS * PEER_NKEYS
RW_CHUNK = 64
RW_INTERLEAVE = 16
RW_PAIRS_LONG = 2
RW_PASSES = (2, 1, 1, 1, 1)
VMEM_LIMIT = 56 * 1024 * 1024
LOG2E = 1.4426950408889634
PEER_CAND = 80
GATE_ROWS = 32

C_R, C_K, C_V = 0, 1024, 2048
C_Q, C_KD, C_VD = 3072, 4096, 5120
C_GA, C_GB = 6144, 7168
C_QI = 8192
C_G = 8704
C_M = 8960
C_KW = 9088
P_COLS = 9216
IN_W = 9064

NT = (((1,), (1,)), ((), ()))
NN = (((1,), (0,)), ((), ()))


def _pack_in(w):
    z = lambda k: jnp.zeros(w.shape[:-1] + (k,), w.dtype)
    return jnp.concatenate([w[..., 0:3072], w[..., 3360:6432], w[..., 7016:9064], w[..., 6432:6944],
                            w[..., 3200:3360], z(256 - D_GATE), w[..., 3072:3200],
                            w[..., 6944:7016], z(LANES - IDX_DIM - IDX_HEADS)], axis=-1)


def _pack_rw(a):
    return _pack_in(jnp.concatenate([a, jnp.zeros(a.shape[:-1] + (IN_W - RW_IN,), a.dtype)], axis=-1))


def _unpack_rw(p):
    return jnp.concatenate([p[..., :3072], p[..., C_M:C_M + 128], p[..., C_G:C_G + D_GATE]], axis=-1)


def _split_bf16(x, n):
    parts = []
    r = x
    for _ in range(n):
        p = r.astype(BF16)
        parts.append(p)
        r = r - p.astype(F32)
    return parts


def _mm(a, b, pa=1, pb=1, dims=NN):
    aps = _split_bf16(a, pa) if a.dtype != BF16 else [a]
    bps = _split_bf16(b, pb) if b.dtype != BF16 else [b]
    order = max(len(aps), len(bps))
    out = None
    for i, ap in enumerate(aps):
        for j, bp in enumerate(bps):
            if i + j >= order:
                continue
            t = lax.dot_general(ap, bp, dims, preferred_element_type=F32)
            out = t if out is None else out + t
    return out


def _sigmoid(x):
    return 1.0 / (1.0 + jnp.exp(-x))


def _softplus(z):
    return jnp.maximum(z, 0.0) + jnp.log(1.0 + jnp.exp(-jnp.abs(z)))


def _cparams(sem):
    return pltpu.CompilerParams(dimension_semantics=sem, vmem_limit_bytes=VMEM_LIMIT)


def _ada_kernel(c_ref, w_ref, b_ref, o_ref):
    c = c_ref[...]
    s = c * _sigmoid(c)
    o_ref[...] = _mm(s, w_ref[...], 2, 2) + b_ref[...]


def _ada(c, w, b):
    m, d = c.shape
    n = w.shape[1]
    tn = 1024
    return pl.pallas_call(
        _ada_kernel,
        out_shape=jax.ShapeDtypeStruct((m, n), F32),
        grid=(n // tn,),
        in_specs=[pl.BlockSpec((m, d), lambda j: (0, 0)),
                  pl.BlockSpec((d, tn), lambda j: (0, j)),
                  pl.BlockSpec((1, tn), lambda j: (0, j))],
        out_specs=pl.BlockSpec((m, tn), lambda j: (0, j)),
        compiler_params=_cparams(("arbitrary",)),
    )(c, w, b.reshape(1, n))


def _seq_operand(vec, seq_len, tm):
    b, d = vec.shape
    if seq_len % tm == 0:
        per = seq_len // tm
        arr = vec.reshape(b, 1, d)
        spec = pl.BlockSpec((None, 1, d), lambda *g: (g[0] // per, 0, 0))
    else:
        assert tm % seq_len == 0
        arr = jnp.repeat(vec, seq_len, axis=0)
        spec = pl.BlockSpec((tm, d), lambda *g: (g[0], 0))
    return arr, spec


def _row_tile(n, seq_len, cap):
    tm = min(cap, n)
    while n % tm or (seq_len % tm and tm % seq_len):
        tm //= 2
    return tm


def _inproj_kernel(x_ref, sc_ref, sh_ref, nw_ref, w_ref, o_ref, h_scr):
    @pl.when(pl.program_id(1) == 0)
    def _():
        x = x_ref[...]
        y = x * lax.rsqrt(jnp.mean(x * x, axis=-1, keepdims=True) + EPS) * nw_ref[...]
        h_scr[...] = (y * (1.0 + sc_ref[...]) + sh_ref[...]).astype(BF16)

    o_ref[...] = jnp.dot(h_scr[...], w_ref[...], preferred_element_type=F32)


def _inproj(x2, sc, sh, nw, w16, seq_len):
    n, d = x2.shape
    tm = _row_tile(n, seq_len, 1024)
    tn = 1024
    sc_a, sc_s = _seq_operand(sc, seq_len, tm)
    sh_a, sh_s = _seq_operand(sh, seq_len, tm)
    return pl.pallas_call(
        _inproj_kernel,
        out_shape=jax.ShapeDtypeStruct((n, P_COLS), F32),
        grid=(n // tm, P_COLS // tn),
        in_specs=[pl.BlockSpec((tm, d), lambda i, j: (i, 0)), sc_s, sh_s,
                  pl.BlockSpec((1, d), lambda i, j: (0, 0)),
                  pl.BlockSpec((d, tn), lambda i, j: (0, j))],
        out_specs=pl.BlockSpec((tm, tn), lambda i, j: (i, j)),
        scratch_shapes=[pltpu.VMEM((tm, d), BF16)],
        compiler_params=_cparams(("parallel", "arbitrary")),
    )(x2, sc_a, sh_a, nw.reshape(1, d), w16)


def _lane_lo(shape):
    return lax.broadcasted_iota(I32, shape, len(shape) - 1) < HEAD_DIM


def _pair_sum(x):
    lo = _lane_lo(x.shape)
    s0 = jnp.sum(jnp.where(lo, x, 0.0), axis=-1, keepdims=True)
    s1 = jnp.sum(jnp.where(lo, 0.0, x), axis=-1, keepdims=True)
    return jnp.where(lo, s0, s1)


def _stack2(x):
    lo = _lane_lo(x.shape)
    return jnp.concatenate([jnp.where(lo, x, 0.0), jnp.where(lo, 0.0, x)], axis=0)


def _rwkv_kernel(t_real, npair, nchunk, pr, pk, pv, pg, pm, sr, sk, sv, sg, sm, mr, mk, mv, mg, mmu,
                 w0, a0, kkw, kaw, rkw, lnw, lnb, wup, aup, gup, z0, o_ref, zf_ref,
                 r_s, lw_s, k_s, v_s, a_s, b_s, y_s, bonus_s, g_s):
    C = RW_CHUNK
    t_pad = r_s.shape[1]

    def mix(p_ref, s_ref, m_ref):
        p = p_ref[...]
        prev = pltpu.roll(p, 1, 0)
        row = lax.broadcasted_iota(I32, p.shape, 0)
        prev = jnp.where(row == 0, s_ref[...], prev)
        return p + (prev - p) * m_ref[...]

    xg, xm = mix(pg, sg, mg), mix(pm, sm, mmu)
    th16, xm16, sg16 = jnp.tanh(xm).astype(BF16), xm.astype(BF16), _sigmoid(xg).astype(BF16)
    xr_all, xk_all, xv_all = mix(pr, sr, mr), mix(pk, sk, mk), mix(pv, sv, mv)

    def put(ref, pp, val):
        if t_pad > t_real:
            val = jnp.concatenate([val, jnp.zeros((t_pad - t_real, LANES), F32)], axis=0)
        ref[pp] = val

    for pp in range(npair):
        cs = slice(pp * LANES, (pp + 1) * LANES)
        xr, xk, xv = xr_all[:, cs], xk_all[:, cs], xv_all[:, cs]
        dw = jnp.dot(th16, wup[:, cs], preferred_element_type=F32)
        lw = -jnp.exp(-_softplus(-(w0[:, cs] + dw)) - 0.5)
        asig = _sigmoid(a0[:, cs] + jnp.dot(xm16, aup[:, cs], preferred_element_type=F32))
        g_s[pp] = jnp.dot(sg16, gup[:, cs], preferred_element_type=F32)
        kk = xk * kkw[:, cs]
        kk = kk * lax.rsqrt(_pair_sum(kk * kk) + 1e-12)
        kmod = xk * (1.0 + (asig - 1.0) * kaw[:, cs])
        bonus_s[pp] = _pair_sum(xr * kmod * rkw[:, cs]) * xv
        put(r_s, pp, xr)
        put(lw_s, pp, lw)
        put(k_s, pp, kmod)
        put(v_s, pp, xv)
        put(a_s, pp, -kk)
        put(b_s, pp, kk * asig)

    n2 = 2 * C
    ri = lax.broadcasted_iota(I32, (n2, n2), 0)
    ci = lax.broadcasted_iota(I32, (n2, n2), 1)
    same = (ri // C) == (ci // C)
    strict = same & ((ri % C) > (ci % C))
    incl = same & ((ri % C) >= (ci % C))
    eye = ri == ci
    eye_f = jnp.where(eye, 1.0, 0.0)
    tri = jnp.where(lax.broadcasted_iota(I32, (C, C), 0) >= lax.broadcasted_iota(I32, (C, C), 1), 1.0, 0.0
                    ).astype(BF16)
    zeros_sq = jnp.zeros((n2, LANES), F32)

    pc_, pg_, pi_, po_, ps_ = RW_PASSES

    def local(chains):
        each = lambda f, *cols: [f(*xs) for xs in zip(*cols)]
        lwc = [lw_s[pp, sl, :] for sl, pp in chains]
        cum = each(lambda l: _mm(tri, l, 1, pc_), lwc)
        cum_last = each(lambda c: c[C - 1:C, :], cum)
        ec, eci = each(jnp.exp, cum), each(lambda c: jnp.exp(-c), cum)
        ecp = each(lambda c, l: jnp.exp(c - l), cum, lwc)
        ecl = each(lambda c, cl: jnp.exp(cl - c), cum, cum_last)
        a_c = [a_s[pp, sl, :] for sl, pp in chains]
        b_c = [b_s[pp, sl, :] for sl, pp in chains]
        k_c = [k_s[pp, sl, :] for sl, pp in chains]
        r_c = [r_s[pp, sl, :] for sl, pp in chains]
        As = each(lambda a, e: _stack2(a * e), a_c, ecp)
        Rs = each(lambda r, e: _stack2(r * e), r_c, ec)
        Bs = each(lambda b, e: _stack2(b * e), b_c, eci)
        Ks = each(lambda k, e: _stack2(k * e), k_c, eci)
        Bt = each(lambda b, e: _stack2(b * e), b_c, ecl)
        Kt = each(lambda k, e: _stack2(k * e), k_c, ecl)
        Vs = [_stack2(v_s[pp, sl, :]) for sl, pp in chains]

        G = each(lambda a, r, b, k: _mm(jnp.concatenate([a, r], axis=0), jnp.concatenate([b, k], axis=0),
                                        pg_, pg_, NT), As, Rs, Bs, Ks)
        a_ab = each(lambda g: jnp.where(strict, g[:n2, :n2], 0.0), G)
        a_ak = each(lambda g: jnp.where(strict, g[:n2, n2:], 0.0), G)
        a_rb = each(lambda g: jnp.where(incl, g[n2:, :n2], 0.0), G)
        a_rk = each(lambda g: jnp.where(incl, g[n2:, n2:], 0.0), G)

        lp = a_ab
        tm_ = each(lambda a: eye_f + a, a_ab)
        step = 2
        while step < C:
            lp = each(lambda l: _mm(l, l, pi_, pi_), lp)
            tm_ = each(lambda t, l: t + _mm(t, l, pi_, pi_), tm_, lp)
            step *= 2

        w1 = each(lambda a, v: _mm(a, v, po_, po_), a_ak, Vs)
        mu_ = each(lambda t, a, w: _mm(t, jnp.concatenate([a, w], axis=1), po_, po_), tm_, As, w1)
        rhs = each(lambda m, v: jnp.concatenate([m, jnp.concatenate([zeros_sq, v], axis=1)], axis=0), mu_, Vs)
        lhs = each(lambda rb, rk, b, k: jnp.concatenate([jnp.concatenate([rb, rk], axis=1),
                                                         jnp.concatenate([b.T, k.T], axis=1)], axis=0),
                   a_rb, a_rk, Bt, Kt)
        out2 = each(lambda l, r: _mm(l, r, po_, po_), lhs, rhs)
        m23 = each(lambda r, o, cl: jnp.concatenate([r + o[:n2, :LANES],
                                                     jnp.where(eye, jnp.exp(cl), 0.0) + o[n2:, :LANES]], axis=0),
                   Rs, out2, cum_last)
        return [(m, o[:n2, LANES:], o[n2:, LANES:]) for m, o in zip(m23, out2)]

    def step_chunks(i, zs):
        sls = [pl.ds(pl.multiple_of((i * nchunk + j) * C, C), C) for j in range(nchunk)]
        parts = local([(sl, pp) for sl in sls for pp in range(npair)])
        zs = list(zs)
        for j, sl in enumerate(sls):
            for pp in range(npair):
                m23, y_loc, z_loc = parts[j * npair + pp]
                yz = _mm(m23, zs[pp], ps_, ps_)
                y = yz[:n2] + y_loc
                y_s[pp, sl, :] = y[:C] + y[C:]
                zs[pp] = yz[n2:] + z_loc
        return tuple(zs)

    zs = lax.fori_loop(0, t_pad // (C * nchunk), step_chunks, tuple(z0[pp] for pp in range(npair)))
    for pp in range(npair):
        zf_ref[pp] = zs[pp]
        cs = slice(pp * LANES, (pp + 1) * LANES)
        y = y_s[pp, 0:t_real, :]
        mean = _pair_sum(y) * (1.0 / HEAD_DIM)
        dlt = y - mean
        var = _pair_sum(dlt * dlt) * (1.0 / HEAD_DIM)
        yn = dlt * lax.rsqrt(var + GN_EPS) * lnw[:, cs] + lnb[:, cs]
        o_ref[:, cs] = ((yn + bonus_s[pp]) * g_s[pp]).astype(o_ref.dtype)


def _rwkv(P, nb, t, prev, mu, w0, a0, k_k, k_a, r_k, lnx_w, lnx_b, wup, aup, gup, z0):
    t_pad = max(t, RW_CHUNK)
    assert t % 8 == 0 and t_pad % RW_CHUNK == 0
    n_chunks = t_pad // RW_CHUNK
    nchunk = min(RW_INTERLEAVE // RW_PAIRS_LONG, n_chunks)
    npair = min(N_PAIRS, max(1, RW_INTERLEAVE // nchunk))
    wp = npair * LANES

    def cblk(c0, w, per_pair):
        return (lambda p: c0 // w + p) if per_pair else (lambda p: c0 // w)

    def pcol(c0, w, pp):
        f = cblk(c0, w, pp)
        return pl.BlockSpec((t, w), lambda b, p: (b, f(p)))

    def prevcol(c0, w, pp):
        f = cblk(c0, w, pp)
        return pl.BlockSpec((None, 1, w), lambda b, p: (b, 0, f(p)))

    def mucol(c0, w, pp):
        f = cblk(c0, w, pp)
        return pl.BlockSpec((1, w), lambda b, p: (0, f(p)))

    def hvec():
        return pl.BlockSpec((1, wp), lambda b, p: (0, p))

    cols = [(C_R, wp, True), (C_K, wp, True), (C_V, wp, True), (C_G, 256, False), (C_M, LANES, False)]
    in_specs = ([pcol(*c) for c in cols] + [prevcol(*c) for c in cols] + [mucol(*c) for c in cols]
                + [hvec() for _ in range(7)]
                + [pl.BlockSpec((LANES, wp), lambda b, p: (0, p)),
                   pl.BlockSpec((LANES, wp), lambda b, p: (0, p)),
                   pl.BlockSpec((256, wp), lambda b, p: (0, p)),
                   pl.BlockSpec((None, npair, LANES, LANES), lambda b, p: (b, p, 0, 0))])
    vecs = [v.reshape(1, D_MODEL) for v in (w0, a0, k_k, k_a, r_k, lnx_w, lnx_b)]
    o, zf = pl.pallas_call(
        functools.partial(_rwkv_kernel, t, npair, nchunk),
        out_shape=(jax.ShapeDtypeStruct((nb * t, D_MODEL), BF16),
                   jax.ShapeDtypeStruct((nb, N_PAIRS, LANES, LANES), F32)),
        grid=(nb, N_PAIRS // npair),
        in_specs=in_specs,
        out_specs=(pl.BlockSpec((t, wp), lambda b, p: (b, p)),
                   pl.BlockSpec((None, npair, LANES, LANES), lambda b, p: (b, p, 0, 0))),
        scratch_shapes=([pltpu.VMEM((npair, t_pad, LANES), F32) for _ in range(7)]
                        + [pltpu.VMEM((npair, t, LANES), F32) for _ in range(2)]),
        compiler_params=_cparams(("parallel", "arbitrary")),
    )(P, P, P, P, P, prev, prev, prev, prev, prev, mu, mu, mu, mu, mu, *vecs, wup, aup, gup, z0)
    return o, zf


def _state_to_pairs(s):
    nb = s.shape[0]
    zt = jnp.swapaxes(s, -1, -2).reshape(nb, N_PAIRS, 2, HEAD_DIM, HEAD_DIM)
    zero = jnp.zeros_like(zt[:, :, 0])
    top = jnp.concatenate([zt[:, :, 0], zero], axis=-1)
    bot = jnp.concatenate([zero, zt[:, :, 1]], axis=-1)
    return jnp.concatenate([top, bot], axis=-2)


def _pairs_to_state(z):
    nb = z.shape[0]
    h0 = z[:, :, :HEAD_DIM, :HEAD_DIM]
    h1 = z[:, :, HEAD_DIM:, HEAD_DIM:]
    s = jnp.stack([h0, h1], axis=2).reshape(nb, N_HEADS, HEAD_DIM, HEAD_DIM)
    return jnp.swapaxes(s, -1, -2)


def _rope(x, cos, sin_signed):
    w = x.shape[1]
    reps = w // LANES
    cw = jnp.concatenate([cos] * reps, axis=1) if reps > 1 else cos
    sw = jnp.concatenate([sin_signed] * reps, axis=1) if reps > 1 else sin_signed
    lane = lax.broadcasted_iota(I32, x.shape, 1)
    fwd = pltpu.roll(x, w - 32, 1)
    bwd = pltpu.roll(x, 32, 1)
    partner = jnp.where((lane % HEAD_DIM) < 32, fwd, bwd)
    return x * cw + partner * sw


def _head_rms(x, nw, e_dn, e_up):
    ms = _mm(x * x, e_dn, 2, 1) * (1.0 / HEAD_DIM)
    r = lax.rsqrt(ms + EPS)
    return x * _mm(r, e_up, 2, 1) * nw


def _dsa_prep_kernel(pq, pkd, pvd, pqi, pkw, cos_ref, sin_ref, qn, kn, edn, eup,
                     q16, k32, k16, v32, v16, qi16, kw32, ki2):
    cos, sin = cos_ref[...], sin_ref[...]
    e_dn, e_up = edn[...], eup[...]
    def put_pairs(ref, x):
        for p in range(N_PAIRS):
            ref[p] = x[:, p * LANES:(p + 1) * LANES].astype(ref.dtype)

    q = _rope(_head_rms(pq[...], qn[...], e_dn, e_up), cos, sin)
    put_pairs(q16, q * (HEAD_DIM ** -0.5 * LOG2E))
    k = _rope(_head_rms(pkd[...], kn[...], e_dn, e_up), cos, sin)
    k32[...] = k
    put_pairs(k16, k)
    v = pvd[...]
    v32[...] = v
    put_pairs(v16, v)
    qi16[...] = _rope(pqi[...], cos, sin).astype(BF16)
    kw = pkw[...]
    lane = lax.broadcasted_iota(I32, kw.shape, 1)
    wi_scale = (IDX_HEADS * IDX_DIM) ** -0.5
    kr = _rope(kw, cos, sin)
    kw32[...] = jnp.where(lane < IDX_DIM, kr, jnp.where(lane < IDX_DIM + IDX_HEADS, kw * wi_scale, 0.0))
    ki2[...] = jnp.where(lane < IDX_DIM, kr, pltpu.roll(kr, IDX_DIM, 1)).astype(BF16)


def _dsa_prep(P, pos_rows, q_norm_w, k_norm_w):
    n = P.shape[0]
    tm = 512 if n % 512 == 0 else n
    half = HEAD_DIM // 2
    inv = ROPE_THETA ** (-jnp.arange(half, dtype=F32) / half)
    ang = pos_rows.astype(F32)[:, None] * inv[None, :]
    cos = jnp.tile(jnp.cos(ang), (1, 4))
    sin = jnp.sin(ang)
    sin_signed = jnp.tile(jnp.concatenate([-sin, sin], axis=1), (1, 2))
    head_of = jnp.arange(D_MODEL) // HEAD_DIM
    e_dn = (head_of[:, None] == jnp.arange(LANES)[None, :]).astype(BF16)
    e_up = e_dn.T
    qn = jnp.tile(q_norm_w, N_HEADS).reshape(1, D_MODEL)
    kn = jnp.tile(k_norm_w, N_HEADS).reshape(1, D_MODEL)

    def col(c0, w):
        return pl.BlockSpec((tm, w), lambda i, c0=c0, w=w: (i, c0 // w))

    def row(w):
        return pl.BlockSpec((tm, w), lambda i: (i, 0))

    def const(shape):
        return pl.BlockSpec(shape, lambda i: (0, 0))

    pairs = jax.ShapeDtypeStruct((N_PAIRS, n, LANES), BF16)
    pair_spec = pl.BlockSpec((N_PAIRS, tm, LANES), lambda i: (0, i, 0))
    return pl.pallas_call(
        _dsa_prep_kernel,
        out_shape=(pairs, jax.ShapeDtypeStruct((n, D_MODEL), F32), pairs, jax.ShapeDtypeStruct((n, D_MODEL), F32),
                   pairs, jax.ShapeDtypeStruct((n, IDX_HEADS * IDX_DIM), BF16),
                   jax.ShapeDtypeStruct((n, LANES), F32), jax.ShapeDtypeStruct((n, LANES), BF16)),
        grid=(n // tm,),
        in_specs=[col(C_Q, 1024), col(C_KD, 1024), col(C_VD, 1024), col(C_QI, 512), col(C_KW, LANES),
                  row(LANES), row(LANES), const((1, D_MODEL)), const((1, D_MODEL)),
                  const((D_MODEL, LANES)), const((LANES, D_MODEL))],
        out_specs=(pair_spec, row(D_MODEL), pair_spec, row(D_MODEL), pair_spec, row(512), row(LANES), row(LANES)),
        compiler_params=_cparams(("parallel",)),
    )(P, P, P, P, P, cos, sin_signed, qn, kn, e_dn, e_up)


CODE_NEG_INF = -1 - 0x7F800000
ATTN_TQ = 256
ATTN_PAIRS = 2
ATTN_CASES = 4
SAMPLE_PAIRS = 4


def _index_scores(qi, wi, ki_list):
    outs = []
    for ki in ki_list:
        acc = None
        for h in range(IDX_HEADS):
            qpair = qi[:, (h // 2) * LANES:(h // 2 + 1) * LANES]
            lo = _lane_lo(qpair.shape)
            qh = jnp.where(lo if h % 2 == 0 else jnp.logical_not(lo), qpair, jnp.zeros_like(qpair))
            rel = lax.dot_general(qh, ki, NT, preferred_element_type=F32)
            term = wi[:, IDX_DIM + h:IDX_DIM + h + 1] * jnp.maximum(rel, 0.0)
            acc = term if acc is None else acc + term
        outs.append(acc)
    return outs


def _select_topk(keys, topk, bias_refs):
    tq = keys[0].shape[0]
    neg = -jnp.inf

    def write(masks):
        for ref, k, msk in zip(bias_refs, keys, masks):
            ref[:, 0:k.shape[1]] = jnp.where(msk, 0.0, neg)

    def count(pred_list):
        tot = None
        for p in pred_list:
            c = jnp.sum(jnp.where(p, 1.0, 0.0), axis=-1, keepdims=True)
            tot = c if tot is None else tot + c
        return tot

    def threshold(c):
        bits = jnp.where(c >= 0, c, c ^ jnp.int32(0x7FFFFFFF))
        return jnp.where(c < jnp.int32(CODE_NEG_INF), neg, lax.bitcast_convert_type(bits, F32))

    few = count([k > neg for k in keys]) <= topk

    def pending(cnt):
        return jnp.max(jnp.where(few | (cnt == topk), 0.0, 1.0))

    def bit_step(state):
        i, c, cnt, _ = state
        trial = c + jnp.left_shift(jnp.int32(1), 31 - i)
        cnt_t = count([k >= threshold(trial) for k in keys])
        take = cnt_t >= topk
        cnt = jnp.where(take, cnt_t, cnt)
        return i + 1, jnp.where(take, trial, c), cnt, pending(cnt)

    cnt0 = jnp.full((tq, 1), float(sum(k.shape[1] for k in keys)), F32)
    state = (jnp.int32(0), jnp.full((tq, 1), -2 ** 31, I32), cnt0, pending(cnt0))
    _, code, _, _ = lax.while_loop(lambda s: (s[0] < 32) & (s[3] > 0.0), bit_step, state)
    thr = threshold(code)
    ge = [(k >= thr) & (k > neg) for k in keys]
    write(ge)
    surplus = jnp.max(count(ge)) > topk

    @pl.when(surplus)
    def _():
        gt = [k > thr for k in keys]
        need = topk - count(gt)
        ties = [(k == thr) & (k > neg) for k in keys]
        offs, idx = 0, []
        for k in keys:
            idx.append(lax.broadcasted_iota(I32, k.shape, 1) + offs)
            offs += k.shape[1]
        nbits = max(1, (offs - 1).bit_length() + 1)

        def idx_step(i, m):
            trial = m + jnp.left_shift(jnp.int32(1), nbits - 1 - i)
            cnt = count([t & (ix < trial) for t, ix in zip(ties, idx)])
            return jnp.where(cnt <= need, trial, m)

        cut = lax.fori_loop(0, nbits, idx_step, jnp.zeros((tq, 1), I32))
        write([g | (t & (ix < cut)) for g, t, ix in zip(gt, ties, idx)])


def _attend_pairs(q_pairs, k_lists, v_lists, bias_list):
    lo = _lane_lo(q_pairs[0].shape)
    zero = jnp.zeros_like(q_pairs[0])
    heads = []
    for pi, q in enumerate(q_pairs):
        heads += [(jnp.where(lo, q, zero), pi), (jnp.where(lo, zero, q), pi)]
    s = [[lax.dot_general(qh, k, NT, preferred_element_type=F32) + b for k, b in zip(k_lists[pi], bias_list)]
         for qh, pi in heads]
    m = []
    for sh in s:
        mh = None
        for sj in sh:
            mx = jnp.max(sj, axis=-1, keepdims=True)
            mh = mx if mh is None else jnp.maximum(mh, mx)
        m.append(mh)
    p = [[jnp.exp2(sj - mh) for sj in sh] for sh, mh in zip(s, m)]
    den = [functools.reduce(lambda a, b: a + b, [jnp.sum(pj, axis=-1, keepdims=True) for pj in ph]) for ph in p]
    acc = [functools.reduce(lambda a, b: a + b,
                            [jnp.dot(pj.astype(BF16), v, preferred_element_type=F32)
                             for pj, v in zip(ph, v_lists[pi])])
           for ph, (_, pi) in zip(p, heads)]
    outs = [a / d for a, d in zip(acc, den)]
    return [jnp.where(lo, outs[2 * i], outs[2 * i + 1]) for i in range(len(q_pairs))]


def _attn_prompt_kernel(topk, ncase, q_ref, qi_ref, kw_ref, k_ref, v_ref, ki2_ref, o_ref, bias_s):
    tq = q_ref.shape[1]
    t = k_ref.shape[1]
    i = pl.program_id(1)
    lstep = t // ncase
    case = ((i + 1) * tq - 1) // lstep

    def run(L):
        score = _index_scores(qi_ref[...], kw_ref[...], [ki2_ref[0:L, :]])[0]
        qpos = i * tq + lax.broadcasted_iota(I32, (tq, L), 0)
        kpos = lax.broadcasted_iota(I32, (tq, L), 1)
        adm = (qpos // CHUNK) >= (kpos // CHUNK)
        _select_topk([jnp.where(adm, score, -jnp.inf)], topk, [bias_s])

        def pairs(g, carry):
            ps = [ATTN_PAIRS * g + j for j in range(ATTN_PAIRS)]
            outs = _attend_pairs([q_ref[p] for p in ps], [[k_ref[p, 0:L, :]] for p in ps],
                                 [[v_ref[p, 0:L, :]] for p in ps], [bias_s[:, 0:L]])
            for p, o in zip(ps, outs):
                o_ref[p] = o.astype(o_ref.dtype)
            return carry

        lax.fori_loop(0, N_PAIRS // ATTN_PAIRS, pairs, 0)

    for c in range(ncase):
        pl.when(case == c)(functools.partial(run, (c + 1) * lstep))


def _attn_prompt(q16, qi16, kw32, k16, v16, ki2, nb, t):
    tq = min(ATTN_TQ, t)
    topk = min(TOPK_MAX, t // 4)
    nq = t // tq
    ncase = min(ATTN_CASES, nq)

    def qrow(w):
        return pl.BlockSpec((tq, w), lambda b, i: (b * nq + i, 0))

    def qpairs():
        return pl.BlockSpec((N_PAIRS, tq, LANES), lambda b, i: (0, b * nq + i, 0))

    def kpairs():
        return pl.BlockSpec((N_PAIRS, t, LANES), lambda b, i: (0, b, 0))

    return pl.pallas_call(
        functools.partial(_attn_prompt_kernel, topk, ncase),
        out_shape=jax.ShapeDtypeStruct((N_PAIRS, nb * t, LANES), BF16),
        grid=(nb, nq),
        in_specs=[qpairs(), qrow(512), qrow(LANES), kpairs(), kpairs(),
                  pl.BlockSpec((t, LANES), lambda b, i: (b, 0))],
        out_specs=qpairs(),
        scratch_shapes=[pltpu.VMEM((tq, t), F32)],
        compiler_params=_cparams(("parallel", "arbitrary")),
    )(q16, qi16, kw32, k16, v16, ki2)


def _attn_sample_kernel(topk, past, q_ref, qi_ref, kw_ref, ck_ref, cv_ref, cki2_ref, k_ref, v_ref, ki2_ref, o_ref,
                        biasc_s, biasn_s):
    npairs, ts = q_ref.shape[0], q_ref.shape[1]

    @pl.when(pl.program_id(1) == 0)
    def _():
        sc, sn = _index_scores(qi_ref[...], kw_ref[...], [cki2_ref[...], ki2_ref[...]])
        qpos = past + lax.broadcasted_iota(I32, (ts, 1), 0)
        kpos_c = lax.broadcasted_iota(I32, sc.shape, 1)
        kpos_n = past + lax.broadcasted_iota(I32, sn.shape, 1)
        keys = [jnp.where((qpos // CHUNK) >= (kpos_c // CHUNK), sc, -jnp.inf),
                jnp.where((qpos // CHUNK) >= (kpos_n // CHUNK), sn, -jnp.inf)]
        _select_topk(keys, topk, [biasc_s, biasn_s])

    lanes = [slice(p * LANES, (p + 1) * LANES) for p in range(npairs)]
    outs = _attend_pairs([q_ref[p] for p in range(npairs)],
                         [[ck_ref[:, cs].astype(BF16), k_ref[p]] for p, cs in enumerate(lanes)],
                         [[cv_ref[:, cs].astype(BF16), v_ref[p]] for p, cs in enumerate(lanes)],
                         [biasc_s[...], biasn_s[...]])
    for p, o in enumerate(outs):
        o_ref[p] = o.astype(o_ref.dtype)


def _attn_sample(q16, qi16, kw32, k16, v16, ki2, cache_k, cache_v, cache_kidx, nb, ts):
    past = cache_k.shape[1]
    cki2 = jnp.concatenate([cache_kidx, cache_kidx], axis=-1).astype(BF16)
    topk = min(TOPK_MAX, (past + ts) // 4)

    def qrow(w):
        return pl.BlockSpec((ts, w), lambda b, p: (b, 0))

    sp = SAMPLE_PAIRS

    def qpair():
        return pl.BlockSpec((sp, ts, LANES), lambda b, p: (p, b, 0))

    def cache(pair):
        if pair:
            return pl.BlockSpec((None, past, sp * LANES), lambda b, p: (b, 0, p))
        return pl.BlockSpec((None, past, LANES), lambda b, p: (b, 0, 0))

    return pl.pallas_call(
        functools.partial(_attn_sample_kernel, topk, past),
        out_shape=jax.ShapeDtypeStruct((N_PAIRS, nb * ts, LANES), BF16),
        grid=(nb, N_PAIRS // sp),
        in_specs=[qpair(), qrow(512), qrow(LANES), cache(True), cache(True), cache(False),
                  qpair(), qpair(), qrow(LANES)],
        out_specs=qpair(),
        scratch_shapes=[pltpu.VMEM((ts, past), F32), pltpu.VMEM((ts, ts), F32)],
        compiler_params=_cparams(("parallel", "arbitrary")),
    )(q16, qi16, kw32, cache_k, cache_v, cki2, k16, v16, ki2)


def _merge_kernel(x_ref, oa_ref, ob_ref, pga_ref, pgb_ref, bga_ref, bgb_ref, g1_ref, sc2_ref, sh2_ref, nw_ref,
                  wpa_ref, wpb_ref, wout_ref, x1_ref, h2_ref):
    ga = _sigmoid(pga_ref[...] + bga_ref[...])
    gb = _sigmoid(pgb_ref[...] + bgb_ref[...])
    ob = jnp.concatenate([ob_ref[p] for p in range(N_PAIRS)], axis=1)
    m = (ga * jnp.dot(oa_ref[...], wpa_ref[...], preferred_element_type=F32)
         + gb * jnp.dot(ob, wpb_ref[...], preferred_element_type=F32))
    x1 = x_ref[...] + g1_ref[...] * jnp.dot(m.astype(BF16), wout_ref[...], preferred_element_type=F32)
    x1_ref[...] = x1
    y = x1 * lax.rsqrt(jnp.mean(x1 * x1, axis=-1, keepdims=True) + EPS) * nw_ref[...]
    h2_ref[...] = (y * (1.0 + sc2_ref[...]) + sh2_ref[...]).astype(BF16)


def _merge(x2, o_a, o_b, P, b_gate, g1, sc2, sh2, nw2, wpa, wpb, wout, seq_len):
    n, d = x2.shape
    tm = _row_tile(n, seq_len, 512)
    g1_a, g1_s = _seq_operand(g1, seq_len, tm)
    sc_a, sc_s = _seq_operand(sc2, seq_len, tm)
    sh_a, sh_s = _seq_operand(sh2, seq_len, tm)

    def row():
        return pl.BlockSpec((tm, d), lambda i: (i, 0))

    def const(shape):
        return pl.BlockSpec(shape, lambda i: (0, 0))

    bg = b_gate.reshape(1, 2 * d)
    return pl.pallas_call(
        _merge_kernel,
        out_shape=(jax.ShapeDtypeStruct((n, d), F32), jax.ShapeDtypeStruct((n, d), BF16)),
        grid=(n // tm,),
        in_specs=[row(), row(), pl.BlockSpec((N_PAIRS, tm, LANES), lambda i: (0, i, 0)),
                  pl.BlockSpec((tm, d), lambda i: (i, C_GA // d)), pl.BlockSpec((tm, d), lambda i: (i, C_GB // d)),
                  pl.BlockSpec((1, d), lambda i: (0, 0)), pl.BlockSpec((1, d), lambda i: (0, 1)),
                  g1_s, sc_s, sh_s, const((1, d)), const((d, d)), const((d, d)), const((d, d))],
        out_specs=(row(), row()),
        compiler_params=_cparams(("parallel",)),
    )(x2, o_a, o_b, P, P, bg, bg, g1_a, sc_a, sh_a, nw2.reshape(1, d), wpa, wpb, wout)


def _top_exact(s, k):
    rows = lax.broadcasted_iota(I32, s.shape, 0).astype(F32)
    cur = s
    rank = jnp.full(s.shape, float(k), F32)
    vals = []
    for r in range(k):
        m = jnp.max(cur, axis=0, keepdims=True)
        first = jnp.min(jnp.where(cur == m, rows, 1e9), axis=0, keepdims=True)
        hit = rows == first
        vals.append(m)
        rank = jnp.where(hit, float(r), rank)
        cur = jnp.where(hit, -jnp.inf, cur)
    return vals, rank


def _top_fast(ss, k):
    curs = list(ss)
    ranks = [jnp.full(s.shape, float(k), F32) for s in ss]
    vals = [[] for _ in ss]
    for r in range(k):
        ms = [jnp.max(c, axis=0, keepdims=True) for c in curs]
        hits = [c == m for c, m in zip(curs, ms)]
        ranks = [jnp.where(h, float(r), rk) for h, rk in zip(hits, ranks)]
        curs = [jnp.where(h, -jnp.inf, c) for h, c in zip(hits, curs)]
        for v, m in zip(vals, ms):
            v.append(m)
    cleans = [jnp.max(jnp.abs(jnp.sum(jnp.where(rk < k, 1.0, 0.0), axis=0, keepdims=True) - k)) == 0.0
              for rk in ranks]
    return vals, ranks, cleans


def _top(src_scr, k, vals_scr, rank_scr, redo_s):
    n = src_scr.shape[0]
    vals, ranks, cleans = _top_fast([src_scr[i] for i in range(n)], k)
    for i in range(n):
        vals_scr[i] = jnp.concatenate(vals[i], axis=0)
        rank_scr[i] = ranks[i]
        redo_s[i] = jnp.where(cleans[i], 0, 1).astype(I32)

    def redo(i, carry):
        @pl.when(redo_s[i] == 1)
        def _():
            vals_e, rank_e = _top_exact(src_scr[i], k)
            vals_scr[i] = jnp.concatenate(vals_e, axis=0)
            rank_scr[i] = rank_e

        return carry

    lax.fori_loop(0, n, redo, 0)


def _peer_sel_kernel(h_ref, wpqt_ref, kbd_ref, g_ref, cnt_ref, r2_ref, p2_ref, s_scr, vals_scr, rank_scr,
                     cand_scr, cvals_scr, crank_scr, redo_s):
    K = PEER_TOPK
    tm = h_ref.shape[0]
    qt = lax.dot_general(wpqt_ref[...], h_ref[...], NT, preferred_element_type=F32)
    s_scr[...] = jnp.dot(kbd_ref[...], qt.astype(BF16), preferred_element_type=F32
                         ).reshape(2 * PEER_HEADS, PEER_NKEYS, tm)
    sub8 = lax.broadcasted_iota(I32, (8, tm), 0)
    neg = jnp.full((8, tm), -jnp.inf, F32)
    _top(s_scr, K, vals_scr, rank_scr, redo_s)
    for hd in range(PEER_HEADS):
        c1, c2 = vals_scr[2 * hd], vals_scr[2 * hd + 1]
        blocks = [c1[0:1] + c2, c1[1:2] + c2[0:8]]
        for k1 in range(2, 8):
            blocks.append(jnp.where(sub8 < K // (k1 + 1), c1[k1:k1 + 1] + c2[0:8], neg))
        blocks.append(c1[8:16] + c2[0:1])
        cand_scr[hd] = jnp.concatenate(blocks, axis=0)
    _top(cand_scr, K, cvals_scr, crank_scr, redo_s)
    for hd in range(PEER_HEADS):
        s1, s2 = s_scr[2 * hd], s_scr[2 * hd + 1]
        c1, c2 = vals_scr[2 * hd], vals_scr[2 * hd + 1]
        rank1, rank2 = rank_scr[2 * hd], rank_scr[2 * hd + 1]
        cand = cand_scr[hd]
        taken = crank_scr[hd] < K
        z = jnp.sum(jnp.where(taken, jnp.exp(cand - (c1[0:1] + c2[0:1])), 0.0), axis=0, keepdims=True)
        tk = jnp.where(taken, 1.0, 0.0)
        per_k1 = [jnp.sum(tk[0:16], axis=0, keepdims=True)]
        per_k1 += [jnp.sum(tk[8 + 8 * k1:16 + 8 * k1], axis=0, keepdims=True) for k1 in range(1, 8)]
        cnt16 = jnp.concatenate(per_k1 + [tk[72:80]], axis=0)
        cnt = jnp.zeros(s1.shape, F32)
        for k1 in range(K):
            cnt = jnp.where(rank1 == float(k1), cnt16[k1:k1 + 1], cnt)
        g_ref[hd] = jnp.where(rank1 < K, jnp.exp(s1 - c1[0:1]) / z, 0.0)
        cnt_ref[hd] = cnt
        p2 = jnp.where(rank2 < K, jnp.exp(s2 - c2[0:1]), 0.0)
        cb = r2_ref.shape[-1]
        for tc in range(tm // cb):
            r2_ref[hd, tc] = rank2[:, tc * cb:(tc + 1) * cb].astype(r2_ref.dtype)
            p2_ref[hd, tc] = p2[:, tc * cb:(tc + 1) * cb].astype(p2_ref.dtype)


def _peer_select(h2, wpqt, kbd):
    n, d = h2.shape
    tm = 256 if n % 256 == 0 else n
    cb = min(LANES, tm)
    big = jax.ShapeDtypeStruct((PEER_HEADS, PEER_NKEYS, n), F32)
    blocked = jax.ShapeDtypeStruct((PEER_HEADS, n // cb, PEER_NKEYS, cb), BF16)

    def blk():
        return pl.BlockSpec((PEER_HEADS, PEER_NKEYS, tm), lambda i: (0, 0, i))

    def blk4():
        return pl.BlockSpec((PEER_HEADS, tm // cb, PEER_NKEYS, cb), lambda i: (0, i, 0, 0))

    return pl.pallas_call(
        _peer_sel_kernel,
        out_shape=(big, big, blocked, blocked),
        grid=(n // tm,),
        in_specs=[pl.BlockSpec((tm, d), lambda i: (i, 0)),
                  pl.BlockSpec((d, d), lambda i: (0, 0)),
                  pl.BlockSpec((2 * d, d), lambda i: (0, 0))],
        out_specs=(blk(), blk(), blk4(), blk4()),
        scratch_shapes=[pltpu.VMEM((2 * PEER_HEADS, PEER_NKEYS, tm), F32),
                        pltpu.VMEM((2 * PEER_HEADS, PEER_TOPK, tm), F32),
                        pltpu.VMEM((2 * PEER_HEADS, PEER_NKEYS, tm), F32),
                        pltpu.VMEM((PEER_HEADS, PEER_CAND, tm), F32),
                        pltpu.VMEM((PEER_HEADS, PEER_TOPK, tm), F32),
                        pltpu.VMEM((PEER_HEADS, PEER_CAND, tm), F32),
                        pltpu.SMEM((2 * PEER_HEADS,), I32)],
        compiler_params=_cparams(("parallel",)),
    )(h2, wpqt, kbd)


def _gelu_tanh(x):
    return 0.5 * x * (1.0 + jnp.tanh(0.7978845608028654 * (x + 0.044715 * (x * x * x))))


def _peer_main_kernel(ni1, h_ref, x1_ref, g2_ref, u_ref, vt_ref, g_ref, cnt_ref, r2_ref, p2_ref, y_ref, acc, gate_s):
    j = pl.program_id(1)
    tm = h_ref.shape[0]

    @pl.when(j == 0)
    def _():
        acc[...] = jnp.zeros_like(acc)

    cb = r2_ref.shape[-1]
    reps = PEER_NKEYS // 16
    zero = jnp.zeros((PEER_NKEYS, cb), BF16)
    for l in range(ni1):
        for tc in range(tm // cb):
            ts = slice(tc * cb, (tc + 1) * cb)
            w = None
            for hd in range(PEER_HEADS):
                c16 = jnp.broadcast_to(cnt_ref[hd, l:l + 1, ts], (16, cb)).astype(BF16)
                g16 = jnp.broadcast_to(g_ref[hd, l:l + 1, ts], (16, cb)).astype(BF16)
                t = (jnp.where(r2_ref[hd, tc] < jnp.concatenate([c16] * reps, axis=0), p2_ref[hd, tc], zero)
                     * jnp.concatenate([g16] * reps, axis=0))
                w = t if w is None else w + t
            gate_s[tc, l * PEER_NKEYS:(l + 1) * PEER_NKEYS, :] = w

    act = lax.dot_general(u_ref[...], h_ref[...], NT, preferred_element_type=F32)
    gate = jnp.concatenate([gate_s[tc] for tc in range(tm // cb)], axis=1)
    coef = gate * _gelu_tanh(act.astype(BF16))
    acc[...] += jnp.dot(vt_ref[...], coef, preferred_element_type=F32)

    @pl.when(j == pl.num_programs(1) - 1)
    def _():
        y_ref[...] = x1_ref[...] + g2_ref[...] * acc[...].T


def _peer_main(h2, x1, g2, u16, vt16, g, cnt, r2, p2, seq_len):
    n, d = h2.shape
    tm = _row_tile(n, seq_len, 512)
    ni1 = 16
    et = ni1 * PEER_NKEYS
    cb = r2.shape[-1]
    g2_a, g2_s = _seq_operand(g2, seq_len, tm)

    def row():
        return pl.BlockSpec((tm, d), lambda i, j: (i, 0))

    return pl.pallas_call(
        functools.partial(_peer_main_kernel, ni1),
        out_shape=jax.ShapeDtypeStruct((n, d), F32),
        grid=(n // tm, N_EXPERTS // et),
        in_specs=[row(), row(), g2_s,
                  pl.BlockSpec((et, d), lambda i, j: (j, 0)),
                  pl.BlockSpec((d, et), lambda i, j: (0, j)),
                  pl.BlockSpec((PEER_HEADS, ni1, tm), lambda i, j: (0, j, i)),
                  pl.BlockSpec((PEER_HEADS, ni1, tm), lambda i, j: (0, j, i)),
                  pl.BlockSpec((PEER_HEADS, tm // cb, PEER_NKEYS, cb), lambda i, j: (0, i, 0, 0)),
                  pl.BlockSpec((PEER_HEADS, tm // cb, PEER_NKEYS, cb), lambda i, j: (0, i, 0, 0))],
        out_specs=row(),
        scratch_shapes=[pltpu.VMEM((d, tm), F32), pltpu.VMEM((tm // cb, et, cb), BF16)],
        compiler_params=_cparams(("parallel", "arbitrary")),
    )(h2, x1, g2_a, u16, vt16, g, cnt, r2, p2)


def _layer(x, mod, pos, shift_prev, s0, cache, lw):
    nb, t, d = x.shape
    n = nb * t
    sh1, sc1, g1, sh2, sc2, g2 = [mod[:, i * d:(i + 1) * d] for i in range(6)]
    x2 = x.reshape(n, d)
    P = _inproj(x2, sc1, sh1, lw['norm1_w'], lw['w_in16'], t)

    prev = _pack_rw(shift_prev).reshape(nb, 1, P_COLS)
    o_a, zf = _rwkv(P, nb, t, prev, lw['mu'], lw['w0'], lw['a0'], lw['k_k'], lw['k_a'], lw['r_k'], lw['lnx_w'],
                    lw['lnx_b'], lw['wup'], lw['aup'], lw['gup'], _state_to_pairs(s0))
    wkv = _pairs_to_state(zf)
    shift_last = _unpack_rw(P.reshape(nb, t, P_COLS)[:, -1, :])

    q16, k32, k16, v32, v16, qi16, kw32, ki2 = _dsa_prep(P, jnp.tile(pos, nb), lw['q_norm_w'], lw['k_norm_w'])
    if cache is None:
        o_b = _attn_prompt(q16, qi16, kw32, k16, v16, ki2, nb, t)
    else:
        ck, cv, cki = cache
        past = ck.shape[1]
        o_b = _attn_sample(q16, qi16, kw32, k16, v16, ki2, ck.reshape(nb, past, d), cv.reshape(nb, past, d), cki,
                           nb, t)

    x1, h2 = _merge(x2, o_a, o_b, P, lw['b_gate'], g1, sc2, sh2, lw['norm2_w'], lw['wpa'], lw['wpb'], lw['wout'], t)
    g, cnt, r2, p2 = _peer_select(h2, lw['wpqt'], lw['kbd'])
    y = _peer_main(h2, x1, g2, lw['u16'], lw['vt16'], g, cnt, r2, p2, t)

    k_new = k32.reshape(nb, t, N_HEADS, HEAD_DIM)
    v_new = v32.reshape(nb, t, N_HEADS, HEAD_DIM)
    ki_new = kw32[:, :IDX_DIM].reshape(nb, t, IDX_DIM)
    return y.reshape(nb, t, d), wkv, shift_last, k_new, v_new, ki_new


def _layer_weights(l, w_in, b_gate, mu_rw, w0, w_up, a0, a_up, g_up, k_k, k_a, r_k, lnx_w, lnx_b, q_norm_w, k_norm_w,
                   w_proj_a, w_proj_b, w_out, norm1_w, norm2_w, w_pq, peer_keys, peer_u, peer_v):
    d = D_MODEL
    zeros = lambda r: jnp.zeros((r, d), F32)
    keys = peer_keys[l].reshape(2 * PEER_HEADS, PEER_NKEYS, PEER_DHALF)
    eye = jnp.eye(2 * PEER_HEADS, dtype=F32)
    kbd = (eye[:, None, :, None] * keys[:, :, None, :]).reshape(2 * d, d)
    return {
        'w_in16': _pack_in(w_in[l]).astype(BF16), 'b_gate': b_gate[l], 'mu': _pack_rw(mu_rw[l]).reshape(1, P_COLS),
        'w0': w0[l], 'a0': a0[l], 'k_k': k_k[l], 'k_a': k_a[l], 'r_k': r_k[l].reshape(d), 'lnx_w': lnx_w[l],
        'lnx_b': lnx_b[l],
        'wup': jnp.concatenate([w_up[l], zeros(LANES - D_DECAY)], axis=0).astype(BF16),
        'aup': jnp.concatenate([zeros(D_DECAY), a_up[l]], axis=0).astype(BF16),
        'gup': jnp.concatenate([g_up[l], zeros(256 - D_GATE)], axis=0).astype(BF16),
        'q_norm_w': q_norm_w[l], 'k_norm_w': k_norm_w[l], 'norm1_w': norm1_w[l], 'norm2_w': norm2_w[l],
        'wpa': w_proj_a[l].astype(BF16), 'wpb': w_proj_b[l].astype(BF16), 'wout': w_out[l].astype(BF16),
        'wpqt': w_pq[l].T.astype(BF16), 'kbd': kbd.astype(BF16),
        'u16': peer_u[l].astype(BF16), 'vt16': peer_v[l].T.astype(BF16),
    }


def kernel(x_prompt, x_sample, c_prompt, c_sample, cache_k, cache_v, cache_kidx, state_wkv, state_shift, w_ada, b_ada,
           norm1_w, w_in, b_gate, mu_rw, w0, w_up, a0, a_up, g_up, k_k, k_a, r_k, lnx_w, lnx_b, q_norm_w, k_norm_w,
           w_proj_a, w_proj_b, w_out, norm2_w, w_pq, peer_keys, peer_u, peer_v):
    depth = w_in.shape[0]
    bp, tp = x_prompt.shape[:2]
    bs, ts = x_sample.shape[:2]
    past = cache_k.shape[2]
    dt = x_prompt.dtype
    pos_p = jnp.arange(tp, dtype=I32)
    pos_s = past + jnp.arange(ts, dtype=I32)
    zero_shift = jnp.zeros((bp, RW_IN), dt)
    zero_wkv = jnp.zeros((bp, N_HEADS, HEAD_DIM, HEAD_DIM), dt)
    c_all = jnp.concatenate([c_prompt, c_sample], axis=0)
    xp, xs = x_prompt, x_sample
    outs_p, outs_s = [], []
    for l in range(depth):
        lw = _layer_weights(l, w_in, b_gate, mu_rw, w0, w_up, a0, a_up, g_up, k_k, k_a, r_k, lnx_w, lnx_b, q_norm_w,
                            k_norm_w, w_proj_a, w_proj_b, w_out, norm1_w, norm2_w, w_pq, peer_keys, peer_u, peer_v)
        mod = _ada(c_all, w_ada[l], b_ada[l])
        xp, *rest_p = _layer(xp, mod[:bp], pos_p, zero_shift, zero_wkv, None, lw)
        xs, *rest_s = _layer(xs, mod[bp:], pos_s, state_shift[l], state_wkv[l],
                             (cache_k[l], cache_v[l], cache_kidx[l]), lw)
        outs_p.append(rest_p)
        outs_s.append(rest_s)
    stack = lambda outs, i: jnp.stack([o[i] for o in outs])
    return (xp, xs,
            stack(outs_p, 0), stack(outs_p, 1), stack(outs_p, 2), stack(outs_p, 3), stack(outs_p, 4),
            stack(outs_s, 0), stack(outs_s, 1), stack(outs_s, 2), stack(outs_s, 3), stack(outs_s, 4))
```

```python
import functools

import jax
import jax.numpy as jnp
from jax import lax
from jax.experimental import pallas as pl
from jax.experimental.pallas import tpu as pltpu

F32 = jnp.float32
BF16 = jnp.bfloat16
I32 = jnp.int32

LANES = 128
D_MODEL = 1024
EPS = 1e-6
GN_EPS = 64e-5
ROPE_THETA = 10000.0
CHUNK = 64
TOPK_MAX = 256
HEAD_DIM = 64
N_HEADS = D_MODEL // HEAD_DIM
N_PAIRS = N_HEADS // 2
IDX_HEADS = 8
IDX_DIM = 64
D_DECAY = 64
D_AAA = 64
D_GATE = 160
RW_IN = 3 * D_MODEL + D_DECAY + D_AAA + D_GATE
PEER_HEADS = 8
PEER_NKEYS = 128
PEER_TOPK = 16
PEER_DHALF = 64
N_EXPERTS = PEER_NKEYS * PEER_NKEYS
RW_CHUNK = 64
RW_INTERLEAVE = 16
RW_PAIRS_LONG = 2
RW_PASSES = (2, 1, 1, 1, 1)
VMEM_LIMIT = 56 * 1024 * 1024
LOG2E = 1.4426950408889634
PEER_CAND = 80

C_R, C_K, C_V = 0, 1024, 2048
C_Q, C_KD, C_VD = 3072, 4096, 5120
C_GA, C_GB = 6144, 7168
C_QI = 8192
C_G = 8704
C_M = 8960
C_KW = 9088
P_COLS = 9216
IN_W = 9064

NT = (((1,), (1,)), ((), ()))
NN = (((1,), (0,)), ((), ()))


def _pack_in(w):
    z = lambda k: jnp.zeros(w.shape[:-1] + (k,), w.dtype)
    return jnp.concatenate([w[..., 0:3072], w[..., 3360:6432], w[..., 7016:9064], w[..., 6432:6944],
                            w[..., 3200:3360], z(256 - D_GATE), w[..., 3072:3200],
                            w[..., 6944:7016], z(LANES - IDX_DIM - IDX_HEADS)], axis=-1)


def _pack_rw(a):
    return _pack_in(jnp.concatenate([a, jnp.zeros(a.shape[:-1] + (IN_W - RW_IN,), a.dtype)], axis=-1))


def _unpack_rw(p):
    return jnp.concatenate([p[..., :3072], p[..., C_M:C_M + 128], p[..., C_G:C_G + D_GATE]], axis=-1)


def _split_bf16(x, n):
    parts = []
    r = x
    for _ in range(n):
        p = r.astype(BF16)
        parts.append(p)
        r = r - p.astype(F32)
    return parts


def _mm(a, b, pa=1, pb=1, dims=NN):
    aps = _split_bf16(a, pa) if a.dtype != BF16 else [a]
    bps = _split_bf16(b, pb) if b.dtype != BF16 else [b]
    order = max(len(aps), len(bps))
    out = None
    for i, ap in enumerate(aps):
        for j, bp in enumerate(bps):
            if i + j >= order:
                continue
            t = lax.dot_general(ap, bp, dims, preferred_element_type=F32)
            out = t if out is None else out + t
    return out


def _sigmoid(x):
    return 1.0 / (1.0 + jnp.exp(-x))


def _softplus(z):
    return jnp.maximum(z, 0.0) + jnp.log(1.0 + jnp.exp(-jnp.abs(z)))


def _cparams(sem):
    return pltpu.CompilerParams(dimension_semantics=sem, vmem_limit_bytes=VMEM_LIMIT)


def _ada_kernel(c_ref, w_ref, b_ref, o_ref):
    c = c_ref[...]
    s = c * _sigmoid(c)
    o_ref[...] = _mm(s, w_ref[...], 2, 2) + b_ref[...]


def _ada(c, w, b):
    m, d = c.shape
    n = w.shape[1]
    tn = 1024
    return pl.pallas_call(
        _ada_kernel,
        out_shape=jax.ShapeDtypeStruct((m, n), F32),
        grid=(n // tn,),
        in_specs=[pl.BlockSpec((m, d), lambda j: (0, 0)),
                  pl.BlockSpec((d, tn), lambda j: (0, j)),
                  pl.BlockSpec((1, tn), lambda j: (0, j))],
        out_specs=pl.BlockSpec((m, tn), lambda j: (0, j)),
        compiler_params=_cparams(("arbitrary",)),
    )(c, w, b.reshape(1, n))


def _seq_operand(vec, seq_len, tm):
    b, d = vec.shape
    if seq_len % tm == 0:
        per = seq_len // tm
        arr = vec.reshape(b, 1, d)
        spec = pl.BlockSpec((None, 1, d), lambda *g: (g[0] // per, 0, 0))
    else:
        assert tm % seq_len == 0
        arr = jnp.repeat(vec, seq_len, axis=0)
        spec = pl.BlockSpec((tm, d), lambda *g: (g[0], 0))
    return arr, spec


def _row_tile(n, seq_len, cap):
    tm = min(cap, n)
    while n % tm or (seq_len % tm and tm % seq_len):
        tm //= 2
    return tm


def _inproj_kernel(x_ref, sc_ref, sh_ref, nw_ref, w_ref, o_ref, h_scr):
    @pl.when(pl.program_id(1) == 0)
    def _():
        x = x_ref[...]
        y = x * lax.rsqrt(jnp.mean(x * x, axis=-1, keepdims=True) + EPS) * nw_ref[...]
        h_scr[...] = (y * (1.0 + sc_ref[...]) + sh_ref[...]).astype(BF16)

    o_ref[...] = jnp.dot(h_scr[...], w_ref[...], preferred_element_type=F32)


def _inproj(x2, sc, sh, nw, w16, seq_len):
    n, d = x2.shape
    tm = _row_tile(n, seq_len, 1024)
    tn = P_COLS // 4
    sc_a, sc_s = _seq_operand(sc, seq_len, tm)
    sh_a, sh_s = _seq_operand(sh, seq_len, tm)
    return pl.pallas_call(
        _inproj_kernel,
        out_shape=jax.ShapeDtypeStruct((n, P_COLS), F32),
        grid=(n // tm, P_COLS // tn),
        in_specs=[pl.BlockSpec((tm, d), lambda i, j: (i, 0)), sc_s, sh_s,
                  pl.BlockSpec((1, d), lambda i, j: (0, 0)),
                  pl.BlockSpec((d, tn), lambda i, j: (0, j))],
        out_specs=pl.BlockSpec((tm, tn), lambda i, j: (i, j)),
        scratch_shapes=[pltpu.VMEM((tm, d), BF16)],
        compiler_params=_cparams(("parallel", "arbitrary")),
    )(x2, sc_a, sh_a, nw.reshape(1, d), w16)


def _lane_lo(shape):
    return lax.broadcasted_iota(I32, shape, len(shape) - 1) < HEAD_DIM


def _pair_sum(x):
    lo = _lane_lo(x.shape)
    s0 = jnp.sum(jnp.where(lo, x, 0.0), axis=-1, keepdims=True)
    s1 = jnp.sum(jnp.where(lo, 0.0, x), axis=-1, keepdims=True)
    return jnp.where(lo, s0, s1)


def _stack2(x):
    lo = _lane_lo(x.shape)
    return jnp.concatenate([jnp.where(lo, x, 0.0), jnp.where(lo, 0.0, x)], axis=0)


def _rwkv_kernel(t_real, npair, nchunk, pr, pk, pv, pg, pm, sr, sk, sv, sg, sm, mr, mk, mv, mg, mmu,
                 w0, a0, kkw, kaw, rkw, lnw, lnb, wup, aup, gup, z0, o_ref, zf_ref,
                 r_s, lw_s, k_s, v_s, a_s, b_s, y_s, bonus_s, g_s):
    C = RW_CHUNK
    t_pad = r_s.shape[1]

    def mix(p_ref, s_ref, m_ref):
        p = p_ref[...]
        prev = pltpu.roll(p, 1, 0)
        row = lax.broadcasted_iota(I32, p.shape, 0)
        prev = jnp.where(row == 0, s_ref[...], prev)
        return p + (prev - p) * m_ref[...]

    xg, xm = mix(pg, sg, mg), mix(pm, sm, mmu)
    th16, xm16, sg16 = jnp.tanh(xm).astype(BF16), xm.astype(BF16), _sigmoid(xg).astype(BF16)
    xr_all, xk_all, xv_all = mix(pr, sr, mr), mix(pk, sk, mk), mix(pv, sv, mv)

    def put(ref, pp, val):
        if t_pad > t_real:
            val = jnp.concatenate([val, jnp.zeros((t_pad - t_real, LANES), F32)], axis=0)
        ref[pp] = val

    for pp in range(npair):
        cs = slice(pp * LANES, (pp + 1) * LANES)
        xr, xk, xv = xr_all[:, cs], xk_all[:, cs], xv_all[:, cs]
        dw = jnp.dot(th16, wup[:, cs], preferred_element_type=F32)
        lw = -jnp.exp(-_softplus(-(w0[:, cs] + dw)) - 0.5)
        asig = _sigmoid(a0[:, cs] + jnp.dot(xm16, aup[:, cs], preferred_element_type=F32))
        g_s[pp] = jnp.dot(sg16, gup[:, cs], preferred_element_type=F32)
        kk = xk * kkw[:, cs]
        kk = kk * lax.rsqrt(_pair_sum(kk * kk) + 1e-12)
        kmod = xk * (1.0 + (asig - 1.0) * kaw[:, cs])
        bonus_s[pp] = _pair_sum(xr * kmod * rkw[:, cs]) * xv
        put(r_s, pp, xr)
        put(lw_s, pp, lw)
        put(k_s, pp, kmod)
        put(v_s, pp, xv)
        put(a_s, pp, -kk)
        put(b_s, pp, kk * asig)

    n2 = 2 * C
    ri = lax.broadcasted_iota(I32, (n2, n2), 0)
    ci = lax.broadcasted_iota(I32, (n2, n2), 1)
    same = (ri // C) == (ci // C)
    strict = same & ((ri % C) > (ci % C))
    incl = same & ((ri % C) >= (ci % C))
    eye = ri == ci
    eye_f = jnp.where(eye, 1.0, 0.0)
    tri = jnp.where(lax.broadcasted_iota(I32, (C, C), 0) >= lax.broadcasted_iota(I32, (C, C), 1), 1.0, 0.0
                    ).astype(BF16)
    zeros_sq = jnp.zeros((n2, LANES), F32)

    pc_, pg_, pi_, po_, ps_ = RW_PASSES

    def local(chains):
        each = lambda f, *cols: [f(*xs) for xs in zip(*cols)]
        lwc = [lw_s[pp, sl, :] for sl, pp in chains]
        cum = each(lambda l: _mm(tri, l, 1, pc_), lwc)
        cum_last = each(lambda c: c[C - 1:C, :], cum)
        ec, eci = each(jnp.exp, cum), each(lambda c: jnp.exp(-c), cum)
        ecp = each(lambda c, l: jnp.exp(c - l), cum, lwc)
        ecl = each(lambda c, cl: jnp.exp(cl - c), cum, cum_last)
        a_c = [a_s[pp, sl, :] for sl, pp in chains]
        b_c = [b_s[pp, sl, :] for sl, pp in chains]
        k_c = [k_s[pp, sl, :] for sl, pp in chains]
        r_c = [r_s[pp, sl, :] for sl, pp in chains]
        As = each(lambda a, e: _stack2(a * e), a_c, ecp)
        Rs = each(lambda r, e: _stack2(r * e), r_c, ec)
        Bs = each(lambda b, e: _stack2(b * e), b_c, eci)
        Ks = each(lambda k, e: _stack2(k * e), k_c, eci)
        Bt = each(lambda b, e: _stack2(b * e), b_c, ecl)
        Kt = each(lambda k, e: _stack2(k * e), k_c, ecl)
        Vs = [_stack2(v_s[pp, sl, :]) for sl, pp in chains]

        G = each(lambda a, r, b, k: _mm(jnp.concatenate([a, r], axis=0), jnp.concatenate([b, k], axis=0),
                                        pg_, pg_, NT), As, Rs, Bs, Ks)
        a_ab = each(lambda g: jnp.where(strict, g[:n2, :n2], 0.0), G)
        a_ak = each(lambda g: jnp.where(strict, g[:n2, n2:], 0.0), G)
        a_rb = each(lambda g: jnp.where(incl, g[n2:, :n2], 0.0), G)
        a_rk = each(lambda g: jnp.where(incl, g[n2:, n2:], 0.0), G)

        lp = a_ab
        tm_ = each(lambda a: eye_f + a, a_ab)
        step = 2
        while step < C:
            lp = each(lambda l: _mm(l, l, pi_, pi_), lp)
            tm_ = each(lambda t, l: t + _mm(t, l, pi_, pi_), tm_, lp)
            step *= 2

        w1 = each(lambda a, v: _mm(a, v, po_, po_), a_ak, Vs)
        mu_ = each(lambda t, a, w: _mm(t, jnp.concatenate([a, w], axis=1), po_, po_), tm_, As, w1)
        rhs = each(lambda m, v: jnp.concatenate([m, jnp.concatenate([zeros_sq, v], axis=1)], axis=0), mu_, Vs)
        lhs = each(lambda rb, rk, b, k: jnp.concatenate([jnp.concatenate([rb, rk], axis=1),
                                                         jnp.concatenate([b.T, k.T], axis=1)], axis=0),
                   a_rb, a_rk, Bt, Kt)
        out2 = each(lambda l, r: _mm(l, r, po_, po_), lhs, rhs)
        m23 = each(lambda r, o, cl: jnp.concatenate([r + o[:n2, :LANES],
                                                     jnp.where(eye, jnp.exp(cl), 0.0) + o[n2:, :LANES]], axis=0),
                   Rs, out2, cum_last)
        return [(m, o[:n2, LANES:], o[n2:, LANES:]) for m, o in zip(m23, out2)]

    def step_chunks(i, zs):
        sls = [pl.ds(pl.multiple_of((i * nchunk + j) * C, C), C) for j in range(nchunk)]
        parts = local([(sl, pp) for sl in sls for pp in range(npair)])
        zs = list(zs)
        for j, sl in enumerate(sls):
            for pp in range(npair):
                m23, y_loc, z_loc = parts[j * npair + pp]
                yz = _mm(m23, zs[pp], ps_, ps_)
                y = yz[:n2] + y_loc
                y_s[pp, sl, :] = y[:C] + y[C:]
                zs[pp] = yz[n2:] + z_loc
        return tuple(zs)

    zs = lax.fori_loop(0, t_pad // (C * nchunk), step_chunks, tuple(z0[pp] for pp in range(npair)))
    for pp in range(npair):
        zf_ref[pp] = zs[pp]
        cs = slice(pp * LANES, (pp + 1) * LANES)
        y = y_s[pp, 0:t_real, :]
        mean = _pair_sum(y) * (1.0 / HEAD_DIM)
        dlt = y - mean
        var = _pair_sum(dlt * dlt) * (1.0 / HEAD_DIM)
        yn = dlt * lax.rsqrt(var + GN_EPS) * lnw[:, cs] + lnb[:, cs]
        o_ref[:, cs] = ((yn + bonus_s[pp]) * g_s[pp]).astype(o_ref.dtype)


def _rwkv(P, nb, t, prev, mu, w0, a0, k_k, k_a, r_k, lnx_w, lnx_b, wup, aup, gup, z0):
    t_pad = max(t, RW_CHUNK)
    assert t % 8 == 0 and t_pad % RW_CHUNK == 0
    n_chunks = t_pad // RW_CHUNK
    nchunk = min(RW_INTERLEAVE // RW_PAIRS_LONG, n_chunks)
    npair = min(N_PAIRS, max(1, RW_INTERLEAVE // nchunk))
    wp = npair * LANES

    def cblk(c0, w, per_pair):
        return (lambda p: c0 // w + p) if per_pair else (lambda p: c0 // w)

    def pcol(c0, w, pp):
        f = cblk(c0, w, pp)
        return pl.BlockSpec((t, w), lambda b, p: (b, f(p)))

    def prevcol(c0, w, pp):
        f = cblk(c0, w, pp)
        return pl.BlockSpec((None, 1, w), lambda b, p: (b, 0, f(p)))

    def mucol(c0, w, pp):
        f = cblk(c0, w, pp)
        return pl.BlockSpec((1, w), lambda b, p: (0, f(p)))

    def hvec():
        return pl.BlockSpec((1, wp), lambda b, p: (0, p))

    cols = [(C_R, wp, True), (C_K, wp, True), (C_V, wp, True), (C_G, 256, False), (C_M, LANES, False)]
    in_specs = ([pcol(*c) for c in cols] + [prevcol(*c) for c in cols] + [mucol(*c) for c in cols]
                + [hvec() for _ in range(7)]
                + [pl.BlockSpec((LANES, wp), lambda b, p: (0, p)),
                   pl.BlockSpec((LANES, wp), lambda b, p: (0, p)),
                   pl.BlockSpec((256, wp), lambda b, p: (0, p)),
                   pl.BlockSpec((None, npair, LANES, LANES), lambda b, p: (b, p, 0, 0))])
    vecs = [v.reshape(1, D_MODEL) for v in (w0, a0, k_k, k_a, r_k, lnx_w, lnx_b)]
    o, zf = pl.pallas_call(
        functools.partial(_rwkv_kernel, t, npair, nchunk),
        out_shape=(jax.ShapeDtypeStruct((nb * t, D_MODEL), BF16),
                   jax.ShapeDtypeStruct((nb, N_PAIRS, LANES, LANES), F32)),
        grid=(nb, N_PAIRS // npair),
        in_specs=in_specs,
        out_specs=(pl.BlockSpec((t, wp), lambda b, p: (b, p)),
                   pl.BlockSpec((None, npair, LANES, LANES), lambda b, p: (b, p, 0, 0))),
        scratch_shapes=([pltpu.VMEM((npair, t_pad, LANES), F32) for _ in range(7)]
                        + [pltpu.VMEM((npair, t, LANES), F32) for _ in range(2)]),
        compiler_params=_cparams(("parallel", "arbitrary")),
    )(P, P, P, P, P, prev, prev, prev, prev, prev, mu, mu, mu, mu, mu, *vecs, wup, aup, gup, z0)
    return o, zf


def _state_to_pairs(s):
    nb = s.shape[0]
    zt = jnp.swapaxes(s, -1, -2).reshape(nb, N_PAIRS, 2, HEAD_DIM, HEAD_DIM)
    zero = jnp.zeros_like(zt[:, :, 0])
    top = jnp.concatenate([zt[:, :, 0], zero], axis=-1)
    bot = jnp.concatenate([zero, zt[:, :, 1]], axis=-1)
    return jnp.concatenate([top, bot], axis=-2)


def _pairs_to_state(z):
    nb = z.shape[0]
    h0 = z[:, :, :HEAD_DIM, :HEAD_DIM]
    h1 = z[:, :, HEAD_DIM:, HEAD_DIM:]
    s = jnp.stack([h0, h1], axis=2).reshape(nb, N_HEADS, HEAD_DIM, HEAD_DIM)
    return jnp.swapaxes(s, -1, -2)


def _rope(x, cos, sin_signed):
    w = x.shape[1]
    reps = w // LANES
    cw = jnp.concatenate([cos] * reps, axis=1) if reps > 1 else cos
    sw = jnp.concatenate([sin_signed] * reps, axis=1) if reps > 1 else sin_signed
    lane = lax.broadcasted_iota(I32, x.shape, 1)
    fwd = pltpu.roll(x, w - 32, 1)
    bwd = pltpu.roll(x, 32, 1)
    partner = jnp.where((lane % HEAD_DIM) < 32, fwd, bwd)
    return x * cw + partner * sw


def _head_rms(x, nw, e_dn, e_up):
    ms = _mm(x * x, e_dn, 2, 1) * (1.0 / HEAD_DIM)
    r = lax.rsqrt(ms + EPS)
    return x * _mm(r, e_up, 2, 1) * nw


def _dsa_prep_kernel(pq, pkd, pvd, pqi, pkw, cos_ref, sin_ref, qn, kn, edn, eup,
                     q16, k32, k16, v32, v16, qi16, kw32, ki2):
    cos, sin = cos_ref[...], sin_ref[...]
    e_dn, e_up = edn[...], eup[...]
    def put_pairs(ref, x):
        for p in range(N_PAIRS):
            ref[p] = x[:, p * LANES:(p + 1) * LANES].astype(ref.dtype)

    q = _rope(_head_rms(pq[...], qn[...], e_dn, e_up), cos, sin)
    put_pairs(q16, q * (HEAD_DIM ** -0.5 * LOG2E))
    k = _rope(_head_rms(pkd[...], kn[...], e_dn, e_up), cos, sin)
    k32[...] = k
    put_pairs(k16, k)
    v = pvd[...]
    v32[...] = v
    put_pairs(v16, v)
    qi16[...] = _rope(pqi[...], cos, sin).astype(BF16)
    kw = pkw[...]
    lane = lax.broadcasted_iota(I32, kw.shape, 1)
    wi_scale = (IDX_HEADS * IDX_DIM) ** -0.5
    kr = _rope(kw, cos, sin)
    kw32[...] = jnp.where(lane < IDX_DIM, kr, jnp.where(lane < IDX_DIM + IDX_HEADS, kw * wi_scale, 0.0))
    ki2[...] = jnp.where(lane < IDX_DIM, kr, pltpu.roll(kr, IDX_DIM, 1)).astype(BF16)


def _dsa_prep(P, pos_rows, q_norm_w, k_norm_w):
    n = P.shape[0]
    tm = 512 if n % 512 == 0 else n
    half = HEAD_DIM // 2
    inv = ROPE_THETA ** (-jnp.arange(half, dtype=F32) / half)
    ang = pos_rows.astype(F32)[:, None] * inv[None, :]
    cos = jnp.tile(jnp.cos(ang), (1, 4))
    sin = jnp.sin(ang)
    sin_signed = jnp.tile(jnp.concatenate([-sin, sin], axis=1), (1, 2))
    head_of = jnp.arange(D_MODEL) // HEAD_DIM
    e_dn = (head_of[:, None] == jnp.arange(LANES)[None, :]).astype(BF16)
    e_up = e_dn.T
    qn = jnp.tile(q_norm_w, N_HEADS).reshape(1, D_MODEL)
    kn = jnp.tile(k_norm_w, N_HEADS).reshape(1, D_MODEL)

    def col(c0, w):
        return pl.BlockSpec((tm, w), lambda i, c0=c0, w=w: (i, c0 // w))

    def row(w):
        return pl.BlockSpec((tm, w), lambda i: (i, 0))

    def const(shape):
        return pl.BlockSpec(shape, lambda i: (0, 0))

    pairs = jax.ShapeDtypeStruct((N_PAIRS, n, LANES), BF16)
    pair_spec = pl.BlockSpec((N_PAIRS, tm, LANES), lambda i: (0, i, 0))
    return pl.pallas_call(
        _dsa_prep_kernel,
        out_shape=(pairs, jax.ShapeDtypeStruct((n, D_MODEL), F32), pairs, jax.ShapeDtypeStruct((n, D_MODEL), F32),
                   pairs, jax.ShapeDtypeStruct((n, IDX_HEADS * IDX_DIM), BF16),
                   jax.ShapeDtypeStruct((n, LANES), F32), jax.ShapeDtypeStruct((n, LANES), BF16)),
        grid=(n // tm,),
        in_specs=[col(C_Q, 1024), col(C_KD, 1024), col(C_VD, 1024), col(C_QI, 512), col(C_KW, LANES),
                  row(LANES), row(LANES), const((1, D_MODEL)), const((1, D_MODEL)),
                  const((D_MODEL, LANES)), const((LANES, D_MODEL))],
        out_specs=(pair_spec, row(D_MODEL), pair_spec, row(D_MODEL), pair_spec, row(512), row(LANES), row(LANES)),
        compiler_params=_cparams(("parallel",)),
    )(P, P, P, P, P, cos, sin_signed, qn, kn, e_dn, e_up)


CODE_NEG_INF = -1 - 0x7F800000
BITS_PER_CHECK = 2
ATTN_TQ = 256
ATTN_PAIRS = 2
ATTN_CASES = 4
SAMPLE_PAIRS = 4


def _index_scores(qi, wi, ki_list):
    outs = []
    for ki in ki_list:
        acc = None
        for h in range(IDX_HEADS):
            qpair = qi[:, (h // 2) * LANES:(h // 2 + 1) * LANES]
            lo = _lane_lo(qpair.shape)
            qh = jnp.where(lo if h % 2 == 0 else jnp.logical_not(lo), qpair, jnp.zeros_like(qpair))
            rel = lax.dot_general(qh, ki, NT, preferred_element_type=F32)
            term = wi[:, IDX_DIM + h:IDX_DIM + h + 1] * jnp.maximum(rel, 0.0)
            acc = term if acc is None else acc + term
        outs.append(acc)
    return outs


def _select_topk(keys, topk, bias_refs):
    tq = keys[0].shape[0]
    neg = -jnp.inf

    def write(masks):
        for ref, k, msk in zip(bias_refs, keys, masks):
            ref[:, 0:k.shape[1]] = jnp.where(msk, 0.0, neg)

    def count(pred_list):
        tot = None
        for p in pred_list:
            c = jnp.sum(jnp.where(p, 1.0, 0.0), axis=-1, keepdims=True)
            tot = c if tot is None else tot + c
        return tot

    def threshold(c):
        bits = jnp.where(c >= 0, c, c ^ jnp.int32(0x7FFFFFFF))
        return jnp.where(c < jnp.int32(CODE_NEG_INF), neg, lax.bitcast_convert_type(bits, F32))

    few = count([k > neg for k in keys]) <= topk

    def pending(cnt):
        return jnp.max(jnp.where(few | (cnt == topk), 0.0, 1.0))

    def bit_step(state):
        i, c, cnt, _ = state
        for b in range(BITS_PER_CHECK):
            trial = c + jnp.left_shift(jnp.int32(1), 31 - (i + b))
            cnt_t = count([k >= threshold(trial) for k in keys])
            take = cnt_t >= topk
            cnt = jnp.where(take, cnt_t, cnt)
            c = jnp.where(take, trial, c)
        return i + BITS_PER_CHECK, c, cnt, pending(cnt)

    cnt0 = jnp.full((tq, 1), float(sum(k.shape[1] for k in keys)), F32)
    state = (jnp.int32(0), jnp.full((tq, 1), -2 ** 31, I32), cnt0, pending(cnt0))
    _, code, _, _ = lax.while_loop(lambda s: (s[0] < 32) & (s[3] > 0.0), bit_step, state)
    thr = threshold(code)
    ge = [(k >= thr) & (k > neg) for k in keys]
    write(ge)
    surplus = jnp.max(count(ge)) > topk

    @pl.when(surplus)
    def _():
        gt = [k > thr for k in keys]
        need = topk - count(gt)
        ties = [(k == thr) & (k > neg) for k in keys]
        offs, idx = 0, []
        for k in keys:
            idx.append(lax.broadcasted_iota(I32, k.shape, 1) + offs)
            offs += k.shape[1]
        nbits = max(1, (offs - 1).bit_length() + 1)

        def idx_step(i, m):
            trial = m + jnp.left_shift(jnp.int32(1), nbits - 1 - i)
            cnt = count([t & (ix < trial) for t, ix in zip(ties, idx)])
            return jnp.where(cnt <= need, trial, m)

        cut = lax.fori_loop(0, nbits, idx_step, jnp.zeros((tq, 1), I32))
        write([g | (t & (ix < cut)) for g, t, ix in zip(gt, ties, idx)])


def _attend_pairs(q_pairs, k_lists, v_lists, bias_list):
    lo = _lane_lo(q_pairs[0].shape)
    zero = jnp.zeros_like(q_pairs[0])
    heads = []
    for pi, q in enumerate(q_pairs):
        heads += [(jnp.where(lo, q, zero), pi), (jnp.where(lo, zero, q), pi)]
    s = [[lax.dot_general(qh, k, NT, preferred_element_type=F32) + b for k, b in zip(k_lists[pi], bias_list)]
         for qh, pi in heads]
    m = []
    for sh in s:
        mh = None
        for sj in sh:
            mx = jnp.max(sj, axis=-1, keepdims=True)
            mh = mx if mh is None else jnp.maximum(mh, mx)
        m.append(mh)
    p = [[jnp.exp2(sj - mh) for sj in sh] for sh, mh in zip(s, m)]
    den = [functools.reduce(lambda a, b: a + b, [jnp.sum(pj, axis=-1, keepdims=True) for pj in ph]) for ph in p]
    acc = [functools.reduce(lambda a, b: a + b,
                            [jnp.dot(pj.astype(BF16), v, preferred_element_type=F32)
                             for pj, v in zip(ph, v_lists[pi])])
           for ph, (_, pi) in zip(p, heads)]
    outs = [a / d for a, d in zip(acc, den)]
    return [jnp.where(lo, outs[2 * i], outs[2 * i + 1]) for i in range(len(q_pairs))]


def _attn_prompt_kernel(topk, ncase, q_ref, qi_ref, kw_ref, k_ref, v_ref, ki2_ref, o_ref, bias_s):
    tq = q_ref.shape[1]
    t = k_ref.shape[1]
    i = pl.program_id(1)
    lstep = t // ncase
    case = ((i + 1) * tq - 1) // lstep

    def run(L):
        score = _index_scores(qi_ref[...], kw_ref[...], [ki2_ref[0:L, :]])[0]
        qpos = i * tq + lax.broadcasted_iota(I32, (tq, L), 0)
        kpos = lax.broadcasted_iota(I32, (tq, L), 1)
        adm = (qpos // CHUNK) >= (kpos // CHUNK)
        _select_topk([jnp.where(adm, score, -jnp.inf)], topk, [bias_s])

        def pairs(g, carry):
            ps = [ATTN_PAIRS * g + j for j in range(ATTN_PAIRS)]
            outs = _attend_pairs([q_ref[p] for p in ps], [[k_ref[p, 0:L, :]] for p in ps],
                                 [[v_ref[p, 0:L, :]] for p in ps], [bias_s[:, 0:L]])
            for p, o in zip(ps, outs):
                o_ref[p] = o.astype(o_ref.dtype)
            return carry

        lax.fori_loop(0, N_PAIRS // ATTN_PAIRS, pairs, 0)

    for c in range(ncase):
        pl.when(case == c)(functools.partial(run, (c + 1) * lstep))


def _attn_prompt(q16, qi16, kw32, k16, v16, ki2, nb, t):
    tq = min(ATTN_TQ, t)
    topk = min(TOPK_MAX, t // 4)
    nq = t // tq
    ncase = min(ATTN_CASES, nq)

    def qrow(w):
        return pl.BlockSpec((tq, w), lambda b, i: (b * nq + i, 0))

    def qpairs():
        return pl.BlockSpec((N_PAIRS, tq, LANES), lambda b, i: (0, b * nq + i, 0))

    def kpairs():
        return pl.BlockSpec((N_PAIRS, t, LANES), lambda b, i: (0, b, 0))

    return pl.pallas_call(
        functools.partial(_attn_prompt_kernel, topk, ncase),
        out_shape=jax.ShapeDtypeStruct((N_PAIRS, nb * t, LANES), BF16),
        grid=(nb, nq),
        in_specs=[qpairs(), qrow(512), qrow(LANES), kpairs(), kpairs(),
                  pl.BlockSpec((t, LANES), lambda b, i: (b, 0))],
        out_specs=qpairs(),
        scratch_shapes=[pltpu.VMEM((tq, t), F32)],
        compiler_params=_cparams(("parallel", "arbitrary")),
    )(q16, qi16, kw32, k16, v16, ki2)


def _attn_sample_kernel(topk, past, q_ref, qi_ref, kw_ref, ck_ref, cv_ref, cki2_ref, k_ref, v_ref, ki2_ref, o_ref,
                        biasc_s, biasn_s):
    npairs, ts = q_ref.shape[0], q_ref.shape[1]

    @pl.when(pl.program_id(1) == 0)
    def _():
        sc, sn = _index_scores(qi_ref[...], kw_ref[...], [cki2_ref[...], ki2_ref[...]])
        qpos = past + lax.broadcasted_iota(I32, (ts, 1), 0)
        kpos_c = lax.broadcasted_iota(I32, sc.shape, 1)
        kpos_n = past + lax.broadcasted_iota(I32, sn.shape, 1)
        keys = [jnp.where((qpos // CHUNK) >= (kpos_c // CHUNK), sc, -jnp.inf),
                jnp.where((qpos // CHUNK) >= (kpos_n // CHUNK), sn, -jnp.inf)]
        _select_topk(keys, topk, [biasc_s, biasn_s])

    lanes = [slice(p * LANES, (p + 1) * LANES) for p in range(npairs)]
    outs = _attend_pairs([q_ref[p] for p in range(npairs)],
                         [[ck_ref[:, cs].astype(BF16), k_ref[p]] for p, cs in enumerate(lanes)],
                         [[cv_ref[:, cs].astype(BF16), v_ref[p]] for p, cs in enumerate(lanes)],
                         [biasc_s[...], biasn_s[...]])
    for p, o in enumerate(outs):
        o_ref[p] = o.astype(o_ref.dtype)


def _attn_sample(q16, qi16, kw32, k16, v16, ki2, cache_k, cache_v, cache_kidx, nb, ts):
    past = cache_k.shape[1]
    cki2 = jnp.concatenate([cache_kidx, cache_kidx], axis=-1).astype(BF16)
    topk = min(TOPK_MAX, (past + ts) // 4)

    def qrow(w):
        return pl.BlockSpec((ts, w), lambda b, p: (b, 0))

    sp = SAMPLE_PAIRS

    def qpair():
        return pl.BlockSpec((sp, ts, LANES), lambda b, p: (p, b, 0))

    def cache(pair):
        if pair:
            return pl.BlockSpec((None, past, sp * LANES), lambda b, p: (b, 0, p))
        return pl.BlockSpec((None, past, LANES), lambda b, p: (b, 0, 0))

    return pl.pallas_call(
        functools.partial(_attn_sample_kernel, topk, past),
        out_shape=jax.ShapeDtypeStruct((N_PAIRS, nb * ts, LANES), BF16),
        grid=(nb, N_PAIRS // sp),
        in_specs=[qpair(), qrow(512), qrow(LANES), cache(True), cache(True), cache(False),
                  qpair(), qpair(), qrow(LANES)],
        out_specs=qpair(),
        scratch_shapes=[pltpu.VMEM((ts, past), F32), pltpu.VMEM((ts, ts), F32)],
        compiler_params=_cparams(("parallel", "arbitrary")),
    )(q16, qi16, kw32, cache_k, cache_v, cki2, k16, v16, ki2)


def _merge_kernel(x_ref, oa_ref, ob_ref, pga_ref, pgb_ref, bga_ref, bgb_ref, g1_ref, sc2_ref, sh2_ref, nw_ref,
                  wpa_ref, wpb_ref, wout_ref, x1_ref, h2_ref):
    ga = _sigmoid(pga_ref[...] + bga_ref[...])
    gb = _sigmoid(pgb_ref[...] + bgb_ref[...])
    ob = jnp.concatenate([ob_ref[p] for p in range(N_PAIRS)], axis=1)
    m = (ga * jnp.dot(oa_ref[...], wpa_ref[...], preferred_element_type=F32)
         + gb * jnp.dot(ob, wpb_ref[...], preferred_element_type=F32))
    x1 = x_ref[...] + g1_ref[...] * jnp.dot(m.astype(BF16), wout_ref[...], preferred_element_type=F32)
    x1_ref[...] = x1
    y = x1 * lax.rsqrt(jnp.mean(x1 * x1, axis=-1, keepdims=True) + EPS) * nw_ref[...]
    h2_ref[...] = (y * (1.0 + sc2_ref[...]) + sh2_ref[...]).astype(BF16)


def _merge(x2, o_a, o_b, P, b_gate, g1, sc2, sh2, nw2, wpa, wpb, wout, seq_len):
    n, d = x2.shape
    tm = _row_tile(n, seq_len, 512)
    g1_a, g1_s = _seq_operand(g1, seq_len, tm)
    sc_a, sc_s = _seq_operand(sc2, seq_len, tm)
    sh_a, sh_s = _seq_operand(sh2, seq_len, tm)

    def row():
        return pl.BlockSpec((tm, d), lambda i: (i, 0))

    def const(shape):
        return pl.BlockSpec(shape, lambda i: (0, 0))

    bg = b_gate.reshape(1, 2 * d)
    return pl.pallas_call(
        _merge_kernel,
        out_shape=(jax.ShapeDtypeStruct((n, d), F32), jax.ShapeDtypeStruct((n, d), BF16)),
        grid=(n // tm,),
        in_specs=[row(), row(), pl.BlockSpec((N_PAIRS, tm, LANES), lambda i: (0, i, 0)),
                  pl.BlockSpec((tm, d), lambda i: (i, C_GA // d)), pl.BlockSpec((tm, d), lambda i: (i, C_GB // d)),
                  pl.BlockSpec((1, d), lambda i: (0, 0)), pl.BlockSpec((1, d), lambda i: (0, 1)),
                  g1_s, sc_s, sh_s, const((1, d)), const((d, d)), const((d, d)), const((d, d))],
        out_specs=(row(), row()),
        compiler_params=_cparams(("parallel",)),
    )(x2, o_a, o_b, P, P, bg, bg, g1_a, sc_a, sh_a, nw2.reshape(1, d), wpa, wpb, wout)


def _top_exact(s, k):
    rows = lax.broadcasted_iota(I32, s.shape, 0).astype(F32)
    cur = s
    rank = jnp.full(s.shape, float(k), F32)
    vals = []
    for r in range(k):
        m = jnp.max(cur, axis=0, keepdims=True)
        first = jnp.min(jnp.where(cur == m, rows, 1e9), axis=0, keepdims=True)
        hit = rows == first
        vals.append(m)
        rank = jnp.where(hit, float(r), rank)
        cur = jnp.where(hit, -jnp.inf, cur)
    return vals, rank


def _top_fast(ss, k):
    curs = list(ss)
    ranks = [jnp.full(s.shape, float(k), F32) for s in ss]
    vals = [[] for _ in ss]
    for r in range(k):
        ms = [jnp.max(c, axis=0, keepdims=True) for c in curs]
        hits = [c == m for c, m in zip(curs, ms)]
        ranks = [jnp.where(h, float(r), rk) for h, rk in zip(hits, ranks)]
        curs = [jnp.where(h, -jnp.inf, c) for h, c in zip(hits, curs)]
        for v, m in zip(vals, ms):
            v.append(m)
    cleans = [jnp.max(jnp.abs(jnp.sum(jnp.where(rk < k, 1.0, 0.0), axis=0, keepdims=True) - k)) == 0.0
              for rk in ranks]
    return vals, ranks, cleans


def _top(src_scr, k, vals_scr, rank_scr, redo_s):
    n = src_scr.shape[0]
    vals, ranks, cleans = _top_fast([src_scr[i] for i in range(n)], k)
    for i in range(n):
        vals_scr[i] = jnp.concatenate(vals[i], axis=0)
        rank_scr[i] = ranks[i]
        redo_s[i] = jnp.where(cleans[i], 0, 1).astype(I32)

    def redo(i, carry):
        @pl.when(redo_s[i] == 1)
        def _():
            vals_e, rank_e = _top_exact(src_scr[i], k)
            vals_scr[i] = jnp.concatenate(vals_e, axis=0)
            rank_scr[i] = rank_e

        return carry

    lax.fori_loop(0, n, redo, 0)


def _peer_sel_kernel(h_ref, wpqt_ref, kbd_ref, g_ref, cnt_ref, r2_ref, p2_ref, s_scr, vals_scr, rank_scr,
                     cand_scr, cvals_scr, crank_scr, redo_s):
    K = PEER_TOPK
    tm = h_ref.shape[0]
    qt = lax.dot_general(wpqt_ref[...], h_ref[...], NT, preferred_element_type=F32)
    s_scr[...] = jnp.dot(kbd_ref[...], qt.astype(BF16), preferred_element_type=F32
                         ).reshape(2 * PEER_HEADS, PEER_NKEYS, tm)
    sub8 = lax.broadcasted_iota(I32, (8, tm), 0)
    neg = jnp.full((8, tm), -jnp.inf, F32)
    _top(s_scr, K, vals_scr, rank_scr, redo_s)
    for hd in range(PEER_HEADS):
        c1, c2 = vals_scr[2 * hd], vals_scr[2 * hd + 1]
        blocks = [c1[0:1] + c2, c1[1:2] + c2[0:8]]
        for k1 in range(2, 8):
            blocks.append(jnp.where(sub8 < K // (k1 + 1), c1[k1:k1 + 1] + c2[0:8], neg))
        blocks.append(c1[8:16] + c2[0:1])
        cand_scr[hd] = jnp.concatenate(blocks, axis=0)
    _top(cand_scr, K, cvals_scr, crank_scr, redo_s)
    for hd in range(PEER_HEADS):
        s1, s2 = s_scr[2 * hd], s_scr[2 * hd + 1]
        c1, c2 = vals_scr[2 * hd], vals_scr[2 * hd + 1]
        rank1, rank2 = rank_scr[2 * hd], rank_scr[2 * hd + 1]
        cand = cand_scr[hd]
        taken = crank_scr[hd] < K
        z = jnp.sum(jnp.where(taken, jnp.exp(cand - (c1[0:1] + c2[0:1])), 0.0), axis=0, keepdims=True)
        tk = jnp.where(taken, 1.0, 0.0)
        per_k1 = [jnp.sum(tk[0:16], axis=0, keepdims=True)]
        per_k1 += [jnp.sum(tk[8 + 8 * k1:16 + 8 * k1], axis=0, keepdims=True) for k1 in range(1, 8)]
        cnt16 = jnp.concatenate(per_k1 + [tk[72:80]], axis=0)
        cnt = jnp.zeros(s1.shape, F32)
        for k1 in range(K):
            cnt = jnp.where(rank1 == float(k1), cnt16[k1:k1 + 1], cnt)
        g_ref[hd] = jnp.where(rank1 < K, jnp.exp(s1 - c1[0:1]) / z, 0.0)
        cnt_ref[hd] = cnt
        p2 = jnp.where(rank2 < K, jnp.exp(s2 - c2[0:1]), 0.0)
        cb = r2_ref.shape[-1]
        for tc in range(tm // cb):
            r2_ref[hd, tc] = rank2[:, tc * cb:(tc + 1) * cb].astype(r2_ref.dtype)
            p2_ref[hd, tc] = p2[:, tc * cb:(tc + 1) * cb].astype(p2_ref.dtype)


def _peer_select(h2, wpqt, kbd):
    n, d = h2.shape
    tm = 256 if n % 256 == 0 else n
    cb = min(LANES, tm)
    big = jax.ShapeDtypeStruct((PEER_HEADS, PEER_NKEYS, n), F32)
    blocked = jax.ShapeDtypeStruct((PEER_HEADS, n // cb, PEER_NKEYS, cb), BF16)

    def blk():
        return pl.BlockSpec((PEER_HEADS, PEER_NKEYS, tm), lambda i: (0, 0, i))

    def blk4():
        return pl.BlockSpec((PEER_HEADS, tm // cb, PEER_NKEYS, cb), lambda i: (0, i, 0, 0))

    return pl.pallas_call(
        _peer_sel_kernel,
        out_shape=(big, big, blocked, blocked),
        grid=(n // tm,),
        in_specs=[pl.BlockSpec((tm, d), lambda i: (i, 0)),
                  pl.BlockSpec((d, d), lambda i: (0, 0)),
                  pl.BlockSpec((2 * d, d), lambda i: (0, 0))],
        out_specs=(blk(), blk(), blk4(), blk4()),
        scratch_shapes=[pltpu.VMEM((2 * PEER_HEADS, PEER_NKEYS, tm), F32),
                        pltpu.VMEM((2 * PEER_HEADS, PEER_TOPK, tm), F32),
                        pltpu.VMEM((2 * PEER_HEADS, PEER_NKEYS, tm), F32),
                        pltpu.VMEM((PEER_HEADS, PEER_CAND, tm), F32),
                        pltpu.VMEM((PEER_HEADS, PEER_TOPK, tm), F32),
                        pltpu.VMEM((PEER_HEADS, PEER_CAND, tm), F32),
                        pltpu.SMEM((2 * PEER_HEADS,), I32)],
        compiler_params=_cparams(("parallel",)),
    )(h2, wpqt, kbd)


def _gelu_tanh(x):
    return 0.5 * x * (1.0 + jnp.tanh(0.7978845608028654 * (x + 0.044715 * (x * x * x))))


def _peer_main_kernel(ni1, h_ref, x1_ref, g2_ref, u_ref, vt_ref, g_ref, cnt_ref, r2_ref, p2_ref, y_ref, acc, gate_s):
    j = pl.program_id(1)
    tm = h_ref.shape[0]

    @pl.when(j == 0)
    def _():
        acc[...] = jnp.zeros_like(acc)

    cb = r2_ref.shape[-1]
    reps = PEER_NKEYS // 16
    zero = jnp.zeros((PEER_NKEYS, cb), BF16)
    for l in range(ni1):
        for tc in range(tm // cb):
            ts = slice(tc * cb, (tc + 1) * cb)
            w = None
            for hd in range(PEER_HEADS):
                c16 = jnp.broadcast_to(cnt_ref[hd, l:l + 1, ts], (16, cb)).astype(BF16)
                g16 = jnp.broadcast_to(g_ref[hd, l:l + 1, ts], (16, cb)).astype(BF16)
                t = (jnp.where(r2_ref[hd, tc] < jnp.concatenate([c16] * reps, axis=0), p2_ref[hd, tc], zero)
                     * jnp.concatenate([g16] * reps, axis=0))
                w = t if w is None else w + t
            gate_s[tc, l * PEER_NKEYS:(l + 1) * PEER_NKEYS, :] = w

    act = lax.dot_general(u_ref[...], h_ref[...], NT, preferred_element_type=F32)
    gate = jnp.concatenate([gate_s[tc] for tc in range(tm // cb)], axis=1)
    coef = gate * _gelu_tanh(act.astype(BF16))
    acc[...] += jnp.dot(vt_ref[...], coef, preferred_element_type=F32)

    @pl.when(j == pl.num_programs(1) - 1)
    def _():
        y_ref[...] = x1_ref[...] + g2_ref[...] * acc[...].T


def _peer_main(h2, x1, g2, u16, vt16, g, cnt, r2, p2, seq_len):
    n, d = h2.shape
    tm = _row_tile(n, seq_len, 512)
    ni1 = 16
    et = ni1 * PEER_NKEYS
    cb = r2.shape[-1]
    g2_a, g2_s = _seq_operand(g2, seq_len, tm)

    def row():
        return pl.BlockSpec((tm, d), lambda i, j: (i, 0))

    return pl.pallas_call(
        functools.partial(_peer_main_kernel, ni1),
        out_shape=jax.ShapeDtypeStruct((n, d), F32),
        grid=(n // tm, N_EXPERTS // et),
        in_specs=[row(), row(), g2_s,
                  pl.BlockSpec((et, d), lambda i, j: (j, 0)),
                  pl.BlockSpec((d, et), lambda i, j: (0, j)),
                  pl.BlockSpec((PEER_HEADS, ni1, tm), lambda i, j: (0, j, i)),
                  pl.BlockSpec((PEER_HEADS, ni1, tm), lambda i, j: (0, j, i)),
                  pl.BlockSpec((PEER_HEADS, tm // cb, PEER_NKEYS, cb), lambda i, j: (0, i, 0, 0)),
                  pl.BlockSpec((PEER_HEADS, tm // cb, PEER_NKEYS, cb), lambda i, j: (0, i, 0, 0))],
        out_specs=row(),
        scratch_shapes=[pltpu.VMEM((d, tm), F32), pltpu.VMEM((tm // cb, et, cb), BF16)],
        compiler_params=_cparams(("parallel", "arbitrary")),
    )(h2, x1, g2_a, u16, vt16, g, cnt, r2, p2)


def _layer(x, mod, pos, shift_prev, s0, cache, lw):
    nb, t, d = x.shape
    n = nb * t
    sh1, sc1, g1, sh2, sc2, g2 = [mod[:, i * d:(i + 1) * d] for i in range(6)]
    x2 = x.reshape(n, d)
    P = _inproj(x2, sc1, sh1, lw['norm1_w'], lw['w_in16'], t)

    prev = _pack_rw(shift_prev).reshape(nb, 1, P_COLS)
    o_a, zf = _rwkv(P, nb, t, prev, lw['mu'], lw['w0'], lw['a0'], lw['k_k'], lw['k_a'], lw['r_k'], lw['lnx_w'],
                    lw['lnx_b'], lw['wup'], lw['aup'], lw['gup'], _state_to_pairs(s0))
    wkv = _pairs_to_state(zf)
    shift_last = _unpack_rw(P.reshape(nb, t, P_COLS)[:, -1, :])

    q16, k32, k16, v32, v16, qi16, kw32, ki2 = _dsa_prep(P, jnp.tile(pos, nb), lw['q_norm_w'], lw['k_norm_w'])
    if cache is None:
        o_b = _attn_prompt(q16, qi16, kw32, k16, v16, ki2, nb, t)
    else:
        ck, cv, cki = cache
        past = ck.shape[1]
        o_b = _attn_sample(q16, qi16, kw32, k16, v16, ki2, ck.reshape(nb, past, d), cv.reshape(nb, past, d), cki,
                           nb, t)

    x1, h2 = _merge(x2, o_a, o_b, P, lw['b_gate'], g1, sc2, sh2, lw['norm2_w'], lw['wpa'], lw['wpb'], lw['wout'], t)
    g, cnt, r2, p2 = _peer_select(h2, lw['wpqt'], lw['kbd'])
    y = _peer_main(h2, x1, g2, lw['u16'], lw['vt16'], g, cnt, r2, p2, t)

    k_new = k32.reshape(nb, t, N_HEADS, HEAD_DIM)
    v_new = v32.reshape(nb, t, N_HEADS, HEAD_DIM)
    ki_new = kw32[:, :IDX_DIM].reshape(nb, t, IDX_DIM)
    return y.reshape(nb, t, d), wkv, shift_last, k_new, v_new, ki_new


def _layer_weights(l, w_in, b_gate, mu_rw, w0, w_up, a0, a_up, g_up, k_k, k_a, r_k, lnx_w, lnx_b, q_norm_w, k_norm_w,
                   w_proj_a, w_proj_b, w_out, norm1_w, norm2_w, w_pq, peer_keys, peer_u, peer_v):
    d = D_MODEL
    zeros = lambda r: jnp.zeros((r, d), F32)
    keys = peer_keys[l].reshape(2 * PEER_HEADS, PEER_NKEYS, PEER_DHALF)
    eye = jnp.eye(2 * PEER_HEADS, dtype=F32)
    kbd = (eye[:, None, :, None] * keys[:, :, None, :]).reshape(2 * d, d)
    return {
        'w_in16': _pack_in(w_in[l]).astype(BF16), 'b_gate': b_gate[l], 'mu': _pack_rw(mu_rw[l]).reshape(1, P_COLS),
        'w0': w0[l], 'a0': a0[l], 'k_k': k_k[l], 'k_a': k_a[l], 'r_k': r_k[l].reshape(d), 'lnx_w': lnx_w[l],
        'lnx_b': lnx_b[l],
        'wup': jnp.concatenate([w_up[l], zeros(LANES - D_DECAY)], axis=0).astype(BF16),
        'aup': jnp.concatenate([zeros(D_DECAY), a_up[l]], axis=0).astype(BF16),
        'gup': jnp.concatenate([g_up[l], zeros(256 - D_GATE)], axis=0).astype(BF16),
        'q_norm_w': q_norm_w[l], 'k_norm_w': k_norm_w[l], 'norm1_w': norm1_w[l], 'norm2_w': norm2_w[l],
        'wpa': w_proj_a[l].astype(BF16), 'wpb': w_proj_b[l].astype(BF16), 'wout': w_out[l].astype(BF16),
        'wpqt': w_pq[l].T.astype(BF16), 'kbd': kbd.astype(BF16),
        'u16': peer_u[l].astype(BF16), 'vt16': peer_v[l].T.astype(BF16),
    }


def kernel(x_prompt, x_sample, c_prompt, c_sample, cache_k, cache_v, cache_kidx, state_wkv, state_shift, w_ada, b_ada,
           norm1_w, w_in, b_gate, mu_rw, w0, w_up, a0, a_up, g_up, k_k, k_a, r_k, lnx_w, lnx_b, q_norm_w, k_norm_w,
           w_proj_a, w_proj_b, w_out, norm2_w, w_pq, peer_keys, peer_u, peer_v):
    depth = w_in.shape[0]
    bp, tp = x_prompt.shape[:2]
    bs, ts = x_sample.shape[:2]
    past = cache_k.shape[2]
    dt = x_prompt.dtype
    pos_p = jnp.arange(tp, dtype=I32)
    pos_s = past + jnp.arange(ts, dtype=I32)
    zero_shift = jnp.zeros((bp, RW_IN), dt)
    zero_wkv = jnp.zeros((bp, N_HEADS, HEAD_DIM, HEAD_DIM), dt)
    c_all = jnp.concatenate([c_prompt, c_sample], axis=0)
    xp, xs = x_prompt, x_sample
    outs_p, outs_s = [], []
    for l in range(depth):
        lw = _layer_weights(l, w_in, b_gate, mu_rw, w0, w_up, a0, a_up, g_up, k_k, k_a, r_k, lnx_w, lnx_b, q_norm_w,
                            k_norm_w, w_proj_a, w_proj_b, w_out, norm1_w, norm2_w, w_pq, peer_keys, peer_u, peer_v)
        mod = _ada(c_all, w_ada[l], b_ada[l])
        xp, *rest_p = _layer(xp, mod[:bp], pos_p, zero_shift, zero_wkv, None, lw)
        xs, *rest_s = _layer(xs, mod[bp:], pos_s, state_shift[l], state_wkv[l],
                             (cache_k[l], cache_v[l], cache_kidx[l]), lw)
        outs_p.append(rest_p)
        outs_s.append(rest_s)
    stack = lambda outs, i: jnp.stack([o[i] for o in outs])
    return (xp, xs,
            stack(outs_p, 0), stack(outs_p, 1), stack(outs_p, 2), stack(outs_p, 3), stack(outs_p, 4),
            stack(outs_s, 0), stack(outs_s, 1), stack(outs_s, 2), stack(outs_s, 3), stack(outs_s, 4))
```

```python
import functools

import jax
import jax.numpy as jnp
from jax import lax
from jax.experimental import pallas as pl
from jax.experimental.pallas import tpu as pltpu

F32 = jnp.float32
BF16 = jnp.bfloat16
I32 = jnp.int32

LANES = 128
D_MODEL = 1024
EPS = 1e-6
GN_EPS = 64e-5
ROPE_THETA = 10000.0
CHUNK = 64
TOPK_MAX = 256
HEAD_DIM = 64
N_HEADS = D_MODEL // HEAD_DIM
N_PAIRS = N_HEADS // 2
IDX_HEADS = 8
IDX_DIM = 64
D_DECAY = 64
D_AAA = 64
D_GATE = 160
RW_IN = 3 * D_MODEL + D_DECAY + D_AAA + D_GATE
PEER_HEADS = 8
PEER_NKEYS = 128
PEER_TOPK = 16
PEER_DHALF = 64
N_EXPERTS = PEER_NKEYS * PEER_NKEYS
RW_CHUNK = 64
RW_INTERLEAVE = 16
RW_PAIRS_LONG = 2
RW_PASSES = (2, 1, 1, 1, 1)
VMEM_LIMIT = 56 * 1024 * 1024
LOG2E = 1.4426950408889634
PEER_CAND = 80

C_R, C_K, C_V = 0, 1024, 2048
C_Q, C_KD, C_VD = 3072, 4096, 5120
C_GA, C_GB = 6144, 7168
C_QI = 8192
C_G = 8704
C_M = 8960
C_KW = 9088
P_COLS = 9216
IN_W = 9064

NT = (((1,), (1,)), ((), ()))
NN = (((1,), (0,)), ((), ()))


def _pack_in(w):
    z = lambda k: jnp.zeros(w.shape[:-1] + (k,), w.dtype)
    return jnp.concatenate([w[..., 0:3072], w[..., 3360:6432], w[..., 7016:9064], w[..., 6432:6944],
                            w[..., 3200:3360], z(256 - D_GATE), w[..., 3072:3200],
                            w[..., 6944:7016], z(LANES - IDX_DIM - IDX_HEADS)], axis=-1)


def _pack_rw(a):
    return _pack_in(jnp.concatenate([a, jnp.zeros(a.shape[:-1] + (IN_W - RW_IN,), a.dtype)], axis=-1))


def _unpack_rw(p):
    return jnp.concatenate([p[..., :3072], p[..., C_M:C_M + 128], p[..., C_G:C_G + D_GATE]], axis=-1)


def _split_bf16(x, n):
    parts = []
    r = x
    for _ in range(n):
        p = r.astype(BF16)
        parts.append(p)
        r = r - p.astype(F32)
    return parts


def _mm(a, b, pa=1, pb=1, dims=NN):
    aps = _split_bf16(a, pa) if a.dtype != BF16 else [a]
    bps = _split_bf16(b, pb) if b.dtype != BF16 else [b]
    order = max(len(aps), len(bps))
    out = None
    for i, ap in enumerate(aps):
        for j, bp in enumerate(bps):
            if i + j >= order:
                continue
            t = lax.dot_general(ap, bp, dims, preferred_element_type=F32)
            out = t if out is None else out + t
    return out


def _sigmoid(x):
    return 1.0 / (1.0 + jnp.exp(-x))


def _softplus(z):
    return jnp.maximum(z, 0.0) + jnp.log(1.0 + jnp.exp(-jnp.abs(z)))


def _cparams(sem):
    return pltpu.CompilerParams(dimension_semantics=sem, vmem_limit_bytes=VMEM_LIMIT)


def _ada_kernel(c_ref, w_ref, b_ref, o_ref):
    c = c_ref[...]
    s = c * _sigmoid(c)
    o_ref[...] = _mm(s, w_ref[...], 2, 2) + b_ref[...]


def _ada(c, w, b):
    m, d = c.shape
    n = w.shape[1]
    tn = 1024
    return pl.pallas_call(
        _ada_kernel,
        out_shape=jax.ShapeDtypeStruct((m, n), F32),
        grid=(n // tn,),
        in_specs=[pl.BlockSpec((m, d), lambda j: (0, 0)),
                  pl.BlockSpec((d, tn), lambda j: (0, j)),
                  pl.BlockSpec((1, tn), lambda j: (0, j))],
        out_specs=pl.BlockSpec((m, tn), lambda j: (0, j)),
        compiler_params=_cparams(("arbitrary",)),
    )(c, w, b.reshape(1, n))


def _seq_operand(vec, seq_len, tm):
    b, d = vec.shape
    if seq_len % tm == 0:
        per = seq_len // tm
        arr = vec.reshape(b, 1, d)
        spec = pl.BlockSpec((None, 1, d), lambda *g: (g[0] // per, 0, 0))
    else:
        assert tm % seq_len == 0
        arr = jnp.repeat(vec, seq_len, axis=0)
        spec = pl.BlockSpec((tm, d), lambda *g: (g[0], 0))
    return arr, spec


def _row_tile(n, seq_len, cap):
    tm = min(cap, n)
    while n % tm or (seq_len % tm and tm % seq_len):
        tm //= 2
    return tm


def _inproj_kernel(x_ref, sc_ref, sh_ref, nw_ref, w_ref, o_ref, h_scr):
    @pl.when(pl.program_id(1) == 0)
    def _():
        x = x_ref[...]
        y = x * lax.rsqrt(jnp.mean(x * x, axis=-1, keepdims=True) + EPS) * nw_ref[...]
        h_scr[...] = (y * (1.0 + sc_ref[...]) + sh_ref[...]).astype(BF16)

    o_ref[...] = jnp.dot(h_scr[...], w_ref[...], preferred_element_type=F32)


def _inproj(x2, sc, sh, nw, w16, seq_len):
    n, d = x2.shape
    tm = _row_tile(n, seq_len, 1024)
    tn = P_COLS // 4
    sc_a, sc_s = _seq_operand(sc, seq_len, tm)
    sh_a, sh_s = _seq_operand(sh, seq_len, tm)
    return pl.pallas_call(
        _inproj_kernel,
        out_shape=jax.ShapeDtypeStruct((n, P_COLS), F32),
        grid=(n // tm, P_COLS // tn),
        in_specs=[pl.BlockSpec((tm, d), lambda i, j: (i, 0)), sc_s, sh_s,
                  pl.BlockSpec((1, d), lambda i, j: (0, 0)),
                  pl.BlockSpec((d, tn), lambda i, j: (0, j))],
        out_specs=pl.BlockSpec((tm, tn), lambda i, j: (i, j)),
        scratch_shapes=[pltpu.VMEM((tm, d), BF16)],
        compiler_params=_cparams(("parallel", "arbitrary")),
    )(x2, sc_a, sh_a, nw.reshape(1, d), w16)


def _lane_lo(shape):
    return lax.broadcasted_iota(I32, shape, len(shape) - 1) < HEAD_DIM


def _pair_sum(x):
    lo = _lane_lo(x.shape)
    s0 = jnp.sum(jnp.where(lo, x, 0.0), axis=-1, keepdims=True)
    s1 = jnp.sum(jnp.where(lo, 0.0, x), axis=-1, keepdims=True)
    return jnp.where(lo, s0, s1)


def _stack2(x):
    lo = _lane_lo(x.shape)
    return jnp.concatenate([jnp.where(lo, x, 0.0), jnp.where(lo, 0.0, x)], axis=0)


def _rwkv_kernel(t_real, npair, nchunk, pr, pk, pv, pg, pm, sr, sk, sv, sg, sm, mr, mk, mv, mg, mmu,
                 w0, a0, kkw, kaw, rkw, lnw, lnb, wup, aup, gup, z0, o_ref, zf_ref,
                 r_s, lw_s, k_s, v_s, a_s, b_s, y_s, bonus_s, g_s):
    C = RW_CHUNK
    t_pad = r_s.shape[1]

    def mix(p_ref, s_ref, m_ref):
        p = p_ref[...]
        prev = pltpu.roll(p, 1, 0)
        row = lax.broadcasted_iota(I32, p.shape, 0)
        prev = jnp.where(row == 0, s_ref[...], prev)
        return p + (prev - p) * m_ref[...]

    xg, xm = mix(pg, sg, mg), mix(pm, sm, mmu)
    th16, xm16, sg16 = jnp.tanh(xm).astype(BF16), xm.astype(BF16), _sigmoid(xg).astype(BF16)
    xr_all, xk_all, xv_all = mix(pr, sr, mr), mix(pk, sk, mk), mix(pv, sv, mv)

    def put(ref, pp, val):
        if t_pad > t_real:
            val = jnp.concatenate([val, jnp.zeros((t_pad - t_real, LANES), F32)], axis=0)
        ref[pp] = val

    for pp in range(npair):
        cs = slice(pp * LANES, (pp + 1) * LANES)
        xr, xk, xv = xr_all[:, cs], xk_all[:, cs], xv_all[:, cs]
        dw = jnp.dot(th16, wup[:, cs], preferred_element_type=F32)
        lw = -jnp.exp(-_softplus(-(w0[:, cs] + dw)) - 0.5)
        asig = _sigmoid(a0[:, cs] + jnp.dot(xm16, aup[:, cs], preferred_element_type=F32))
        g_s[pp] = jnp.dot(sg16, gup[:, cs], preferred_element_type=F32)
        kk = xk * kkw[:, cs]
        kk = kk * lax.rsqrt(_pair_sum(kk * kk) + 1e-12)
        kmod = xk * (1.0 + (asig - 1.0) * kaw[:, cs])
        bonus_s[pp] = _pair_sum(xr * kmod * rkw[:, cs]) * xv
        put(r_s, pp, xr)
        put(lw_s, pp, lw)
        put(k_s, pp, kmod)
        put(v_s, pp, xv)
        put(a_s, pp, -kk)
        put(b_s, pp, kk * asig)

    n2 = 2 * C
    ri = lax.broadcasted_iota(I32, (n2, n2), 0)
    ci = lax.broadcasted_iota(I32, (n2, n2), 1)
    same = (ri // C) == (ci // C)
    strict = same & ((ri % C) > (ci % C))
    incl = same & ((ri % C) >= (ci % C))
    eye = ri == ci
    eye_f = jnp.where(eye, 1.0, 0.0)
    tri = jnp.where(lax.broadcasted_iota(I32, (C, C), 0) >= lax.broadcasted_iota(I32, (C, C), 1), 1.0, 0.0
                    ).astype(BF16)
    zeros_sq = jnp.zeros((n2, LANES), F32)

    pc_, pg_, pi_, po_, ps_ = RW_PASSES

    def local(chains):
        each = lambda f, *cols: [f(*xs) for xs in zip(*cols)]
        lwc = [lw_s[pp, sl, :] for sl, pp in chains]
        cum = each(lambda l: _mm(tri, l, 1, pc_), lwc)
        cum_last = each(lambda c: c[C - 1:C, :], cum)
        ec, eci = each(jnp.exp, cum), each(lambda c: jnp.exp(-c), cum)
        ecp = each(lambda c, l: jnp.exp(c - l), cum, lwc)
        ecl = each(lambda c, cl: jnp.exp(cl - c), cum, cum_last)
        a_c = [a_s[pp, sl, :] for sl, pp in chains]
        b_c = [b_s[pp, sl, :] for sl, pp in chains]
        k_c = [k_s[pp, sl, :] for sl, pp in chains]
        r_c = [r_s[pp, sl, :] for sl, pp in chains]
        As = each(lambda a, e: _stack2(a * e), a_c, ecp)
        Rs = each(lambda r, e: _stack2(r * e), r_c, ec)
        Bs = each(lambda b, e: _stack2(b * e), b_c, eci)
        Ks = each(lambda k, e: _stack2(k * e), k_c, eci)
        Bt = each(lambda b, e: _stack2(b * e), b_c, ecl)
        Kt = each(lambda k, e: _stack2(k * e), k_c, ecl)
        Vs = [_stack2(v_s[pp, sl, :]) for sl, pp in chains]

        G = each(lambda a, r, b, k: _mm(jnp.concatenate([a, r], axis=0), jnp.concatenate([b, k], axis=0),
                                        pg_, pg_, NT), As, Rs, Bs, Ks)
        a_ab = each(lambda g: jnp.where(strict, g[:n2, :n2], 0.0), G)
        a_ak = each(lambda g: jnp.where(strict, g[:n2, n2:], 0.0), G)
        a_rb = each(lambda g: jnp.where(incl, g[n2:, :n2], 0.0), G)
        a_rk = each(lambda g: jnp.where(incl, g[n2:, n2:], 0.0), G)

        lp = a_ab
        tm_ = each(lambda a: eye_f + a, a_ab)
        step = 2
        while step < C:
            lp = each(lambda l: _mm(l, l, pi_, pi_), lp)
            tm_ = each(lambda t, l: t + _mm(t, l, pi_, pi_), tm_, lp)
            step *= 2

        w1 = each(lambda a, v: _mm(a, v, po_, po_), a_ak, Vs)
        mu_ = each(lambda t, a, w: _mm(t, jnp.concatenate([a, w], axis=1), po_, po_), tm_, As, w1)
        rhs = each(lambda m, v: jnp.concatenate([m, jnp.concatenate([zeros_sq, v], axis=1)], axis=0), mu_, Vs)
        lhs = each(lambda rb, rk, b, k: jnp.concatenate([jnp.concatenate([rb, rk], axis=1),
                                                         jnp.concatenate([b.T, k.T], axis=1)], axis=0),
                   a_rb, a_rk, Bt, Kt)
        out2 = each(lambda l, r: _mm(l, r, po_, po_), lhs, rhs)
        m23 = each(lambda r, o, cl: jnp.concatenate([r + o[:n2, :LANES],
                                                     jnp.where(eye, jnp.exp(cl), 0.0) + o[n2:, :LANES]], axis=0),
                   Rs, out2, cum_last)
        return [(m, o[:n2, LANES:], o[n2:, LANES:]) for m, o in zip(m23, out2)]

    def step_chunks(i, zs):
        sls = [pl.ds(pl.multiple_of((i * nchunk + j) * C, C), C) for j in range(nchunk)]
        parts = local([(sl, pp) for sl in sls for pp in range(npair)])
        zs = list(zs)
        for j, sl in enumerate(sls):
            for pp in range(npair):
                m23, y_loc, z_loc = parts[j * npair + pp]
                yz = _mm(m23, zs[pp], ps_, ps_)
                y = yz[:n2] + y_loc
                y_s[pp, sl, :] = y[:C] + y[C:]
                zs[pp] = yz[n2:] + z_loc
        return tuple(zs)

    zs = lax.fori_loop(0, t_pad // (C * nchunk), step_chunks, tuple(z0[pp] for pp in range(npair)))
    for pp in range(npair):
        zf_ref[pp] = zs[pp]
        cs = slice(pp * LANES, (pp + 1) * LANES)
        y = y_s[pp, 0:t_real, :]
        mean = _pair_sum(y) * (1.0 / HEAD_DIM)
        dlt = y - mean
        var = _pair_sum(dlt * dlt) * (1.0 / HEAD_DIM)
        yn = dlt * lax.rsqrt(var + GN_EPS) * lnw[:, cs] + lnb[:, cs]
        o_ref[:, cs] = ((yn + bonus_s[pp]) * g_s[pp]).astype(o_ref.dtype)


def _rwkv(P, nb, t, prev, mu, w0, a0, k_k, k_a, r_k, lnx_w, lnx_b, wup, aup, gup, z0):
    t_pad = max(t, RW_CHUNK)
    assert t % 8 == 0 and t_pad % RW_CHUNK == 0
    n_chunks = t_pad // RW_CHUNK
    nchunk = min(RW_INTERLEAVE // RW_PAIRS_LONG, n_chunks)
    npair = min(N_PAIRS, max(1, RW_INTERLEAVE // nchunk))
    wp = npair * LANES

    def cblk(c0, w, per_pair):
        return (lambda p: c0 // w + p) if per_pair else (lambda p: c0 // w)

    def pcol(c0, w, pp):
        f = cblk(c0, w, pp)
        return pl.BlockSpec((t, w), lambda b, p: (b, f(p)))

    def prevcol(c0, w, pp):
        f = cblk(c0, w, pp)
        return pl.BlockSpec((None, 1, w), lambda b, p: (b, 0, f(p)))

    def mucol(c0, w, pp):
        f = cblk(c0, w, pp)
        return pl.BlockSpec((1, w), lambda b, p: (0, f(p)))

    def hvec():
        return pl.BlockSpec((1, wp), lambda b, p: (0, p))

    cols = [(C_R, wp, True), (C_K, wp, True), (C_V, wp, True), (C_G, 256, False), (C_M, LANES, False)]
    in_specs = ([pcol(*c) for c in cols] + [prevcol(*c) for c in cols] + [mucol(*c) for c in cols]
                + [hvec() for _ in range(7)]
                + [pl.BlockSpec((LANES, wp), lambda b, p: (0, p)),
                   pl.BlockSpec((LANES, wp), lambda b, p: (0, p)),
                   pl.BlockSpec((256, wp), lambda b, p: (0, p)),
                   pl.BlockSpec((None, npair, LANES, LANES), lambda b, p: (b, p, 0, 0))])
    vecs = [v.reshape(1, D_MODEL) for v in (w0, a0, k_k, k_a, r_k, lnx_w, lnx_b)]
    o, zf = pl.pallas_call(
        functools.partial(_rwkv_kernel, t, npair, nchunk),
        out_shape=(jax.ShapeDtypeStruct((nb * t, D_MODEL), BF16),
                   jax.ShapeDtypeStruct((nb, N_PAIRS, LANES, LANES), F32)),
        grid=(nb, N_PAIRS // npair),
        in_specs=in_specs,
        out_specs=(pl.BlockSpec((t, wp), lambda b, p: (b, p)),
                   pl.BlockSpec((None, npair, LANES, LANES), lambda b, p: (b, p, 0, 0))),
        scratch_shapes=([pltpu.VMEM((npair, t_pad, LANES), F32) for _ in range(7)]
                        + [pltpu.VMEM((npair, t, LANES), F32) for _ in range(2)]),
        compiler_params=_cparams(("parallel", "arbitrary")),
    )(P, P, P, P, P, prev, prev, prev, prev, prev, mu, mu, mu, mu, mu, *vecs, wup, aup, gup, z0)
    return o, zf


def _state_to_pairs(s):
    nb = s.shape[0]
    zt = jnp.swapaxes(s, -1, -2).reshape(nb, N_PAIRS, 2, HEAD_DIM, HEAD_DIM)
    zero = jnp.zeros_like(zt[:, :, 0])
    top = jnp.concatenate([zt[:, :, 0], zero], axis=-1)
    bot = jnp.concatenate([zero, zt[:, :, 1]], axis=-1)
    return jnp.concatenate([top, bot], axis=-2)


def _pairs_to_state(z):
    nb = z.shape[0]
    h0 = z[:, :, :HEAD_DIM, :HEAD_DIM]
    h1 = z[:, :, HEAD_DIM:, HEAD_DIM:]
    s = jnp.stack([h0, h1], axis=2).reshape(nb, N_HEADS, HEAD_DIM, HEAD_DIM)
    return jnp.swapaxes(s, -1, -2)


def _rope(x, cos, sin_signed):
    w = x.shape[1]
    reps = w // LANES
    cw = jnp.concatenate([cos] * reps, axis=1) if reps > 1 else cos
    sw = jnp.concatenate([sin_signed] * reps, axis=1) if reps > 1 else sin_signed
    lane = lax.broadcasted_iota(I32, x.shape, 1)
    fwd = pltpu.roll(x, w - 32, 1)
    bwd = pltpu.roll(x, 32, 1)
    partner = jnp.where((lane % HEAD_DIM) < 32, fwd, bwd)
    return x * cw + partner * sw


def _head_rms(x, nw, e_dn, e_up):
    ms = _mm(x * x, e_dn, 2, 1) * (1.0 / HEAD_DIM)
    r = lax.rsqrt(ms + EPS)
    return x * _mm(r, e_up, 2, 1) * nw


def _dsa_prep_kernel(pq, pkd, pvd, pqi, pkw, cos_ref, sin_ref, qn, kn, edn, eup,
                     q16, k32, k16, v32, v16, qi16, kw32, ki2):
    cos, sin = cos_ref[...], sin_ref[...]
    e_dn, e_up = edn[...], eup[...]
    def put_pairs(ref, x):
        for p in range(N_PAIRS):
            ref[p] = x[:, p * LANES:(p + 1) * LANES].astype(ref.dtype)

    q = _rope(_head_rms(pq[...], qn[...], e_dn, e_up), cos, sin)
    put_pairs(q16, q * (HEAD_DIM ** -0.5 * LOG2E))
    k = _rope(_head_rms(pkd[...], kn[...], e_dn, e_up), cos, sin)
    k32[...] = k
    put_pairs(k16, k)
    v = pvd[...]
    v32[...] = v
    put_pairs(v16, v)
    qi16[...] = _rope(pqi[...], cos, sin).astype(BF16)
    kw = pkw[...]
    lane = lax.broadcasted_iota(I32, kw.shape, 1)
    wi_scale = (IDX_HEADS * IDX_DIM) ** -0.5
    kr = _rope(kw, cos, sin)
    kw32[...] = jnp.where(lane < IDX_DIM, kr, jnp.where(lane < IDX_DIM + IDX_HEADS, kw * wi_scale, 0.0))
    ki2[...] = jnp.where(lane < IDX_DIM, kr, pltpu.roll(kr, IDX_DIM, 1)).astype(BF16)


def _dsa_prep(P, pos_rows, q_norm_w, k_norm_w):
    n = P.shape[0]
    tm = 512 if n % 512 == 0 else n
    half = HEAD_DIM // 2
    inv = ROPE_THETA ** (-jnp.arange(half, dtype=F32) / half)
    ang = pos_rows.astype(F32)[:, None] * inv[None, :]
    cos = jnp.tile(jnp.cos(ang), (1, 4))
    sin = jnp.sin(ang)
    sin_signed = jnp.tile(jnp.concatenate([-sin, sin], axis=1), (1, 2))
    head_of = jnp.arange(D_MODEL) // HEAD_DIM
    e_dn = (head_of[:, None] == jnp.arange(LANES)[None, :]).astype(BF16)
    e_up = e_dn.T
    qn = jnp.tile(q_norm_w, N_HEADS).reshape(1, D_MODEL)
    kn = jnp.tile(k_norm_w, N_HEADS).reshape(1, D_MODEL)

    def col(c0, w):
        return pl.BlockSpec((tm, w), lambda i, c0=c0, w=w: (i, c0 // w))

    def row(w):
        return pl.BlockSpec((tm, w), lambda i: (i, 0))

    def const(shape):
        return pl.BlockSpec(shape, lambda i: (0, 0))

    pairs = jax.ShapeDtypeStruct((N_PAIRS, n, LANES), BF16)
    pair_spec = pl.BlockSpec((N_PAIRS, tm, LANES), lambda i: (0, i, 0))
    return pl.pallas_call(
        _dsa_prep_kernel,
        out_shape=(pairs, jax.ShapeDtypeStruct((n, D_MODEL), F32), pairs, jax.ShapeDtypeStruct((n, D_MODEL), F32),
                   pairs, jax.ShapeDtypeStruct((n, IDX_HEADS * IDX_DIM), BF16),
                   jax.ShapeDtypeStruct((n, LANES), F32), jax.ShapeDtypeStruct((n, LANES), BF16)),
        grid=(n // tm,),
        in_specs=[col(C_Q, 1024), col(C_KD, 1024), col(C_VD, 1024), col(C_QI, 512), col(C_KW, LANES),
                  row(LANES), row(LANES), const((1, D_MODEL)), const((1, D_MODEL)),
                  const((D_MODEL, LANES)), const((LANES, D_MODEL))],
        out_specs=(pair_spec, row(D_MODEL), pair_spec, row(D_MODEL), pair_spec, row(512), row(LANES), row(LANES)),
        compiler_params=_cparams(("parallel",)),
    )(P, P, P, P, P, cos, sin_signed, qn, kn, e_dn, e_up)


CODE_NEG_INF = -1 - 0x7F800000
BITS_PER_CHECK = 2
ATTN_TQ = 256
ATTN_PAIRS = 2
ATTN_CASES = 4
SAMPLE_PAIRS = 8


def _index_scores(qi, wi, ki_list):
    outs = []
    for ki in ki_list:
        acc = None
        for h in range(IDX_HEADS):
            qpair = qi[:, (h // 2) * LANES:(h // 2 + 1) * LANES]
            lo = _lane_lo(qpair.shape)
            qh = jnp.where(lo if h % 2 == 0 else jnp.logical_not(lo), qpair, jnp.zeros_like(qpair))
            rel = lax.dot_general(qh, ki, NT, preferred_element_type=F32)
            term = wi[:, IDX_DIM + h:IDX_DIM + h + 1] * jnp.maximum(rel, 0.0)
            acc = term if acc is None else acc + term
        outs.append(acc)
    return outs


def _select_topk(keys, topk, bias_refs):
    tq = keys[0].shape[0]
    neg = -jnp.inf

    def write(masks):
        for ref, k, msk in zip(bias_refs, keys, masks):
            ref[:, 0:k.shape[1]] = jnp.where(msk, 0.0, neg)

    def count(pred_list):
        tot = None
        for p in pred_list:
            c = jnp.sum(jnp.where(p, 1.0, 0.0), axis=-1, keepdims=True)
            tot = c if tot is None else tot + c
        return tot

    def threshold(c):
        bits = jnp.where(c >= 0, c, c ^ jnp.int32(0x7FFFFFFF))
        return jnp.where(c < jnp.int32(CODE_NEG_INF), neg, lax.bitcast_convert_type(bits, F32))

    few = count([k > neg for k in keys]) <= topk

    def pending(cnt):
        return jnp.max(jnp.where(few | (cnt == topk), 0.0, 1.0))

    def bit_step(state):
        i, c, cnt, _ = state
        for b in range(BITS_PER_CHECK):
            trial = c + jnp.left_shift(jnp.int32(1), 31 - (i + b))
            cnt_t = count([k >= threshold(trial) for k in keys])
            take = cnt_t >= topk
            cnt = jnp.where(take, cnt_t, cnt)
            c = jnp.where(take, trial, c)
        return i + BITS_PER_CHECK, c, cnt, pending(cnt)

    cnt0 = jnp.full((tq, 1), float(sum(k.shape[1] for k in keys)), F32)
    state = (jnp.int32(0), jnp.full((tq, 1), -2 ** 31, I32), cnt0, pending(cnt0))
    _, code, _, _ = lax.while_loop(lambda s: (s[0] < 32) & (s[3] > 0.0), bit_step, state)
    thr = threshold(code)
    ge = [(k >= thr) & (k > neg) for k in keys]
    write(ge)
    surplus = jnp.max(count(ge)) > topk

    @pl.when(surplus)
    def _():
        gt = [k > thr for k in keys]
        need = topk - count(gt)
        ties = [(k == thr) & (k > neg) for k in keys]
        offs, idx = 0, []
        for k in keys:
            idx.append(lax.broadcasted_iota(I32, k.shape, 1) + offs)
            offs += k.shape[1]
        nbits = max(1, (offs - 1).bit_length() + 1)

        def idx_step(i, m):
            trial = m + jnp.left_shift(jnp.int32(1), nbits - 1 - i)
            cnt = count([t & (ix < trial) for t, ix in zip(ties, idx)])
            return jnp.where(cnt <= need, trial, m)

        cut = lax.fori_loop(0, nbits, idx_step, jnp.zeros((tq, 1), I32))
        write([g | (t & (ix < cut)) for g, t, ix in zip(gt, ties, idx)])


def _attend_pairs(q_pairs, k_lists, v_lists, bias_list):
    lo = _lane_lo(q_pairs[0].shape)
    zero = jnp.zeros_like(q_pairs[0])
    heads = []
    for pi, q in enumerate(q_pairs):
        heads += [(jnp.where(lo, q, zero), pi), (jnp.where(lo, zero, q), pi)]
    s = [[lax.dot_general(qh, k, NT, preferred_element_type=F32) + b for k, b in zip(k_lists[pi], bias_list)]
         for qh, pi in heads]
    m = []
    for sh in s:
        mh = None
        for sj in sh:
            mx = jnp.max(sj, axis=-1, keepdims=True)
            mh = mx if mh is None else jnp.maximum(mh, mx)
        m.append(mh)
    p = [[jnp.exp2(sj - mh) for sj in sh] for sh, mh in zip(s, m)]
    den = [functools.reduce(lambda a, b: a + b, [jnp.sum(pj, axis=-1, keepdims=True) for pj in ph]) for ph in p]
    acc = [functools.reduce(lambda a, b: a + b,
                            [jnp.dot(pj.astype(BF16), v, preferred_element_type=F32)
                             for pj, v in zip(ph, v_lists[pi])])
           for ph, (_, pi) in zip(p, heads)]
    outs = [a / d for a, d in zip(acc, den)]
    return [jnp.where(lo, outs[2 * i], outs[2 * i + 1]) for i in range(len(q_pairs))]


def _attn_prompt_kernel(topk, ncase, q_ref, qi_ref, kw_ref, k_ref, v_ref, ki2_ref, o_ref, bias_s):
    tq = q_ref.shape[1]
    t = k_ref.shape[1]
    i = pl.program_id(1)
    lstep = t // ncase
    case = ((i + 1) * tq - 1) // lstep

    def run(L):
        score = _index_scores(qi_ref[...], kw_ref[...], [ki2_ref[0:L, :]])[0]
        qpos = i * tq + lax.broadcasted_iota(I32, (tq, L), 0)
        kpos = lax.broadcasted_iota(I32, (tq, L), 1)
        adm = (qpos // CHUNK) >= (kpos // CHUNK)
        _select_topk([jnp.where(adm, score, -jnp.inf)], topk, [bias_s])

        def pairs(g, carry):
            ps = [ATTN_PAIRS * g + j for j in range(ATTN_PAIRS)]
            outs = _attend_pairs([q_ref[p] for p in ps], [[k_ref[p, 0:L, :]] for p in ps],
                                 [[v_ref[p, 0:L, :]] for p in ps], [bias_s[:, 0:L]])
            for p, o in zip(ps, outs):
                o_ref[p] = o.astype(o_ref.dtype)
            return carry

        lax.fori_loop(0, N_PAIRS // ATTN_PAIRS, pairs, 0)

    for c in range(ncase):
        pl.when(case == c)(functools.partial(run, (c + 1) * lstep))


def _attn_prompt(q16, qi16, kw32, k16, v16, ki2, nb, t):
    tq = min(ATTN_TQ, t)
    topk = min(TOPK_MAX, t // 4)
    nq = t // tq
    ncase = min(ATTN_CASES, nq)

    def qrow(w):
        return pl.BlockSpec((tq, w), lambda b, i: (b * nq + i, 0))

    def qpairs():
        return pl.BlockSpec((N_PAIRS, tq, LANES), lambda b, i: (0, b * nq + i, 0))

    def kpairs():
        return pl.BlockSpec((N_PAIRS, t, LANES), lambda b, i: (0, b, 0))

    return pl.pallas_call(
        functools.partial(_attn_prompt_kernel, topk, ncase),
        out_shape=jax.ShapeDtypeStruct((N_PAIRS, nb * t, LANES), BF16),
        grid=(nb, nq),
        in_specs=[qpairs(), qrow(512), qrow(LANES), kpairs(), kpairs(),
                  pl.BlockSpec((t, LANES), lambda b, i: (b, 0))],
        out_specs=qpairs(),
        scratch_shapes=[pltpu.VMEM((tq, t), F32)],
        compiler_params=_cparams(("parallel", "arbitrary")),
    )(q16, qi16, kw32, k16, v16, ki2)


def _attn_sample_kernel(topk, past, q_ref, qi_ref, kw_ref, ck_ref, cv_ref, cki2_ref, k_ref, v_ref, ki2_ref, o_ref,
                        biasc_s, biasn_s):
    npairs, ts = q_ref.shape[0], q_ref.shape[1]

    @pl.when(pl.program_id(1) == 0)
    def _():
        sc, sn = _index_scores(qi_ref[...], kw_ref[...], [cki2_ref[...], ki2_ref[...]])
        qpos = past + lax.broadcasted_iota(I32, (ts, 1), 0)
        kpos_c = lax.broadcasted_iota(I32, sc.shape, 1)
        kpos_n = past + lax.broadcasted_iota(I32, sn.shape, 1)
        keys = [jnp.where((qpos // CHUNK) >= (kpos_c // CHUNK), sc, -jnp.inf),
                jnp.where((qpos // CHUNK) >= (kpos_n // CHUNK), sn, -jnp.inf)]
        _select_topk(keys, topk, [biasc_s, biasn_s])

    lanes = [slice(p * LANES, (p + 1) * LANES) for p in range(npairs)]
    outs = _attend_pairs([q_ref[p] for p in range(npairs)],
                         [[ck_ref[:, cs].astype(BF16), k_ref[p]] for p, cs in enumerate(lanes)],
                         [[cv_ref[:, cs].astype(BF16), v_ref[p]] for p, cs in enumerate(lanes)],
                         [biasc_s[...], biasn_s[...]])
    for p, o in enumerate(outs):
        o_ref[p] = o.astype(o_ref.dtype)


def _attn_sample(q16, qi16, kw32, k16, v16, ki2, cache_k, cache_v, cache_kidx, nb, ts):
    past = cache_k.shape[1]
    cki2 = jnp.concatenate([cache_kidx, cache_kidx], axis=-1).astype(BF16)
    topk = min(TOPK_MAX, (past + ts) // 4)

    def qrow(w):
        return pl.BlockSpec((ts, w), lambda b, p: (b, 0))

    sp = SAMPLE_PAIRS

    def qpair():
        return pl.BlockSpec((sp, ts, LANES), lambda b, p: (p, b, 0))

    def cache(pair):
        if pair:
            return pl.BlockSpec((None, past, sp * LANES), lambda b, p: (b, 0, p))
        return pl.BlockSpec((None, past, LANES), lambda b, p: (b, 0, 0))

    return pl.pallas_call(
        functools.partial(_attn_sample_kernel, topk, past),
        out_shape=jax.ShapeDtypeStruct((N_PAIRS, nb * ts, LANES), BF16),
        grid=(nb, N_PAIRS // sp),
        in_specs=[qpair(), qrow(512), qrow(LANES), cache(True), cache(True), cache(False),
                  qpair(), qpair(), qrow(LANES)],
        out_specs=qpair(),
        scratch_shapes=[pltpu.VMEM((ts, past), F32), pltpu.VMEM((ts, ts), F32)],
        compiler_params=_cparams(("parallel", "arbitrary")),
    )(q16, qi16, kw32, cache_k, cache_v, cki2, k16, v16, ki2)


def _merge_kernel(x_ref, oa_ref, ob_ref, pga_ref, pgb_ref, bga_ref, bgb_ref, g1_ref, sc2_ref, sh2_ref, nw_ref,
                  wpa_ref, wpb_ref, wout_ref, x1_ref, h2_ref):
    ga = _sigmoid(pga_ref[...] + bga_ref[...])
    gb = _sigmoid(pgb_ref[...] + bgb_ref[...])
    ob = jnp.concatenate([ob_ref[p] for p in range(N_PAIRS)], axis=1)
    m = (ga * jnp.dot(oa_ref[...], wpa_ref[...], preferred_element_type=F32)
         + gb * jnp.dot(ob, wpb_ref[...], preferred_element_type=F32))
    x1 = x_ref[...] + g1_ref[...] * jnp.dot(m.astype(BF16), wout_ref[...], preferred_element_type=F32)
    x1_ref[...] = x1
    y = x1 * lax.rsqrt(jnp.mean(x1 * x1, axis=-1, keepdims=True) + EPS) * nw_ref[...]
    h2_ref[...] = (y * (1.0 + sc2_ref[...]) + sh2_ref[...]).astype(BF16)


def _merge(x2, o_a, o_b, P, b_gate, g1, sc2, sh2, nw2, wpa, wpb, wout, seq_len):
    n, d = x2.shape
    tm = _row_tile(n, seq_len, 512)
    g1_a, g1_s = _seq_operand(g1, seq_len, tm)
    sc_a, sc_s = _seq_operand(sc2, seq_len, tm)
    sh_a, sh_s = _seq_operand(sh2, seq_len, tm)

    def row():
        return pl.BlockSpec((tm, d), lambda i: (i, 0))

    def const(shape):
        return pl.BlockSpec(shape, lambda i: (0, 0))

    bg = b_gate.reshape(1, 2 * d)
    return pl.pallas_call(
        _merge_kernel,
        out_shape=(jax.ShapeDtypeStruct((n, d), F32), jax.ShapeDtypeStruct((n, d), BF16)),
        grid=(n // tm,),
        in_specs=[row(), row(), pl.BlockSpec((N_PAIRS, tm, LANES), lambda i: (0, i, 0)),
                  pl.BlockSpec((tm, d), lambda i: (i, C_GA // d)), pl.BlockSpec((tm, d), lambda i: (i, C_GB // d)),
                  pl.BlockSpec((1, d), lambda i: (0, 0)), pl.BlockSpec((1, d), lambda i: (0, 1)),
                  g1_s, sc_s, sh_s, const((1, d)), const((d, d)), const((d, d)), const((d, d))],
        out_specs=(row(), row()),
        compiler_params=_cparams(("parallel",)),
    )(x2, o_a, o_b, P, P, bg, bg, g1_a, sc_a, sh_a, nw2.reshape(1, d), wpa, wpb, wout)


def _top_exact(s, k):
    rows = lax.broadcasted_iota(I32, s.shape, 0).astype(F32)
    cur = s
    rank = jnp.full(s.shape, float(k), F32)
    vals = []
    for r in range(k):
        m = jnp.max(cur, axis=0, keepdims=True)
        first = jnp.min(jnp.where(cur == m, rows, 1e9), axis=0, keepdims=True)
        hit = rows == first
        vals.append(m)
        rank = jnp.where(hit, float(r), rank)
        cur = jnp.where(hit, -jnp.inf, cur)
    return vals, rank


def _top_fast(ss, k):
    curs = list(ss)
    ranks = [jnp.full(s.shape, float(k), F32) for s in ss]
    vals = [[] for _ in ss]
    for r in range(k):
        ms = [jnp.max(c, axis=0, keepdims=True) for c in curs]
        hits = [c == m for c, m in zip(curs, ms)]
        ranks = [jnp.where(h, float(r), rk) for h, rk in zip(hits, ranks)]
        curs = [jnp.where(h, -jnp.inf, c) for h, c in zip(hits, curs)]
        for v, m in zip(vals, ms):
            v.append(m)
    cleans = [jnp.max(jnp.abs(jnp.sum(jnp.where(rk < k, 1.0, 0.0), axis=0, keepdims=True) - k)) == 0.0
              for rk in ranks]
    return vals, ranks, cleans


def _top(src_scr, k, vals_scr, rank_scr, redo_s):
    n = src_scr.shape[0]
    vals, ranks, cleans = _top_fast([src_scr[i] for i in range(n)], k)
    for i in range(n):
        vals_scr[i] = jnp.concatenate(vals[i], axis=0)
        rank_scr[i] = ranks[i]
        redo_s[i] = jnp.where(cleans[i], 0, 1).astype(I32)

    def redo(i, carry):
        @pl.when(redo_s[i] == 1)
        def _():
            vals_e, rank_e = _top_exact(src_scr[i], k)
            vals_scr[i] = jnp.concatenate(vals_e, axis=0)
            rank_scr[i] = rank_e

        return carry

    lax.fori_loop(0, n, redo, 0)


def _peer_sel_kernel(h_ref, wpqt_ref, kbd_ref, g_ref, cnt_ref, r2_ref, p2_ref, s_scr, vals_scr, rank_scr,
                     cand_scr, cvals_scr, crank_scr, redo_s):
    K = PEER_TOPK
    tm = h_ref.shape[0]
    qt = lax.dot_general(wpqt_ref[...], h_ref[...], NT, preferred_element_type=F32)
    s_scr[...] = jnp.dot(kbd_ref[...], qt.astype(BF16), preferred_element_type=F32
                         ).reshape(2 * PEER_HEADS, PEER_NKEYS, tm)
    sub8 = lax.broadcasted_iota(I32, (8, tm), 0)
    neg = jnp.full((8, tm), -jnp.inf, F32)
    _top(s_scr, K, vals_scr, rank_scr, redo_s)
    for hd in range(PEER_HEADS):
        c1, c2 = vals_scr[2 * hd], vals_scr[2 * hd + 1]
        blocks = [c1[0:1] + c2, c1[1:2] + c2[0:8]]
        for k1 in range(2, 8):
            blocks.append(jnp.where(sub8 < K // (k1 + 1), c1[k1:k1 + 1] + c2[0:8], neg))
        blocks.append(c1[8:16] + c2[0:1])
        cand_scr[hd] = jnp.concatenate(blocks, axis=0)
    _top(cand_scr, K, cvals_scr, crank_scr, redo_s)
    for hd in range(PEER_HEADS):
        s1, s2 = s_scr[2 * hd], s_scr[2 * hd + 1]
        c1, c2 = vals_scr[2 * hd], vals_scr[2 * hd + 1]
        rank1, rank2 = rank_scr[2 * hd], rank_scr[2 * hd + 1]
        cand = cand_scr[hd]
        taken = crank_scr[hd] < K
        z = jnp.sum(jnp.where(taken, jnp.exp(cand - (c1[0:1] + c2[0:1])), 0.0), axis=0, keepdims=True)
        tk = jnp.where(taken, 1.0, 0.0)
        per_k1 = [jnp.sum(tk[0:16], axis=0, keepdims=True)]
        per_k1 += [jnp.sum(tk[8 + 8 * k1:16 + 8 * k1], axis=0, keepdims=True) for k1 in range(1, 8)]
        cnt16 = jnp.concatenate(per_k1 + [tk[72:80]], axis=0)
        cnt = jnp.zeros(s1.shape, F32)
        for k1 in range(K):
            cnt = jnp.where(rank1 == float(k1), cnt16[k1:k1 + 1], cnt)
        g_ref[hd] = jnp.where(rank1 < K, jnp.exp(s1 - c1[0:1]) / z, 0.0)
        cnt_ref[hd] = cnt
        p2 = jnp.where(rank2 < K, jnp.exp(s2 - c2[0:1]), 0.0)
        cb = r2_ref.shape[-1]
        for tc in range(tm // cb):
            r2_ref[hd, tc] = rank2[:, tc * cb:(tc + 1) * cb].astype(r2_ref.dtype)
            p2_ref[hd, tc] = p2[:, tc * cb:(tc + 1) * cb].astype(p2_ref.dtype)


def _peer_select(h2, wpqt, kbd):
    n, d = h2.shape
    tm = 256 if n % 256 == 0 else n
    cb = min(LANES, tm)
    big = jax.ShapeDtypeStruct((PEER_HEADS, PEER_NKEYS, n), F32)
    blocked = jax.ShapeDtypeStruct((PEER_HEADS, n // cb, PEER_NKEYS, cb), BF16)

    def blk():
        return pl.BlockSpec((PEER_HEADS, PEER_NKEYS, tm), lambda i: (0, 0, i))

    def blk4():
        return pl.BlockSpec((PEER_HEADS, tm // cb, PEER_NKEYS, cb), lambda i: (0, i, 0, 0))

    return pl.pallas_call(
        _peer_sel_kernel,
        out_shape=(big, big, blocked, blocked),
        grid=(n // tm,),
        in_specs=[pl.BlockSpec((tm, d), lambda i: (i, 0)),
                  pl.BlockSpec((d, d), lambda i: (0, 0)),
                  pl.BlockSpec((2 * d, d), lambda i: (0, 0))],
        out_specs=(blk(), blk(), blk4(), blk4()),
        scratch_shapes=[pltpu.VMEM((2 * PEER_HEADS, PEER_NKEYS, tm), F32),
                        pltpu.VMEM((2 * PEER_HEADS, PEER_TOPK, tm), F32),
                        pltpu.VMEM((2 * PEER_HEADS, PEER_NKEYS, tm), F32),
                        pltpu.VMEM((PEER_HEADS, PEER_CAND, tm), F32),
                        pltpu.VMEM((PEER_HEADS, PEER_TOPK, tm), F32),
                        pltpu.VMEM((PEER_HEADS, PEER_CAND, tm), F32),
                        pltpu.SMEM((2 * PEER_HEADS,), I32)],
        compiler_params=_cparams(("parallel",)),
    )(h2, wpqt, kbd)


def _gelu_tanh(x):
    return 0.5 * x * (1.0 + jnp.tanh(0.7978845608028654 * (x + 0.044715 * (x * x * x))))


def _peer_main_kernel(ni1, h_ref, x1_ref, g2_ref, u_ref, vt_ref, g_ref, cnt_ref, r2_ref, p2_ref, y_ref, acc, gate_s):
    j = pl.program_id(1)
    tm = h_ref.shape[0]

    @pl.when(j == 0)
    def _():
        acc[...] = jnp.zeros_like(acc)

    cb = r2_ref.shape[-1]
    reps = PEER_NKEYS // 16
    zero = jnp.zeros((PEER_NKEYS, cb), BF16)
    for l in range(ni1):
        for tc in range(tm // cb):
            ts = slice(tc * cb, (tc + 1) * cb)
            w = None
            for hd in range(PEER_HEADS):
                c16 = jnp.broadcast_to(cnt_ref[hd, l:l + 1, ts], (16, cb)).astype(BF16)
                g16 = jnp.broadcast_to(g_ref[hd, l:l + 1, ts], (16, cb)).astype(BF16)
                t = (jnp.where(r2_ref[hd, tc] < jnp.concatenate([c16] * reps, axis=0), p2_ref[hd, tc], zero)
                     * jnp.concatenate([g16] * reps, axis=0))
                w = t if w is None else w + t
            gate_s[tc, l * PEER_NKEYS:(l + 1) * PEER_NKEYS, :] = w

    act = lax.dot_general(u_ref[...], h_ref[...], NT, preferred_element_type=F32)
    gate = jnp.concatenate([gate_s[tc] for tc in range(tm // cb)], axis=1)
    coef = gate * _gelu_tanh(act.astype(BF16))
    acc[...] += jnp.dot(vt_ref[...], coef, preferred_element_type=F32)

    @pl.when(j == pl.num_programs(1) - 1)
    def _():
        y_ref[...] = x1_ref[...] + g2_ref[...] * acc[...].T


def _peer_main(h2, x1, g2, u16, vt16, g, cnt, r2, p2, seq_len):
    n, d = h2.shape
    tm = _row_tile(n, seq_len, 512)
    ni1 = 16
    et = ni1 * PEER_NKEYS
    cb = r2.shape[-1]
    g2_a, g2_s = _seq_operand(g2, seq_len, tm)

    def row():
        return pl.BlockSpec((tm, d), lambda i, j: (i, 0))

    return pl.pallas_call(
        functools.partial(_peer_main_kernel, ni1),
        out_shape=jax.ShapeDtypeStruct((n, d), F32),
        grid=(n // tm, N_EXPERTS // et),
        in_specs=[row(), row(), g2_s,
                  pl.BlockSpec((et, d), lambda i, j: (j, 0)),
                  pl.BlockSpec((d, et), lambda i, j: (0, j)),
                  pl.BlockSpec((PEER_HEADS, ni1, tm), lambda i, j: (0, j, i)),
                  pl.BlockSpec((PEER_HEADS, ni1, tm), lambda i, j: (0, j, i)),
                  pl.BlockSpec((PEER_HEADS, tm // cb, PEER_NKEYS, cb), lambda i, j: (0, i, 0, 0)),
                  pl.BlockSpec((PEER_HEADS, tm // cb, PEER_NKEYS, cb), lambda i, j: (0, i, 0, 0))],
        out_specs=row(),
        scratch_shapes=[pltpu.VMEM((d, tm), F32), pltpu.VMEM((tm // cb, et, cb), BF16)],
        compiler_params=_cparams(("parallel", "arbitrary")),
    )(h2, x1, g2_a, u16, vt16, g, cnt, r2, p2)


def _layer(x, mod, pos, shift_prev, s0, cache, lw):
    nb, t, d = x.shape
    n = nb * t
    sh1, sc1, g1, sh2, sc2, g2 = [mod[:, i * d:(i + 1) * d] for i in range(6)]
    x2 = x.reshape(n, d)
    P = _inproj(x2, sc1, sh1, lw['norm1_w'], lw['w_in16'], t)

    prev = _pack_rw(shift_prev).reshape(nb, 1, P_COLS)
    o_a, zf = _rwkv(P, nb, t, prev, lw['mu'], lw['w0'], lw['a0'], lw['k_k'], lw['k_a'], lw['r_k'], lw['lnx_w'],
                    lw['lnx_b'], lw['wup'], lw['aup'], lw['gup'], _state_to_pairs(s0))
    wkv = _pairs_to_state(zf)
    shift_last = _unpack_rw(P.reshape(nb, t, P_COLS)[:, -1, :])

    q16, k32, k16, v32, v16, qi16, kw32, ki2 = _dsa_prep(P, jnp.tile(pos, nb), lw['q_norm_w'], lw['k_norm_w'])
    if cache is None:
        o_b = _attn_prompt(q16, qi16, kw32, k16, v16, ki2, nb, t)
    else:
        ck, cv, cki = cache
        past = ck.shape[1]
        o_b = _attn_sample(q16, qi16, kw32, k16, v16, ki2, ck.reshape(nb, past, d).astype(BF16),
                           cv.reshape(nb, past, d).astype(BF16), cki, nb, t)

    x1, h2 = _merge(x2, o_a, o_b, P, lw['b_gate'], g1, sc2, sh2, lw['norm2_w'], lw['wpa'], lw['wpb'], lw['wout'], t)
    g, cnt, r2, p2 = _peer_select(h2, lw['wpqt'], lw['kbd'])
    y = _peer_main(h2, x1, g2, lw['u16'], lw['vt16'], g, cnt, r2, p2, t)

    k_new = k32.reshape(nb, t, N_HEADS, HEAD_DIM)
    v_new = v32.reshape(nb, t, N_HEADS, HEAD_DIM)
    ki_new = kw32[:, :IDX_DIM].reshape(nb, t, IDX_DIM)
    return y.reshape(nb, t, d), wkv, shift_last, k_new, v_new, ki_new


def _layer_weights(l, w_in, b_gate, mu_rw, w0, w_up, a0, a_up, g_up, k_k, k_a, r_k, lnx_w, lnx_b, q_norm_w, k_norm_w,
                   w_proj_a, w_proj_b, w_out, norm1_w, norm2_w, w_pq, peer_keys, peer_u, peer_v):
    d = D_MODEL
    zeros = lambda r: jnp.zeros((r, d), F32)
    keys = peer_keys[l].reshape(2 * PEER_HEADS, PEER_NKEYS, PEER_DHALF)
    eye = jnp.eye(2 * PEER_HEADS, dtype=F32)
    kbd = (eye[:, None, :, None] * keys[:, :, None, :]).reshape(2 * d, d)
    return {
        'w_in16': _pack_in(w_in[l]).astype(BF16), 'b_gate': b_gate[l], 'mu': _pack_rw(mu_rw[l]).reshape(1, P_COLS),
        'w0': w0[l], 'a0': a0[l], 'k_k': k_k[l], 'k_a': k_a[l], 'r_k': r_k[l].reshape(d), 'lnx_w': lnx_w[l],
        'lnx_b': lnx_b[l],
        'wup': jnp.concatenate([w_up[l], zeros(LANES - D_DECAY)], axis=0).astype(BF16),
        'aup': jnp.concatenate([zeros(D_DECAY), a_up[l]], axis=0).astype(BF16),
        'gup': jnp.concatenate([g_up[l], zeros(256 - D_GATE)], axis=0).astype(BF16),
        'q_norm_w': q_norm_w[l], 'k_norm_w': k_norm_w[l], 'norm1_w': norm1_w[l], 'norm2_w': norm2_w[l],
        'wpa': w_proj_a[l].astype(BF16), 'wpb': w_proj_b[l].astype(BF16), 'wout': w_out[l].astype(BF16),
        'wpqt': w_pq[l].T.astype(BF16), 'kbd': kbd.astype(BF16),
        'u16': peer_u[l].astype(BF16), 'vt16': peer_v[l].T.astype(BF16),
    }


def kernel(x_prompt, x_sample, c_prompt, c_sample, cache_k, cache_v, cache_kidx, state_wkv, state_shift, w_ada, b_ada,
           norm1_w, w_in, b_gate, mu_rw, w0, w_up, a0, a_up, g_up, k_k, k_a, r_k, lnx_w, lnx_b, q_norm_w, k_norm_w,
           w_proj_a, w_proj_b, w_out, norm2_w, w_pq, peer_keys, peer_u, peer_v):
    depth = w_in.shape[0]
    bp, tp = x_prompt.shape[:2]
    bs, ts = x_sample.shape[:2]
    past = cache_k.shape[2]
    dt = x_prompt.dtype
    pos_p = jnp.arange(tp, dtype=I32)
    pos_s = past + jnp.arange(ts, dtype=I32)
    zero_shift = jnp.zeros((bp, RW_IN), dt)
    zero_wkv = jnp.zeros((bp, N_HEADS, HEAD_DIM, HEAD_DIM), dt)
    c_all = jnp.concatenate([c_prompt, c_sample], axis=0)
    xp, xs = x_prompt, x_sample
    outs_p, outs_s = [], []
    for l in range(depth):
        lw = _layer_weights(l, w_in, b_gate, mu_rw, w0, w_up, a0, a_up, g_up, k_k, k_a, r_k, lnx_w, lnx_b, q_norm_w,
                            k_norm_w, w_proj_a, w_proj_b, w_out, norm1_w, norm2_w, w_pq, peer_keys, peer_u, peer_v)
        mod = _ada(c_all, w_ada[l], b_ada[l])
        xp, *rest_p = _layer(xp, mod[:bp], pos_p, zero_shift, zero_wkv, None, lw)
        xs, *rest_s = _layer(xs, mod[bp:], pos_s, state_shift[l], state_wkv[l],
                             (cache_k[l], cache_v[l], cache_kidx[l]), lw)
        outs_p.append(rest_p)
        outs_s.append(rest_s)
    stack = lambda outs, i: jnp.stack([o[i] for o in outs])
    return (xp, xs,
            stack(outs_p, 0), stack(outs_p, 1), stack(outs_p, 2), stack(outs_p, 3), stack(outs_p, 4),
            stack(outs_s, 0), stack(outs_s, 1), stack(outs_s, 2), stack(outs_s, 3), stack(outs_s, 4))
```

```python
import functools

import jax
import jax.numpy as jnp
from jax import lax
from jax.experimental import pallas as pl
from jax.experimental.pallas import tpu as pltpu

F32 = jnp.float32
BF16 = jnp.bfloat16
I32 = jnp.int32

LANES = 128
D_MODEL = 1024
EPS = 1e-6
GN_EPS = 64e-5
ROPE_THETA = 10000.0
CHUNK = 64
TOPK_MAX = 256
HEAD_DIM = 64
N_HEADS = D_MODEL // HEAD_DIM
N_PAIRS = N_HEADS // 2
IDX_HEADS = 8
IDX_DIM = 64
D_DECAY = 64
D_AAA = 64
D_GATE = 160
RW_IN = 3 * D_MODEL + D_DECAY + D_AAA + D_GATE
PEER_HEADS = 8
PEER_NKEYS = 128
PEER_TOPK = 16
PEER_DHALF = 64
N_EXPERTS = PEER_NKEYS * PEER_NKEYS
RW_CHUNK = 64
RW_INTERLEAVE = 16
RW_PAIRS_LONG = 2
RW_PASSES = (2, 1, 1, 1, 1)
VMEM_LIMIT = 56 * 1024 * 1024
LOG2E = 1.4426950408889634
PEER_CAND = 80

C_R, C_K, C_V = 0, 1024, 2048
C_Q, C_KD, C_VD = 3072, 4096, 5120
C_GA, C_GB = 6144, 7168
C_QI = 8192
C_G = 8704
C_M = 8960
C_KW = 9088
P_COLS = 9216
IN_W = 9064

NT = (((1,), (1,)), ((), ()))
NN = (((1,), (0,)), ((), ()))


def _pack_in(w):
    z = lambda k: jnp.zeros(w.shape[:-1] + (k,), w.dtype)
    return jnp.concatenate([w[..., 0:3072], w[..., 3360:6432], w[..., 7016:9064], w[..., 6432:6944],
                            w[..., 3200:3360], z(256 - D_GATE), w[..., 3072:3200],
                            w[..., 6944:7016], z(LANES - IDX_DIM - IDX_HEADS)], axis=-1)


def _pack_rw(a):
    return _pack_in(jnp.concatenate([a, jnp.zeros(a.shape[:-1] + (IN_W - RW_IN,), a.dtype)], axis=-1))


def _unpack_rw(p):
    return jnp.concatenate([p[..., :3072], p[..., C_M:C_M + 128], p[..., C_G:C_G + D_GATE]], axis=-1)


def _split_bf16(x, n):
    parts = []
    r = x
    for _ in range(n):
        p = r.astype(BF16)
        parts.append(p)
        r = r - p.astype(F32)
    return parts


def _mm(a, b, pa=1, pb=1, dims=NN):
    aps = _split_bf16(a, pa) if a.dtype != BF16 else [a]
    bps = _split_bf16(b, pb) if b.dtype != BF16 else [b]
    order = max(len(aps), len(bps))
    out = None
    for i, ap in enumerate(aps):
        for j, bp in enumerate(bps):
            if i + j >= order:
                continue
            t = lax.dot_general(ap, bp, dims, preferred_element_type=F32)
            out = t if out is None else out + t
    return out


def _sigmoid(x):
    return 1.0 / (1.0 + jnp.exp(-x))


def _softplus(z):
    return jnp.maximum(z, 0.0) + jnp.log(1.0 + jnp.exp(-jnp.abs(z)))


def _cparams(sem):
    return pltpu.CompilerParams(dimension_semantics=sem, vmem_limit_bytes=VMEM_LIMIT)


def _ada_kernel(c_ref, w_ref, b_ref, o_ref):
    c = c_ref[...]
    s = c * _sigmoid(c)
    o_ref[...] = _mm(s, w_ref[...], 2, 2) + b_ref[...]


def _ada(c, w, b):
    m, d = c.shape
    n = w.shape[1]
    tn = 1024
    return pl.pallas_call(
        _ada_kernel,
        out_shape=jax.ShapeDtypeStruct((m, n), F32),
        grid=(n // tn,),
        in_specs=[pl.BlockSpec((m, d), lambda j: (0, 0)),
                  pl.BlockSpec((d, tn), lambda j: (0, j)),
                  pl.BlockSpec((1, tn), lambda j: (0, j))],
        out_specs=pl.BlockSpec((m, tn), lambda j: (0, j)),
        compiler_params=_cparams(("arbitrary",)),
    )(c, w, b.reshape(1, n))


def _seq_operand(vec, seq_len, tm):
    b, d = vec.shape
    if seq_len % tm == 0:
        per = seq_len // tm
        arr = vec.reshape(b, 1, d)
        spec = pl.BlockSpec((None, 1, d), lambda *g: (g[0] // per, 0, 0))
    else:
        assert tm % seq_len == 0
        arr = jnp.repeat(vec, seq_len, axis=0)
        spec = pl.BlockSpec((tm, d), lambda *g: (g[0], 0))
    return arr, spec


def _row_tile(n, seq_len, cap):
    tm = min(cap, n)
    while n % tm or (seq_len % tm and tm % seq_len):
        tm //= 2
    return tm


def _inproj_kernel(x_ref, sc_ref, sh_ref, nw_ref, w_ref, o_ref, h_scr):
    @pl.when(pl.program_id(1) == 0)
    def _():
        x = x_ref[...]
        y = x * lax.rsqrt(jnp.mean(x * x, axis=-1, keepdims=True) + EPS) * nw_ref[...]
        h_scr[...] = (y * (1.0 + sc_ref[...]) + sh_ref[...]).astype(BF16)

    o_ref[...] = jnp.dot(h_scr[...], w_ref[...], preferred_element_type=F32)


def _inproj(x2, sc, sh, nw, w16, seq_len):
    n, d = x2.shape
    tm = _row_tile(n, seq_len, 1024)
    tn = P_COLS // 4
    sc_a, sc_s = _seq_operand(sc, seq_len, tm)
    sh_a, sh_s = _seq_operand(sh, seq_len, tm)
    return pl.pallas_call(
        _inproj_kernel,
        out_shape=jax.ShapeDtypeStruct((n, P_COLS), F32),
        grid=(n // tm, P_COLS // tn),
        in_specs=[pl.BlockSpec((tm, d), lambda i, j: (i, 0)), sc_s, sh_s,
                  pl.BlockSpec((1, d), lambda i, j: (0, 0)),
                  pl.BlockSpec((d, tn), lambda i, j: (0, j))],
        out_specs=pl.BlockSpec((tm, tn), lambda i, j: (i, j)),
        scratch_shapes=[pltpu.VMEM((tm, d), BF16)],
        compiler_params=_cparams(("parallel", "arbitrary")),
    )(x2, sc_a, sh_a, nw.reshape(1, d), w16)


def _lane_lo(shape):
    return lax.broadcasted_iota(I32, shape, len(shape) - 1) < HEAD_DIM


def _pair_sum(x):
    lo = _lane_lo(x.shape)
    s0 = jnp.sum(jnp.where(lo, x, 0.0), axis=-1, keepdims=True)
    s1 = jnp.sum(jnp.where(lo, 0.0, x), axis=-1, keepdims=True)
    return jnp.where(lo, s0, s1)


def _stack2(x):
    lo = _lane_lo(x.shape)
    return jnp.concatenate([jnp.where(lo, x, 0.0), jnp.where(lo, 0.0, x)], axis=0)


def _rwkv_kernel(t_real, npair, nchunk, pr, pk, pv, pg, pm, sr, sk, sv, sg, sm, mr, mk, mv, mg, mmu,
                 w0, a0, kkw, kaw, rkw, lnw, lnb, wup, aup, gup, z0, o_ref, zf_ref,
                 r_s, lw_s, k_s, v_s, a_s, b_s, y_s, bonus_s, g_s):
    C = RW_CHUNK
    t_pad = r_s.shape[1]

    def mix(p_ref, s_ref, m_ref):
        p = p_ref[...]
        prev = pltpu.roll(p, 1, 0)
        row = lax.broadcasted_iota(I32, p.shape, 0)
        prev = jnp.where(row == 0, s_ref[...], prev)
        return p + (prev - p) * m_ref[...]

    xg, xm = mix(pg, sg, mg), mix(pm, sm, mmu)
    th16, xm16, sg16 = jnp.tanh(xm).astype(BF16), xm.astype(BF16), _sigmoid(xg).astype(BF16)
    xr_all, xk_all, xv_all = mix(pr, sr, mr), mix(pk, sk, mk), mix(pv, sv, mv)

    def put(ref, pp, val):
        if t_pad > t_real:
            val = jnp.concatenate([val, jnp.zeros((t_pad - t_real, LANES), F32)], axis=0)
        ref[pp] = val

    for pp in range(npair):
        cs = slice(pp * LANES, (pp + 1) * LANES)
        xr, xk, xv = xr_all[:, cs], xk_all[:, cs], xv_all[:, cs]
        dw = jnp.dot(th16, wup[:, cs], preferred_element_type=F32)
        lw = -jnp.exp(-_softplus(-(w0[:, cs] + dw)) - 0.5)
        asig = _sigmoid(a0[:, cs] + jnp.dot(xm16, aup[:, cs], preferred_element_type=F32))
        g_s[pp] = jnp.dot(sg16, gup[:, cs], preferred_element_type=F32)
        kk = xk * kkw[:, cs]
        kk = kk * lax.rsqrt(_pair_sum(kk * kk) + 1e-12)
        kmod = xk * (1.0 + (asig - 1.0) * kaw[:, cs])
        bonus_s[pp] = _pair_sum(xr * kmod * rkw[:, cs]) * xv
        put(r_s, pp, xr)
        put(lw_s, pp, lw)
        put(k_s, pp, kmod)
        put(v_s, pp, xv)
        put(a_s, pp, -kk)
        put(b_s, pp, kk * asig)

    n2 = 2 * C
    ri = lax.broadcasted_iota(I32, (n2, n2), 0)
    ci = lax.broadcasted_iota(I32, (n2, n2), 1)
    same = (ri // C) == (ci // C)
    strict = same & ((ri % C) > (ci % C))
    incl = same & ((ri % C) >= (ci % C))
    eye = ri == ci
    eye_f = jnp.where(eye, 1.0, 0.0)
    tri = jnp.where(lax.broadcasted_iota(I32, (C, C), 0) >= lax.broadcasted_iota(I32, (C, C), 1), 1.0, 0.0
                    ).astype(BF16)
    zeros_sq = jnp.zeros((n2, LANES), F32)

    pc_, pg_, pi_, po_, ps_ = RW_PASSES

    def local(chains):
        each = lambda f, *cols: [f(*xs) for xs in zip(*cols)]
        lwc = [lw_s[pp, sl, :] for sl, pp in chains]
        cum = each(lambda l: _mm(tri, l, 1, pc_), lwc)
        cum_last = each(lambda c: c[C - 1:C, :], cum)
        ec, eci = each(jnp.exp, cum), each(lambda c: jnp.exp(-c), cum)
        ecp = each(lambda c, l: jnp.exp(c - l), cum, lwc)
        ecl = each(lambda c, cl: jnp.exp(cl - c), cum, cum_last)
        a_c = [a_s[pp, sl, :] for sl, pp in chains]
        b_c = [b_s[pp, sl, :] for sl, pp in chains]
        k_c = [k_s[pp, sl, :] for sl, pp in chains]
        r_c = [r_s[pp, sl, :] for sl, pp in chains]
        As = each(lambda a, e: _stack2(a * e), a_c, ecp)
        Rs = each(lambda r, e: _stack2(r * e), r_c, ec)
        Bs = each(lambda b, e: _stack2(b * e), b_c, eci)
        Ks = each(lambda k, e: _stack2(k * e), k_c, eci)
        Bt = each(lambda b, e: _stack2(b * e), b_c, ecl)
        Kt = each(lambda k, e: _stack2(k * e), k_c, ecl)
        Vs = [_stack2(v_s[pp, sl, :]) for sl, pp in chains]

        G = each(lambda a, r, b, k: _mm(jnp.concatenate([a, r], axis=0), jnp.concatenate([b, k], axis=0),
                                        pg_, pg_, NT), As, Rs, Bs, Ks)
        a_ab = each(lambda g: jnp.where(strict, g[:n2, :n2], 0.0), G)
        a_ak = each(lambda g: jnp.where(strict, g[:n2, n2:], 0.0), G)
        a_rb = each(lambda g: jnp.where(incl, g[n2:, :n2], 0.0), G)
        a_rk = each(lambda g: jnp.where(incl, g[n2:, n2:], 0.0), G)

        lp = a_ab
        tm_ = each(lambda a: eye_f + a, a_ab)
        step = 2
        while step < C:
            lp = each(lambda l: _mm(l, l, pi_, pi_), lp)
            tm_ = each(lambda t, l: t + _mm(t, l, pi_, pi_), tm_, lp)
            step *= 2

        w1 = each(lambda a, v: _mm(a, v, po_, po_), a_ak, Vs)
        mu_ = each(lambda t, a, w: _mm(t, jnp.concatenate([a, w], axis=1), po_, po_), tm_, As, w1)
        rhs = each(lambda m, v: jnp.concatenate([m, jnp.concatenate([zeros_sq, v], axis=1)], axis=0), mu_, Vs)
        lhs = each(lambda rb, rk, b, k: jnp.concatenate([jnp.concatenate([rb, rk], axis=1),
                                                         jnp.concatenate([b.T, k.T], axis=1)], axis=0),
                   a_rb, a_rk, Bt, Kt)
        out2 = each(lambda l, r: _mm(l, r, po_, po_), lhs, rhs)
        m23 = each(lambda r, o, cl: jnp.concatenate([r + o[:n2, :LANES],
                                                     jnp.where(eye, jnp.exp(cl), 0.0) + o[n2:, :LANES]], axis=0),
                   Rs, out2, cum_last)
        return [(m, o[:n2, LANES:], o[n2:, LANES:]) for m, o in zip(m23, out2)]

    def step_chunks(i, zs):
        sls = [pl.ds(pl.multiple_of((i * nchunk + j) * C, C), C) for j in range(nchunk)]
        parts = local([(sl, pp) for sl in sls for pp in range(npair)])
        zs = list(zs)
        for j, sl in enumerate(sls):
            for pp in range(npair):
                m23, y_loc, z_loc = parts[j * npair + pp]
                yz = _mm(m23, zs[pp], ps_, ps_)
                y = yz[:n2] + y_loc
                y_s[pp, sl, :] = y[:C] + y[C:]
                zs[pp] = yz[n2:] + z_loc
        return tuple(zs)

    zs = lax.fori_loop(0, t_pad // (C * nchunk), step_chunks, tuple(z0[pp] for pp in range(npair)))
    for pp in range(npair):
        zf_ref[pp] = zs[pp]
        cs = slice(pp * LANES, (pp + 1) * LANES)
        y = y_s[pp, 0:t_real, :]
        mean = _pair_sum(y) * (1.0 / HEAD_DIM)
        dlt = y - mean
        var = _pair_sum(dlt * dlt) * (1.0 / HEAD_DIM)
        yn = dlt * lax.rsqrt(var + GN_EPS) * lnw[:, cs] + lnb[:, cs]
        o_ref[:, cs] = ((yn + bonus_s[pp]) * g_s[pp]).astype(o_ref.dtype)


def _rwkv(P, nb, t, prev, mu, w0, a0, k_k, k_a, r_k, lnx_w, lnx_b, wup, aup, gup, z0):
    t_pad = max(t, RW_CHUNK)
    assert t % 8 == 0 and t_pad % RW_CHUNK == 0
    n_chunks = t_pad // RW_CHUNK
    nchunk = min(RW_INTERLEAVE // RW_PAIRS_LONG, n_chunks)
    npair = min(N_PAIRS, max(1, RW_INTERLEAVE // nchunk))
    wp = npair * LANES

    def cblk(c0, w, per_pair):
        return (lambda p: c0 // w + p) if per_pair else (lambda p: c0 // w)

    def pcol(c0, w, pp):
        f = cblk(c0, w, pp)
        return pl.BlockSpec((t, w), lambda b, p: (b, f(p)))

    def prevcol(c0, w, pp):
        f = cblk(c0, w, pp)
        return pl.BlockSpec((None, 1, w), lambda b, p: (b, 0, f(p)))

    def mucol(c0, w, pp):
        f = cblk(c0, w, pp)
        return pl.BlockSpec((1, w), lambda b, p: (0, f(p)))

    def hvec():
        return pl.BlockSpec((1, wp), lambda b, p: (0, p))

    cols = [(C_R, wp, True), (C_K, wp, True), (C_V, wp, True), (C_G, 256, False), (C_M, LANES, False)]
    in_specs = ([pcol(*c) for c in cols] + [prevcol(*c) for c in cols] + [mucol(*c) for c in cols]
                + [hvec() for _ in range(7)]
                + [pl.BlockSpec((LANES, wp), lambda b, p: (0, p)),
                   pl.BlockSpec((LANES, wp), lambda b, p: (0, p)),
                   pl.BlockSpec((256, wp), lambda b, p: (0, p)),
                   pl.BlockSpec((None, npair, LANES, LANES), lambda b, p: (b, p, 0, 0))])
    vecs = [v.reshape(1, D_MODEL) for v in (w0, a0, k_k, k_a, r_k, lnx_w, lnx_b)]
    o, zf = pl.pallas_call(
        functools.partial(_rwkv_kernel, t, npair, nchunk),
        out_shape=(jax.ShapeDtypeStruct((nb * t, D_MODEL), BF16),
                   jax.ShapeDtypeStruct((nb, N_PAIRS, LANES, LANES), F32)),
        grid=(nb, N_PAIRS // npair),
        in_specs=in_specs,
        out_specs=(pl.BlockSpec((t, wp), lambda b, p: (b, p)),
                   pl.BlockSpec((None, npair, LANES, LANES), lambda b, p: (b, p, 0, 0))),
        scratch_shapes=([pltpu.VMEM((npair, t_pad, LANES), F32) for _ in range(7)]
                        + [pltpu.VMEM((npair, t, LANES), F32) for _ in range(2)]),
        compiler_params=_cparams(("parallel", "arbitrary")),
    )(P, P, P, P, P, prev, prev, prev, prev, prev, mu, mu, mu, mu, mu, *vecs, wup, aup, gup, z0)
    return o, zf


def _state_to_pairs(s):
    nb = s.shape[0]
    zt = jnp.swapaxes(s, -1, -2).reshape(nb, N_PAIRS, 2, HEAD_DIM, HEAD_DIM)
    zero = jnp.zeros_like(zt[:, :, 0])
    top = jnp.concatenate([zt[:, :, 0], zero], axis=-1)
    bot = jnp.concatenate([zero, zt[:, :, 1]], axis=-1)
    return jnp.concatenate([top, bot], axis=-2)


def _pairs_to_state(z):
    nb = z.shape[0]
    h0 = z[:, :, :HEAD_DIM, :HEAD_DIM]
    h1 = z[:, :, HEAD_DIM:, HEAD_DIM:]
    s = jnp.stack([h0, h1], axis=2).reshape(nb, N_HEADS, HEAD_DIM, HEAD_DIM)
    return jnp.swapaxes(s, -1, -2)


def _rope(x, cos, sin_signed):
    w = x.shape[1]
    reps = w // LANES
    cw = jnp.concatenate([cos] * reps, axis=1) if reps > 1 else cos
    sw = jnp.concatenate([sin_signed] * reps, axis=1) if reps > 1 else sin_signed
    lane = lax.broadcasted_iota(I32, x.shape, 1)
    fwd = pltpu.roll(x, w - 32, 1)
    bwd = pltpu.roll(x, 32, 1)
    partner = jnp.where((lane % HEAD_DIM) < 32, fwd, bwd)
    return x * cw + partner * sw


def _head_rms(x, nw, e_dn, e_up):
    ms = _mm(x * x, e_dn, 2, 1) * (1.0 / HEAD_DIM)
    r = lax.rsqrt(ms + EPS)
    return x * _mm(r, e_up, 2, 1) * nw


def _dsa_prep_kernel(pq, pkd, pvd, pqi, pkw, cos_ref, sin_ref, qn, kn, edn, eup,
                     q16, k32, k16, v32, v16, qi16, kw32, ki2):
    cos, sin = cos_ref[...], sin_ref[...]
    e_dn, e_up = edn[...], eup[...]
    def put_pairs(ref, x):
        for p in range(N_PAIRS):
            ref[p] = x[:, p * LANES:(p + 1) * LANES].astype(ref.dtype)

    q = _rope(_head_rms(pq[...], qn[...], e_dn, e_up), cos, sin)
    put_pairs(q16, q * (HEAD_DIM ** -0.5 * LOG2E))
    k = _rope(_head_rms(pkd[...], kn[...], e_dn, e_up), cos, sin)
    k32[...] = k
    put_pairs(k16, k)
    v = pvd[...]
    v32[...] = v
    put_pairs(v16, v)
    qi16[...] = _rope(pqi[...], cos, sin).astype(BF16)
    kw = pkw[...]
    lane = lax.broadcasted_iota(I32, kw.shape, 1)
    wi_scale = (IDX_HEADS * IDX_DIM) ** -0.5
    kr = _rope(kw, cos, sin)
    kw32[...] = jnp.where(lane < IDX_DIM, kr, jnp.where(lane < IDX_DIM + IDX_HEADS, kw * wi_scale, 0.0))
    ki2[...] = jnp.where(lane < IDX_DIM, kr, pltpu.roll(kr, IDX_DIM, 1)).astype(BF16)


def _dsa_prep(P, pos_rows, q_norm_w, k_norm_w):
    n = P.shape[0]
    tm = 512 if n % 512 == 0 else n
    half = HEAD_DIM // 2
    inv = ROPE_THETA ** (-jnp.arange(half, dtype=F32) / half)
    ang = pos_rows.astype(F32)[:, None] * inv[None, :]
    cos = jnp.tile(jnp.cos(ang), (1, 4))
    sin = jnp.sin(ang)
    sin_signed = jnp.tile(jnp.concatenate([-sin, sin], axis=1), (1, 2))
    head_of = jnp.arange(D_MODEL) // HEAD_DIM
    e_dn = (head_of[:, None] == jnp.arange(LANES)[None, :]).astype(BF16)
    e_up = e_dn.T
    qn = jnp.tile(q_norm_w, N_HEADS).reshape(1, D_MODEL)
    kn = jnp.tile(k_norm_w, N_HEADS).reshape(1, D_MODEL)

    def col(c0, w):
        return pl.BlockSpec((tm, w), lambda i, c0=c0, w=w: (i, c0 // w))

    def row(w):
        return pl.BlockSpec((tm, w), lambda i: (i, 0))

    def const(shape):
        return pl.BlockSpec(shape, lambda i: (0, 0))

    pairs = jax.ShapeDtypeStruct((N_PAIRS, n, LANES), BF16)
    pair_spec = pl.BlockSpec((N_PAIRS, tm, LANES), lambda i: (0, i, 0))
    return pl.pallas_call(
        _dsa_prep_kernel,
        out_shape=(pairs, jax.ShapeDtypeStruct((n, D_MODEL), F32), pairs, jax.ShapeDtypeStruct((n, D_MODEL), F32),
                   pairs, jax.ShapeDtypeStruct((n, IDX_HEADS * IDX_DIM), BF16),
                   jax.ShapeDtypeStruct((n, LANES), F32), jax.ShapeDtypeStruct((n, LANES), BF16)),
        grid=(n // tm,),
        in_specs=[col(C_Q, 1024), col(C_KD, 1024), col(C_VD, 1024), col(C_QI, 512), col(C_KW, LANES),
                  row(LANES), row(LANES), const((1, D_MODEL)), const((1, D_MODEL)),
                  const((D_MODEL, LANES)), const((LANES, D_MODEL))],
        out_specs=(pair_spec, row(D_MODEL), pair_spec, row(D_MODEL), pair_spec, row(512), row(LANES), row(LANES)),
        compiler_params=_cparams(("parallel",)),
    )(P, P, P, P, P, cos, sin_signed, qn, kn, e_dn, e_up)


CODE_NEG_INF = -1 - 0x7F800000
BITS_PER_CHECK = 2
ATTN_TQ = 256
ATTN_PAIRS = 2
ATTN_CASES = 4
SAMPLE_PAIRS = 8


def _index_scores(qi, wi, ki_list):
    outs = []
    for ki in ki_list:
        acc = None
        for h in range(IDX_HEADS):
            qpair = qi[:, (h // 2) * LANES:(h // 2 + 1) * LANES]
            lo = _lane_lo(qpair.shape)
            qh = jnp.where(lo if h % 2 == 0 else jnp.logical_not(lo), qpair, jnp.zeros_like(qpair))
            rel = lax.dot_general(qh, ki, NT, preferred_element_type=F32)
            term = wi[:, IDX_DIM + h:IDX_DIM + h + 1] * jnp.maximum(rel, 0.0)
            acc = term if acc is None else acc + term
        outs.append(acc)
    return outs


def _select_topk(keys, topk, bias_refs):
    tq = keys[0].shape[0]
    neg = -jnp.inf

    def write(masks):
        for ref, k, msk in zip(bias_refs, keys, masks):
            ref[:, 0:k.shape[1]] = jnp.where(msk, 0.0, neg)

    def count(pred_list):
        tot = None
        for p in pred_list:
            c = jnp.sum(jnp.where(p, 1.0, 0.0), axis=-1, keepdims=True)
            tot = c if tot is None else tot + c
        return tot

    def threshold(c):
        bits = jnp.where(c >= 0, c, c ^ jnp.int32(0x7FFFFFFF))
        return jnp.where(c < jnp.int32(CODE_NEG_INF), neg, lax.bitcast_convert_type(bits, F32))

    few = count([k > neg for k in keys]) <= topk

    def pending(cnt):
        return jnp.max(jnp.where(few | (cnt == topk), 0.0, 1.0))

    def bit_step(state):
        i, c, cnt, _ = state
        for b in range(BITS_PER_CHECK):
            trial = c + jnp.left_shift(jnp.int32(1), 31 - (i + b))
            cnt_t = count([k >= threshold(trial) for k in keys])
            take = cnt_t >= topk
            cnt = jnp.where(take, cnt_t, cnt)
            c = jnp.where(take, trial, c)
        return i + BITS_PER_CHECK, c, cnt, pending(cnt)

    cnt0 = jnp.full((tq, 1), float(sum(k.shape[1] for k in keys)), F32)
    state = (jnp.int32(0), jnp.full((tq, 1), -2 ** 31, I32), cnt0, pending(cnt0))
    _, code, _, _ = lax.while_loop(lambda s: (s[0] < 32) & (s[3] > 0.0), bit_step, state)
    thr = threshold(code)
    ge = [(k >= thr) & (k > neg) for k in keys]
    write(ge)
    surplus = jnp.max(count(ge)) > topk

    @pl.when(surplus)
    def _():
        gt = [k > thr for k in keys]
        need = topk - count(gt)
        ties = [(k == thr) & (k > neg) for k in keys]
        offs, idx = 0, []
        for k in keys:
            idx.append(lax.broadcasted_iota(I32, k.shape, 1) + offs)
            offs += k.shape[1]
        nbits = max(1, (offs - 1).bit_length() + 1)

        def idx_step(i, m):
            trial = m + jnp.left_shift(jnp.int32(1), nbits - 1 - i)
            cnt = count([t & (ix < trial) for t, ix in zip(ties, idx)])
            return jnp.where(cnt <= need, trial, m)

        cut = lax.fori_loop(0, nbits, idx_step, jnp.zeros((tq, 1), I32))
        write([g | (t & (ix < cut)) for g, t, ix in zip(gt, ties, idx)])


def _attend_pairs(q_pairs, k_lists, v_lists, bias_list):
    lo = _lane_lo(q_pairs[0].shape)
    zero = jnp.zeros_like(q_pairs[0])
    heads = []
    for pi, q in enumerate(q_pairs):
        heads += [(jnp.where(lo, q, zero), pi), (jnp.where(lo, zero, q), pi)]
    s = [[lax.dot_general(qh, k, NT, preferred_element_type=F32) + b for k, b in zip(k_lists[pi], bias_list)]
         for qh, pi in heads]
    m = []
    for sh in s:
        mh = None
        for sj in sh:
            mx = jnp.max(sj, axis=-1, keepdims=True)
            mh = mx if mh is None else jnp.maximum(mh, mx)
        m.append(mh)
    p = [[jnp.exp2(sj - mh) for sj in sh] for sh, mh in zip(s, m)]
    den = [functools.reduce(lambda a, b: a + b, [jnp.sum(pj, axis=-1, keepdims=True) for pj in ph]) for ph in p]
    acc = [functools.reduce(lambda a, b: a + b,
                            [jnp.dot(pj.astype(BF16), v, preferred_element_type=F32)
                             for pj, v in zip(ph, v_lists[pi])])
           for ph, (_, pi) in zip(p, heads)]
    outs = [a / d for a, d in zip(acc, den)]
    return [jnp.where(lo, outs[2 * i], outs[2 * i + 1]) for i in range(len(q_pairs))]


def _attn_prompt_kernel(topk, ncase, q_ref, qi_ref, kw_ref, k_ref, v_ref, ki2_ref, o_ref, bias_s):
    tq = q_ref.shape[1]
    t = k_ref.shape[1]
    i = pl.program_id(1)
    lstep = t // ncase
    case = ((i + 1) * tq - 1) // lstep

    def run(L):
        score = _index_scores(qi_ref[...], kw_ref[...], [ki2_ref[0:L, :]])[0]
        qpos = i * tq + lax.broadcasted_iota(I32, (tq, L), 0)
        kpos = lax.broadcasted_iota(I32, (tq, L), 1)
        adm = (qpos // CHUNK) >= (kpos // CHUNK)
        _select_topk([jnp.where(adm, score, -jnp.inf)], topk, [bias_s])

        def pairs(g, carry):
            ps = [ATTN_PAIRS * g + j for j in range(ATTN_PAIRS)]
            outs = _attend_pairs([q_ref[p] for p in ps], [[k_ref[p, 0:L, :]] for p in ps],
                                 [[v_ref[p, 0:L, :]] for p in ps], [bias_s[:, 0:L]])
            for p, o in zip(ps, outs):
                o_ref[p] = o.astype(o_ref.dtype)
            return carry

        lax.fori_loop(0, N_PAIRS // ATTN_PAIRS, pairs, 0)

    for c in range(ncase):
        pl.when(case == c)(functools.partial(run, (c + 1) * lstep))


def _attn_prompt(q16, qi16, kw32, k16, v16, ki2, nb, t):
    tq = min(ATTN_TQ, t)
    topk = min(TOPK_MAX, t // 4)
    nq = t // tq
    ncase = min(ATTN_CASES, nq)

    def qrow(w):
        return pl.BlockSpec((tq, w), lambda b, i: (b * nq + i, 0))

    def qpairs():
        return pl.BlockSpec((N_PAIRS, tq, LANES), lambda b, i: (0, b * nq + i, 0))

    def kpairs():
        return pl.BlockSpec((N_PAIRS, t, LANES), lambda b, i: (0, b, 0))

    return pl.pallas_call(
        functools.partial(_attn_prompt_kernel, topk, ncase),
        out_shape=jax.ShapeDtypeStruct((N_PAIRS, nb * t, LANES), BF16),
        grid=(nb, nq),
        in_specs=[qpairs(), qrow(512), qrow(LANES), kpairs(), kpairs(),
                  pl.BlockSpec((t, LANES), lambda b, i: (b, 0))],
        out_specs=qpairs(),
        scratch_shapes=[pltpu.VMEM((tq, t), F32)],
        compiler_params=_cparams(("parallel", "arbitrary")),
    )(q16, qi16, kw32, k16, v16, ki2)


def _attn_sample_kernel(topk, past, q_ref, qi_ref, kw_ref, ck_ref, cv_ref, cki2_ref, k_ref, v_ref, ki2_ref, o_ref,
                        biasc_s, biasn_s):
    npairs, ts = q_ref.shape[0], q_ref.shape[1]

    @pl.when(pl.program_id(1) == 0)
    def _():
        sc, sn = _index_scores(qi_ref[...], kw_ref[...], [cki2_ref[...], ki2_ref[...]])
        qpos = past + lax.broadcasted_iota(I32, (ts, 1), 0)
        kpos_c = lax.broadcasted_iota(I32, sc.shape, 1)
        kpos_n = past + lax.broadcasted_iota(I32, sn.shape, 1)
        keys = [jnp.where((qpos // CHUNK) >= (kpos_c // CHUNK), sc, -jnp.inf),
                jnp.where((qpos // CHUNK) >= (kpos_n // CHUNK), sn, -jnp.inf)]
        _select_topk(keys, topk, [biasc_s, biasn_s])

    lanes = [slice(p * LANES, (p + 1) * LANES) for p in range(npairs)]
    outs = _attend_pairs([q_ref[p] for p in range(npairs)],
                         [[ck_ref[:, cs].astype(BF16), k_ref[p]] for p, cs in enumerate(lanes)],
                         [[cv_ref[:, cs].astype(BF16), v_ref[p]] for p, cs in enumerate(lanes)],
                         [biasc_s[...], biasn_s[...]])
    for p, o in enumerate(outs):
        o_ref[p] = o.astype(o_ref.dtype)


def _attn_sample(q16, qi16, kw32, k16, v16, ki2, cache_k, cache_v, cache_kidx, nb, ts):
    past = cache_k.shape[1]
    cki2 = jnp.concatenate([cache_kidx, cache_kidx], axis=-1).astype(BF16)
    topk = min(TOPK_MAX, (past + ts) // 4)

    def qrow(w):
        return pl.BlockSpec((ts, w), lambda b, p: (b, 0))

    sp = SAMPLE_PAIRS

    def qpair():
        return pl.BlockSpec((sp, ts, LANES), lambda b, p: (p, b, 0))

    def cache(pair):
        if pair:
            return pl.BlockSpec((None, past, sp * LANES), lambda b, p: (b, 0, p))
        return pl.BlockSpec((None, past, LANES), lambda b, p: (b, 0, 0))

    return pl.pallas_call(
        functools.partial(_attn_sample_kernel, topk, past),
        out_shape=jax.ShapeDtypeStruct((N_PAIRS, nb * ts, LANES), BF16),
        grid=(nb, N_PAIRS // sp),
        in_specs=[qpair(), qrow(512), qrow(LANES), cache(True), cache(True), cache(False),
                  qpair(), qpair(), qrow(LANES)],
        out_specs=qpair(),
        scratch_shapes=[pltpu.VMEM((ts, past), F32), pltpu.VMEM((ts, ts), F32)],
        compiler_params=_cparams(("parallel", "arbitrary")),
    )(q16, qi16, kw32, cache_k, cache_v, cki2, k16, v16, ki2)


def _merge_kernel(x_ref, oa_ref, ob_ref, pga_ref, pgb_ref, bga_ref, bgb_ref, g1_ref, sc2_ref, sh2_ref, nw_ref,
                  wpa_ref, wpb_ref, wout_ref, x1_ref, h2_ref):
    ga = _sigmoid(pga_ref[...] + bga_ref[...])
    gb = _sigmoid(pgb_ref[...] + bgb_ref[...])
    ob = jnp.concatenate([ob_ref[p] for p in range(N_PAIRS)], axis=1)
    m = (ga * jnp.dot(oa_ref[...], wpa_ref[...], preferred_element_type=F32)
         + gb * jnp.dot(ob, wpb_ref[...], preferred_element_type=F32))
    x1 = x_ref[...] + g1_ref[...] * jnp.dot(m.astype(BF16), wout_ref[...], preferred_element_type=F32)
    x1_ref[...] = x1
    y = x1 * lax.rsqrt(jnp.mean(x1 * x1, axis=-1, keepdims=True) + EPS) * nw_ref[...]
    h2_ref[...] = (y * (1.0 + sc2_ref[...]) + sh2_ref[...]).astype(BF16)


def _merge(x2, o_a, o_b, P, b_gate, g1, sc2, sh2, nw2, wpa, wpb, wout, seq_len):
    n, d = x2.shape
    tm = _row_tile(n, seq_len, 512)
    g1_a, g1_s = _seq_operand(g1, seq_len, tm)
    sc_a, sc_s = _seq_operand(sc2, seq_len, tm)
    sh_a, sh_s = _seq_operand(sh2, seq_len, tm)

    def row():
        return pl.BlockSpec((tm, d), lambda i: (i, 0))

    def const(shape):
        return pl.BlockSpec(shape, lambda i: (0, 0))

    bg = b_gate.reshape(1, 2 * d)
    return pl.pallas_call(
        _merge_kernel,
        out_shape=(jax.ShapeDtypeStruct((n, d), F32), jax.ShapeDtypeStruct((n, d), BF16)),
        grid=(n // tm,),
        in_specs=[row(), row(), pl.BlockSpec((N_PAIRS, tm, LANES), lambda i: (0, i, 0)),
                  pl.BlockSpec((tm, d), lambda i: (i, C_GA // d)), pl.BlockSpec((tm, d), lambda i: (i, C_GB // d)),
                  pl.BlockSpec((1, d), lambda i: (0, 0)), pl.BlockSpec((1, d), lambda i: (0, 1)),
                  g1_s, sc_s, sh_s, const((1, d)), const((d, d)), const((d, d)), const((d, d))],
        out_specs=(row(), row()),
        compiler_params=_cparams(("parallel",)),
    )(x2, o_a, o_b, P, P, bg, bg, g1_a, sc_a, sh_a, nw2.reshape(1, d), wpa, wpb, wout)


def _top_exact(s, k):
    rows = lax.broadcasted_iota(I32, s.shape, 0).astype(F32)
    cur = s
    rank = jnp.full(s.shape, float(k), F32)
    vals = []
    for r in range(k):
        m = jnp.max(cur, axis=0, keepdims=True)
        first = jnp.min(jnp.where(cur == m, rows, 1e9), axis=0, keepdims=True)
        hit = rows == first
        vals.append(m)
        rank = jnp.where(hit, float(r), rank)
        cur = jnp.where(hit, -jnp.inf, cur)
    return vals, rank


def _top_fast(ss, k):
    curs = list(ss)
    ranks = [jnp.full(s.shape, float(k), F32) for s in ss]
    vals = [[] for _ in ss]
    for r in range(k):
        ms = [jnp.max(c, axis=0, keepdims=True) for c in curs]
        hits = [c == m for c, m in zip(curs, ms)]
        ranks = [jnp.where(h, float(r), rk) for h, rk in zip(hits, ranks)]
        curs = [jnp.where(h, -jnp.inf, c) for h, c in zip(hits, curs)]
        for v, m in zip(vals, ms):
            v.append(m)
    cleans = [jnp.max(jnp.abs(jnp.sum(jnp.where(rk < k, 1.0, 0.0), axis=0, keepdims=True) - k)) == 0.0
              for rk in ranks]
    return vals, ranks, cleans


def _top(src_scr, k, vals_scr, rank_scr, redo_s):
    n = src_scr.shape[0]
    vals, ranks, cleans = _top_fast([src_scr[i] for i in range(n)], k)
    for i in range(n):
        vals_scr[i] = jnp.concatenate(vals[i], axis=0)
        rank_scr[i] = ranks[i]
        redo_s[i] = jnp.where(cleans[i], 0, 1).astype(I32)

    def redo(i, carry):
        @pl.when(redo_s[i] == 1)
        def _():
            vals_e, rank_e = _top_exact(src_scr[i], k)
            vals_scr[i] = jnp.concatenate(vals_e, axis=0)
            rank_scr[i] = rank_e

        return carry

    lax.fori_loop(0, n, redo, 0)


def _peer_sel_kernel(h_ref, wpqt_ref, kbd_ref, g_ref, cnt_ref, r2_ref, p2_ref, s_scr, vals_scr, rank_scr,
                     cand_scr, cvals_scr, crank_scr, redo_s):
    K = PEER_TOPK
    tm = h_ref.shape[0]
    qt = lax.dot_general(wpqt_ref[...], h_ref[...], NT, preferred_element_type=F32)
    s_scr[...] = jnp.dot(kbd_ref[...], qt.astype(BF16), preferred_element_type=F32
                         ).reshape(2 * PEER_HEADS, PEER_NKEYS, tm)
    sub8 = lax.broadcasted_iota(I32, (8, tm), 0)
    neg = jnp.full((8, tm), -jnp.inf, F32)
    _top(s_scr, K, vals_scr, rank_scr, redo_s)
    for hd in range(PEER_HEADS):
        c1, c2 = vals_scr[2 * hd], vals_scr[2 * hd + 1]
        blocks = [c1[0:1] + c2, c1[1:2] + c2[0:8]]
        for k1 in range(2, 8):
            blocks.append(jnp.where(sub8 < K // (k1 + 1), c1[k1:k1 + 1] + c2[0:8], neg))
        blocks.append(c1[8:16] + c2[0:1])
        cand_scr[hd] = jnp.concatenate(blocks, axis=0)
    _top(cand_scr, K, cvals_scr, crank_scr, redo_s)
    for hd in range(PEER_HEADS):
        s1, s2 = s_scr[2 * hd], s_scr[2 * hd + 1]
        c1, c2 = vals_scr[2 * hd], vals_scr[2 * hd + 1]
        rank1, rank2 = rank_scr[2 * hd], rank_scr[2 * hd + 1]
        cand = cand_scr[hd]
        taken = crank_scr[hd] < K
        z = jnp.sum(jnp.where(taken, jnp.exp(cand - (c1[0:1] + c2[0:1])), 0.0), axis=0, keepdims=True)
        tk = jnp.where(taken, 1.0, 0.0)
        per_k1 = [jnp.sum(tk[0:16], axis=0, keepdims=True)]
        per_k1 += [jnp.sum(tk[8 + 8 * k1:16 + 8 * k1], axis=0, keepdims=True) for k1 in range(1, 8)]
        cnt16 = jnp.concatenate(per_k1 + [tk[72:80]], axis=0)
        cnt = jnp.zeros(s1.shape, F32)
        for k1 in range(K):
            cnt = jnp.where(rank1 == float(k1), cnt16[k1:k1 + 1], cnt)
        g_ref[hd] = jnp.where(rank1 < K, jnp.exp(s1 - c1[0:1]) / z, 0.0)
        cnt_ref[hd] = cnt
        p2 = jnp.where(rank2 < K, jnp.exp(s2 - c2[0:1]), 0.0)
        cb = r2_ref.shape[-1]
        for tc in range(tm // cb):
            r2_ref[hd, tc] = rank2[:, tc * cb:(tc + 1) * cb].astype(r2_ref.dtype)
            p2_ref[hd, tc] = p2[:, tc * cb:(tc + 1) * cb].astype(p2_ref.dtype)


def _peer_select(h2, wpqt, kbd):
    n, d = h2.shape
    tm = 256 if n % 256 == 0 else n
    cb = min(LANES, tm)
    big = jax.ShapeDtypeStruct((PEER_HEADS, PEER_NKEYS, n), F32)
    blocked = jax.ShapeDtypeStruct((PEER_HEADS, n // cb, PEER_NKEYS, cb), BF16)

    def blk():
        return pl.BlockSpec((PEER_HEADS, PEER_NKEYS, tm), lambda i: (0, 0, i))

    def blk4():
        return pl.BlockSpec((PEER_HEADS, tm // cb, PEER_NKEYS, cb), lambda i: (0, i, 0, 0))

    return pl.pallas_call(
        _peer_sel_kernel,
        out_shape=(big, big, blocked, blocked),
        grid=(n // tm,),
        in_specs=[pl.BlockSpec((tm, d), lambda i: (i, 0)),
                  pl.BlockSpec((d, d), lambda i: (0, 0)),
                  pl.BlockSpec((2 * d, d), lambda i: (0, 0))],
        out_specs=(blk(), blk(), blk4(), blk4()),
        scratch_shapes=[pltpu.VMEM((2 * PEER_HEADS, PEER_NKEYS, tm), F32),
                        pltpu.VMEM((2 * PEER_HEADS, PEER_TOPK, tm), F32),
                        pltpu.VMEM((2 * PEER_HEADS, PEER_NKEYS, tm), F32),
                        pltpu.VMEM((PEER_HEADS, PEER_CAND, tm), F32),
                        pltpu.VMEM((PEER_HEADS, PEER_TOPK, tm), F32),
                        pltpu.VMEM((PEER_HEADS, PEER_CAND, tm), F32),
                        pltpu.SMEM((2 * PEER_HEADS,), I32)],
        compiler_params=_cparams(("parallel",)),
    )(h2, wpqt, kbd)


def _gelu_tanh(x):
    return 0.5 * x * (1.0 + jnp.tanh(0.7978845608028654 * (x + 0.044715 * (x * x * x))))


def _peer_main_kernel(ni1, h_ref, x1_ref, g2_ref, u_ref, vt_ref, g_ref, cnt_ref, r2_ref, p2_ref, y_ref, acc, gate_s):
    j = pl.program_id(1)
    tm = h_ref.shape[0]

    @pl.when(j == 0)
    def _():
        acc[...] = jnp.zeros_like(acc)

    cb = r2_ref.shape[-1]
    reps = PEER_NKEYS // 16
    zero = jnp.zeros((PEER_NKEYS, cb), BF16)
    for l in range(ni1):
        for tc in range(tm // cb):
            ts = slice(tc * cb, (tc + 1) * cb)
            w = None
            for hd in range(PEER_HEADS):
                c16 = jnp.broadcast_to(cnt_ref[hd, l:l + 1, ts], (16, cb)).astype(BF16)
                g16 = jnp.broadcast_to(g_ref[hd, l:l + 1, ts], (16, cb)).astype(BF16)
                t = (jnp.where(r2_ref[hd, tc] < jnp.concatenate([c16] * reps, axis=0), p2_ref[hd, tc], zero)
                     * jnp.concatenate([g16] * reps, axis=0))
                w = t if w is None else w + t
            gate_s[tc, l * PEER_NKEYS:(l + 1) * PEER_NKEYS, :] = w

    act = lax.dot_general(u_ref[...], h_ref[...], NT, preferred_element_type=F32)
    gate = jnp.concatenate([gate_s[tc] for tc in range(tm // cb)], axis=1)
    coef = gate * _gelu_tanh(act.astype(BF16))
    acc[...] += jnp.dot(vt_ref[...], coef, preferred_element_type=F32)

    @pl.when(j == pl.num_programs(1) - 1)
    def _():
        y_ref[...] = x1_ref[...] + g2_ref[...] * acc[...].T


def _peer_main(h2, x1, g2, u16, vt16, g, cnt, r2, p2, seq_len):
    n, d = h2.shape
    tm = _row_tile(n, seq_len, 512)
    ni1 = 16
    et = ni1 * PEER_NKEYS
    cb = r2.shape[-1]
    g2_a, g2_s = _seq_operand(g2, seq_len, tm)

    def row():
        return pl.BlockSpec((tm, d), lambda i, j: (i, 0))

    return pl.pallas_call(
        functools.partial(_peer_main_kernel, ni1),
        out_shape=jax.ShapeDtypeStruct((n, d), F32),
        grid=(n // tm, N_EXPERTS // et),
        in_specs=[row(), row(), g2_s,
                  pl.BlockSpec((et, d), lambda i, j: (j, 0)),
                  pl.BlockSpec((d, et), lambda i, j: (0, j)),
                  pl.BlockSpec((PEER_HEADS, ni1, tm), lambda i, j: (0, j, i)),
                  pl.BlockSpec((PEER_HEADS, ni1, tm), lambda i, j: (0, j, i)),
                  pl.BlockSpec((PEER_HEADS, tm // cb, PEER_NKEYS, cb), lambda i, j: (0, i, 0, 0)),
                  pl.BlockSpec((PEER_HEADS, tm // cb, PEER_NKEYS, cb), lambda i, j: (0, i, 0, 0))],
        out_specs=row(),
        scratch_shapes=[pltpu.VMEM((d, tm), F32), pltpu.VMEM((tm // cb, et, cb), BF16)],
        compiler_params=_cparams(("parallel", "arbitrary")),
    )(h2, x1, g2_a, u16, vt16, g, cnt, r2, p2)


def _layer(x, mod, pos, shift_prev, s0, cache, lw):
    nb, t, d = x.shape
    n = nb * t
    sh1, sc1, g1, sh2, sc2, g2 = [mod[:, i * d:(i + 1) * d] for i in range(6)]
    x2 = x.reshape(n, d)
    P = _inproj(x2, sc1, sh1, lw['norm1_w'], lw['w_in16'], t)

    prev = _pack_rw(shift_prev).reshape(nb, 1, P_COLS)
    o_a, zf = _rwkv(P, nb, t, prev, lw['mu'], lw['w0'], lw['a0'], lw['k_k'], lw['k_a'], lw['r_k'], lw['lnx_w'],
                    lw['lnx_b'], lw['wup'], lw['aup'], lw['gup'], _state_to_pairs(s0))
    wkv = _pairs_to_state(zf)
    shift_last = _unpack_rw(P.reshape(nb, t, P_COLS)[:, -1, :])

    q16, k32, k16, v32, v16, qi16, kw32, ki2 = _dsa_prep(P, jnp.tile(pos, nb), lw['q_norm_w'], lw['k_norm_w'])
    if cache is None:
        o_b = _attn_prompt(q16, qi16, kw32, k16, v16, ki2, nb, t)
    else:
        ck, cv, cki = cache
        past = ck.shape[1]
        o_b = _attn_sample(q16, qi16, kw32, k16, v16, ki2, ck.reshape(nb, past, d), cv.reshape(nb, past, d), cki,
                           nb, t)

    x1, h2 = _merge(x2, o_a, o_b, P, lw['b_gate'], g1, sc2, sh2, lw['norm2_w'], lw['wpa'], lw['wpb'], lw['wout'], t)
    g, cnt, r2, p2 = _peer_select(h2, lw['wpqt'], lw['kbd'])
    y = _peer_main(h2, x1, g2, lw['u16'], lw['vt16'], g, cnt, r2, p2, t)

    k_new = k32.reshape(nb, t, N_HEADS, HEAD_DIM)
    v_new = v32.reshape(nb, t, N_HEADS, HEAD_DIM)
    ki_new = kw32[:, :IDX_DIM].reshape(nb, t, IDX_DIM)
    return y.reshape(nb, t, d), wkv, shift_last, k_new, v_new, ki_new


def _layer_weights(l, w_in, b_gate, mu_rw, w0, w_up, a0, a_up, g_up, k_k, k_a, r_k, lnx_w, lnx_b, q_norm_w, k_norm_w,
                   w_proj_a, w_proj_b, w_out, norm1_w, norm2_w, w_pq, peer_keys, peer_u, peer_v):
    d = D_MODEL
    zeros = lambda r: jnp.zeros((r, d), F32)
    keys = peer_keys[l].reshape(2 * PEER_HEADS, PEER_NKEYS, PEER_DHALF)
    eye = jnp.eye(2 * PEER_HEADS, dtype=F32)
    kbd = (eye[:, None, :, None] * keys[:, :, None, :]).reshape(2 * d, d)
    return {
        'w_in16': _pack_in(w_in[l]).astype(BF16), 'b_gate': b_gate[l], 'mu': _pack_rw(mu_rw[l]).reshape(1, P_COLS),
        'w0': w0[l], 'a0': a0[l], 'k_k': k_k[l], 'k_a': k_a[l], 'r_k': r_k[l].reshape(d), 'lnx_w': lnx_w[l],
        'lnx_b': lnx_b[l],
        'wup': jnp.concatenate([w_up[l], zeros(LANES - D_DECAY)], axis=0).astype(BF16),
        'aup': jnp.concatenate([zeros(D_DECAY), a_up[l]], axis=0).astype(BF16),
        'gup': jnp.concatenate([g_up[l], zeros(256 - D_GATE)], axis=0).astype(BF16),
        'q_norm_w': q_norm_w[l], 'k_norm_w': k_norm_w[l], 'norm1_w': norm1_w[l], 'norm2_w': norm2_w[l],
        'wpa': w_proj_a[l].astype(BF16), 'wpb': w_proj_b[l].astype(BF16), 'wout': w_out[l].astype(BF16),
        'wpqt': w_pq[l].T.astype(BF16), 'kbd': kbd.astype(BF16),
        'u16': peer_u[l].astype(BF16), 'vt16': peer_v[l].T.astype(BF16),
    }


def kernel(x_prompt, x_sample, c_prompt, c_sample, cache_k, cache_v, cache_kidx, state_wkv, state_shift, w_ada, b_ada,
           norm1_w, w_in, b_gate, mu_rw, w0, w_up, a0, a_up, g_up, k_k, k_a, r_k, lnx_w, lnx_b, q_norm_w, k_norm_w,
           w_proj_a, w_proj_b, w_out, norm2_w, w_pq, peer_keys, peer_u, peer_v):
    depth = w_in.shape[0]
    bp, tp = x_prompt.shape[:2]
    bs, ts = x_sample.shape[:2]
    past = cache_k.shape[2]
    dt = x_prompt.dtype
    pos_p = jnp.arange(tp, dtype=I32)
    pos_s = past + jnp.arange(ts, dtype=I32)
    zero_shift = jnp.zeros((bp, RW_IN), dt)
    zero_wkv = jnp.zeros((bp, N_HEADS, HEAD_DIM, HEAD_DIM), dt)
    c_all = jnp.concatenate([c_prompt, c_sample], axis=0)
    xp, xs = x_prompt, x_sample
    outs_p, outs_s = [], []
    for l in range(depth):
        lw = _layer_weights(l, w_in, b_gate, mu_rw, w0, w_up, a0, a_up, g_up, k_k, k_a, r_k, lnx_w, lnx_b, q_norm_w,
                            k_norm_w, w_proj_a, w_proj_b, w_out, norm1_w, norm2_w, w_pq, peer_keys, peer_u, peer_v)
        mod = _ada(c_all, w_ada[l], b_ada[l])
        xp, *rest_p = _layer(xp, mod[:bp], pos_p, zero_shift, zero_wkv, None, lw)
        xs, *rest_s = _layer(xs, mod[bp:], pos_s, state_shift[l], state_wkv[l],
                             (cache_k[l], cache_v[l], cache_kidx[l]), lw)
        outs_p.append(rest_p)
        outs_s.append(rest_s)
    stack = lambda outs, i: jnp.stack([o[i] for o in outs])
    return (xp, xs,
            stack(outs_p, 0), stack(outs_p, 1), stack(outs_p, 2), stack(outs_p, 3), stack(outs_p, 4),
            stack(outs_s, 0), stack(outs_s, 1), stack(outs_s, 2), stack(outs_s, 3), stack(outs_s, 4))
```

```python
import functools

import jax
import jax.numpy as jnp
from jax import lax
from jax.experimental import pallas as pl
from jax.experimental.pallas import tpu as pltpu

F32 = jnp.float32
BF16 = jnp.bfloat16
I32 = jnp.int32

LANES = 128
D_MODEL = 1024
EPS = 1e-6
GN_EPS = 64e-5
ROPE_THETA = 10000.0
CHUNK = 64
TOPK_MAX = 256
HEAD_DIM = 64
N_HEADS = D_MODEL // HEAD_DIM
N_PAIRS = N_HEADS // 2
IDX_HEADS = 8
IDX_DIM = 64
D_DECAY = 64
D_AAA = 64
D_GATE = 160
RW_IN = 3 * D_MODEL + D_DECAY + D_AAA + D_GATE
PEER_HEADS = 8
PEER_NKEYS = 128
PEER_TOPK = 16
PEER_DHALF = 64
N_EXPERTS = PEER_NKEYS * PEER_NKEYS
RW_CHUNK = 64
RW_BASE = 8
RW_INTERLEAVE = 16
RW_PAIRS_LONG = 2
RW_PASSES = (2, 1, 1, 1, 1)
VMEM_LIMIT = 56 * 1024 * 1024
LOG2E = 1.4426950408889634
PEER_CAND = 80

C_R, C_K, C_V = 0, 1024, 2048
C_Q, C_KD, C_VD = 3072, 4096, 5120
C_GA, C_GB = 6144, 7168
C_QI = 8192
C_G = 8704
C_M = 8960
C_KW = 9088
P_COLS = 9216
IN_W = 9064

NT = (((1,), (1,)), ((), ()))
NN = (((1,), (0,)), ((), ()))


def _pack_in(w):
    z = lambda k: jnp.zeros(w.shape[:-1] + (k,), w.dtype)
    return jnp.concatenate([w[..., 0:3072], w[..., 3360:6432], w[..., 7016:9064], w[..., 6432:6944],
                            w[..., 3200:3360], z(256 - D_GATE), w[..., 3072:3200],
                            w[..., 6944:7016], z(LANES - IDX_DIM - IDX_HEADS)], axis=-1)


def _pack_rw(a):
    return _pack_in(jnp.concatenate([a, jnp.zeros(a.shape[:-1] + (IN_W - RW_IN,), a.dtype)], axis=-1))


def _unpack_rw(p):
    return jnp.concatenate([p[..., :3072], p[..., C_M:C_M + 128], p[..., C_G:C_G + D_GATE]], axis=-1)


def _split_bf16(x, n):
    parts = []
    r = x
    for _ in range(n):
        p = r.astype(BF16)
        parts.append(p)
        r = r - p.astype(F32)
    return parts


def _mm(a, b, pa=1, pb=1, dims=NN):
    aps = _split_bf16(a, pa) if a.dtype != BF16 else [a]
    bps = _split_bf16(b, pb) if b.dtype != BF16 else [b]
    order = max(len(aps), len(bps))
    out = None
    for i, ap in enumerate(aps):
        for j, bp in enumerate(bps):
            if i + j >= order:
                continue
            t = lax.dot_general(ap, bp, dims, preferred_element_type=F32)
            out = t if out is None else out + t
    return out


def _sigmoid(x):
    return 1.0 / (1.0 + jnp.exp(-x))


def _softplus(z):
    return jnp.maximum(z, 0.0) + jnp.log(1.0 + jnp.exp(-jnp.abs(z)))


def _cparams(sem):
    return pltpu.CompilerParams(dimension_semantics=sem, vmem_limit_bytes=VMEM_LIMIT)


def _ada_kernel(c_ref, w_ref, b_ref, o_ref):
    c = c_ref[...]
    s = c * _sigmoid(c)
    o_ref[...] = _mm(s, w_ref[...], 2, 2) + b_ref[...]


def _ada(c, w, b):
    m, d = c.shape
    n = w.shape[1]
    tn = 1024
    return pl.pallas_call(
        _ada_kernel,
        out_shape=jax.ShapeDtypeStruct((m, n), F32),
        grid=(n // tn,),
        in_specs=[pl.BlockSpec((m, d), lambda j: (0, 0)),
                  pl.BlockSpec((d, tn), lambda j: (0, j)),
                  pl.BlockSpec((1, tn), lambda j: (0, j))],
        out_specs=pl.BlockSpec((m, tn), lambda j: (0, j)),
        compiler_params=_cparams(("arbitrary",)),
    )(c, w, b.reshape(1, n))


def _seq_operand(vec, seq_len, tm):
    b, d = vec.shape
    if seq_len % tm == 0:
        per = seq_len // tm
        arr = vec.reshape(b, 1, d)
        spec = pl.BlockSpec((None, 1, d), lambda *g: (g[0] // per, 0, 0))
    else:
        assert tm % seq_len == 0
        arr = jnp.repeat(vec, seq_len, axis=0)
        spec = pl.BlockSpec((tm, d), lambda *g: (g[0], 0))
    return arr, spec


def _row_tile(n, seq_len, cap):
    tm = min(cap, n)
    while n % tm or (seq_len % tm and tm % seq_len):
        tm //= 2
    return tm


def _inproj_kernel(x_ref, sc_ref, sh_ref, nw_ref, w_ref, o_ref, h_scr):
    @pl.when(pl.program_id(1) == 0)
    def _():
        x = x_ref[...]
        y = x * lax.rsqrt(jnp.mean(x * x, axis=-1, keepdims=True) + EPS) * nw_ref[...]
        h_scr[...] = (y * (1.0 + sc_ref[...]) + sh_ref[...]).astype(BF16)

    o_ref[...] = jnp.dot(h_scr[...], w_ref[...], preferred_element_type=F32)


def _inproj(x2, sc, sh, nw, w16, seq_len):
    n, d = x2.shape
    tm = _row_tile(n, seq_len, 1024)
    tn = P_COLS // 4
    sc_a, sc_s = _seq_operand(sc, seq_len, tm)
    sh_a, sh_s = _seq_operand(sh, seq_len, tm)
    return pl.pallas_call(
        _inproj_kernel,
        out_shape=jax.ShapeDtypeStruct((n, P_COLS), F32),
        grid=(n // tm, P_COLS // tn),
        in_specs=[pl.BlockSpec((tm, d), lambda i, j: (i, 0)), sc_s, sh_s,
                  pl.BlockSpec((1, d), lambda i, j: (0, 0)),
                  pl.BlockSpec((d, tn), lambda i, j: (0, j))],
        out_specs=pl.BlockSpec((tm, tn), lambda i, j: (i, j)),
        scratch_shapes=[pltpu.VMEM((tm, d), BF16)],
        compiler_params=_cparams(("parallel", "arbitrary")),
    )(x2, sc_a, sh_a, nw.reshape(1, d), w16)


def _lane_lo(shape):
    return lax.broadcasted_iota(I32, shape, len(shape) - 1) < HEAD_DIM


def _pair_sum(x):
    lo = _lane_lo(x.shape)
    s0 = jnp.sum(jnp.where(lo, x, 0.0), axis=-1, keepdims=True)
    s1 = jnp.sum(jnp.where(lo, 0.0, x), axis=-1, keepdims=True)
    return jnp.where(lo, s0, s1)


def _stack2(x):
    lo = _lane_lo(x.shape)
    return jnp.concatenate([jnp.where(lo, x, 0.0), jnp.where(lo, 0.0, x)], axis=0)


def _rwkv_kernel(t_real, npair, nchunk, pr, pk, pv, pg, pm, sr, sk, sv, sg, sm, mr, mk, mv, mg, mmu,
                 w0, a0, kkw, kaw, rkw, lnw, lnb, wup, aup, gup, z0, o_ref, zf_ref,
                 r_s, lw_s, k_s, v_s, a_s, b_s, y_s, bonus_s, g_s):
    C = RW_CHUNK
    t_pad = r_s.shape[1]

    def mix(p_ref, s_ref, m_ref):
        p = p_ref[...]
        prev = pltpu.roll(p, 1, 0)
        row = lax.broadcasted_iota(I32, p.shape, 0)
        prev = jnp.where(row == 0, s_ref[...], prev)
        return p + (prev - p) * m_ref[...]

    xg, xm = mix(pg, sg, mg), mix(pm, sm, mmu)
    th16, xm16, sg16 = jnp.tanh(xm).astype(BF16), xm.astype(BF16), _sigmoid(xg).astype(BF16)
    xr_all, xk_all, xv_all = mix(pr, sr, mr), mix(pk, sk, mk), mix(pv, sv, mv)

    def put(ref, pp, val):
        if t_pad > t_real:
            val = jnp.concatenate([val, jnp.zeros((t_pad - t_real, LANES), F32)], axis=0)
        ref[pp] = val

    for pp in range(npair):
        cs = slice(pp * LANES, (pp + 1) * LANES)
        xr, xk, xv = xr_all[:, cs], xk_all[:, cs], xv_all[:, cs]
        dw = jnp.dot(th16, wup[:, cs], preferred_element_type=F32)
        lw = -jnp.exp(-_softplus(-(w0[:, cs] + dw)) - 0.5)
        asig = _sigmoid(a0[:, cs] + jnp.dot(xm16, aup[:, cs], preferred_element_type=F32))
        g_s[pp] = jnp.dot(sg16, gup[:, cs], preferred_element_type=F32)
        kk = xk * kkw[:, cs]
        kk = kk * lax.rsqrt(_pair_sum(kk * kk) + 1e-12)
        kmod = xk * (1.0 + (asig - 1.0) * kaw[:, cs])
        bonus_s[pp] = _pair_sum(xr * kmod * rkw[:, cs]) * xv
        put(r_s, pp, xr)
        put(lw_s, pp, lw)
        put(k_s, pp, kmod)
        put(v_s, pp, xv)
        put(a_s, pp, -kk)
        put(b_s, pp, kk * asig)

    n2 = 2 * C
    ri = lax.broadcasted_iota(I32, (n2, n2), 0)
    ci = lax.broadcasted_iota(I32, (n2, n2), 1)
    same = (ri // C) == (ci // C)
    strict = same & ((ri % C) > (ci % C))
    incl = same & ((ri % C) >= (ci % C))
    eye = ri == ci
    eye_f = jnp.where(eye, 1.0, 0.0)
    blk = {}
    size = RW_BASE
    while size <= C:
        blk[size] = (ri // size) == (ci // size)
        size *= 2
    tri = jnp.where(lax.broadcasted_iota(I32, (C, C), 0) >= lax.broadcasted_iota(I32, (C, C), 1), 1.0, 0.0
                    ).astype(BF16)
    zeros_sq = jnp.zeros((n2, LANES), F32)

    pc_, pg_, pi_, po_, ps_ = RW_PASSES

    def local(chains):
        each = lambda f, *cols: [f(*xs) for xs in zip(*cols)]
        lwc = [lw_s[pp, sl, :] for sl, pp in chains]
        cum = each(lambda l: _mm(tri, l, 1, pc_), lwc)
        cum_last = each(lambda c: c[C - 1:C, :], cum)
        ec, eci = each(jnp.exp, cum), each(lambda c: jnp.exp(-c), cum)
        ecp = each(lambda c, l: jnp.exp(c - l), cum, lwc)
        ecl = each(lambda c, cl: jnp.exp(cl - c), cum, cum_last)
        a_c = [a_s[pp, sl, :] for sl, pp in chains]
        b_c = [b_s[pp, sl, :] for sl, pp in chains]
        k_c = [k_s[pp, sl, :] for sl, pp in chains]
        r_c = [r_s[pp, sl, :] for sl, pp in chains]
        As = each(lambda a, e: _stack2(a * e), a_c, ecp)
        Rs = each(lambda r, e: _stack2(r * e), r_c, ec)
        Bs = each(lambda b, e: _stack2(b * e), b_c, eci)
        Ks = each(lambda k, e: _stack2(k * e), k_c, eci)
        Bt = each(lambda b, e: _stack2(b * e), b_c, ecl)
        Kt = each(lambda k, e: _stack2(k * e), k_c, ecl)
        Vs = [_stack2(v_s[pp, sl, :]) for sl, pp in chains]

        G = each(lambda a, r, b, k: _mm(jnp.concatenate([a, r], axis=0), jnp.concatenate([b, k], axis=0),
                                        pg_, pg_, NT), As, Rs, Bs, Ks)
        a_ab = each(lambda g: jnp.where(strict, g[:n2, :n2], 0.0), G)
        a_ak = each(lambda g: jnp.where(strict, g[:n2, n2:], 0.0), G)
        a_rb = each(lambda g: jnp.where(incl, g[n2:, :n2], 0.0), G)
        a_rk = each(lambda g: jnp.where(incl, g[n2:, n2:], 0.0), G)

        lp = each(lambda a: jnp.where(blk[RW_BASE], a, 0.0), a_ab)
        tm_ = each(lambda l: eye_f + l, lp)
        step = 2
        while step < RW_BASE:
            lp = each(lambda l: _mm(l, l, pi_, pi_), lp)
            tm_ = each(lambda t, l: t + _mm(t, l, pi_, pi_), tm_, lp)
            step *= 2
        size = RW_BASE
        while size < C:
            off = each(lambda a: jnp.where(blk[2 * size] & jnp.logical_not(blk[size]), a, 0.0), a_ab)
            tm_ = each(lambda t, o: t + _mm(t, _mm(o, t, pi_, pi_), pi_, pi_), tm_, off)
            size *= 2

        w1 = each(lambda a, v: _mm(a, v, po_, po_), a_ak, Vs)
        mu_ = each(lambda t, a, w: _mm(t, jnp.concatenate([a, w], axis=1), po_, po_), tm_, As, w1)
        rhs = each(lambda m, v: jnp.concatenate([m, jnp.concatenate([zeros_sq, v], axis=1)], axis=0), mu_, Vs)
        lhs = each(lambda rb, rk, b, k: jnp.concatenate([jnp.concatenate([rb, rk], axis=1),
                                                         jnp.concatenate([b.T, k.T], axis=1)], axis=0),
                   a_rb, a_rk, Bt, Kt)
        out2 = each(lambda l, r: _mm(l, r, po_, po_), lhs, rhs)
        m23 = each(lambda r, o, cl: jnp.concatenate([r + o[:n2, :LANES],
                                                     jnp.where(eye, jnp.exp(cl), 0.0) + o[n2:, :LANES]], axis=0),
                   Rs, out2, cum_last)
        return [(m, o[:n2, LANES:], o[n2:, LANES:]) for m, o in zip(m23, out2)]

    def step_chunks(i, zs):
        sls = [pl.ds(pl.multiple_of((i * nchunk + j) * C, C), C) for j in range(nchunk)]
        parts = local([(sl, pp) for sl in sls for pp in range(npair)])
        zs = list(zs)
        for j, sl in enumerate(sls):
            for pp in range(npair):
                m23, y_loc, z_loc = parts[j * npair + pp]
                yz = _mm(m23, zs[pp], ps_, ps_)
                y = yz[:n2] + y_loc
                y_s[pp, sl, :] = y[:C] + y[C:]
                zs[pp] = yz[n2:] + z_loc
        return tuple(zs)

    zs = lax.fori_loop(0, t_pad // (C * nchunk), step_chunks, tuple(z0[pp] for pp in range(npair)))
    for pp in range(npair):
        zf_ref[pp] = zs[pp]
        cs = slice(pp * LANES, (pp + 1) * LANES)
        y = y_s[pp, 0:t_real, :]
        mean = _pair_sum(y) * (1.0 / HEAD_DIM)
        dlt = y - mean
        var = _pair_sum(dlt * dlt) * (1.0 / HEAD_DIM)
        yn = dlt * lax.rsqrt(var + GN_EPS) * lnw[:, cs] + lnb[:, cs]
        o_ref[:, cs] = ((yn + bonus_s[pp]) * g_s[pp]).astype(o_ref.dtype)


def _rwkv(P, nb, t, prev, mu, w0, a0, k_k, k_a, r_k, lnx_w, lnx_b, wup, aup, gup, z0):
    t_pad = max(t, RW_CHUNK)
    assert t % 8 == 0 and t_pad % RW_CHUNK == 0
    n_chunks = t_pad // RW_CHUNK
    nchunk = min(RW_INTERLEAVE // RW_PAIRS_LONG, n_chunks)
    npair = min(N_PAIRS, max(1, RW_INTERLEAVE // nchunk))
    wp = npair * LANES

    def cblk(c0, w, per_pair):
        return (lambda p: c0 // w + p) if per_pair else (lambda p: c0 // w)

    def pcol(c0, w, pp):
        f = cblk(c0, w, pp)
        return pl.BlockSpec((t, w), lambda b, p: (b, f(p)))

    def prevcol(c0, w, pp):
        f = cblk(c0, w, pp)
        return pl.BlockSpec((None, 1, w), lambda b, p: (b, 0, f(p)))

    def mucol(c0, w, pp):
        f = cblk(c0, w, pp)
        return pl.BlockSpec((1, w), lambda b, p: (0, f(p)))

    def hvec():
        return pl.BlockSpec((1, wp), lambda b, p: (0, p))

    cols = [(C_R, wp, True), (C_K, wp, True), (C_V, wp, True), (C_G, 256, False), (C_M, LANES, False)]
    in_specs = ([pcol(*c) for c in cols] + [prevcol(*c) for c in cols] + [mucol(*c) for c in cols]
                + [hvec() for _ in range(7)]
                + [pl.BlockSpec((LANES, wp), lambda b, p: (0, p)),
                   pl.BlockSpec((LANES, wp), lambda b, p: (0, p)),
                   pl.BlockSpec((256, wp), lambda b, p: (0, p)),
                   pl.BlockSpec((None, npair, LANES, LANES), lambda b, p: (b, p, 0, 0))])
    vecs = [v.reshape(1, D_MODEL) for v in (w0, a0, k_k, k_a, r_k, lnx_w, lnx_b)]
    o, zf = pl.pallas_call(
        functools.partial(_rwkv_kernel, t, npair, nchunk),
        out_shape=(jax.ShapeDtypeStruct((nb * t, D_MODEL), BF16),
                   jax.ShapeDtypeStruct((nb, N_PAIRS, LANES, LANES), F32)),
        grid=(nb, N_PAIRS // npair),
        in_specs=in_specs,
        out_specs=(pl.BlockSpec((t, wp), lambda b, p: (b, p)),
                   pl.BlockSpec((None, npair, LANES, LANES), lambda b, p: (b, p, 0, 0))),
        scratch_shapes=([pltpu.VMEM((npair, t_pad, LANES), F32) for _ in range(7)]
                        + [pltpu.VMEM((npair, t, LANES), F32) for _ in range(2)]),
        compiler_params=_cparams(("parallel", "arbitrary")),
    )(P, P, P, P, P, prev, prev, prev, prev, prev, mu, mu, mu, mu, mu, *vecs, wup, aup, gup, z0)
    return o, zf


def _state_to_pairs(s):
    nb = s.shape[0]
    zt = jnp.swapaxes(s, -1, -2).reshape(nb, N_PAIRS, 2, HEAD_DIM, HEAD_DIM)
    zero = jnp.zeros_like(zt[:, :, 0])
    top = jnp.concatenate([zt[:, :, 0], zero], axis=-1)
    bot = jnp.concatenate([zero, zt[:, :, 1]], axis=-1)
    return jnp.concatenate([top, bot], axis=-2)


def _pairs_to_state(z):
    nb = z.shape[0]
    h0 = z[:, :, :HEAD_DIM, :HEAD_DIM]
    h1 = z[:, :, HEAD_DIM:, HEAD_DIM:]
    s = jnp.stack([h0, h1], axis=2).reshape(nb, N_HEADS, HEAD_DIM, HEAD_DIM)
    return jnp.swapaxes(s, -1, -2)


def _rope(x, cos, sin_signed):
    w = x.shape[1]
    reps = w // LANES
    cw = jnp.concatenate([cos] * reps, axis=1) if reps > 1 else cos
    sw = jnp.concatenate([sin_signed] * reps, axis=1) if reps > 1 else sin_signed
    lane = lax.broadcasted_iota(I32, x.shape, 1)
    fwd = pltpu.roll(x, w - 32, 1)
    bwd = pltpu.roll(x, 32, 1)
    partner = jnp.where((lane % HEAD_DIM) < 32, fwd, bwd)
    return x * cw + partner * sw


def _head_rms(x, nw, e_dn, e_up):
    ms = _mm(x * x, e_dn, 2, 1) * (1.0 / HEAD_DIM)
    r = lax.rsqrt(ms + EPS)
    return x * _mm(r, e_up, 2, 1) * nw


def _dsa_prep_kernel(pq, pkd, pvd, pqi, pkw, cos_ref, sin_ref, qn, kn, edn, eup,
                     q16, k32, k16, v32, v16, qi16, kw32, ki2):
    cos, sin = cos_ref[...], sin_ref[...]
    e_dn, e_up = edn[...], eup[...]
    def put_pairs(ref, x):
        for p in range(N_PAIRS):
            ref[p] = x[:, p * LANES:(p + 1) * LANES].astype(ref.dtype)

    q = _rope(_head_rms(pq[...], qn[...], e_dn, e_up), cos, sin)
    put_pairs(q16, q * (HEAD_DIM ** -0.5 * LOG2E))
    k = _rope(_head_rms(pkd[...], kn[...], e_dn, e_up), cos, sin)
    k32[...] = k
    put_pairs(k16, k)
    v = pvd[...]
    v32[...] = v
    put_pairs(v16, v)
    qi16[...] = _rope(pqi[...], cos, sin).astype(BF16)
    kw = pkw[...]
    lane = lax.broadcasted_iota(I32, kw.shape, 1)
    wi_scale = (IDX_HEADS * IDX_DIM) ** -0.5
    kr = _rope(kw, cos, sin)
    kw32[...] = jnp.where(lane < IDX_DIM, kr, jnp.where(lane < IDX_DIM + IDX_HEADS, kw * wi_scale, 0.0))
    ki2[...] = jnp.where(lane < IDX_DIM, kr, pltpu.roll(kr, IDX_DIM, 1)).astype(BF16)


def _dsa_prep(P, pos_rows, q_norm_w, k_norm_w):
    n = P.shape[0]
    tm = 512 if n % 512 == 0 else n
    half = HEAD_DIM // 2
    inv = ROPE_THETA ** (-jnp.arange(half, dtype=F32) / half)
    ang = pos_rows.astype(F32)[:, None] * inv[None, :]
    cos = jnp.tile(jnp.cos(ang), (1, 4))
    sin = jnp.sin(ang)
    sin_signed = jnp.tile(jnp.concatenate([-sin, sin], axis=1), (1, 2))
    head_of = jnp.arange(D_MODEL) // HEAD_DIM
    e_dn = (head_of[:, None] == jnp.arange(LANES)[None, :]).astype(BF16)
    e_up = e_dn.T
    qn = jnp.tile(q_norm_w, N_HEADS).reshape(1, D_MODEL)
    kn = jnp.tile(k_norm_w, N_HEADS).reshape(1, D_MODEL)

    def col(c0, w):
        return pl.BlockSpec((tm, w), lambda i, c0=c0, w=w: (i, c0 // w))

    def row(w):
        return pl.BlockSpec((tm, w), lambda i: (i, 0))

    def const(shape):
        return pl.BlockSpec(shape, lambda i: (0, 0))

    pairs = jax.ShapeDtypeStruct((N_PAIRS, n, LANES), BF16)
    pair_spec = pl.BlockSpec((N_PAIRS, tm, LANES), lambda i: (0, i, 0))
    return pl.pallas_call(
        _dsa_prep_kernel,
        out_shape=(pairs, jax.ShapeDtypeStruct((n, D_MODEL), F32), pairs, jax.ShapeDtypeStruct((n, D_MODEL), F32),
                   pairs, jax.ShapeDtypeStruct((n, IDX_HEADS * IDX_DIM), BF16),
                   jax.ShapeDtypeStruct((n, LANES), F32), jax.ShapeDtypeStruct((n, LANES), BF16)),
        grid=(n // tm,),
        in_specs=[col(C_Q, 1024), col(C_KD, 1024), col(C_VD, 1024), col(C_QI, 512), col(C_KW, LANES),
                  row(LANES), row(LANES), const((1, D_MODEL)), const((1, D_MODEL)),
                  const((D_MODEL, LANES)), const((LANES, D_MODEL))],
        out_specs=(pair_spec, row(D_MODEL), pair_spec, row(D_MODEL), pair_spec, row(512), row(LANES), row(LANES)),
        compiler_params=_cparams(("parallel",)),
    )(P, P, P, P, P, cos, sin_signed, qn, kn, e_dn, e_up)


CODE_NEG_INF = -1 - 0x7F800000
BITS_PER_CHECK = 2
ATTN_TQ = 256
ATTN_PAIRS = 2
ATTN_CASES = 4
SAMPLE_PAIRS = 8


def _index_scores(qi, wi, ki_list):
    outs = []
    for ki in ki_list:
        acc = None
        for h in range(IDX_HEADS):
            qpair = qi[:, (h // 2) * LANES:(h // 2 + 1) * LANES]
            lo = _lane_lo(qpair.shape)
            qh = jnp.where(lo if h % 2 == 0 else jnp.logical_not(lo), qpair, jnp.zeros_like(qpair))
            rel = lax.dot_general(qh, ki, NT, preferred_element_type=F32)
            term = wi[:, IDX_DIM + h:IDX_DIM + h + 1] * jnp.maximum(rel, 0.0)
            acc = term if acc is None else acc + term
        outs.append(acc)
    return outs


def _select_topk(keys, topk, bias_refs):
    tq = keys[0].shape[0]
    neg = -jnp.inf

    def write(masks):
        for ref, k, msk in zip(bias_refs, keys, masks):
            ref[:, 0:k.shape[1]] = jnp.where(msk, 0.0, neg)

    def count(pred_list):
        tot = None
        for p in pred_list:
            c = jnp.sum(jnp.where(p, 1.0, 0.0), axis=-1, keepdims=True)
            tot = c if tot is None else tot + c
        return tot

    def threshold(c):
        bits = jnp.where(c >= 0, c, c ^ jnp.int32(0x7FFFFFFF))
        return jnp.where(c < jnp.int32(CODE_NEG_INF), neg, lax.bitcast_convert_type(bits, F32))

    few = count([k > neg for k in keys]) <= topk

    def pending(cnt):
        return jnp.max(jnp.where(few | (cnt == topk), 0.0, 1.0))

    def bit_step(state):
        i, c, cnt, _ = state
        for b in range(BITS_PER_CHECK):
            trial = c + jnp.left_shift(jnp.int32(1), 31 - (i + b))
            cnt_t = count([k >= threshold(trial) for k in keys])
            take = cnt_t >= topk
            cnt = jnp.where(take, cnt_t, cnt)
            c = jnp.where(take, trial, c)
        return i + BITS_PER_CHECK, c, cnt, pending(cnt)

    cnt0 = jnp.full((tq, 1), float(sum(k.shape[1] for k in keys)), F32)
    state = (jnp.int32(0), jnp.full((tq, 1), -2 ** 31, I32), cnt0, pending(cnt0))
    _, code, _, _ = lax.while_loop(lambda s: (s[0] < 32) & (s[3] > 0.0), bit_step, state)
    thr = threshold(code)
    ge = [(k >= thr) & (k > neg) for k in keys]
    write(ge)
    surplus = jnp.max(count(ge)) > topk

    @pl.when(surplus)
    def _():
        gt = [k > thr for k in keys]
        need = topk - count(gt)
        ties = [(k == thr) & (k > neg) for k in keys]
        offs, idx = 0, []
        for k in keys:
            idx.append(lax.broadcasted_iota(I32, k.shape, 1) + offs)
            offs += k.shape[1]
        nbits = max(1, (offs - 1).bit_length() + 1)

        def idx_step(i, m):
            trial = m + jnp.left_shift(jnp.int32(1), nbits - 1 - i)
            cnt = count([t & (ix < trial) for t, ix in zip(ties, idx)])
            return jnp.where(cnt <= need, trial, m)

        cut = lax.fori_loop(0, nbits, idx_step, jnp.zeros((tq, 1), I32))
        write([g | (t & (ix < cut)) for g, t, ix in zip(gt, ties, idx)])


def _attend_pairs(q_pairs, k_lists, v_lists, bias_list):
    lo = _lane_lo(q_pairs[0].shape)
    zero = jnp.zeros_like(q_pairs[0])
    heads = []
    for pi, q in enumerate(q_pairs):
        heads += [(jnp.where(lo, q, zero), pi), (jnp.where(lo, zero, q), pi)]
    s = [[lax.dot_general(qh, k, NT, preferred_element_type=F32) + b for k, b in zip(k_lists[pi], bias_list)]
         for qh, pi in heads]
    m = []
    for sh in s:
        mh = None
        for sj in sh:
            mx = jnp.max(sj, axis=-1, keepdims=True)
            mh = mx if mh is None else jnp.maximum(mh, mx)
        m.append(mh)
    p = [[jnp.exp2(sj - mh) for sj in sh] for sh, mh in zip(s, m)]
    den = [functools.reduce(lambda a, b: a + b, [jnp.sum(pj, axis=-1, keepdims=True) for pj in ph]) for ph in p]
    acc = [functools.reduce(lambda a, b: a + b,
                            [jnp.dot(pj.astype(BF16), v, preferred_element_type=F32)
                             for pj, v in zip(ph, v_lists[pi])])
           for ph, (_, pi) in zip(p, heads)]
    outs = [a / d for a, d in zip(acc, den)]
    return [jnp.where(lo, outs[2 * i], outs[2 * i + 1]) for i in range(len(q_pairs))]


def _attn_prompt_kernel(topk, ncase, q_ref, qi_ref, kw_ref, k_ref, v_ref, ki2_ref, o_ref, bias_s):
    tq = q_ref.shape[1]
    t = k_ref.shape[1]
    i = pl.program_id(1)
    lstep = t // ncase
    case = ((i + 1) * tq - 1) // lstep

    def run(L):
        score = _index_scores(qi_ref[...], kw_ref[...], [ki2_ref[0:L, :]])[0]
        qpos = i * tq + lax.broadcasted_iota(I32, (tq, L), 0)
        kpos = lax.broadcasted_iota(I32, (tq, L), 1)
        adm = (qpos // CHUNK) >= (kpos // CHUNK)
        _select_topk([jnp.where(adm, score, -jnp.inf)], topk, [bias_s])

        def pairs(g, carry):
            ps = [ATTN_PAIRS * g + j for j in range(ATTN_PAIRS)]
            outs = _attend_pairs([q_ref[p] for p in ps], [[k_ref[p, 0:L, :]] for p in ps],
                                 [[v_ref[p, 0:L, :]] for p in ps], [bias_s[:, 0:L]])
            for p, o in zip(ps, outs):
                o_ref[p] = o.astype(o_ref.dtype)
            return carry

        lax.fori_loop(0, N_PAIRS // ATTN_PAIRS, pairs, 0)

    for c in range(ncase):
        pl.when(case == c)(functools.partial(run, (c + 1) * lstep))


def _attn_prompt(q16, qi16, kw32, k16, v16, ki2, nb, t):
    tq = min(ATTN_TQ, t)
    topk = min(TOPK_MAX, t // 4)
    nq = t // tq
    ncase = min(ATTN_CASES, nq)

    def qrow(w):
        return pl.BlockSpec((tq, w), lambda b, i: (b * nq + i, 0))

    def qpairs():
        return pl.BlockSpec((N_PAIRS, tq, LANES), lambda b, i: (0, b * nq + i, 0))

    def kpairs():
        return pl.BlockSpec((N_PAIRS, t, LANES), lambda b, i: (0, b, 0))

    return pl.pallas_call(
        functools.partial(_attn_prompt_kernel, topk, ncase),
        out_shape=jax.ShapeDtypeStruct((N_PAIRS, nb * t, LANES), BF16),
        grid=(nb, nq),
        in_specs=[qpairs(), qrow(512), qrow(LANES), kpairs(), kpairs(),
                  pl.BlockSpec((t, LANES), lambda b, i: (b, 0))],
        out_specs=qpairs(),
        scratch_shapes=[pltpu.VMEM((tq, t), F32)],
        compiler_params=_cparams(("parallel", "arbitrary")),
    )(q16, qi16, kw32, k16, v16, ki2)


def _attn_sample_kernel(topk, past, q_ref, qi_ref, kw_ref, ck_ref, cv_ref, cki2_ref, k_ref, v_ref, ki2_ref, o_ref,
                        biasc_s, biasn_s):
    npairs, ts = q_ref.shape[0], q_ref.shape[1]

    @pl.when(pl.program_id(1) == 0)
    def _():
        sc, sn = _index_scores(qi_ref[...], kw_ref[...], [cki2_ref[...], ki2_ref[...]])
        qpos = past + lax.broadcasted_iota(I32, (ts, 1), 0)
        kpos_c = lax.broadcasted_iota(I32, sc.shape, 1)
        kpos_n = past + lax.broadcasted_iota(I32, sn.shape, 1)
        keys = [jnp.where((qpos // CHUNK) >= (kpos_c // CHUNK), sc, -jnp.inf),
                jnp.where((qpos // CHUNK) >= (kpos_n // CHUNK), sn, -jnp.inf)]
        _select_topk(keys, topk, [biasc_s, biasn_s])

    lanes = [slice(p * LANES, (p + 1) * LANES) for p in range(npairs)]
    outs = _attend_pairs([q_ref[p] for p in range(npairs)],
                         [[ck_ref[:, cs].astype(BF16), k_ref[p]] for p, cs in enumerate(lanes)],
                         [[cv_ref[:, cs].astype(BF16), v_ref[p]] for p, cs in enumerate(lanes)],
                         [biasc_s[...], biasn_s[...]])
    for p, o in enumerate(outs):
        o_ref[p] = o.astype(o_ref.dtype)


def _attn_sample(q16, qi16, kw32, k16, v16, ki2, cache_k, cache_v, cache_kidx, nb, ts):
    past = cache_k.shape[1]
    cki2 = jnp.concatenate([cache_kidx, cache_kidx], axis=-1).astype(BF16)
    topk = min(TOPK_MAX, (past + ts) // 4)

    def qrow(w):
        return pl.BlockSpec((ts, w), lambda b, p: (b, 0))

    sp = SAMPLE_PAIRS

    def qpair():
        return pl.BlockSpec((sp, ts, LANES), lambda b, p: (p, b, 0))

    def cache(pair):
        if pair:
            return pl.BlockSpec((None, past, sp * LANES), lambda b, p: (b, 0, p))
        return pl.BlockSpec((None, past, LANES), lambda b, p: (b, 0, 0))

    return pl.pallas_call(
        functools.partial(_attn_sample_kernel, topk, past),
        out_shape=jax.ShapeDtypeStruct((N_PAIRS, nb * ts, LANES), BF16),
        grid=(nb, N_PAIRS // sp),
        in_specs=[qpair(), qrow(512), qrow(LANES), cache(True), cache(True), cache(False),
                  qpair(), qpair(), qrow(LANES)],
        out_specs=qpair(),
        scratch_shapes=[pltpu.VMEM((ts, past), F32), pltpu.VMEM((ts, ts), F32)],
        compiler_params=_cparams(("parallel", "arbitrary")),
    )(q16, qi16, kw32, cache_k, cache_v, cki2, k16, v16, ki2)


def _merge_kernel(x_ref, oa_ref, ob_ref, pga_ref, pgb_ref, bga_ref, bgb_ref, g1_ref, sc2_ref, sh2_ref, nw_ref,
                  wpa_ref, wpb_ref, wout_ref, x1_ref, h2_ref):
    ga = _sigmoid(pga_ref[...] + bga_ref[...])
    gb = _sigmoid(pgb_ref[...] + bgb_ref[...])
    ob = jnp.concatenate([ob_ref[p] for p in range(N_PAIRS)], axis=1)
    m = (ga * jnp.dot(oa_ref[...], wpa_ref[...], preferred_element_type=F32)
         + gb * jnp.dot(ob, wpb_ref[...], preferred_element_type=F32))
    x1 = x_ref[...] + g1_ref[...] * jnp.dot(m.astype(BF16), wout_ref[...], preferred_element_type=F32)
    x1_ref[...] = x1
    y = x1 * lax.rsqrt(jnp.mean(x1 * x1, axis=-1, keepdims=True) + EPS) * nw_ref[...]
    h2_ref[...] = (y * (1.0 + sc2_ref[...]) + sh2_ref[...]).astype(BF16)


def _merge(x2, o_a, o_b, P, b_gate, g1, sc2, sh2, nw2, wpa, wpb, wout, seq_len):
    n, d = x2.shape
    tm = _row_tile(n, seq_len, 512)
    g1_a, g1_s = _seq_operand(g1, seq_len, tm)
    sc_a, sc_s = _seq_operand(sc2, seq_len, tm)
    sh_a, sh_s = _seq_operand(sh2, seq_len, tm)

    def row():
        return pl.BlockSpec((tm, d), lambda i: (i, 0))

    def const(shape):
        return pl.BlockSpec(shape, lambda i: (0, 0))

    bg = b_gate.reshape(1, 2 * d)
    return pl.pallas_call(
        _merge_kernel,
        out_shape=(jax.ShapeDtypeStruct((n, d), F32), jax.ShapeDtypeStruct((n, d), BF16)),
        grid=(n // tm,),
        in_specs=[row(), row(), pl.BlockSpec((N_PAIRS, tm, LANES), lambda i: (0, i, 0)),
                  pl.BlockSpec((tm, d), lambda i: (i, C_GA // d)), pl.BlockSpec((tm, d), lambda i: (i, C_GB // d)),
                  pl.BlockSpec((1, d), lambda i: (0, 0)), pl.BlockSpec((1, d), lambda i: (0, 1)),
                  g1_s, sc_s, sh_s, const((1, d)), const((d, d)), const((d, d)), const((d, d))],
        out_specs=(row(), row()),
        compiler_params=_cparams(("parallel",)),
    )(x2, o_a, o_b, P, P, bg, bg, g1_a, sc_a, sh_a, nw2.reshape(1, d), wpa, wpb, wout)


def _top_exact(s, k):
    rows = lax.broadcasted_iota(I32, s.shape, 0).astype(F32)
    cur = s
    rank = jnp.full(s.shape, float(k), F32)
    vals = []
    for r in range(k):
        m = jnp.max(cur, axis=0, keepdims=True)
        first = jnp.min(jnp.where(cur == m, rows, 1e9), axis=0, keepdims=True)
        hit = rows == first
        vals.append(m)
        rank = jnp.where(hit, float(r), rank)
        cur = jnp.where(hit, -jnp.inf, cur)
    return vals, rank


def _top_fast(ss, k):
    curs = list(ss)
    ranks = [jnp.full(s.shape, float(k), F32) for s in ss]
    vals = [[] for _ in ss]
    for r in range(k):
        ms = [jnp.max(c, axis=0, keepdims=True) for c in curs]
        hits = [c == m for c, m in zip(curs, ms)]
        ranks = [jnp.where(h, float(r), rk) for h, rk in zip(hits, ranks)]
        curs = [jnp.where(h, -jnp.inf, c) for h, c in zip(hits, curs)]
        for v, m in zip(vals, ms):
            v.append(m)
    cleans = [jnp.max(jnp.abs(jnp.sum(jnp.where(rk < k, 1.0, 0.0), axis=0, keepdims=True) - k)) == 0.0
              for rk in ranks]
    return vals, ranks, cleans


def _top(src_scr, k, vals_scr, rank_scr, redo_s):
    n = src_scr.shape[0]
    vals, ranks, cleans = _top_fast([src_scr[i] for i in range(n)], k)
    for i in range(n):
        vals_scr[i] = jnp.concatenate(vals[i], axis=0)
        rank_scr[i] = ranks[i]
        redo_s[i] = jnp.where(cleans[i], 0, 1).astype(I32)

    def redo(i, carry):
        @pl.when(redo_s[i] == 1)
        def _():
            vals_e, rank_e = _top_exact(src_scr[i], k)
            vals_scr[i] = jnp.concatenate(vals_e, axis=0)
            rank_scr[i] = rank_e

        return carry

    lax.fori_loop(0, n, redo, 0)


def _peer_sel_kernel(h_ref, wpqt_ref, kbd_ref, g_ref, cnt_ref, r2_ref, p2_ref, s_scr, vals_scr, rank_scr,
                     cand_scr, cvals_scr, crank_scr, redo_s):
    K = PEER_TOPK
    tm = h_ref.shape[0]
    qt = lax.dot_general(wpqt_ref[...], h_ref[...], NT, preferred_element_type=F32)
    s_scr[...] = jnp.dot(kbd_ref[...], qt.astype(BF16), preferred_element_type=F32
                         ).reshape(2 * PEER_HEADS, PEER_NKEYS, tm)
    sub8 = lax.broadcasted_iota(I32, (8, tm), 0)
    neg = jnp.full((8, tm), -jnp.inf, F32)
    _top(s_scr, K, vals_scr, rank_scr, redo_s)
    for hd in range(PEER_HEADS):
        c1, c2 = vals_scr[2 * hd], vals_scr[2 * hd + 1]
        blocks = [c1[0:1] + c2, c1[1:2] + c2[0:8]]
        for k1 in range(2, 8):
            blocks.append(jnp.where(sub8 < K // (k1 + 1), c1[k1:k1 + 1] + c2[0:8], neg))
        blocks.append(c1[8:16] + c2[0:1])
        cand_scr[hd] = jnp.concatenate(blocks, axis=0)
    _top(cand_scr, K, cvals_scr, crank_scr, redo_s)
    for hd in range(PEER_HEADS):
        s1, s2 = s_scr[2 * hd], s_scr[2 * hd + 1]
        c1, c2 = vals_scr[2 * hd], vals_scr[2 * hd + 1]
        rank1, rank2 = rank_scr[2 * hd], rank_scr[2 * hd + 1]
        cand = cand_scr[hd]
        taken = crank_scr[hd] < K
        z = jnp.sum(jnp.where(taken, jnp.exp(cand - (c1[0:1] + c2[0:1])), 0.0), axis=0, keepdims=True)
        tk = jnp.where(taken, 1.0, 0.0)
        per_k1 = [jnp.sum(tk[0:16], axis=0, keepdims=True)]
        per_k1 += [jnp.sum(tk[8 + 8 * k1:16 + 8 * k1], axis=0, keepdims=True) for k1 in range(1, 8)]
        cnt16 = jnp.concatenate(per_k1 + [tk[72:80]], axis=0)
        cnt = jnp.zeros(s1.shape, F32)
        for k1 in range(K):
            cnt = jnp.where(rank1 == float(k1), cnt16[k1:k1 + 1], cnt)
        g_ref[hd] = jnp.where(rank1 < K, jnp.exp(s1 - c1[0:1]) / z, 0.0)
        cnt_ref[hd] = cnt
        p2 = jnp.where(rank2 < K, jnp.exp(s2 - c2[0:1]), 0.0)
        cb = r2_ref.shape[-1]
        for tc in range(tm // cb):
            r2_ref[hd, tc] = rank2[:, tc * cb:(tc + 1) * cb].astype(r2_ref.dtype)
            p2_ref[hd, tc] = p2[:, tc * cb:(tc + 1) * cb].astype(p2_ref.dtype)


def _peer_select(h2, wpqt, kbd):
    n, d = h2.shape
    tm = 256 if n % 256 == 0 else n
    cb = min(LANES, tm)
    big = jax.ShapeDtypeStruct((PEER_HEADS, PEER_NKEYS, n), F32)
    blocked = jax.ShapeDtypeStruct((PEER_HEADS, n // cb, PEER_NKEYS, cb), BF16)

    def blk():
        return pl.BlockSpec((PEER_HEADS, PEER_NKEYS, tm), lambda i: (0, 0, i))

    def blk4():
        return pl.BlockSpec((PEER_HEADS, tm // cb, PEER_NKEYS, cb), lambda i: (0, i, 0, 0))

    return pl.pallas_call(
        _peer_sel_kernel,
        out_shape=(big, big, blocked, blocked),
        grid=(n // tm,),
        in_specs=[pl.BlockSpec((tm, d), lambda i: (i, 0)),
                  pl.BlockSpec((d, d), lambda i: (0, 0)),
                  pl.BlockSpec((2 * d, d), lambda i: (0, 0))],
        out_specs=(blk(), blk(), blk4(), blk4()),
        scratch_shapes=[pltpu.VMEM((2 * PEER_HEADS, PEER_NKEYS, tm), F32),
                        pltpu.VMEM((2 * PEER_HEADS, PEER_TOPK, tm), F32),
                        pltpu.VMEM((2 * PEER_HEADS, PEER_NKEYS, tm), F32),
                        pltpu.VMEM((PEER_HEADS, PEER_CAND, tm), F32),
                        pltpu.VMEM((PEER_HEADS, PEER_TOPK, tm), F32),
                        pltpu.VMEM((PEER_HEADS, PEER_CAND, tm), F32),
                        pltpu.SMEM((2 * PEER_HEADS,), I32)],
        compiler_params=_cparams(("parallel",)),
    )(h2, wpqt, kbd)


def _gelu_tanh(x):
    return 0.5 * x * (1.0 + jnp.tanh(0.7978845608028654 * (x + 0.044715 * (x * x * x))))


def _peer_main_kernel(ni1, h_ref, x1_ref, g2_ref, u_ref, vt_ref, g_ref, cnt_ref, r2_ref, p2_ref, y_ref, acc, gate_s):
    j = pl.program_id(1)
    tm = h_ref.shape[0]

    @pl.when(j == 0)
    def _():
        acc[...] = jnp.zeros_like(acc)

    cb = r2_ref.shape[-1]
    reps = PEER_NKEYS // 16
    zero = jnp.zeros((PEER_NKEYS, cb), BF16)
    for l in range(ni1):
        for tc in range(tm // cb):
            ts = slice(tc * cb, (tc + 1) * cb)
            w = None
            for hd in range(PEER_HEADS):
                c16 = jnp.broadcast_to(cnt_ref[hd, l:l + 1, ts], (16, cb)).astype(BF16)
                g16 = jnp.broadcast_to(g_ref[hd, l:l + 1, ts], (16, cb)).astype(BF16)
                t = (jnp.where(r2_ref[hd, tc] < jnp.concatenate([c16] * reps, axis=0), p2_ref[hd, tc], zero)
                     * jnp.concatenate([g16] * reps, axis=0))
                w = t if w is None else w + t
            gate_s[tc, l * PEER_NKEYS:(l + 1) * PEER_NKEYS, :] = w

    act = lax.dot_general(u_ref[...], h_ref[...], NT, preferred_element_type=F32)
    gate = jnp.concatenate([gate_s[tc] for tc in range(tm // cb)], axis=1)
    coef = gate * _gelu_tanh(act.astype(BF16))
    acc[...] += jnp.dot(vt_ref[...], coef, preferred_element_type=F32)

    @pl.when(j == pl.num_programs(1) - 1)
    def _():
        y_ref[...] = x1_ref[...] + g2_ref[...] * acc[...].T


def _peer_main(h2, x1, g2, u16, vt16, g, cnt, r2, p2, seq_len):
    n, d = h2.shape
    tm = _row_tile(n, seq_len, 512)
    ni1 = 16
    et = ni1 * PEER_NKEYS
    cb = r2.shape[-1]
    g2_a, g2_s = _seq_operand(g2, seq_len, tm)

    def row():
        return pl.BlockSpec((tm, d), lambda i, j: (i, 0))

    return pl.pallas_call(
        functools.partial(_peer_main_kernel, ni1),
        out_shape=jax.ShapeDtypeStruct((n, d), F32),
        grid=(n // tm, N_EXPERTS // et),
        in_specs=[row(), row(), g2_s,
                  pl.BlockSpec((et, d), lambda i, j: (j, 0)),
                  pl.BlockSpec((d, et), lambda i, j: (0, j)),
                  pl.BlockSpec((PEER_HEADS, ni1, tm), lambda i, j: (0, j, i)),
                  pl.BlockSpec((PEER_HEADS, ni1, tm), lambda i, j: (0, j, i)),
                  pl.BlockSpec((PEER_HEADS, tm // cb, PEER_NKEYS, cb), lambda i, j: (0, i, 0, 0)),
                  pl.BlockSpec((PEER_HEADS, tm // cb, PEER_NKEYS, cb), lambda i, j: (0, i, 0, 0))],
        out_specs=row(),
        scratch_shapes=[pltpu.VMEM((d, tm), F32), pltpu.VMEM((tm // cb, et, cb), BF16)],
        compiler_params=_cparams(("parallel", "arbitrary")),
    )(h2, x1, g2_a, u16, vt16, g, cnt, r2, p2)


def _layer(x, mod, pos, shift_prev, s0, cache, lw):
    nb, t, d = x.shape
    n = nb * t
    sh1, sc1, g1, sh2, sc2, g2 = [mod[:, i * d:(i + 1) * d] for i in range(6)]
    x2 = x.reshape(n, d)
    P = _inproj(x2, sc1, sh1, lw['norm1_w'], lw['w_in16'], t)

    prev = _pack_rw(shift_prev).reshape(nb, 1, P_COLS)
    o_a, zf = _rwkv(P, nb, t, prev, lw['mu'], lw['w0'], lw['a0'], lw['k_k'], lw['k_a'], lw['r_k'], lw['lnx_w'],
                    lw['lnx_b'], lw['wup'], lw['aup'], lw['gup'], _state_to_pairs(s0))
    wkv = _pairs_to_state(zf)
    shift_last = _unpack_rw(P.reshape(nb, t, P_COLS)[:, -1, :])

    q16, k32, k16, v32, v16, qi16, kw32, ki2 = _dsa_prep(P, jnp.tile(pos, nb), lw['q_norm_w'], lw['k_norm_w'])
    if cache is None:
        o_b = _attn_prompt(q16, qi16, kw32, k16, v16, ki2, nb, t)
    else:
        ck, cv, cki = cache
        past = ck.shape[1]
        o_b = _attn_sample(q16, qi16, kw32, k16, v16, ki2, ck.reshape(nb, past, d), cv.reshape(nb, past, d), cki,
                           nb, t)

    x1, h2 = _merge(x2, o_a, o_b, P, lw['b_gate'], g1, sc2, sh2, lw['norm2_w'], lw['wpa'], lw['wpb'], lw['wout'], t)
    g, cnt, r2, p2 = _peer_select(h2, lw['wpqt'], lw['kbd'])
    y = _peer_main(h2, x1, g2, lw['u16'], lw['vt16'], g, cnt, r2, p2, t)

    k_new = k32.reshape(nb, t, N_HEADS, HEAD_DIM)
    v_new = v32.reshape(nb, t, N_HEADS, HEAD_DIM)
    ki_new = kw32[:, :IDX_DIM].reshape(nb, t, IDX_DIM)
    return y.reshape(nb, t, d), wkv, shift_last, k_new, v_new, ki_new


def _layer_weights(l, w_in, b_gate, mu_rw, w0, w_up, a0, a_up, g_up, k_k, k_a, r_k, lnx_w, lnx_b, q_norm_w, k_norm_w,
                   w_proj_a, w_proj_b, w_out, norm1_w, norm2_w, w_pq, peer_keys, peer_u, peer_v):
    d = D_MODEL
    zeros = lambda r: jnp.zeros((r, d), F32)
    keys = peer_keys[l].reshape(2 * PEER_HEADS, PEER_NKEYS, PEER_DHALF)
    eye = jnp.eye(2 * PEER_HEADS, dtype=F32)
    kbd = (eye[:, None, :, None] * keys[:, :, None, :]).reshape(2 * d, d)
    return {
        'w_in16': _pack_in(w_in[l]).astype(BF16), 'b_gate': b_gate[l], 'mu': _pack_rw(mu_rw[l]).reshape(1, P_COLS),
        'w0': w0[l], 'a0': a0[l], 'k_k': k_k[l], 'k_a': k_a[l], 'r_k': r_k[l].reshape(d), 'lnx_w': lnx_w[l],
        'lnx_b': lnx_b[l],
        'wup': jnp.concatenate([w_up[l], zeros(LANES - D_DECAY)], axis=0).astype(BF16),
        'aup': jnp.concatenate([zeros(D_DECAY), a_up[l]], axis=0).astype(BF16),
        'gup': jnp.concatenate([g_up[l], zeros(256 - D_GATE)], axis=0).astype(BF16),
        'q_norm_w': q_norm_w[l], 'k_norm_w': k_norm_w[l], 'norm1_w': norm1_w[l], 'norm2_w': norm2_w[l],
        'wpa': w_proj_a[l].astype(BF16), 'wpb': w_proj_b[l].astype(BF16), 'wout': w_out[l].astype(BF16),
        'wpqt': w_pq[l].T.astype(BF16), 'kbd': kbd.astype(BF16),
        'u16': peer_u[l].astype(BF16), 'vt16': peer_v[l].T.astype(BF16),
    }


def kernel(x_prompt, x_sample, c_prompt, c_sample, cache_k, cache_v, cache_kidx, state_wkv, state_shift, w_ada, b_ada,
           norm1_w, w_in, b_gate, mu_rw, w0, w_up, a0, a_up, g_up, k_k, k_a, r_k, lnx_w, lnx_b, q_norm_w, k_norm_w,
           w_proj_a, w_proj_b, w_out, norm2_w, w_pq, peer_keys, peer_u, peer_v):
    depth = w_in.shape[0]
    bp, tp = x_prompt.shape[:2]
    bs, ts = x_sample.shape[:2]
    past = cache_k.shape[2]
    dt = x_prompt.dtype
    pos_p = jnp.arange(tp, dtype=I32)
    pos_s = past + jnp.arange(ts, dtype=I32)
    zero_shift = jnp.zeros((bp, RW_IN), dt)
    zero_wkv = jnp.zeros((bp, N_HEADS, HEAD_DIM, HEAD_DIM), dt)
    c_all = jnp.concatenate([c_prompt, c_sample], axis=0)
    xp, xs = x_prompt, x_sample
    outs_p, outs_s = [], []
    for l in range(depth):
        lw = _layer_weights(l, w_in, b_gate, mu_rw, w0, w_up, a0, a_up, g_up, k_k, k_a, r_k, lnx_w, lnx_b, q_norm_w,
                            k_norm_w, w_proj_a, w_proj_b, w_out, norm1_w, norm2_w, w_pq, peer_keys, peer_u, peer_v)
        mod = _ada(c_all, w_ada[l], b_ada[l])
        xp, *rest_p = _layer(xp, mod[:bp], pos_p, zero_shift, zero_wkv, None, lw)
        xs, *rest_s = _layer(xs, mod[bp:], pos_s, state_shift[l], state_wkv[l],
                             (cache_k[l], cache_v[l], cache_kidx[l]), lw)
        outs_p.append(rest_p)
        outs_s.append(rest_s)
    stack = lambda outs, i: jnp.stack([o[i] for o in outs])
    return (xp, xs,
            stack(outs_p, 0), stack(outs_p, 1), stack(outs_p, 2), stack(outs_p, 3), stack(outs_p, 4),
            stack(outs_s, 0), stack(outs_s, 1), stack(outs_s, 2), stack(outs_s, 3), stack(outs_s, 4))
```
